```python
import jax
import jax.numpy as jnp
from jax import lax
import numpy as np

D_MODEL = 1024
BATCH = 4
SEQ = 4096
DEPTH = 1
DEC_BATCH = 32
DEC_SEQ = 32
PAST_LEN = 2048

CHUNK = 64
D_HGRN = D_MODEL
HGRN_EXPAND = 128
HGRN_HEADS = D_HGRN // HGRN_EXPAND
HGRN_DK = HGRN_EXPAND
HGRN_DV = D_HGRN // HGRN_HEADS
HGRN_BLOCK = CHUNK // 4
D_RWKV = D_MODEL
RWKV_HEAD = 64
RWKV_HEADS = D_RWKV // RWKV_HEAD
DECAY_LORA = max(32, int(round(1.8 * D_RWKV ** 0.5 / 32)) * 32)
ICL_LORA = max(32, int(round(1.8 * D_RWKV ** 0.5 / 32)) * 32)
GATE_LORA = max(32, int(round(0.6 * D_RWKV ** 0.8 / 32)) * 32)
RWKV_GN_EPS = 64e-5
N_RWKV_IN = 3 * D_RWKV + DECAY_LORA + ICL_LORA + GATE_LORA
N_IN = 4 * D_HGRN + N_RWKV_IN + 2 * D_MODEL
N_MEM = 256
X_HEADS = 4
X_HEAD_DIM = D_MODEL // X_HEADS
N_GROUPS = 4
EXPERTS_PER_GROUP = 8
N_EXPERTS = N_GROUPS * EXPERTS_PER_GROUP
TOP_K = 2
D_EXPERT = D_MODEL // 4
NORM_EPS = 1e-6

kernel_name = 'hgrn2_rwkv7_memxattn_hmoe_stream_step'


def rmsnorm(x, g):
    xf = x.astype(jnp.float32)
    y = xf * lax.rsqrt(jnp.mean(xf * xf, axis=-1, keepdims=True) + NORM_EPS)
    return (y * g.astype(jnp.float32)).astype(x.dtype)


def hgrn2_blocked(q, k, g, v, s0):
    B, T, H, _ = q.shape
    DV = v.shape[-1]
    L = HGRN_BLOCK
    pad = (-T) % L
    n = (T + pad) // L

    def blocks(t):
        t = jnp.pad(t.astype(jnp.float32), ((0, 0), (0, pad), (0, 0), (0, 0)))
        return t.reshape(B, n, L, H, t.shape[-1]).transpose(1, 0, 3, 2, 4)

    causal = jnp.tril(jnp.ones((L, L), dtype=bool))

    def step(S, blk):
        qb, kb, gb, vb = blk
        b = jnp.cumsum(gb, axis=2)
        b_last = b[:, :, -1:, :]
        q_dec = qb * jnp.exp(b)
        scores = jnp.einsum('bhld,bhmd->bhlm', q_dec, kb * jnp.exp(-b))
        scores = jnp.where(causal, scores, 0.0)
        o = jnp.einsum('bhlm,bhmv->bhlv', scores, vb) + jnp.einsum('bhld,bhdv->bhlv', q_dec, S)
        S = (jnp.exp(b_last[:, :, 0, :])[..., None] * S
             + jnp.einsum('bhld,bhlv->bhdv', kb * jnp.exp(b_last - b), vb))
        return S, o

    S, o = lax.scan(step, s0.astype(jnp.float32), (blocks(q), blocks(k), blocks(g), blocks(v)))
    o = o.transpose(1, 0, 3, 2, 4).reshape(B, n * L, H, DV)[:, :T]
    return o, S


def rwkv7_scan(r, w, k, v, a, b, s0):
    time_major = lambda t: jnp.moveaxis(t.astype(jnp.float32), 1, 0)

    def step(S, inp):
        rt, wt, kt, vt, at, bt = inp
        sa = jnp.einsum('bhvk,bhk->bhv', S, at)
        S = S * wt[:, :, None, :] + sa[..., None] * bt[:, :, None, :] + vt[..., None] * kt[:, :, None, :]
        return S, jnp.einsum('bhvk,bhk->bhv', S, rt)

    S, o = lax.scan(step, s0.astype(jnp.float32), tuple(time_major(t) for t in (r, w, k, v, a, b)))
    return jnp.moveaxis(o, 0, 1), S


def token_shift(p, buf, mu):
    prev = jnp.concatenate([buf.astype(p.dtype), p[:, :-1]], axis=1)
    return p + (prev - p) * mu, p[:, -1:]


def parallel_mixers(u, s_h, s_r, buf, lb, lp):
    B, T, _ = u.shape
    f32 = jnp.float32
    p = u @ lp['w_in']
    cuts = np.cumsum([D_HGRN, D_HGRN, D_HGRN, D_HGRN, N_RWKV_IN, D_MODEL]).tolist()
    hq, hf, hi, hg, pr, gate_a, gate_b = jnp.split(p, cuts, axis=-1)

    hh = lambda t, d: t.reshape(B, T, HGRN_HEADS, d)
    f = lb + (1.0 - lb) * jax.nn.sigmoid(hf.astype(f32))
    o_h, s_h_new = hgrn2_blocked(hh(jax.nn.silu(hq), HGRN_DK), hh(1.0 - f, HGRN_DK),
                                 hh(jnp.log(f), HGRN_DK), hh(hi, HGRN_DV), s_h)
    o_h = o_h * lax.rsqrt(jnp.mean(o_h * o_h, axis=-1, keepdims=True) + NORM_EPS)
    o_h = (o_h * lp['hgrn_norm_g'].astype(f32)
           * jax.nn.silu(hh(hg, HGRN_DV).astype(f32))).reshape(B, T, D_HGRN)

    sp, buf_new = token_shift(pr, buf, lp['rwkv_mu'])
    rcuts = np.cumsum([D_RWKV, D_RWKV, D_RWKV, DECAY_LORA, ICL_LORA]).tolist()
    r, k, v, xw, xa, xg = jnp.split(sp, rcuts, axis=-1)
    rh = lambda t: t.reshape(B, T, RWKV_HEADS, RWKV_HEAD)
    w_log = -jax.nn.softplus(-(lp['rwkv_w0'] + jnp.tanh(xw) @ lp['rwkv_w2']).astype(f32)) - 0.5
    decay = jnp.exp(-jnp.exp(w_log))
    a_icl = jax.nn.sigmoid((lp['rwkv_a0'] + xa @ lp['rwkv_a2']).astype(f32))
    g_out = (jax.nn.sigmoid(xg) @ lp['rwkv_g2']).astype(f32)
    kk = rh(k.astype(f32) * lp['rwkv_k_k'].astype(f32))
    kk = kk / jnp.maximum(jnp.sqrt(jnp.sum(kk * kk, axis=-1, keepdims=True)), 1e-12)
    k_mod = k.astype(f32) * (1.0 + (a_icl - 1.0) * lp['rwkv_k_a'].astype(f32))
    o_r, s_r_new = rwkv7_scan(rh(r), rh(decay), rh(k_mod), rh(v), -kk, kk * rh(a_icl), s_r)
    mean = jnp.mean(o_r, axis=-1, keepdims=True)
    var = jnp.mean(jnp.square(o_r - mean), axis=-1, keepdims=True)
    o_r = (((o_r - mean) * lax.rsqrt(var + RWKV_GN_EPS)).reshape(B, T, D_RWKV)
           * lp['rwkv_ln_g'].astype(f32) + lp['rwkv_ln_b'].astype(f32))
    bonus = jnp.sum(rh(r).astype(f32) * rh(k_mod) * lp['rwkv_r_k'].astype(f32), axis=-1, keepdims=True)
    o_r = (o_r + (bonus * rh(v).astype(f32)).reshape(B, T, D_RWKV)) * g_out

    merged = jax.nn.sigmoid(gate_a.astype(f32)) * o_h + jax.nn.sigmoid(gate_b.astype(f32)) * o_r
    return merged.astype(u.dtype), s_h_new.astype(s_h.dtype), s_r_new.astype(s_r.dtype), buf_new.astype(buf.dtype)


def memory_kv(mem, norm_g, wk, wv):
    B, M, _ = mem.shape
    m = rmsnorm(mem, norm_g)
    return (m @ wk).reshape(B, M, X_HEADS, X_HEAD_DIM), (m @ wv).reshape(B, M, X_HEADS, X_HEAD_DIM)


def memory_cross_attention(u, mem_k, mem_v, wq, wo):
    B, T, _ = u.shape
    q = (u @ wq).reshape(B, T, X_HEADS, X_HEAD_DIM)
    s = jnp.einsum('bthd,bmhd->bhtm', q.astype(jnp.float32), mem_k.astype(jnp.float32)) * (X_HEAD_DIM ** -0.5)
    attn = jax.nn.softmax(s, axis=-1)
    o = jnp.einsum('bhtm,bmhd->bthd', attn, mem_v.astype(jnp.float32)).reshape(B, T, D_MODEL)
    return o.astype(u.dtype) @ wo


def hier_moe(u, lp):
    B, T, D = u.shape
    x = u.reshape(-1, D)
    N = x.shape[0]
    f32 = jnp.float32
    lg = (x @ lp['router_group_w']).astype(f32) + lp['router_group_b'].astype(f32)
    pg = jax.nn.softmax(lg, axis=-1)
    c = jnp.argmax(lg, axis=-1)
    pg_c = jnp.take_along_axis(pg, c[:, None], axis=1)
    le = ((x @ lp['router_expert_w']).astype(f32).reshape(N, N_GROUPS, EXPERTS_PER_GROUP)
          + lp['router_expert_b'].astype(f32))
    le_c = jnp.take_along_axis(le, c[:, None, None], axis=1)[:, 0]
    top_v, top_i = lax.top_k(le_c, TOP_K)
    wt = jax.nn.softmax(top_v, axis=-1) * pg_c
    w_e = jnp.sum(jax.nn.one_hot(top_i, EXPERTS_PER_GROUP, dtype=f32) * wt[..., None], axis=1)
    w_full = jax.nn.one_hot(c, N_GROUPS, dtype=f32)[:, :, None] * w_e[:, None, :]
    out = jnp.zeros_like(x)
    for gi in range(N_GROUPS):
        sl = slice(gi * EXPERTS_PER_GROUP, (gi + 1) * EXPERTS_PER_GROUP)
        h = (jax.nn.silu(jnp.einsum('nd,edf->nef', x, lp['expert_w_gate'][sl]))
             * jnp.einsum('nd,edf->nef', x, lp['expert_w_up'][sl]))
        h = h * w_full[:, gi, :, None].astype(h.dtype)
        out = out + jnp.einsum('nef,efd->nd', h, lp['expert_w_down'][sl]).astype(x.dtype)
    return out.reshape(B, T, D)


def trunk_layer(h, mem_k, mem_v, s_h, s_r, buf, lb, lp):
    m, s_h, s_r, buf = parallel_mixers(rmsnorm(h, lp['norm_mix_g']), s_h, s_r, buf, lb, lp)
    h = h + m @ lp['w_out']
    h = h + memory_cross_attention(rmsnorm(h, lp['norm_x_g']), mem_k, mem_v, lp['wq_x'], lp['wo_x'])
    h = h + hier_moe(rmsnorm(h, lp['norm_ffn_g']), lp)
    return h, s_h, s_r, buf


def setup_inputs(seed: int = 0) -> dict:
    key = jax.random.key(seed)
    ks = iter(jax.random.split(key, 40))
    nrm = lambda shape, scale: scale * jax.random.normal(next(ks), shape, jnp.float32)
    unif = lambda shape, lo, hi: jax.random.uniform(next(ks), shape, jnp.float32, lo, hi)
    L = DEPTH
    sD = D_MODEL ** -0.5
    return {
        'x_prompt': nrm((BATCH, SEQ, D_MODEL), 1.0),
        'x_sample': nrm((DEC_BATCH, DEC_SEQ, D_MODEL), 1.0),
        'mem_prompt': nrm((BATCH, N_MEM, D_MODEL), 1.0),
        'cache_mem_k': nrm((L, DEC_BATCH, N_MEM, X_HEADS, X_HEAD_DIM), 1.0),
        'cache_mem_v': nrm((L, DEC_BATCH, N_MEM, X_HEADS, X_HEAD_DIM), 1.0),
        'state_hgrn': nrm((L, DEC_BATCH, HGRN_HEADS, HGRN_DK, HGRN_DV), 0.5),
        'state_rwkv': nrm((L, DEC_BATCH, RWKV_HEADS, RWKV_HEAD, RWKV_HEAD), 0.5),
        'state_rwkv_shift': nrm((L, DEC_BATCH, 1, N_RWKV_IN), 1.0),
        'norm_mix_g': 1.0 + nrm((L, D_MODEL), 0.02),
        'w_in': nrm((L, D_MODEL, N_IN), sD),
        'hgrn_lower_bounds': nrm((L + 1, D_HGRN), 0.1),
        'hgrn_norm_g': 1.0 + nrm((L, HGRN_DV), 0.02),
        'rwkv_mu': unif((L, N_RWKV_IN), 0.0, 1.0),
        'rwkv_w0': unif((L, D_RWKV), -5.0, -1.0),
        'rwkv_w2': nrm((L, DECAY_LORA, D_RWKV), 0.1),
        'rwkv_a0': nrm((L, D_RWKV), 0.1),
        'rwkv_a2': nrm((L, ICL_LORA, D_RWKV), ICL_LORA ** -0.5),
        'rwkv_g2': nrm((L, GATE_LORA, D_RWKV), GATE_LORA ** -0.5),
        'rwkv_k_k': 0.85 + nrm((L, D_RWKV), 0.05),
        'rwkv_k_a': 1.0 + nrm((L, D_RWKV), 0.05),
        'rwkv_r_k': nrm((L, RWKV_HEADS, RWKV_HEAD), 0.1),
        'rwkv_ln_g': 1.0 + nrm((L, D_RWKV), 0.02),
        'rwkv_ln_b': nrm((L, D_RWKV), 0.02),
        'w_out': nrm((L, D_MODEL, D_MODEL), sD),
        'norm_x_g': 1.0 + nrm((L, D_MODEL), 0.02),
        'norm_mem_g': 1.0 + nrm((L, D_MODEL), 0.02),
        'wq_x': nrm((L, D_MODEL, D_MODEL), sD),
        'wk_x': nrm((L, D_MODEL, D_MODEL), sD),
        'wv_x': nrm((L, D_MODEL, D_MODEL), sD),
        'wo_x': nrm((L, D_MODEL, D_MODEL), sD),
        'norm_ffn_g': 1.0 + nrm((L, D_MODEL), 0.02),
        'router_group_w': nrm((L, D_MODEL, N_GROUPS), sD),
        'router_group_b': nrm((L, N_GROUPS), 0.01),
        'router_expert_w': nrm((L, D_MODEL, N_EXPERTS), sD),
        'router_expert_b': nrm((L, N_GROUPS, EXPERTS_PER_GROUP), 0.01),
        'expert_w_gate': nrm((L, N_EXPERTS, D_MODEL, D_EXPERT), sD),
        'expert_w_up': nrm((L, N_EXPERTS, D_MODEL, D_EXPERT), sD),
        'expert_w_down': nrm((L, N_EXPERTS, D_EXPERT, D_MODEL), D_EXPERT ** -0.5),
        'norm_final_g': 1.0 + nrm((D_MODEL,), 0.02),
    }


def reference(x_prompt, x_sample, mem_prompt, cache_mem_k, cache_mem_v, state_hgrn, state_rwkv,
              state_rwkv_shift, norm_mix_g, w_in, hgrn_lower_bounds, hgrn_norm_g, rwkv_mu, rwkv_w0,
              rwkv_w2, rwkv_a0, rwkv_a2, rwkv_g2, rwkv_k_k, rwkv_k_a, rwkv_r_k, rwkv_ln_g, rwkv_ln_b,
              w_out, norm_x_g, norm_mem_g, wq_x, wk_x, wv_x, wo_x, norm_ffn_g, router_group_w,
              router_group_b, router_expert_w, router_expert_b, expert_w_gate, expert_w_up,
              expert_w_down, norm_final_g):
    lb_all = jnp.cumsum(jax.nn.softmax(hgrn_lower_bounds.astype(jnp.float32), axis=0), axis=0)
    Bp = x_prompt.shape[0]
    dt = x_prompt.dtype
    hp, hs = x_prompt, x_sample
    mk_p, mv_p, sh_p, sh_s, sr_p, sr_s, bf_p, bf_s = [], [], [], [], [], [], [], []
    for l in range(DEPTH):
        lp = {
            'norm_mix_g': norm_mix_g[l], 'w_in': w_in[l], 'hgrn_norm_g': hgrn_norm_g[l],
            'rwkv_mu': rwkv_mu[l], 'rwkv_w0': rwkv_w0[l], 'rwkv_w2': rwkv_w2[l], 'rwkv_a0': rwkv_a0[l],
            'rwkv_a2': rwkv_a2[l], 'rwkv_g2': rwkv_g2[l], 'rwkv_k_k': rwkv_k_k[l], 'rwkv_k_a': rwkv_k_a[l],
            'rwkv_r_k': rwkv_r_k[l], 'rwkv_ln_g': rwkv_ln_g[l], 'rwkv_ln_b': rwkv_ln_b[l],
            'w_out': w_out[l], 'norm_x_g': norm_x_g[l], 'wq_x': wq_x[l], 'wo_x': wo_x[l],
            'norm_ffn_g': norm_ffn_g[l], 'router_group_w': router_group_w[l],
            'router_group_b': router_group_b[l], 'router_expert_w': router_expert_w[l],
            'router_expert_b': router_expert_b[l], 'expert_w_gate': expert_w_gate[l],
            'expert_w_up': expert_w_up[l], 'expert_w_down': expert_w_down[l],
        }
        lb = lb_all[l]
        mem_k, mem_v = memory_kv(mem_prompt, norm_mem_g[l], wk_x[l], wv_x[l])
        hp, s_h, s_r, buf = trunk_layer(
            hp, mem_k, mem_v,
            jnp.zeros((Bp, HGRN_HEADS, HGRN_DK, HGRN_DV), dt),
            jnp.zeros((Bp, RWKV_HEADS, RWKV_HEAD, RWKV_HEAD), dt),
            jnp.zeros((Bp, 1, N_RWKV_IN), dt), lb, lp)
        mk_p.append(mem_k)
        mv_p.append(mem_v)
        sh_p.append(s_h)
        sr_p.append(s_r)
        bf_p.append(buf)
        hs, s_h2, s_r2, buf2 = trunk_layer(
            hs, cache_mem_k[l], cache_mem_v[l], state_hgrn[l], state_rwkv[l], state_rwkv_shift[l], lb, lp)
        sh_s.append(s_h2)
        sr_s.append(s_r2)
        bf_s.append(buf2)
    y_prompt = rmsnorm(hp, norm_final_g)
    y_sample = rmsnorm(hs, norm_final_g)
    return (y_prompt, y_sample, jnp.stack(mk_p), jnp.stack(mv_p), jnp.stack(sh_p), jnp.stack(sh_s),
            jnp.stack(sr_p), jnp.stack(sr_s), jnp.stack(bf_p), jnp.stack(bf_s))
```

```python
import functools
import math

import jax
import jax.numpy as jnp
from jax import lax
from jax.experimental import pallas as pl
from jax.experimental.pallas import tpu as pltpu

F32 = jnp.float32
BF16 = jnp.bfloat16

NORM_EPS = 1e-6
RWKV_GN_EPS = 64e-5
LANES = 128
HGRN_HEAD = 128
RWKV_HEAD = 64
X_HEADS = 4
N_GROUPS = 4
EXPERTS_PER_GROUP = 8
N_EXPERTS = N_GROUPS * EXPERTS_PER_GROUP
NEG_BIG = -1e30
VMEM_LIMIT = 52 * 1024 * 1024


def _dot(a, b):
    return jnp.dot(a, b, preferred_element_type=F32)


def _dot_nt(a, b):
    return lax.dot_general(a, b, (((1,), (1,)), ((), ())), preferred_element_type=F32)


def _dot_tn(a, b):
    return lax.dot_general(a, b, (((0,), (0,)), ((), ())), preferred_element_type=F32)


def _split(x):
    hi = x.astype(BF16)
    lo = (x - hi.astype(F32)).astype(BF16)
    return hi, lo


def _sigmoid(x):
    return 1.0 / (1.0 + jnp.exp(-x))


def _silu(x):
    return x * _sigmoid(x)


def _rmsnorm(x, g):
    return x * lax.rsqrt(jnp.mean(x * x, axis=-1, keepdims=True) + NORM_EPS) * g


def _cparams(sem):
    return pltpu.CompilerParams(dimension_semantics=sem, vmem_limit_bytes=VMEM_LIMIT)


def _norm_matmul_kernel(x_ref, g_ref, w_ref, o_ref, u_ref):
    @pl.when(pl.program_id(1) == 0)
    def _():
        u_ref[...] = _rmsnorm(x_ref[...], g_ref[...]).astype(BF16)

    o_ref[...] = _dot(u_ref[...], w_ref[...])


def _norm_matmul(x, g, w, tm, tn):
    n, d = x.shape
    m = w.shape[1]
    return pl.pallas_call(
        _norm_matmul_kernel,
        grid=(n // tm, m // tn),
        in_specs=[
            pl.BlockSpec((tm, d), lambda i, j: (i, 0)),
            pl.BlockSpec((1, d), lambda i, j: (0, 0)),
            pl.BlockSpec((d, tn), lambda i, j: (0, j)),
        ],
        out_specs=pl.BlockSpec((tm, tn), lambda i, j: (i, j)),
        out_shape=jax.ShapeDtypeStruct((n, m), F32),
        scratch_shapes=[pltpu.VMEM((tm, d), BF16)],
        compiler_params=_cparams(("parallel", "arbitrary")),
        name="norm_matmul",
    )(x, g.reshape(1, d), w)


def _hgrn_kernel(q_ref, f_ref, i_ref, g_ref, lbp_ref, ng_ref, s0_ref, o_ref, sout_ref, st_ref,
                 *, chunk, n_chunks, heads):
    t = pl.program_id(1)

    @pl.when(t == 0)
    def _():
        for h in range(heads):
            st_ref[h] = s0_ref[h].T

    lbp = lbp_ref[...]
    e = jnp.exp(lbp - jnp.max(lbp, axis=0, keepdims=True))
    lb = e[0:1] / jnp.sum(e, axis=0, keepdims=True)
    ng = ng_ref[...]

    row = lax.broadcasted_iota(jnp.int32, (chunk, chunk), 0)
    col = lax.broadcasted_iota(jnp.int32, (chunk, chunk), 1)
    causal = row >= col
    tri = causal.astype(BF16)
    mid = chunk // 2

    def body(c, carry):
        r0 = pl.multiple_of(c * chunk, chunk)
        rows = pl.ds(r0, chunk)
        f = lb + (1.0 - lb) * _sigmoid(f_ref[rows, :])
        g = jnp.log(f)
        g_hi, g_lo = _split(g)
        b = _dot(tri, g_hi) + _dot(tri, g_lo)
        b_mid = b[mid - 1:mid, :]
        b_last = b[chunk - 1:chunk, :]
        q = _silu(q_ref[rows, :])
        k = 1.0 - f
        v = i_ref[rows, :].astype(BF16)
        qd = (q * jnp.exp(b - b_mid)).astype(BF16)
        kd = (k * jnp.exp(b_mid - b)).astype(BF16)
        qi = (q * jnp.exp(b)).astype(BF16)
        kl = (k * jnp.exp(b_last - b)).astype(BF16)
        el = jnp.exp(b_last)
        gate = _silu(g_ref[rows, :])
        for h in range(heads):
            sl = slice(h * HGRN_HEAD, (h + 1) * HGRN_HEAD)
            st = st_ref[h]
            sc = jnp.where(causal, _dot_nt(qd[:, sl], kd[:, sl]), 0.0)
            o = _dot(sc.astype(BF16), v[:, sl]) + _dot_nt(qi[:, sl], st.astype(BF16))
            st_ref[h] = st * el[:, sl] + _dot_tn(v[:, sl], kl[:, sl])
            o = o * lax.rsqrt(jnp.mean(o * o, axis=-1, keepdims=True) + NORM_EPS)
            o_ref[rows, sl] = o * ng * gate[:, sl]
        return carry

    lax.fori_loop(0, n_chunks, body, 0)

    @pl.when(t == pl.num_programs(1) - 1)
    def _():
        for h in range(heads):
            sout_ref[h] = st_ref[h].T


def _hgrn(p3, lbp, ng, s0, tile, chunk):
    bsz, tlen, _ = p3.shape
    heads, dk, dv = s0.shape[1:]
    d = heads * HGRN_HEAD
    kern = functools.partial(_hgrn_kernel, chunk=chunk, n_chunks=tile // chunk, heads=heads)
    pspec = lambda cb: pl.BlockSpec((None, tile, d), lambda b, t: (b, t, cb))
    return pl.pallas_call(
        kern,
        grid=(bsz, tlen // tile),
        in_specs=[
            pspec(0), pspec(1), pspec(2), pspec(3),
            pl.BlockSpec(lbp.shape, lambda b, t: (0, 0)),
            pl.BlockSpec((1, dv), lambda b, t: (0, 0)),
            pl.BlockSpec((None, heads, dk, dv), lambda b, t: (b, 0, 0, 0)),
        ],
        out_specs=[
            pl.BlockSpec((None, tile, d), lambda b, t: (b, t, 0)),
            pl.BlockSpec((None, heads, dk, dv), lambda b, t: (b, 0, 0, 0)),
        ],
        out_shape=[
            jax.ShapeDtypeStruct((bsz, tlen, d), F32),
            jax.ShapeDtypeStruct((bsz, heads, dk, dv), F32),
        ],
        scratch_shapes=[pltpu.VMEM((heads, dv, dk), F32)],
        compiler_params=_cparams(("parallel", "arbitrary")),
        name="hgrn",
    )(p3, p3, p3, p3, lbp, ng.reshape(1, dv), s0)


_P_MU_R, _P_MU_K, _P_MU_V, _P_W0, _P_A0, _P_KK, _P_KA, _P_RK, _P_LNG, _P_LNB = range(10)
_P_ROWS = 16


def _rwkv_kernel(r_ref, k_ref, v_ref, l0_ref, l1_ref, l2_ref,
                 br_ref, bk_ref, bv_ref, bl_ref, par_ref, mul_ref,
                 w2_ref, a2_ref, g2a_ref, g2b_ref, s0_ref,
                 o_ref, sout_ref, st_ref, prev_ref, *, chunk, n_chunks):
    t = pl.program_id(2)
    tile = chunk * n_chunks

    @pl.when(t == 0)
    def _():
        st_ref[...] = s0_ref[...]
        prev_ref[0:1, :] = br_ref[...]
        prev_ref[1:2, :] = bk_ref[...]
        prev_ref[2:3, :] = bv_ref[...]
        prev_ref[3:4, :] = bl_ref[:, 0:LANES]
        prev_ref[4:5, :] = bl_ref[:, LANES:2 * LANES]
        prev_ref[5:6, :] = bl_ref[:, 2 * LANES:3 * LANES]

    par = par_ref[...]
    prow = lambda i: par[i:i + 1, :]
    mul = mul_ref[...]
    row_id = lax.broadcasted_iota(jnp.int32, (tile, LANES), 0)

    def shifted(x, slot, mu):
        prev = jnp.where(row_id == 0, prev_ref[slot:slot + 1, :], pltpu.roll(x, 1, 0))
        return x + (prev - x) * mu

    pr, pk, pv = r_ref[...], k_ref[...], v_ref[...]
    pl0, pl1, pl2 = l0_ref[...], l1_ref[...], l2_ref[...]
    r = shifted(pr, 0, prow(_P_MU_R))
    k = shifted(pk, 1, prow(_P_MU_K))
    v = shifted(pv, 2, prow(_P_MU_V))
    x0 = shifted(pl0, 3, mul[0:1, :])
    x1 = shifted(pl1, 4, mul[1:2, :])
    x2 = shifted(pl2, 5, mul[2:3, :])
    for slot, x in enumerate((pr, pk, pv, pl0, pl1, pl2)):
        prev_ref[slot:slot + 1, :] = x[tile - 1:tile, :]

    w_lin = prow(_P_W0) + _dot(jnp.tanh(x0).astype(BF16), w2_ref[...])
    z = -w_lin
    softplus = jnp.maximum(z, 0.0) + jnp.log(1.0 + jnp.exp(-jnp.abs(z)))
    lw = -jnp.exp(-softplus - 0.5)
    a_icl = _sigmoid(prow(_P_A0) + _dot(x0.astype(BF16), a2_ref[...]))
    g_out = _dot(_sigmoid(x1).astype(BF16), g2a_ref[...]) + _dot(_sigmoid(x2).astype(BF16), g2b_ref[...])

    lane = lax.broadcasted_iota(jnp.int32, (1, LANES), 1)
    m0 = (lane < RWKV_HEAD).astype(F32)
    m1 = 1.0 - m0
    head_shift = RWKV_HEAD.bit_length() - 1
    hr = lax.broadcasted_iota(jnp.int32, (LANES, LANES), 0) >> head_shift
    hc = lax.broadcasted_iota(jnp.int32, (LANES, LANES), 1) >> head_shift
    head_ones = (hr == hc).astype(BF16)

    def headsum(x):
        hi, lo = _split(x)
        return _dot(hi, head_ones) + _dot(lo, head_ones)

    kk = k * prow(_P_KK)
    kk = kk / jnp.maximum(jnp.sqrt(headsum(kk * kk)), 1e-12)
    k_mod = k * (1.0 + (a_icl - 1.0) * prow(_P_KA))
    a = -kk
    b = kk * a_icl

    n2 = 2 * chunk
    si = lax.broadcasted_iota(jnp.int32, (n2, n2), 0)
    sj = lax.broadcasted_iota(jnp.int32, (n2, n2), 1)
    chunk_shift = chunk.bit_length() - 1
    same_head = (si >> chunk_shift) == (sj >> chunk_shift)
    strict = same_head & ((sj & (chunk - 1)) < (si & (chunk - 1)))
    incl = same_head & ((sj & (chunk - 1)) <= (si & (chunk - 1)))
    eye = (si == sj).astype(F32)
    ti = lax.broadcasted_iota(jnp.int32, (chunk, chunk), 0)
    tj = lax.broadcasted_iota(jnp.int32, (chunk, chunk), 1)
    tri = (ti >= tj).astype(BF16)
    stack = lambda x: jnp.concatenate([x * m0, x * m1], axis=0)
    n_doubling = int(math.log2(chunk)) - 1

    outs = []
    for c in range(n_chunks):
        rows = slice(c * chunk, (c + 1) * chunk)
        lw_c = lw[rows]
        lw_hi, lw_lo = _split(lw_c)
        cum = _dot(tri, lw_hi) + _dot(tri, lw_lo)
        cum_last = cum[chunk - 1:chunk, :]
        e_neg = jnp.exp(-cum)
        a_t = a[rows] * jnp.exp(cum - lw_c)
        b_t = b[rows] * e_neg
        k_t = k_mod[rows] * e_neg
        r_t = r[rows] * jnp.exp(cum)
        e_end = jnp.exp(cum_last - cum)
        b_end = b[rows] * e_end
        k_end = k_mod[rows] * e_end
        v_st = stack(v[rows]).astype(BF16)

        lhs = jnp.concatenate([stack(a_t), stack(r_t)], axis=0).astype(BF16)
        rhs = jnp.concatenate([stack(b_t), stack(k_t)], axis=0).astype(BF16)
        sc = _dot_nt(lhs, rhs)
        a_ab = jnp.where(strict, sc[:n2, :n2], 0.0)
        a_ak = jnp.where(strict, sc[:n2, n2:], 0.0)
        a_rb = jnp.where(incl, sc[n2:, :n2], 0.0)
        a_rk = jnp.where(incl, sc[n2:, n2:], 0.0)

        inv = eye + a_ab
        pw = a_ab
        for _ in range(n_doubling):
            pw_b = pw.astype(BF16)
            pw = _dot(pw_b, pw_b)
            inv = inv + _dot(inv.astype(BF16), pw.astype(BF16))

        st = st_ref[...]
        sh = _dot_nt(lhs, st.astype(BF16))
        av = _dot(jnp.concatenate([a_ak, a_rk], axis=0).astype(BF16), v_st)
        u_st = _dot(inv.astype(BF16), (sh[:n2] + av[:n2]).astype(BF16)).astype(BF16)
        o_st = sh[n2:] + av[n2:] + _dot(a_rb.astype(BF16), u_st)
        outs.append(o_st[:chunk] + o_st[chunk:])
        upd = _dot_tn(jnp.concatenate([u_st, v_st], axis=0),
                      jnp.concatenate([stack(b_end), stack(k_end)], axis=0).astype(BF16))
        st_ref[...] = st * jnp.exp(cum_last) + upd

    o = outs[0] if n_chunks == 1 else jnp.concatenate(outs, axis=0)

    inv_n = 1.0 / RWKV_HEAD
    mean = headsum(o) * inv_n
    dlt = o - mean
    var = headsum(dlt * dlt) * inv_n
    o = dlt * lax.rsqrt(var + RWKV_GN_EPS) * prow(_P_LNG) + prow(_P_LNB)
    bonus = headsum(r * k_mod * prow(_P_RK))
    o_ref[...] = (o + bonus * v) * g_out

    @pl.when(t == pl.num_programs(2) - 1)
    def _():
        sout_ref[...] = st_ref[...]


def _rwkv(p3, buf_rkv, buf_l, par, mul, w2p, a2p, g2a, g2b, s0bd, d, tile, chunk, lora_block0):
    bsz, tlen, _ = p3.shape
    pairs = d // LANES
    kern = functools.partial(_rwkv_kernel, chunk=chunk, n_chunks=tile // chunk)
    pcol = lambda base: pl.BlockSpec((None, tile, LANES), lambda b, j, t: (b, t, base + j))
    pfix = lambda blk: pl.BlockSpec((None, tile, LANES), lambda b, j, t: (b, t, blk))
    bcol = lambda base: pl.BlockSpec((None, 1, LANES), lambda b, j, t: (b, 0, base + j))
    wcol = pl.BlockSpec((LANES, LANES), lambda b, j, t: (0, j))
    return pl.pallas_call(
        kern,
        grid=(bsz, pairs, tlen // tile),
        in_specs=[
            pcol(4 * pairs), pcol(5 * pairs), pcol(6 * pairs),
            pfix(lora_block0), pfix(lora_block0 + 1), pfix(lora_block0 + 2),
            bcol(0), bcol(pairs), bcol(2 * pairs),
            pl.BlockSpec((None, 1, 3 * LANES), lambda b, j, t: (b, 0, 0)),
            pl.BlockSpec((_P_ROWS, LANES), lambda b, j, t: (0, j)),
            pl.BlockSpec((3, LANES), lambda b, j, t: (0, 0)),
            wcol, wcol, wcol, wcol,
            pl.BlockSpec((None, None, LANES, LANES), lambda b, j, t: (b, j, 0, 0)),
        ],
        out_specs=[
            pl.BlockSpec((None, tile, LANES), lambda b, j, t: (b, t, j)),
            pl.BlockSpec((None, None, LANES, LANES), lambda b, j, t: (b, j, 0, 0)),
        ],
        out_shape=[
            jax.ShapeDtypeStruct((bsz, tlen, d), F32),
            jax.ShapeDtypeStruct((bsz, pairs, LANES, LANES), F32),
        ],
        scratch_shapes=[pltpu.VMEM((LANES, LANES), F32), pltpu.VMEM((8, LANES), F32)],
        compiler_params=_cparams(("parallel", "parallel", "arbitrary")),
        name="rwkv",
    )(p3, p3, p3, p3, p3, p3, buf_rkv, buf_rkv, buf_rkv, buf_l, par, mul, w2p, a2p, g2a, g2b, s0bd)


def _attn_kernel(x_ref, oh_ref, or_ref, ga_ref, gb_ref, mk_ref, mv_ref, wout_ref, nxg_ref, wq_ref, wo_ref,
                 nfg_ref, wrh_ref, wrl_ref, rb_ref, h_ref, u_ref, wf_ref, att_ref, *, bt, tt):
    n = bt * tt
    d = x_ref.shape[-1]
    flat = lambda ref: ref[...].reshape(n, d)
    merged = _sigmoid(flat(ga_ref)) * flat(oh_ref) + _sigmoid(flat(gb_ref)) * flat(or_ref)
    h1 = flat(x_ref) + _dot(merged.astype(BF16), wout_ref[...])

    q = _dot(_rmsnorm(h1, nxg_ref[...]).astype(BF16), wq_ref[...])
    dh = d // X_HEADS
    scale = dh ** -0.5
    for bi in range(bt):
        rows = slice(bi * tt, (bi + 1) * tt)
        for hh in range(X_HEADS):
            cols = slice(hh * dh, (hh + 1) * dh)
            s = _dot_nt(q[rows, cols].astype(BF16), mk_ref[bi, :, cols].astype(BF16)) * scale
            p = jnp.exp(s - jnp.max(s, axis=-1, keepdims=True))
            p = p / jnp.sum(p, axis=-1, keepdims=True)
            att_ref[rows, cols] = _dot(p.astype(BF16), mv_ref[bi, :, cols].astype(BF16)).astype(BF16)
    h2 = h1 + _dot(att_ref[...], wo_ref[...])
    h_ref[...] = h2.reshape(bt, tt, d)

    u = _rmsnorm(h2, nfg_ref[...])
    u_ref[...] = u.astype(BF16).reshape(bt, tt, d)
    u_hi, u_lo = _split(u)
    logits = (_dot(u_hi, wrh_ref[...]) + _dot(u_hi, wrl_ref[...]) + _dot(u_lo, wrh_ref[...])) + rb_ref[...]
    lane = lax.broadcasted_iota(jnp.int32, (n, LANES), 1)
    lane_f = lane.astype(F32)
    is_group = (lane >= N_EXPERTS) & (lane < N_EXPERTS + N_GROUPS)
    lg = jnp.where(is_group, logits, NEG_BIG)
    g_max = jnp.max(lg, axis=-1, keepdims=True)
    g_idx = jnp.min(jnp.where(lg == g_max, lane_f, 1e9), axis=-1, keepdims=True) - N_EXPERTS
    pg = 1.0 / jnp.sum(jnp.exp(lg - g_max), axis=-1, keepdims=True)
    in_group = (lane < N_EXPERTS) & ((lane >> (EXPERTS_PER_GROUP.bit_length() - 1)).astype(F32) == g_idx)
    le = jnp.where(in_group, logits, NEG_BIG)
    m1 = jnp.max(le, axis=-1, keepdims=True)
    i1 = jnp.min(jnp.where(le == m1, lane_f, 1e9), axis=-1, keepdims=True)
    le2 = jnp.where(lane_f == i1, NEG_BIG, le)
    m2 = jnp.max(le2, axis=-1, keepdims=True)
    i2 = jnp.min(jnp.where(le2 == m2, lane_f, 1e9), axis=-1, keepdims=True)
    e2 = jnp.exp(m2 - m1)
    w1 = pg / (1.0 + e2)
    wf = jnp.where(lane_f == i1, w1, 0.0) + jnp.where(lane_f == i2, w1 * e2, 0.0)
    wf_ref[...] = wf.reshape(bt, tt, LANES)


def _attn(x3, oh3, or3, p3, mk, mv, wout, nxg, wq, wo, nfg, wrh, wrl, rb, bt, tt, gate_block0):
    bsz, tlen, d = x3.shape
    nmem = mk.shape[1]
    kern = functools.partial(_attn_kernel, bt=bt, tt=tt)
    tok = lambda cb: pl.BlockSpec((bt, tt, d), lambda b, t: (b, t, cb))
    mem = pl.BlockSpec((bt, nmem, d), lambda b, t: (b, 0, 0))
    full = lambda a: pl.BlockSpec(a.shape, lambda b, t: (0,) * a.ndim)
    nxg, nfg = nxg.reshape(1, d), nfg.reshape(1, d)
    return pl.pallas_call(
        kern,
        grid=(bsz // bt, tlen // tt),
        in_specs=[tok(0), tok(0), tok(0), tok(gate_block0), tok(gate_block0 + 1), mem, mem,
                  full(wout), full(nxg), full(wq), full(wo), full(nfg), full(wrh), full(wrl), full(rb)],
        out_specs=[
            pl.BlockSpec((bt, tt, d), lambda b, t: (b, t, 0)),
            pl.BlockSpec((bt, tt, d), lambda b, t: (b, t, 0)),
            pl.BlockSpec((bt, tt, LANES), lambda b, t: (b, t, 0)),
        ],
        out_shape=[
            jax.ShapeDtypeStruct((bsz, tlen, d), F32),
            jax.ShapeDtypeStruct((bsz, tlen, d), BF16),
            jax.ShapeDtypeStruct((bsz, tlen, LANES), F32),
        ],
        scratch_shapes=[pltpu.VMEM((bt * tt, d), BF16)],
        compiler_params=_cparams(("parallel", "parallel")),
        name="attn",
    )(x3, oh3, or3, p3, p3, mk, mv, wout, nxg, wq, wo, nfg, wrh, wrl, rb)


def _moe_kernel(u_ref, wf_ref, h_ref, wg_ref, wu_ref, wd_ref, gf_ref, y_ref, acc_ref):
    e = pl.program_id(1)

    @pl.when(e == 0)
    def _():
        acc_ref[...] = jnp.zeros_like(acc_ref)

    u = u_ref[...]
    hid = _silu(_dot(u, wg_ref[...])) * _dot(u, wu_ref[...])
    lane = lax.broadcasted_iota(jnp.int32, wf_ref.shape, 1)
    w_col = jnp.sum(jnp.where(lane == e, wf_ref[...], 0.0), axis=-1, keepdims=True)
    acc_ref[...] += _dot((hid * w_col).astype(BF16), wd_ref[...])

    @pl.when(e == pl.num_programs(1) - 1)
    def _():
        y_ref[...] = _rmsnorm(h_ref[...] + acc_ref[...], gf_ref[...])


def _moe(u, wf, h, wg, wu, wd, gf, tm):
    n, d = u.shape
    ne, _, f = wg.shape
    return pl.pallas_call(
        _moe_kernel,
        grid=(n // tm, ne),
        in_specs=[
            pl.BlockSpec((tm, d), lambda i, e: (i, 0)),
            pl.BlockSpec((tm, LANES), lambda i, e: (i, 0)),
            pl.BlockSpec((tm, d), lambda i, e: (i, 0)),
            pl.BlockSpec((None, d, f), lambda i, e: (e, 0, 0)),
            pl.BlockSpec((None, d, f), lambda i, e: (e, 0, 0)),
            pl.BlockSpec((None, f, d), lambda i, e: (e, 0, 0)),
            pl.BlockSpec((1, d), lambda i, e: (0, 0)),
        ],
        out_specs=pl.BlockSpec((tm, d), lambda i, e: (i, 0)),
        out_shape=jax.ShapeDtypeStruct((n, d), F32),
        scratch_shapes=[pltpu.VMEM((tm, d), F32)],
        compiler_params=_cparams(("parallel", "arbitrary")),
        name="moe",
    )(u, wf, h, wg, wu, wd, gf.reshape(1, d))


def _pick(n, pref):
    t = min(n, pref)
    while n % t:
        t -= 8
    return t


def _block_diag_pairs(s):
    b, h, n, _ = s.shape
    s = s.reshape(b, h // 2, 2, n, n)
    z = jnp.zeros_like(s[:, :, 0])
    top = jnp.concatenate([s[:, :, 0], z], axis=-1)
    bot = jnp.concatenate([z, s[:, :, 1]], axis=-1)
    return jnp.concatenate([top, bot], axis=-2)


def _pairs_to_heads(sbd):
    b, p, n2, _ = sbd.shape
    n = n2 // 2
    s = jnp.stack([sbd[:, :, :n, :n], sbd[:, :, n:, n:]], axis=2)
    return s.reshape(b, 2 * p, n, n)


def kernel(x_prompt, x_sample, mem_prompt, cache_mem_k, cache_mem_v, state_hgrn, state_rwkv, state_rwkv_shift, norm_mix_g, w_in, hgrn_lower_bounds, hgrn_norm_g, rwkv_mu, rwkv_w0, rwkv_w2, rwkv_a0, rwkv_a2, rwkv_g2, rwkv_k_k, rwkv_k_a, rwkv_r_k, rwkv_ln_g, rwkv_ln_b, w_out, norm_x_g, norm_mem_g, wq_x, wk_x, wv_x, wo_x, norm_ffn_g, router_group_w, router_group_b, router_expert_w, router_expert_b, expert_w_gate, expert_w_up, expert_w_down, norm_final_g):
    assert w_in.shape[0] == 1, "single-layer configuration"
    d = x_prompt.shape[-1]
    n_lora_w, n_lora_a, n_lora_g = rwkv_w2.shape[1], rwkv_a2.shape[1], rwkv_g2.shape[1]
    assert n_lora_w + n_lora_a == LANES and LANES < n_lora_g <= 2 * LANES
    n_lora = n_lora_w + n_lora_a + n_lora_g
    n_rwkv_in = 3 * d + n_lora
    lora_pad = 3 * LANES - n_lora

    w = w_in[0]
    pr0 = 4 * d
    gate0 = pr0 + n_rwkv_in
    w_perm = jnp.concatenate(
        [w[:, :pr0 + 3 * d], w[:, gate0:gate0 + 2 * d], w[:, pr0 + 3 * d:gate0],
         jnp.zeros((d, lora_pad), w.dtype)], axis=1).astype(BF16)
    n_cols = w_perm.shape[1]
    lora_col0 = 9 * d
    gate_block0 = 7
    mu = rwkv_mu[0]
    mul = jnp.pad(mu[3 * d:], (0, lora_pad)).reshape(3, LANES)
    par = jnp.stack([mu[:d], mu[d:2 * d], mu[2 * d:3 * d], rwkv_w0[0], rwkv_a0[0], rwkv_k_k[0], rwkv_k_a[0],
                     rwkv_r_k[0].reshape(d), rwkv_ln_g[0], rwkv_ln_b[0]])
    par = jnp.pad(par, ((0, _P_ROWS - par.shape[0]), (0, 0)))
    w2p = jnp.pad(rwkv_w2[0], ((0, n_lora_a), (0, 0))).astype(BF16)
    a2p = jnp.pad(rwkv_a2[0], ((n_lora_w, 0), (0, 0))).astype(BF16)
    g2a = rwkv_g2[0][:LANES].astype(BF16)
    g2b = jnp.pad(rwkv_g2[0][LANES:], ((0, 2 * LANES - n_lora_g), (0, 0))).astype(BF16)
    wout_b, wq_b, wo_b = w_out[0].astype(BF16), wq_x[0].astype(BF16), wo_x[0].astype(BF16)
    wr = jnp.concatenate([router_expert_w[0], router_group_w[0],
                          jnp.zeros((d, LANES - N_EXPERTS - N_GROUPS), F32)], axis=1)
    wr_hi, wr_lo = _split(wr)
    rb = jnp.pad(jnp.concatenate([router_expert_b[0].reshape(-1), router_group_b[0]]),
                 (0, LANES - N_EXPERTS - N_GROUPS)).reshape(1, LANES)
    wg_b, wu_b, wd_b = (expert_w_gate[0].astype(BF16), expert_w_up[0].astype(BF16),
                        expert_w_down[0].astype(BF16))

    nb_p, n_mem, _ = mem_prompt.shape
    wkv = jnp.concatenate([wk_x[0], wv_x[0]], axis=1).astype(BF16)
    kv = _norm_matmul(mem_prompt.reshape(nb_p * n_mem, d), norm_mem_g[0], wkv, _pick(nb_p * n_mem, 512), d)
    mem_k_p = kv[:, :d].reshape(nb_p, n_mem, d)
    mem_v_p = kv[:, d:].reshape(nb_p, n_mem, d)

    def group(x3, mem_k, mem_v, s_h, s_r, buf, tile_h, tile_r, chunk_r, bt, tt):
        bsz, tlen, _ = x3.shape
        n = bsz * tlen
        p = _norm_matmul(x3.reshape(n, d), norm_mix_g[0], w_perm, _pick(n, 1024), n_cols // 5)
        p3 = p.reshape(bsz, tlen, n_cols)
        o_h, s_h_new = _hgrn(p3, hgrn_lower_bounds, hgrn_norm_g[0], s_h, tile_h, 32)
        buf_rkv = buf[:, :, :3 * d]
        buf_l = jnp.pad(buf[:, :, 3 * d:], ((0, 0), (0, 0), (0, lora_pad)))
        o_r, s_r_bd = _rwkv(p3, buf_rkv, buf_l, par, mul, w2p, a2p, g2a, g2b, _block_diag_pairs(s_r),
                            d, tile_r, chunk_r, lora_col0 // LANES)
        h2, u, wf = _attn(x3, o_h, o_r, p3, mem_k, mem_v, wout_b, norm_x_g[0], wq_b, wo_b, norm_ffn_g[0],
                          wr_hi, wr_lo, rb, bt, tt, gate_block0)
        y = _moe(u.reshape(n, d), wf.reshape(n, LANES), h2.reshape(n, d), wg_b, wu_b, wd_b, norm_final_g,
                 _pick(n, 1024))
        shift = jnp.concatenate([p3[:, -1:, pr0:pr0 + 3 * d], p3[:, -1:, lora_col0:lora_col0 + n_lora]], axis=-1)
        return y.reshape(bsz, tlen, d), s_h_new, _pairs_to_heads(s_r_bd), shift

    nb_s, t_s, _ = x_sample.shape
    t_p = x_prompt.shape[1]
    zeros_like_state = lambda s: jnp.zeros((nb_p,) + s.shape[2:], s.dtype)
    y_p, sh_p, sr_p, bf_p = group(
        x_prompt, mem_k_p, mem_v_p, zeros_like_state(state_hgrn), zeros_like_state(state_rwkv),
        zeros_like_state(state_rwkv_shift), _pick(t_p, 256), _pick(t_p, 256), 64, 1, _pick(t_p, 256))
    y_s, sh_s, sr_s, bf_s = group(
        x_sample, cache_mem_k[0].reshape(nb_s, n_mem, d), cache_mem_v[0].reshape(nb_s, n_mem, d),
        state_hgrn[0], state_rwkv[0], state_rwkv_shift[0], t_s, t_s, t_s, _pick(nb_s, 4), t_s)

    xh = cache_mem_k.shape[3]
    kv_shape = (1, nb_p, n_mem, xh, d // xh)
    return (y_p, y_s, mem_k_p.reshape(kv_shape), mem_v_p.reshape(kv_shape), sh_p[None], sh_s[None],
            sr_p[None], sr_s[None], bf_p[None], bf_s[None])
```

```python
import functools
import math

import jax
import jax.numpy as jnp
from jax import lax
from jax.experimental import pallas as pl
from jax.experimental.pallas import tpu as pltpu

F32 = jnp.float32
BF16 = jnp.bfloat16

NORM_EPS = 1e-6
RWKV_GN_EPS = 64e-5
LANES = 128
HGRN_HEAD = 128
RWKV_HEAD = 64
X_HEADS = 4
N_GROUPS = 4
EXPERTS_PER_GROUP = 8
N_EXPERTS = N_GROUPS * EXPERTS_PER_GROUP
NEG_BIG = -1e30
VMEM_LIMIT = 52 * 1024 * 1024


def _dot(a, b):
    return jnp.dot(a, b, preferred_element_type=F32)


def _dot_nt(a, b):
    return lax.dot_general(a, b, (((1,), (1,)), ((), ())), preferred_element_type=F32)


def _dot_tn(a, b):
    return lax.dot_general(a, b, (((0,), (0,)), ((), ())), preferred_element_type=F32)


def _split(x):
    hi = x.astype(BF16)
    lo = (x - hi.astype(F32)).astype(BF16)
    return hi, lo


def _sigmoid(x):
    return 1.0 / (1.0 + jnp.exp(-x))


def _silu(x):
    return x * _sigmoid(x)


def _rmsnorm(x, g):
    return x * lax.rsqrt(jnp.mean(x * x, axis=-1, keepdims=True) + NORM_EPS) * g


def _cparams(sem):
    return pltpu.CompilerParams(dimension_semantics=sem, vmem_limit_bytes=VMEM_LIMIT)


def _norm_matmul_kernel(x_ref, g_ref, w_ref, o_ref, u_ref):
    @pl.when(pl.program_id(1) == 0)
    def _():
        u_ref[...] = _rmsnorm(x_ref[...], g_ref[...]).astype(BF16)

    o_ref[...] = _dot(u_ref[...], w_ref[...])


def _norm_matmul(x, g, w, tm, tn):
    n, d = x.shape
    m = w.shape[1]
    return pl.pallas_call(
        _norm_matmul_kernel,
        grid=(n // tm, m // tn),
        in_specs=[
            pl.BlockSpec((tm, d), lambda i, j: (i, 0)),
            pl.BlockSpec((1, d), lambda i, j: (0, 0)),
            pl.BlockSpec((d, tn), lambda i, j: (0, j)),
        ],
        out_specs=pl.BlockSpec((tm, tn), lambda i, j: (i, j)),
        out_shape=jax.ShapeDtypeStruct((n, m), F32),
        scratch_shapes=[pltpu.VMEM((tm, d), BF16)],
        compiler_params=_cparams(("parallel", "arbitrary")),
        name="norm_matmul",
    )(x, g.reshape(1, d), w)


def _hgrn_kernel(q_ref, f_ref, i_ref, g_ref, lbp_ref, ng_ref, s0_ref, o_ref, sout_ref, st_ref,
                 *, chunk, n_chunks, heads):
    t = pl.program_id(1)

    @pl.when(t == 0)
    def _():
        for h in range(heads):
            st_ref[h] = s0_ref[h].T

    lbp = lbp_ref[...]
    e = jnp.exp(lbp - jnp.max(lbp, axis=0, keepdims=True))
    lb = e[0:1] / jnp.sum(e, axis=0, keepdims=True)
    ng = ng_ref[...]

    row = lax.broadcasted_iota(jnp.int32, (chunk, chunk), 0)
    col = lax.broadcasted_iota(jnp.int32, (chunk, chunk), 1)
    causal = row >= col
    tri = causal.astype(BF16)
    mid = chunk // 2

    def body(c, carry):
        r0 = pl.multiple_of(c * chunk, chunk)
        rows = pl.ds(r0, chunk)
        f = lb + (1.0 - lb) * _sigmoid(f_ref[rows, :])
        g = jnp.log(f)
        g_hi, g_lo = _split(g)
        b = _dot(tri, g_hi) + _dot(tri, g_lo)
        b_mid = b[mid - 1:mid, :]
        b_last = b[chunk - 1:chunk, :]
        q = _silu(q_ref[rows, :])
        k = 1.0 - f
        v = i_ref[rows, :].astype(BF16)
        qd = (q * jnp.exp(b - b_mid)).astype(BF16)
        kd = (k * jnp.exp(b_mid - b)).astype(BF16)
        qi = (q * jnp.exp(b)).astype(BF16)
        kl = (k * jnp.exp(b_last - b)).astype(BF16)
        el = jnp.exp(b_last)
        gate = _silu(g_ref[rows, :])
        for h in range(heads):
            sl = slice(h * HGRN_HEAD, (h + 1) * HGRN_HEAD)
            st = st_ref[h]
            sc = jnp.where(causal, _dot_nt(qd[:, sl], kd[:, sl]), 0.0)
            o = _dot(sc.astype(BF16), v[:, sl]) + _dot_nt(qi[:, sl], st.astype(BF16))
            st_ref[h] = st * el[:, sl] + _dot_tn(v[:, sl], kl[:, sl])
            o = o * lax.rsqrt(jnp.mean(o * o, axis=-1, keepdims=True) + NORM_EPS)
            o_ref[rows, sl] = o * ng * gate[:, sl]
        return carry

    lax.fori_loop(0, n_chunks, body, 0)

    @pl.when(t == pl.num_programs(1) - 1)
    def _():
        for h in range(heads):
            sout_ref[h] = st_ref[h].T


def _hgrn(p3, lbp, ng, s0, tile, chunk):
    bsz, tlen, _ = p3.shape
    heads, dk, dv = s0.shape[1:]
    d = heads * HGRN_HEAD
    kern = functools.partial(_hgrn_kernel, chunk=chunk, n_chunks=tile // chunk, heads=heads)
    pspec = lambda cb: pl.BlockSpec((None, tile, d), lambda b, t: (b, t, cb))
    return pl.pallas_call(
        kern,
        grid=(bsz, tlen // tile),
        in_specs=[
            pspec(0), pspec(1), pspec(2), pspec(3),
            pl.BlockSpec(lbp.shape, lambda b, t: (0, 0)),
            pl.BlockSpec((1, dv), lambda b, t: (0, 0)),
            pl.BlockSpec((None, heads, dk, dv), lambda b, t: (b, 0, 0, 0)),
        ],
        out_specs=[
            pl.BlockSpec((None, tile, d), lambda b, t: (b, t, 0)),
            pl.BlockSpec((None, heads, dk, dv), lambda b, t: (b, 0, 0, 0)),
        ],
        out_shape=[
            jax.ShapeDtypeStruct((bsz, tlen, d), F32),
            jax.ShapeDtypeStruct((bsz, heads, dk, dv), F32),
        ],
        scratch_shapes=[pltpu.VMEM((heads, dv, dk), F32)],
        compiler_params=_cparams(("parallel", "arbitrary")),
        name="hgrn",
    )(p3, p3, p3, p3, lbp, ng.reshape(1, dv), s0)


_P_MU_R, _P_MU_K, _P_MU_V, _P_W0, _P_A0, _P_KK, _P_KA, _P_RK, _P_LNG, _P_LNB = range(10)
_P_ROWS = 16


def _rwkv_kernel(r_ref, k_ref, v_ref, l_ref, br_ref, bk_ref, bv_ref, bl_ref, par_ref, mul_ref,
                 w2_ref, a2_ref, g2a_ref, g2b_ref, s0_ref,
                 o_ref, sout_ref, st_ref, prev_ref, prevl_ref, *, chunk, pairs):
    t = pl.program_id(1)

    @pl.when(t == 0)
    def _():
        st_ref[...] = s0_ref[...]
        prev_ref[0:1, :] = br_ref[...]
        prev_ref[1:2, :] = bk_ref[...]
        prev_ref[2:3, :] = bv_ref[...]
        prevl_ref[0:1, :] = bl_ref[...]

    par = par_ref[...]
    prow = lambda i: par[i:i + 1, :]

    def shifted(x, prev_row, mu):
        row_id = lax.broadcasted_iota(jnp.int32, x.shape, 0)
        prev = jnp.where(row_id == 0, prev_row, pltpu.roll(x, 1, 0))
        return x + (prev - x) * mu

    pr, pk, pv, plr = r_ref[...], k_ref[...], v_ref[...], l_ref[...]
    r = shifted(pr, prev_ref[0:1, :], prow(_P_MU_R))
    k = shifted(pk, prev_ref[1:2, :], prow(_P_MU_K))
    v = shifted(pv, prev_ref[2:3, :], prow(_P_MU_V))
    xl = shifted(plr, prevl_ref[0:1, :], mul_ref[...])
    for slot, x in enumerate((pr, pk, pv)):
        prev_ref[slot:slot + 1, :] = x[chunk - 1:chunk, :]
    prevl_ref[0:1, :] = plr[chunk - 1:chunk, :]
    x0 = xl[:, 0:LANES]
    x1 = xl[:, LANES:2 * LANES]
    x2 = xl[:, 2 * LANES:3 * LANES]

    w_lin = prow(_P_W0) + _dot(jnp.tanh(x0).astype(BF16), w2_ref[...])
    z = -w_lin
    softplus = jnp.maximum(z, 0.0) + jnp.log(1.0 + jnp.exp(-jnp.abs(z)))
    lw = -jnp.exp(-softplus - 0.5)
    a_icl = _sigmoid(prow(_P_A0) + _dot(x0.astype(BF16), a2_ref[...]))
    g_out = _dot(_sigmoid(x1).astype(BF16), g2a_ref[...]) + _dot(_sigmoid(x2).astype(BF16), g2b_ref[...])

    lane = lax.broadcasted_iota(jnp.int32, (1, LANES), 1)
    m0 = (lane < RWKV_HEAD).astype(F32)
    m1 = 1.0 - m0
    head_shift = RWKV_HEAD.bit_length() - 1
    hr = lax.broadcasted_iota(jnp.int32, (LANES, LANES), 0) >> head_shift
    hc = lax.broadcasted_iota(jnp.int32, (LANES, LANES), 1) >> head_shift
    head_ones = (hr == hc).astype(BF16)
    tile_of = lambda x, j: x[:, j * LANES:(j + 1) * LANES]

    def headsum(x):
        xs = jnp.concatenate([tile_of(x, j) for j in range(pairs)], axis=0)
        hi, lo = _split(xs)
        ys = _dot(hi, head_ones) + _dot(lo, head_ones)
        return jnp.concatenate([ys[j * chunk:(j + 1) * chunk] for j in range(pairs)], axis=1)

    kk = k * prow(_P_KK)
    kk = kk / jnp.maximum(jnp.sqrt(headsum(kk * kk)), 1e-12)
    k_mod = k * (1.0 + (a_icl - 1.0) * prow(_P_KA))
    a = -kk
    b = kk * a_icl

    ti = lax.broadcasted_iota(jnp.int32, (chunk, chunk), 0)
    tj = lax.broadcasted_iota(jnp.int32, (chunk, chunk), 1)
    tri = (ti >= tj).astype(BF16)
    lw_hi, lw_lo = _split(lw)
    cum = _dot(tri, lw_hi) + _dot(tri, lw_lo)
    cum_last = cum[chunk - 1:chunk, :]
    e_neg = jnp.exp(-cum)
    e_end = jnp.exp(cum_last - cum)
    a_t = a * jnp.exp(cum - lw)
    b_t = b * e_neg
    k_t = k_mod * e_neg
    r_t = r * jnp.exp(cum)
    b_end = b * e_end
    k_end = k_mod * e_end
    g_last = jnp.exp(cum_last)

    n2 = 2 * chunk
    si = lax.broadcasted_iota(jnp.int32, (n2, n2), 0)
    sj = lax.broadcasted_iota(jnp.int32, (n2, n2), 1)
    chunk_shift = chunk.bit_length() - 1
    same_head = (si >> chunk_shift) == (sj >> chunk_shift)
    strict = same_head & ((sj & (chunk - 1)) < (si & (chunk - 1)))
    incl = same_head & ((sj & (chunk - 1)) <= (si & (chunk - 1)))
    eye = (si == sj).astype(F32)
    stack = lambda x: jnp.concatenate([x * m0, x * m1], axis=0)
    stack_b = lambda x, j: stack(tile_of(x, j)).astype(BF16)
    every = range(pairs)

    a_st = [stack_b(a_t, j) for j in every]
    r_st = [stack_b(r_t, j) for j in every]
    v_st = [stack_b(v, j) for j in every]
    bend_st = [stack_b(b_end, j) for j in every]
    kend_st = [stack_b(k_end, j) for j in every]
    sc = [_dot_nt(jnp.concatenate([a_st[j], r_st[j]], axis=0),
                  jnp.concatenate([stack_b(b_t, j), stack_b(k_t, j)], axis=0)) for j in every]
    a_ab = [jnp.where(strict, s[:n2, :n2], 0.0) for s in sc]
    a_xk = [jnp.concatenate([jnp.where(strict, s[:n2, n2:], 0.0), jnp.where(incl, s[n2:, n2:], 0.0)],
                            axis=0).astype(BF16) for s in sc]
    a_rb = [jnp.where(incl, s[n2:, :n2], 0.0).astype(BF16) for s in sc]

    inv = [eye + x for x in a_ab]
    pw = a_ab
    for _ in range(int(math.log2(chunk)) - 1):
        pw = [_dot(x.astype(BF16), x.astype(BF16)) for x in pw]
        inv = [i + _dot(i.astype(BF16), p.astype(BF16)) for i, p in zip(inv, pw)]

    av = [_dot(a_xk[j], v_st[j]) for j in every]
    wu = [_dot(inv[j].astype(BF16),
               jnp.concatenate([a_st[j], av[j][:n2].astype(BF16)], axis=1)).astype(BF16) for j in every]
    qo = [_dot(a_rb[j], wu[j]) + jnp.concatenate([r_st[j].astype(F32), av[j][n2:]], axis=1) for j in every]
    mn = [_dot_tn(wu[j], bend_st[j]) for j in every]
    nn = [mn[j][LANES:] + _dot_tn(v_st[j], kend_st[j]) for j in every]

    outs = []
    for j in every:
        s0 = st_ref[j]
        s0_b = s0.astype(BF16)
        q = (qo[j][:chunk, :LANES] + qo[j][chunk:, :LANES]).astype(BF16)
        outs.append(_dot_nt(q, s0_b) + qo[j][:chunk, LANES:] + qo[j][chunk:, LANES:])
        st_ref[j] = s0 * tile_of(g_last, j) + _dot(s0_b, mn[j][:LANES].astype(BF16)) + nn[j]
    o = jnp.concatenate(outs, axis=1)

    inv_n = 1.0 / RWKV_HEAD
    mean = headsum(o) * inv_n
    dlt = o - mean
    var = headsum(dlt * dlt) * inv_n
    o = dlt * lax.rsqrt(var + RWKV_GN_EPS) * prow(_P_LNG) + prow(_P_LNB)
    bonus = headsum(r * k_mod * prow(_P_RK))
    o_ref[...] = (o + bonus * v) * g_out

    @pl.when(t == pl.num_programs(1) - 1)
    def _():
        sout_ref[...] = st_ref[...]


def _rwkv(p3, buf_rkv, buf_l, par, mul, w2p, a2p, g2a, g2b, s0bd, d, chunk):
    bsz, tlen, _ = p3.shape
    pairs = d // LANES
    nl = 3 * LANES
    kern = functools.partial(_rwkv_kernel, chunk=chunk, pairs=pairs)
    pcol = lambda cb: pl.BlockSpec((None, chunk, d), lambda b, t: (b, t, cb))
    bcol = lambda cb: pl.BlockSpec((None, 1, d), lambda b, t: (b, 0, cb))
    full = lambda a: pl.BlockSpec(a.shape, lambda b, t: (0,) * a.ndim)
    state = pl.BlockSpec((None, pairs, LANES, LANES), lambda b, t: (b, 0, 0, 0))
    return pl.pallas_call(
        kern,
        grid=(bsz, tlen // chunk),
        in_specs=[
            pcol(4), pcol(5), pcol(6),
            pl.BlockSpec((None, chunk, nl), lambda b, t: (b, t, 9 * d // nl)),
            bcol(0), bcol(1), bcol(2),
            pl.BlockSpec((None, 1, nl), lambda b, t: (b, 0, 0)),
            full(par), full(mul), full(w2p), full(a2p), full(g2a), full(g2b), state,
        ],
        out_specs=[pl.BlockSpec((None, chunk, d), lambda b, t: (b, t, 0)), state],
        out_shape=[
            jax.ShapeDtypeStruct((bsz, tlen, d), F32),
            jax.ShapeDtypeStruct((bsz, pairs, LANES, LANES), F32),
        ],
        scratch_shapes=[pltpu.VMEM((pairs, LANES, LANES), F32), pltpu.VMEM((8, d), F32),
                        pltpu.VMEM((8, nl), F32)],
        compiler_params=_cparams(("parallel", "arbitrary")),
        name="rwkv",
    )(p3, p3, p3, p3, buf_rkv, buf_rkv, buf_rkv, buf_l, par, mul, w2p, a2p, g2a, g2b, s0bd)


def _attn_kernel(x_ref, oh_ref, or_ref, ga_ref, gb_ref, mk_ref, mv_ref, wout_ref, nxg_ref, wq_ref, wo_ref,
                 nfg_ref, wrh_ref, wrl_ref, rb_ref, h_ref, u_ref, wf_ref, att_ref, *, bt, tt):
    n = bt * tt
    d = x_ref.shape[-1]
    flat = lambda ref: ref[...].reshape(n, d)
    merged = _sigmoid(flat(ga_ref)) * flat(oh_ref) + _sigmoid(flat(gb_ref)) * flat(or_ref)
    h1 = flat(x_ref) + _dot(merged.astype(BF16), wout_ref[...])

    q = _dot(_rmsnorm(h1, nxg_ref[...]).astype(BF16), wq_ref[...])
    dh = d // X_HEADS
    scale = dh ** -0.5
    for bi in range(bt):
        rows = slice(bi * tt, (bi + 1) * tt)
        for hh in range(X_HEADS):
            cols = slice(hh * dh, (hh + 1) * dh)
            s = _dot_nt(q[rows, cols].astype(BF16), mk_ref[bi, :, cols].astype(BF16)) * scale
            p = jnp.exp(s - jnp.max(s, axis=-1, keepdims=True))
            p = p / jnp.sum(p, axis=-1, keepdims=True)
            att_ref[rows, cols] = _dot(p.astype(BF16), mv_ref[bi, :, cols].astype(BF16)).astype(BF16)
    h2 = h1 + _dot(att_ref[...], wo_ref[...])
    h_ref[...] = h2.reshape(bt, tt, d)

    u = _rmsnorm(h2, nfg_ref[...])
    u_ref[...] = u.astype(BF16).reshape(bt, tt, d)
    u_hi, u_lo = _split(u)
    logits = (_dot(u_hi, wrh_ref[...]) + _dot(u_hi, wrl_ref[...]) + _dot(u_lo, wrh_ref[...])) + rb_ref[...]
    lane = lax.broadcasted_iota(jnp.int32, (n, LANES), 1)
    lane_f = lane.astype(F32)
    is_group = (lane >= N_EXPERTS) & (lane < N_EXPERTS + N_GROUPS)
    lg = jnp.where(is_group, logits, NEG_BIG)
    g_max = jnp.max(lg, axis=-1, keepdims=True)
    g_idx = jnp.min(jnp.where(lg == g_max, lane_f, 1e9), axis=-1, keepdims=True) - N_EXPERTS
    pg = 1.0 / jnp.sum(jnp.exp(lg - g_max), axis=-1, keepdims=True)
    in_group = (lane < N_EXPERTS) & ((lane >> (EXPERTS_PER_GROUP.bit_length() - 1)).astype(F32) == g_idx)
    le = jnp.where(in_group, logits, NEG_BIG)
    m1 = jnp.max(le, axis=-1, keepdims=True)
    i1 = jnp.min(jnp.where(le == m1, lane_f, 1e9), axis=-1, keepdims=True)
    le2 = jnp.where(lane_f == i1, NEG_BIG, le)
    m2 = jnp.max(le2, axis=-1, keepdims=True)
    i2 = jnp.min(jnp.where(le2 == m2, lane_f, 1e9), axis=-1, keepdims=True)
    e2 = jnp.exp(m2 - m1)
    w1 = pg / (1.0 + e2)
    wf = jnp.where(lane_f == i1, w1, 0.0) + jnp.where(lane_f == i2, w1 * e2, 0.0)
    wf_ref[...] = wf.reshape(bt, tt, LANES)


def _attn(x3, oh3, or3, p3, mk, mv, wout, nxg, wq, wo, nfg, wrh, wrl, rb, bt, tt, gate_block0):
    bsz, tlen, d = x3.shape
    nmem = mk.shape[1]
    kern = functools.partial(_attn_kernel, bt=bt, tt=tt)
    tok = lambda cb: pl.BlockSpec((bt, tt, d), lambda b, t: (b, t, cb))
    mem = pl.BlockSpec((bt, nmem, d), lambda b, t: (b, 0, 0))
    full = lambda a: pl.BlockSpec(a.shape, lambda b, t: (0,) * a.ndim)
    nxg, nfg = nxg.reshape(1, d), nfg.reshape(1, d)
    return pl.pallas_call(
        kern,
        grid=(bsz // bt, tlen // tt),
        in_specs=[tok(0), tok(0), tok(0), tok(gate_block0), tok(gate_block0 + 1), mem, mem,
                  full(wout), full(nxg), full(wq), full(wo), full(nfg), full(wrh), full(wrl), full(rb)],
        out_specs=[
            pl.BlockSpec((bt, tt, d), lambda b, t: (b, t, 0)),
            pl.BlockSpec((bt, tt, d), lambda b, t: (b, t, 0)),
            pl.BlockSpec((bt, tt, LANES), lambda b, t: (b, t, 0)),
        ],
        out_shape=[
            jax.ShapeDtypeStruct((bsz, tlen, d), F32),
            jax.ShapeDtypeStruct((bsz, tlen, d), BF16),
            jax.ShapeDtypeStruct((bsz, tlen, LANES), F32),
        ],
        scratch_shapes=[pltpu.VMEM((bt * tt, d), BF16)],
        compiler_params=_cparams(("parallel", "parallel")),
        name="attn",
    )(x3, oh3, or3, p3, p3, mk, mv, wout, nxg, wq, wo, nfg, wrh, wrl, rb)


def _moe_kernel(u_ref, wf_ref, h_ref, wg_ref, wu_ref, wd_ref, gf_ref, y_ref, acc_ref):
    e = pl.program_id(1)

    @pl.when(e == 0)
    def _():
        acc_ref[...] = jnp.zeros_like(acc_ref)

    u = u_ref[...]
    hid = _silu(_dot(u, wg_ref[...])) * _dot(u, wu_ref[...])
    lane = lax.broadcasted_iota(jnp.int32, wf_ref.shape, 1)
    w_col = jnp.sum(jnp.where(lane == e, wf_ref[...], 0.0), axis=-1, keepdims=True)
    acc_ref[...] += _dot((hid * w_col).astype(BF16), wd_ref[...])

    @pl.when(e == pl.num_programs(1) - 1)
    def _():
        y_ref[...] = _rmsnorm(h_ref[...] + acc_ref[...], gf_ref[...])


def _moe(u, wf, h, wg, wu, wd, gf, tm):
    n, d = u.shape
    ne, _, f = wg.shape
    return pl.pallas_call(
        _moe_kernel,
        grid=(n // tm, ne),
        in_specs=[
            pl.BlockSpec((tm, d), lambda i, e: (i, 0)),
            pl.BlockSpec((tm, LANES), lambda i, e: (i, 0)),
            pl.BlockSpec((tm, d), lambda i, e: (i, 0)),
            pl.BlockSpec((None, d, f), lambda i, e: (e, 0, 0)),
            pl.BlockSpec((None, d, f), lambda i, e: (e, 0, 0)),
            pl.BlockSpec((None, f, d), lambda i, e: (e, 0, 0)),
            pl.BlockSpec((1, d), lambda i, e: (0, 0)),
        ],
        out_specs=pl.BlockSpec((tm, d), lambda i, e: (i, 0)),
        out_shape=jax.ShapeDtypeStruct((n, d), F32),
        scratch_shapes=[pltpu.VMEM((tm, d), F32)],
        compiler_params=_cparams(("parallel", "arbitrary")),
        name="moe",
    )(u, wf, h, wg, wu, wd, gf.reshape(1, d))


def _pick(n, pref):
    t = min(n, pref)
    while n % t:
        t -= 8
    return t


def _block_diag_pairs(s):
    b, h, n, _ = s.shape
    s = s.reshape(b, h // 2, 2, n, n)
    z = jnp.zeros_like(s[:, :, 0])
    top = jnp.concatenate([s[:, :, 0], z], axis=-1)
    bot = jnp.concatenate([z, s[:, :, 1]], axis=-1)
    return jnp.concatenate([top, bot], axis=-2)


def _pairs_to_heads(sbd):
    b, p, n2, _ = sbd.shape
    n = n2 // 2
    s = jnp.stack([sbd[:, :, :n, :n], sbd[:, :, n:, n:]], axis=2)
    return s.reshape(b, 2 * p, n, n)


def kernel(x_prompt, x_sample, mem_prompt, cache_mem_k, cache_mem_v, state_hgrn, state_rwkv, state_rwkv_shift, norm_mix_g, w_in, hgrn_lower_bounds, hgrn_norm_g, rwkv_mu, rwkv_w0, rwkv_w2, rwkv_a0, rwkv_a2, rwkv_g2, rwkv_k_k, rwkv_k_a, rwkv_r_k, rwkv_ln_g, rwkv_ln_b, w_out, norm_x_g, norm_mem_g, wq_x, wk_x, wv_x, wo_x, norm_ffn_g, router_group_w, router_group_b, router_expert_w, router_expert_b, expert_w_gate, expert_w_up, expert_w_down, norm_final_g):
    assert w_in.shape[0] == 1, "single-layer configuration"
    d = x_prompt.shape[-1]
    n_lora_w, n_lora_a, n_lora_g = rwkv_w2.shape[1], rwkv_a2.shape[1], rwkv_g2.shape[1]
    assert n_lora_w + n_lora_a == LANES and LANES < n_lora_g <= 2 * LANES
    n_lora = n_lora_w + n_lora_a + n_lora_g
    n_rwkv_in = 3 * d + n_lora
    lora_pad = 3 * LANES - n_lora

    w = w_in[0]
    pr0 = 4 * d
    gate0 = pr0 + n_rwkv_in
    w_perm = jnp.concatenate(
        [w[:, :pr0 + 3 * d], w[:, gate0:gate0 + 2 * d], w[:, pr0 + 3 * d:gate0],
         jnp.zeros((d, lora_pad), w.dtype)], axis=1).astype(BF16)
    n_cols = w_perm.shape[1]
    lora_col0 = 9 * d
    gate_block0 = 7
    mu = rwkv_mu[0]
    mul = jnp.pad(mu[3 * d:], (0, lora_pad)).reshape(1, 3 * LANES)
    par = jnp.stack([mu[:d], mu[d:2 * d], mu[2 * d:3 * d], rwkv_w0[0], rwkv_a0[0], rwkv_k_k[0], rwkv_k_a[0],
                     rwkv_r_k[0].reshape(d), rwkv_ln_g[0], rwkv_ln_b[0]])
    par = jnp.pad(par, ((0, _P_ROWS - par.shape[0]), (0, 0)))
    w2p = jnp.pad(rwkv_w2[0], ((0, n_lora_a), (0, 0))).astype(BF16)
    a2p = jnp.pad(rwkv_a2[0], ((n_lora_w, 0), (0, 0))).astype(BF16)
    g2a = rwkv_g2[0][:LANES].astype(BF16)
    g2b = jnp.pad(rwkv_g2[0][LANES:], ((0, 2 * LANES - n_lora_g), (0, 0))).astype(BF16)
    wout_b, wq_b, wo_b = w_out[0].astype(BF16), wq_x[0].astype(BF16), wo_x[0].astype(BF16)
    wr = jnp.concatenate([router_expert_w[0], router_group_w[0],
                          jnp.zeros((d, LANES - N_EXPERTS - N_GROUPS), F32)], axis=1)
    wr_hi, wr_lo = _split(wr)
    rb = jnp.pad(jnp.concatenate([router_expert_b[0].reshape(-1), router_group_b[0]]),
                 (0, LANES - N_EXPERTS - N_GROUPS)).reshape(1, LANES)
    wg_b, wu_b, wd_b = (expert_w_gate[0].astype(BF16), expert_w_up[0].astype(BF16),
                        expert_w_down[0].astype(BF16))

    nb_p, n_mem, _ = mem_prompt.shape
    wkv = jnp.concatenate([wk_x[0], wv_x[0]], axis=1).astype(BF16)
    kv = _norm_matmul(mem_prompt.reshape(nb_p * n_mem, d), norm_mem_g[0], wkv, _pick(nb_p * n_mem, 512), d)
    mem_k_p = kv[:, :d].reshape(nb_p, n_mem, d)
    mem_v_p = kv[:, d:].reshape(nb_p, n_mem, d)

    def group(x3, mem_k, mem_v, s_h, s_r, buf, tile_h, chunk_r, bt, tt):
        bsz, tlen, _ = x3.shape
        n = bsz * tlen
        p = _norm_matmul(x3.reshape(n, d), norm_mix_g[0], w_perm, _pick(n, 1024), n_cols // 5)
        p3 = p.reshape(bsz, tlen, n_cols)
        o_h, s_h_new = _hgrn(p3, hgrn_lower_bounds, hgrn_norm_g[0], s_h, tile_h, 32)
        buf_rkv = buf[:, :, :3 * d]
        buf_l = jnp.pad(buf[:, :, 3 * d:], ((0, 0), (0, 0), (0, lora_pad)))
        o_r, s_r_bd = _rwkv(p3, buf_rkv, buf_l, par, mul, w2p, a2p, g2a, g2b, _block_diag_pairs(s_r), d, chunk_r)
        h2, u, wf = _attn(x3, o_h, o_r, p3, mem_k, mem_v, wout_b, norm_x_g[0], wq_b, wo_b, norm_ffn_g[0],
                          wr_hi, wr_lo, rb, bt, tt, gate_block0)
        y = _moe(u.reshape(n, d), wf.reshape(n, LANES), h2.reshape(n, d), wg_b, wu_b, wd_b, norm_final_g,
                 _pick(n, 1024))
        shift = jnp.concatenate([p3[:, -1:, pr0:pr0 + 3 * d], p3[:, -1:, lora_col0:lora_col0 + n_lora]], axis=-1)
        return y.reshape(bsz, tlen, d), s_h_new, _pairs_to_heads(s_r_bd), shift

    nb_s, t_s, _ = x_sample.shape
    t_p = x_prompt.shape[1]
    zeros_like_state = lambda s: jnp.zeros((nb_p,) + s.shape[2:], s.dtype)
    y_p, sh_p, sr_p, bf_p = group(
        x_prompt, mem_k_p, mem_v_p, zeros_like_state(state_hgrn), zeros_like_state(state_rwkv),
        zeros_like_state(state_rwkv_shift), _pick(t_p, 256), 64, 1, _pick(t_p, 256))
    y_s, sh_s, sr_s, bf_s = group(
        x_sample, cache_mem_k[0].reshape(nb_s, n_mem, d), cache_mem_v[0].reshape(nb_s, n_mem, d),
        state_hgrn[0], state_rwkv[0], state_rwkv_shift[0], t_s, t_s, _pick(nb_s, 4), t_s)

    xh = cache_mem_k.shape[3]
    kv_shape = (1, nb_p, n_mem, xh, d // xh)
    return (y_p, y_s, mem_k_p.reshape(kv_shape), mem_v_p.reshape(kv_shape), sh_p[None], sh_s[None],
            sr_p[None], sr_s[None], bf_p[None], bf_s[None])
```

```python
import functools
import math

import jax
import jax.numpy as jnp
from jax import lax
from jax.experimental import pallas as pl
from jax.experimental.pallas import tpu as pltpu

F32 = jnp.float32
BF16 = jnp.bfloat16

NORM_EPS = 1e-6
RWKV_GN_EPS = 64e-5
LANES = 128
HGRN_HEAD = 128
RWKV_HEAD = 64
X_HEADS = 4
N_GROUPS = 4
EXPERTS_PER_GROUP = 8
N_EXPERTS = N_GROUPS * EXPERTS_PER_GROUP
NEG_BIG = -1e30
VMEM_LIMIT = 52 * 1024 * 1024


def _dot(a, b):
    return jnp.dot(a, b, preferred_element_type=F32)


def _dot_nt(a, b):
    return lax.dot_general(a, b, (((1,), (1,)), ((), ())), preferred_element_type=F32)


def _dot_tn(a, b):
    return lax.dot_general(a, b, (((0,), (0,)), ((), ())), preferred_element_type=F32)


def _split(x):
    hi = x.astype(BF16)
    lo = (x - hi.astype(F32)).astype(BF16)
    return hi, lo


def _sigmoid(x):
    return 1.0 / (1.0 + jnp.exp(-x))


def _silu(x):
    return x * _sigmoid(x)


def _rmsnorm(x, g):
    return x * lax.rsqrt(jnp.mean(x * x, axis=-1, keepdims=True) + NORM_EPS) * g


def _cparams(sem):
    return pltpu.CompilerParams(dimension_semantics=sem, vmem_limit_bytes=VMEM_LIMIT)


def _norm_matmul_kernel(x_ref, g_ref, w_ref, o_ref, u_ref):
    @pl.when(pl.program_id(1) == 0)
    def _():
        u_ref[...] = _rmsnorm(x_ref[...], g_ref[...]).astype(BF16)

    o_ref[...] = _dot(u_ref[...], w_ref[...])


def _norm_matmul(x, g, w, tm, tn):
    n, d = x.shape
    m = w.shape[1]
    return pl.pallas_call(
        _norm_matmul_kernel,
        grid=(n // tm, m // tn),
        in_specs=[
            pl.BlockSpec((tm, d), lambda i, j: (i, 0)),
            pl.BlockSpec((1, d), lambda i, j: (0, 0)),
            pl.BlockSpec((d, tn), lambda i, j: (0, j)),
        ],
        out_specs=pl.BlockSpec((tm, tn), lambda i, j: (i, j)),
        out_shape=jax.ShapeDtypeStruct((n, m), F32),
        scratch_shapes=[pltpu.VMEM((tm, d), BF16)],
        compiler_params=_cparams(("parallel", "arbitrary")),
        name="norm_matmul",
    )(x, g.reshape(1, d), w)


def _hgrn_kernel(q_ref, f_ref, i_ref, g_ref, lbp_ref, ng_ref, s0_ref, o_ref, sout_ref, st_ref,
                 *, chunk, n_chunks, heads):
    t = pl.program_id(1)

    @pl.when(t == 0)
    def _():
        for h in range(heads):
            st_ref[h] = s0_ref[h].T

    lbp = lbp_ref[...]
    e = jnp.exp(lbp - jnp.max(lbp, axis=0, keepdims=True))
    lb = e[0:1] / jnp.sum(e, axis=0, keepdims=True)
    ng = ng_ref[...]

    row = lax.broadcasted_iota(jnp.int32, (chunk, chunk), 0)
    col = lax.broadcasted_iota(jnp.int32, (chunk, chunk), 1)
    causal = row >= col
    mid = chunk // 2
    to_mid = jnp.where((col >= mid) & (col <= row), 1.0, 0.0) - jnp.where((col > row) & (col < mid), 1.0, 0.0)
    coef = jnp.concatenate([causal.astype(F32), to_mid, (col > row).astype(F32)], axis=0).astype(BF16)

    chunks = range(n_chunks)
    qd, kd, qi, kl, el, vv, gate = [], [], [], [], [], [], []
    for c in chunks:
        rows = slice(c * chunk, (c + 1) * chunk)
        f = lb + (1.0 - lb) * _sigmoid(f_ref[rows, :])
        g_hi, g_lo = _split(jnp.log(f))
        bb = _dot(coef, g_hi) + _dot(coef, g_lo)
        b, b_to_mid, b_to_last = bb[:chunk], bb[chunk:2 * chunk], bb[2 * chunk:]
        q = _silu(q_ref[rows, :])
        k = 1.0 - f
        qd.append((q * jnp.exp(b_to_mid)).astype(BF16))
        kd.append((k * jnp.exp(-b_to_mid)).astype(BF16))
        qi.append((q * jnp.exp(b)).astype(BF16))
        kl.append((k * jnp.exp(b_to_last)).astype(BF16))
        el.append(jnp.exp(b[chunk - 1:chunk, :]))
        vv.append(i_ref[rows, :].astype(BF16))
        gate.append(_silu(g_ref[rows, :]))

    head = lambda x, h: x[:, h * HGRN_HEAD:(h + 1) * HGRN_HEAD]
    items = [(c, h) for c in chunks for h in range(heads)]
    sc = {ch: jnp.where(causal, _dot_nt(head(qd[ch[0]], ch[1]), head(kd[ch[0]], ch[1])), 0.0).astype(BF16)
          for ch in items}
    intra = {ch: _dot(sc[ch], head(vv[ch[0]], ch[1])) for ch in items}
    upd = {ch: _dot_tn(head(vv[ch[0]], ch[1]), head(kl[ch[0]], ch[1])) for ch in items}
    for h in range(heads):
        st = st_ref[h]
        for c in chunks:
            o = intra[c, h] + _dot_nt(head(qi[c], h), st.astype(BF16))
            st = st * head(el[c], h) + upd[c, h]
            o = o * lax.rsqrt(jnp.mean(o * o, axis=-1, keepdims=True) + NORM_EPS)
            o_ref[c * chunk:(c + 1) * chunk, h * HGRN_HEAD:(h + 1) * HGRN_HEAD] = o * ng * head(gate[c], h)
        st_ref[h] = st

    @pl.when(t == pl.num_programs(1) - 1)
    def _():
        for h in range(heads):
            sout_ref[h] = st_ref[h].T


def _hgrn(p3, lbp, ng, s0, tile, chunk):
    bsz, tlen, _ = p3.shape
    heads, dk, dv = s0.shape[1:]
    d = heads * HGRN_HEAD
    kern = functools.partial(_hgrn_kernel, chunk=chunk, n_chunks=tile // chunk, heads=heads)
    pspec = lambda cb: pl.BlockSpec((None, tile, d), lambda b, t: (b, t, cb))
    return pl.pallas_call(
        kern,
        grid=(bsz, tlen // tile),
        in_specs=[
            pspec(0), pspec(1), pspec(2), pspec(3),
            pl.BlockSpec(lbp.shape, lambda b, t: (0, 0)),
            pl.BlockSpec((1, dv), lambda b, t: (0, 0)),
            pl.BlockSpec((None, heads, dk, dv), lambda b, t: (b, 0, 0, 0)),
        ],
        out_specs=[
            pl.BlockSpec((None, tile, d), lambda b, t: (b, t, 0)),
            pl.BlockSpec((None, heads, dk, dv), lambda b, t: (b, 0, 0, 0)),
        ],
        out_shape=[
            jax.ShapeDtypeStruct((bsz, tlen, d), F32),
            jax.ShapeDtypeStruct((bsz, heads, dk, dv), F32),
        ],
        scratch_shapes=[pltpu.VMEM((heads, dv, dk), F32)],
        compiler_params=_cparams(("parallel", "arbitrary")),
        name="hgrn",
    )(p3, p3, p3, p3, lbp, ng.reshape(1, dv), s0)


_P_MU_R, _P_MU_K, _P_MU_V, _P_W0, _P_A0, _P_KK, _P_KA, _P_RK, _P_LNG, _P_LNB = range(10)
_P_ROWS = 16


def _rwkv_kernel(r_ref, k_ref, v_ref, l_ref, br_ref, bk_ref, bv_ref, bl_ref, par_ref, mul_ref,
                 w2_ref, a2_ref, g2a_ref, g2b_ref, s0_ref,
                 o_ref, sout_ref, st_ref, prev_ref, prevl_ref, *, chunk, pairs):
    t = pl.program_id(1)

    @pl.when(t == 0)
    def _():
        st_ref[...] = s0_ref[...]
        prev_ref[0:1, :] = br_ref[...]
        prev_ref[1:2, :] = bk_ref[...]
        prev_ref[2:3, :] = bv_ref[...]
        prevl_ref[0:1, :] = bl_ref[...]

    par = par_ref[...]
    prow = lambda i: par[i:i + 1, :]

    def shifted(x, prev_row, mu):
        row_id = lax.broadcasted_iota(jnp.int32, x.shape, 0)
        prev = jnp.where(row_id == 0, prev_row, pltpu.roll(x, 1, 0))
        return x + (prev - x) * mu

    pr, pk, pv, plr = r_ref[...], k_ref[...], v_ref[...], l_ref[...]
    r = shifted(pr, prev_ref[0:1, :], prow(_P_MU_R))
    k = shifted(pk, prev_ref[1:2, :], prow(_P_MU_K))
    v = shifted(pv, prev_ref[2:3, :], prow(_P_MU_V))
    xl = shifted(plr, prevl_ref[0:1, :], mul_ref[...])
    for slot, x in enumerate((pr, pk, pv)):
        prev_ref[slot:slot + 1, :] = x[chunk - 1:chunk, :]
    prevl_ref[0:1, :] = plr[chunk - 1:chunk, :]
    x0 = xl[:, 0:LANES]
    x1 = xl[:, LANES:2 * LANES]
    x2 = xl[:, 2 * LANES:3 * LANES]

    w_lin = prow(_P_W0) + _dot(jnp.tanh(x0).astype(BF16), w2_ref[...])
    z = -w_lin
    softplus = jnp.maximum(z, 0.0) + jnp.log(1.0 + jnp.exp(-jnp.abs(z)))
    lw = -jnp.exp(-softplus - 0.5)
    a_icl = _sigmoid(prow(_P_A0) + _dot(x0.astype(BF16), a2_ref[...]))
    g_out = _dot(_sigmoid(x1).astype(BF16), g2a_ref[...]) + _dot(_sigmoid(x2).astype(BF16), g2b_ref[...])

    lane = lax.broadcasted_iota(jnp.int32, (1, LANES), 1)
    m0 = (lane < RWKV_HEAD).astype(F32)
    m1 = 1.0 - m0
    head_shift = RWKV_HEAD.bit_length() - 1
    hr = lax.broadcasted_iota(jnp.int32, (LANES, LANES), 0) >> head_shift
    hc = lax.broadcasted_iota(jnp.int32, (LANES, LANES), 1) >> head_shift
    head_ones = (hr == hc).astype(BF16)
    tile_of = lambda x, j: x[:, j * LANES:(j + 1) * LANES]

    def headsum(x):
        xs = jnp.concatenate([tile_of(x, j) for j in range(pairs)], axis=0)
        hi, lo = _split(xs)
        ys = _dot(hi, head_ones) + _dot(lo, head_ones)
        return jnp.concatenate([ys[j * chunk:(j + 1) * chunk] for j in range(pairs)], axis=1)

    kk = k * prow(_P_KK)
    kk = kk / jnp.maximum(jnp.sqrt(headsum(kk * kk)), 1e-12)
    k_mod = k * (1.0 + (a_icl - 1.0) * prow(_P_KA))
    a = -kk
    b = kk * a_icl

    ti = lax.broadcasted_iota(jnp.int32, (chunk, chunk), 0)
    tj = lax.broadcasted_iota(jnp.int32, (chunk, chunk), 1)
    tri = (ti >= tj).astype(BF16)
    lw_hi, lw_lo = _split(lw)
    cum = _dot(tri, lw_hi) + _dot(tri, lw_lo)
    cum_last = cum[chunk - 1:chunk, :]
    e_neg = jnp.exp(-cum)
    e_end = jnp.exp(cum_last - cum)
    a_t = a * jnp.exp(cum - lw)
    b_t = b * e_neg
    k_t = k_mod * e_neg
    r_t = r * jnp.exp(cum)
    b_end = b * e_end
    k_end = k_mod * e_end
    g_last = jnp.exp(cum_last)

    n2 = 2 * chunk
    si = lax.broadcasted_iota(jnp.int32, (n2, n2), 0)
    sj = lax.broadcasted_iota(jnp.int32, (n2, n2), 1)
    chunk_shift = chunk.bit_length() - 1
    same_head = (si >> chunk_shift) == (sj >> chunk_shift)
    strict = same_head & ((sj & (chunk - 1)) < (si & (chunk - 1)))
    incl = same_head & ((sj & (chunk - 1)) <= (si & (chunk - 1)))
    eye = (si == sj).astype(F32)
    stack = lambda x: jnp.concatenate([x * m0, x * m1], axis=0)
    stack_b = lambda x, j: stack(tile_of(x, j)).astype(BF16)
    every = range(pairs)

    a_st = [stack_b(a_t, j) for j in every]
    r_st = [stack_b(r_t, j) for j in every]
    v_st = [stack_b(v, j) for j in every]
    bend_st = [stack_b(b_end, j) for j in every]
    kend_st = [stack_b(k_end, j) for j in every]
    sc = [_dot_nt(jnp.concatenate([a_st[j], r_st[j]], axis=0),
                  jnp.concatenate([stack_b(b_t, j), stack_b(k_t, j)], axis=0)) for j in every]
    a_ab = [jnp.where(strict, s[:n2, :n2], 0.0) for s in sc]
    a_xk = [jnp.concatenate([jnp.where(strict, s[:n2, n2:], 0.0), jnp.where(incl, s[n2:, n2:], 0.0)],
                            axis=0).astype(BF16) for s in sc]
    a_rb = [jnp.where(incl, s[n2:, :n2], 0.0).astype(BF16) for s in sc]

    inv = [eye + x for x in a_ab]
    pw = a_ab
    for _ in range(int(math.log2(chunk)) - 1):
        pw = [_dot(x.astype(BF16), x.astype(BF16)) for x in pw]
        inv = [i + _dot(i.astype(BF16), p.astype(BF16)) for i, p in zip(inv, pw)]

    av = [_dot(a_xk[j], v_st[j]) for j in every]
    wu = [_dot(inv[j].astype(BF16),
               jnp.concatenate([a_st[j], av[j][:n2].astype(BF16)], axis=1)).astype(BF16) for j in every]
    qo = [_dot(a_rb[j], wu[j]) + jnp.concatenate([r_st[j].astype(F32), av[j][n2:]], axis=1) for j in every]
    mn = [_dot_tn(wu[j], bend_st[j]) for j in every]
    nn = [mn[j][LANES:] + _dot_tn(v_st[j], kend_st[j]) for j in every]

    outs = []
    for j in every:
        s0 = st_ref[j]
        s0_b = s0.astype(BF16)
        q = (qo[j][:chunk, :LANES] + qo[j][chunk:, :LANES]).astype(BF16)
        outs.append(_dot_nt(q, s0_b) + qo[j][:chunk, LANES:] + qo[j][chunk:, LANES:])
        st_ref[j] = s0 * tile_of(g_last, j) + _dot(s0_b, mn[j][:LANES].astype(BF16)) + nn[j]
    o = jnp.concatenate(outs, axis=1)

    inv_n = 1.0 / RWKV_HEAD
    mean = headsum(o) * inv_n
    dlt = o - mean
    var = headsum(dlt * dlt) * inv_n
    o = dlt * lax.rsqrt(var + RWKV_GN_EPS) * prow(_P_LNG) + prow(_P_LNB)
    bonus = headsum(r * k_mod * prow(_P_RK))
    o_ref[...] = (o + bonus * v) * g_out

    @pl.when(t == pl.num_programs(1) - 1)
    def _():
        sout_ref[...] = st_ref[...]


def _rwkv(p3, buf_rkv, buf_l, par, mul, w2p, a2p, g2a, g2b, s0bd, d, chunk):
    bsz, tlen, _ = p3.shape
    pairs = d // LANES
    nl = 3 * LANES
    kern = functools.partial(_rwkv_kernel, chunk=chunk, pairs=pairs)
    pcol = lambda cb: pl.BlockSpec((None, chunk, d), lambda b, t: (b, t, cb))
    bcol = lambda cb: pl.BlockSpec((None, 1, d), lambda b, t: (b, 0, cb))
    full = lambda a: pl.BlockSpec(a.shape, lambda b, t: (0,) * a.ndim)
    state = pl.BlockSpec((None, pairs, LANES, LANES), lambda b, t: (b, 0, 0, 0))
    return pl.pallas_call(
        kern,
        grid=(bsz, tlen // chunk),
        in_specs=[
            pcol(4), pcol(5), pcol(6),
            pl.BlockSpec((None, chunk, nl), lambda b, t: (b, t, 9 * d // nl)),
            bcol(0), bcol(1), bcol(2),
            pl.BlockSpec((None, 1, nl), lambda b, t: (b, 0, 0)),
            full(par), full(mul), full(w2p), full(a2p), full(g2a), full(g2b), state,
        ],
        out_specs=[pl.BlockSpec((None, chunk, d), lambda b, t: (b, t, 0)), state],
        out_shape=[
            jax.ShapeDtypeStruct((bsz, tlen, d), F32),
            jax.ShapeDtypeStruct((bsz, pairs, LANES, LANES), F32),
        ],
        scratch_shapes=[pltpu.VMEM((pairs, LANES, LANES), F32), pltpu.VMEM((8, d), F32),
                        pltpu.VMEM((8, nl), F32)],
        compiler_params=_cparams(("parallel", "arbitrary")),
        name="rwkv",
    )(p3, p3, p3, p3, buf_rkv, buf_rkv, buf_rkv, buf_l, par, mul, w2p, a2p, g2a, g2b, s0bd)


def _attn_kernel(x_ref, oh_ref, or_ref, ga_ref, gb_ref, mk_ref, mv_ref, wout_ref, nxg_ref, wq_ref, wo_ref,
                 nfg_ref, wrh_ref, wrl_ref, rb_ref, h_ref, u_ref, wf_ref, att_ref, *, bt, tt):
    n = bt * tt
    d = x_ref.shape[-1]
    flat = lambda ref: ref[...].reshape(n, d)
    merged = _sigmoid(flat(ga_ref)) * flat(oh_ref) + _sigmoid(flat(gb_ref)) * flat(or_ref)
    h1 = flat(x_ref) + _dot(merged.astype(BF16), wout_ref[...])

    q = _dot(_rmsnorm(h1, nxg_ref[...]).astype(BF16), wq_ref[...])
    dh = d // X_HEADS
    scale = dh ** -0.5
    for bi in range(bt):
        rows = slice(bi * tt, (bi + 1) * tt)
        for hh in range(X_HEADS):
            cols = slice(hh * dh, (hh + 1) * dh)
            s = _dot_nt(q[rows, cols].astype(BF16), mk_ref[bi, :, cols].astype(BF16)) * scale
            p = jnp.exp(s - jnp.max(s, axis=-1, keepdims=True))
            p = p / jnp.sum(p, axis=-1, keepdims=True)
            att_ref[rows, cols] = _dot(p.astype(BF16), mv_ref[bi, :, cols].astype(BF16)).astype(BF16)
    h2 = h1 + _dot(att_ref[...], wo_ref[...])
    h_ref[...] = h2.reshape(bt, tt, d)

    u = _rmsnorm(h2, nfg_ref[...])
    u_ref[...] = u.astype(BF16).reshape(bt, tt, d)
    u_hi, u_lo = _split(u)
    logits = (_dot(u_hi, wrh_ref[...]) + _dot(u_hi, wrl_ref[...]) + _dot(u_lo, wrh_ref[...])) + rb_ref[...]
    lane = lax.broadcasted_iota(jnp.int32, (n, LANES), 1)
    lane_f = lane.astype(F32)
    is_group = (lane >= N_EXPERTS) & (lane < N_EXPERTS + N_GROUPS)
    lg = jnp.where(is_group, logits, NEG_BIG)
    g_max = jnp.max(lg, axis=-1, keepdims=True)
    g_idx = jnp.min(jnp.where(lg == g_max, lane_f, 1e9), axis=-1, keepdims=True) - N_EXPERTS
    pg = 1.0 / jnp.sum(jnp.exp(lg - g_max), axis=-1, keepdims=True)
    in_group = (lane < N_EXPERTS) & ((lane >> (EXPERTS_PER_GROUP.bit_length() - 1)).astype(F32) == g_idx)
    le = jnp.where(in_group, logits, NEG_BIG)
    m1 = jnp.max(le, axis=-1, keepdims=True)
    i1 = jnp.min(jnp.where(le == m1, lane_f, 1e9), axis=-1, keepdims=True)
    le2 = jnp.where(lane_f == i1, NEG_BIG, le)
    m2 = jnp.max(le2, axis=-1, keepdims=True)
    i2 = jnp.min(jnp.where(le2 == m2, lane_f, 1e9), axis=-1, keepdims=True)
    e2 = jnp.exp(m2 - m1)
    w1 = pg / (1.0 + e2)
    wf = jnp.where(lane_f == i1, w1, 0.0) + jnp.where(lane_f == i2, w1 * e2, 0.0)
    wf_ref[...] = wf.reshape(bt, tt, LANES)


def _attn(x3, oh3, or3, p3, mk, mv, wout, nxg, wq, wo, nfg, wrh, wrl, rb, bt, tt, gate_block0):
    bsz, tlen, d = x3.shape
    nmem = mk.shape[1]
    kern = functools.partial(_attn_kernel, bt=bt, tt=tt)
    tok = lambda cb: pl.BlockSpec((bt, tt, d), lambda b, t: (b, t, cb))
    mem = pl.BlockSpec((bt, nmem, d), lambda b, t: (b, 0, 0))
    full = lambda a: pl.BlockSpec(a.shape, lambda b, t: (0,) * a.ndim)
    nxg, nfg = nxg.reshape(1, d), nfg.reshape(1, d)
    return pl.pallas_call(
        kern,
        grid=(bsz // bt, tlen // tt),
        in_specs=[tok(0), tok(0), tok(0), tok(gate_block0), tok(gate_block0 + 1), mem, mem,
                  full(wout), full(nxg), full(wq), full(wo), full(nfg), full(wrh), full(wrl), full(rb)],
        out_specs=[
            pl.BlockSpec((bt, tt, d), lambda b, t: (b, t, 0)),
            pl.BlockSpec((bt, tt, d), lambda b, t: (b, t, 0)),
            pl.BlockSpec((bt, tt, LANES), lambda b, t: (b, t, 0)),
        ],
        out_shape=[
            jax.ShapeDtypeStruct((bsz, tlen, d), F32),
            jax.ShapeDtypeStruct((bsz, tlen, d), BF16),
            jax.ShapeDtypeStruct((bsz, tlen, LANES), F32),
        ],
        scratch_shapes=[pltpu.VMEM((bt * tt, d), BF16)],
        compiler_params=_cparams(("parallel", "parallel")),
        name="attn",
    )(x3, oh3, or3, p3, p3, mk, mv, wout, nxg, wq, wo, nfg, wrh, wrl, rb)


def _moe_kernel(u_ref, wf_ref, h_ref, wg_ref, wu_ref, wd_ref, gf_ref, y_ref, acc_ref):
    e = pl.program_id(1)

    @pl.when(e == 0)
    def _():
        acc_ref[...] = jnp.zeros_like(acc_ref)

    u = u_ref[...]
    hid = _silu(_dot(u, wg_ref[...])) * _dot(u, wu_ref[...])
    lane = lax.broadcasted_iota(jnp.int32, wf_ref.shape, 1)
    w_col = jnp.sum(jnp.where(lane == e, wf_ref[...], 0.0), axis=-1, keepdims=True)
    acc_ref[...] += _dot((hid * w_col).astype(BF16), wd_ref[...])

    @pl.when(e == pl.num_programs(1) - 1)
    def _():
        y_ref[...] = _rmsnorm(h_ref[...] + acc_ref[...], gf_ref[...])


def _moe(u, wf, h, wg, wu, wd, gf, tm):
    n, d = u.shape
    ne, _, f = wg.shape
    return pl.pallas_call(
        _moe_kernel,
        grid=(n // tm, ne),
        in_specs=[
            pl.BlockSpec((tm, d), lambda i, e: (i, 0)),
            pl.BlockSpec((tm, LANES), lambda i, e: (i, 0)),
            pl.BlockSpec((tm, d), lambda i, e: (i, 0)),
            pl.BlockSpec((None, d, f), lambda i, e: (e, 0, 0)),
            pl.BlockSpec((None, d, f), lambda i, e: (e, 0, 0)),
            pl.BlockSpec((None, f, d), lambda i, e: (e, 0, 0)),
            pl.BlockSpec((1, d), lambda i, e: (0, 0)),
        ],
        out_specs=pl.BlockSpec((tm, d), lambda i, e: (i, 0)),
        out_shape=jax.ShapeDtypeStruct((n, d), F32),
        scratch_shapes=[pltpu.VMEM((tm, d), F32)],
        compiler_params=_cparams(("parallel", "arbitrary")),
        name="moe",
    )(u, wf, h, wg, wu, wd, gf.reshape(1, d))


def _pick(n, pref):
    t = min(n, pref)
    while n % t:
        t -= 8
    return t


def _block_diag_pairs(s):
    b, h, n, _ = s.shape
    s = s.reshape(b, h // 2, 2, n, n)
    z = jnp.zeros_like(s[:, :, 0])
    top = jnp.concatenate([s[:, :, 0], z], axis=-1)
    bot = jnp.concatenate([z, s[:, :, 1]], axis=-1)
    return jnp.concatenate([top, bot], axis=-2)


def _pairs_to_heads(sbd):
    b, p, n2, _ = sbd.shape
    n = n2 // 2
    s = jnp.stack([sbd[:, :, :n, :n], sbd[:, :, n:, n:]], axis=2)
    return s.reshape(b, 2 * p, n, n)


def kernel(x_prompt, x_sample, mem_prompt, cache_mem_k, cache_mem_v, state_hgrn, state_rwkv, state_rwkv_shift, norm_mix_g, w_in, hgrn_lower_bounds, hgrn_norm_g, rwkv_mu, rwkv_w0, rwkv_w2, rwkv_a0, rwkv_a2, rwkv_g2, rwkv_k_k, rwkv_k_a, rwkv_r_k, rwkv_ln_g, rwkv_ln_b, w_out, norm_x_g, norm_mem_g, wq_x, wk_x, wv_x, wo_x, norm_ffn_g, router_group_w, router_group_b, router_expert_w, router_expert_b, expert_w_gate, expert_w_up, expert_w_down, norm_final_g):
    assert w_in.shape[0] == 1, "single-layer configuration"
    d = x_prompt.shape[-1]
    n_lora_w, n_lora_a, n_lora_g = rwkv_w2.shape[1], rwkv_a2.shape[1], rwkv_g2.shape[1]
    assert n_lora_w + n_lora_a == LANES and LANES < n_lora_g <= 2 * LANES
    n_lora = n_lora_w + n_lora_a + n_lora_g
    n_rwkv_in = 3 * d + n_lora
    lora_pad = 3 * LANES - n_lora

    w = w_in[0]
    pr0 = 4 * d
    gate0 = pr0 + n_rwkv_in
    w_perm = jnp.concatenate(
        [w[:, :pr0 + 3 * d], w[:, gate0:gate0 + 2 * d], w[:, pr0 + 3 * d:gate0],
         jnp.zeros((d, lora_pad), w.dtype)], axis=1).astype(BF16)
    n_cols = w_perm.shape[1]
    lora_col0 = 9 * d
    gate_block0 = 7
    mu = rwkv_mu[0]
    mul = jnp.pad(mu[3 * d:], (0, lora_pad)).reshape(1, 3 * LANES)
    par = jnp.stack([mu[:d], mu[d:2 * d], mu[2 * d:3 * d], rwkv_w0[0], rwkv_a0[0], rwkv_k_k[0], rwkv_k_a[0],
                     rwkv_r_k[0].reshape(d), rwkv_ln_g[0], rwkv_ln_b[0]])
    par = jnp.pad(par, ((0, _P_ROWS - par.shape[0]), (0, 0)))
    w2p = jnp.pad(rwkv_w2[0], ((0, n_lora_a), (0, 0))).astype(BF16)
    a2p = jnp.pad(rwkv_a2[0], ((n_lora_w, 0), (0, 0))).astype(BF16)
    g2a = rwkv_g2[0][:LANES].astype(BF16)
    g2b = jnp.pad(rwkv_g2[0][LANES:], ((0, 2 * LANES - n_lora_g), (0, 0))).astype(BF16)
    wout_b, wq_b, wo_b = w_out[0].astype(BF16), wq_x[0].astype(BF16), wo_x[0].astype(BF16)
    wr = jnp.concatenate([router_expert_w[0], router_group_w[0],
                          jnp.zeros((d, LANES - N_EXPERTS - N_GROUPS), F32)], axis=1)
    wr_hi, wr_lo = _split(wr)
    rb = jnp.pad(jnp.concatenate([router_expert_b[0].reshape(-1), router_group_b[0]]),
                 (0, LANES - N_EXPERTS - N_GROUPS)).reshape(1, LANES)
    wg_b, wu_b, wd_b = (expert_w_gate[0].astype(BF16), expert_w_up[0].astype(BF16),
                        expert_w_down[0].astype(BF16))

    nb_p, n_mem, _ = mem_prompt.shape
    wkv = jnp.concatenate([wk_x[0], wv_x[0]], axis=1).astype(BF16)
    kv = _norm_matmul(mem_prompt.reshape(nb_p * n_mem, d), norm_mem_g[0], wkv, _pick(nb_p * n_mem, 512), d)
    mem_k_p = kv[:, :d].reshape(nb_p, n_mem, d)
    mem_v_p = kv[:, d:].reshape(nb_p, n_mem, d)

    def group(x3, mem_k, mem_v, s_h, s_r, buf, tile_h, chunk_r, bt, tt):
        bsz, tlen, _ = x3.shape
        n = bsz * tlen
        p = _norm_matmul(x3.reshape(n, d), norm_mix_g[0], w_perm, _pick(n, 1024), n_cols // 5)
        p3 = p.reshape(bsz, tlen, n_cols)
        o_h, s_h_new = _hgrn(p3, hgrn_lower_bounds, hgrn_norm_g[0], s_h, tile_h, 32)
        buf_rkv = buf[:, :, :3 * d]
        buf_l = jnp.pad(buf[:, :, 3 * d:], ((0, 0), (0, 0), (0, lora_pad)))
        o_r, s_r_bd = _rwkv(p3, buf_rkv, buf_l, par, mul, w2p, a2p, g2a, g2b, _block_diag_pairs(s_r), d, chunk_r)
        h2, u, wf = _attn(x3, o_h, o_r, p3, mem_k, mem_v, wout_b, norm_x_g[0], wq_b, wo_b, norm_ffn_g[0],
                          wr_hi, wr_lo, rb, bt, tt, gate_block0)
        y = _moe(u.reshape(n, d), wf.reshape(n, LANES), h2.reshape(n, d), wg_b, wu_b, wd_b, norm_final_g,
                 _pick(n, 1024))
        shift = jnp.concatenate([p3[:, -1:, pr0:pr0 + 3 * d], p3[:, -1:, lora_col0:lora_col0 + n_lora]], axis=-1)
        return y.reshape(bsz, tlen, d), s_h_new, _pairs_to_heads(s_r_bd), shift

    nb_s, t_s, _ = x_sample.shape
    t_p = x_prompt.shape[1]
    zeros_like_state = lambda s: jnp.zeros((nb_p,) + s.shape[2:], s.dtype)
    y_p, sh_p, sr_p, bf_p = group(
        x_prompt, mem_k_p, mem_v_p, zeros_like_state(state_hgrn), zeros_like_state(state_rwkv),
        zeros_like_state(state_rwkv_shift), _pick(t_p, 256), 64, 1, _pick(t_p, 256))
    y_s, sh_s, sr_s, bf_s = group(
        x_sample, cache_mem_k[0].reshape(nb_s, n_mem, d), cache_mem_v[0].reshape(nb_s, n_mem, d),
        state_hgrn[0], state_rwkv[0], state_rwkv_shift[0], t_s, t_s, _pick(nb_s, 4), t_s)

    xh = cache_mem_k.shape[3]
    kv_shape = (1, nb_p, n_mem, xh, d // xh)
    return (y_p, y_s, mem_k_p.reshape(kv_shape), mem_v_p.reshape(kv_shape), sh_p[None], sh_s[None],
            sr_p[None], sr_s[None], bf_p[None], bf_s[None])
```

```python
import functools
import math

import jax
import jax.numpy as jnp
from jax import lax
from jax.experimental import pallas as pl
from jax.experimental.pallas import tpu as pltpu

F32 = jnp.float32
BF16 = jnp.bfloat16

NORM_EPS = 1e-6
RWKV_GN_EPS = 64e-5
LANES = 128
HGRN_HEAD = 128
RWKV_HEAD = 64
X_HEADS = 4
N_GROUPS = 4
EXPERTS_PER_GROUP = 8
N_EXPERTS = N_GROUPS * EXPERTS_PER_GROUP
NEG_BIG = -1e30
VMEM_LIMIT = 52 * 1024 * 1024
_GID_LANE = N_EXPERTS
_SEG = 16


def _dot(a, b):
    return jnp.dot(a, b, preferred_element_type=F32)


def _dot_nt(a, b):
    return lax.dot_general(a, b, (((1,), (1,)), ((), ())), preferred_element_type=F32)


def _dot_tn(a, b):
    return lax.dot_general(a, b, (((0,), (0,)), ((), ())), preferred_element_type=F32)


def _split(x):
    hi = x.astype(BF16)
    lo = (x - hi.astype(F32)).astype(BF16)
    return hi, lo


def _sigmoid(x):
    return 1.0 / (1.0 + jnp.exp(-x))


def _silu(x):
    return x * _sigmoid(x)


def _rmsnorm(x, g):
    return x * lax.rsqrt(jnp.mean(x * x, axis=-1, keepdims=True) + NORM_EPS) * g


def _cparams(sem):
    return pltpu.CompilerParams(dimension_semantics=sem, vmem_limit_bytes=VMEM_LIMIT)


def _norm_matmul_kernel(x_ref, g_ref, w_ref, o_ref, u_ref):
    @pl.when(pl.program_id(1) == 0)
    def _():
        u_ref[...] = _rmsnorm(x_ref[...], g_ref[...]).astype(BF16)

    o_ref[...] = _dot(u_ref[...], w_ref[...])


def _norm_matmul(x, g, w, tm, tn):
    n, d = x.shape
    m = w.shape[1]
    return pl.pallas_call(
        _norm_matmul_kernel,
        grid=(n // tm, m // tn),
        in_specs=[
            pl.BlockSpec((tm, d), lambda i, j: (i, 0)),
            pl.BlockSpec((1, d), lambda i, j: (0, 0)),
            pl.BlockSpec((d, tn), lambda i, j: (0, j)),
        ],
        out_specs=pl.BlockSpec((tm, tn), lambda i, j: (i, j)),
        out_shape=jax.ShapeDtypeStruct((n, m), F32),
        scratch_shapes=[pltpu.VMEM((tm, d), BF16)],
        compiler_params=_cparams(("parallel", "arbitrary")),
        name="norm_matmul",
    )(x, g.reshape(1, d), w)


def _hgrn_kernel(q_ref, f_ref, i_ref, g_ref, lbp_ref, ng_ref, s0_ref, o_ref, sout_ref, st_ref,
                 *, chunk, n_chunks, heads):
    t = pl.program_id(1)

    @pl.when(t == 0)
    def _():
        for h in range(heads):
            st_ref[h] = s0_ref[h].T

    lbp = lbp_ref[...]
    e = jnp.exp(lbp - jnp.max(lbp, axis=0, keepdims=True))
    lb = e[0:1] / jnp.sum(e, axis=0, keepdims=True)
    ng = ng_ref[...]

    row = lax.broadcasted_iota(jnp.int32, (chunk, chunk), 0)
    col = lax.broadcasted_iota(jnp.int32, (chunk, chunk), 1)
    causal = row >= col
    mid = chunk // 2
    to_mid = jnp.where((col >= mid) & (col <= row), 1.0, 0.0) - jnp.where((col > row) & (col < mid), 1.0, 0.0)
    coef = jnp.concatenate([causal.astype(F32), to_mid, (col > row).astype(F32)], axis=0).astype(BF16)

    chunks = range(n_chunks)
    qd, kd, qi, kl, el, vv, gate = [], [], [], [], [], [], []
    for c in chunks:
        rows = slice(c * chunk, (c + 1) * chunk)
        f = lb + (1.0 - lb) * _sigmoid(f_ref[rows, :])
        g_hi, g_lo = _split(jnp.log(f))
        bb = _dot(coef, g_hi) + _dot(coef, g_lo)
        b, b_to_mid, b_to_last = bb[:chunk], bb[chunk:2 * chunk], bb[2 * chunk:]
        q = _silu(q_ref[rows, :])
        k = 1.0 - f
        qd.append((q * jnp.exp(b_to_mid)).astype(BF16))
        kd.append((k * jnp.exp(-b_to_mid)).astype(BF16))
        qi.append((q * jnp.exp(b)).astype(BF16))
        kl.append((k * jnp.exp(b_to_last)).astype(BF16))
        el.append(jnp.exp(b[chunk - 1:chunk, :]))
        vv.append(i_ref[rows, :].astype(BF16))
        gate.append(_silu(g_ref[rows, :]))

    head = lambda x, h: x[:, h * HGRN_HEAD:(h + 1) * HGRN_HEAD]
    items = [(c, h) for c in chunks for h in range(heads)]
    sc = {ch: jnp.where(causal, _dot_nt(head(qd[ch[0]], ch[1]), head(kd[ch[0]], ch[1])), 0.0).astype(BF16)
          for ch in items}
    intra = {ch: _dot(sc[ch], head(vv[ch[0]], ch[1])) for ch in items}
    upd = {ch: _dot_tn(head(vv[ch[0]], ch[1]), head(kl[ch[0]], ch[1])) for ch in items}
    for h in range(heads):
        st = st_ref[h]
        for c in chunks:
            o = intra[c, h] + _dot_nt(head(qi[c], h), st.astype(BF16))
            st = st * head(el[c], h) + upd[c, h]
            o = o * lax.rsqrt(jnp.mean(o * o, axis=-1, keepdims=True) + NORM_EPS)
            o_ref[c * chunk:(c + 1) * chunk, h * HGRN_HEAD:(h + 1) * HGRN_HEAD] = o * ng * head(gate[c], h)
        st_ref[h] = st

    @pl.when(t == pl.num_programs(1) - 1)
    def _():
        for h in range(heads):
            sout_ref[h] = st_ref[h].T


def _hgrn(p3, lbp, ng, s0, tile, chunk):
    bsz, tlen, _ = p3.shape
    heads, dk, dv = s0.shape[1:]
    d = heads * HGRN_HEAD
    kern = functools.partial(_hgrn_kernel, chunk=chunk, n_chunks=tile // chunk, heads=heads)
    pspec = lambda cb: pl.BlockSpec((None, tile, d), lambda b, t: (b, t, cb))
    return pl.pallas_call(
        kern,
        grid=(bsz, tlen // tile),
        in_specs=[
            pspec(0), pspec(1), pspec(2), pspec(3),
            pl.BlockSpec(lbp.shape, lambda b, t: (0, 0)),
            pl.BlockSpec((1, dv), lambda b, t: (0, 0)),
            pl.BlockSpec((None, heads, dk, dv), lambda b, t: (b, 0, 0, 0)),
        ],
        out_specs=[
            pl.BlockSpec((None, tile, d), lambda b, t: (b, t, 0)),
            pl.BlockSpec((None, heads, dk, dv), lambda b, t: (b, 0, 0, 0)),
        ],
        out_shape=[
            jax.ShapeDtypeStruct((bsz, tlen, d), F32),
            jax.ShapeDtypeStruct((bsz, heads, dk, dv), F32),
        ],
        scratch_shapes=[pltpu.VMEM((heads, dv, dk), F32)],
        compiler_params=_cparams(("parallel", "arbitrary")),
        name="hgrn",
    )(p3, p3, p3, p3, lbp, ng.reshape(1, dv), s0)


_P_MU_R, _P_MU_K, _P_MU_V, _P_W0, _P_A0, _P_KK, _P_KA, _P_RK, _P_LNG, _P_LNB = range(10)
_P_ROWS = 16


def _rwkv_kernel(r_ref, k_ref, v_ref, l_ref, br_ref, bk_ref, bv_ref, bl_ref, par_ref, mul_ref,
                 w2_ref, a2_ref, g2a_ref, g2b_ref, s0_ref,
                 o_ref, sout_ref, st_ref, prev_ref, prevl_ref, *, chunk, pairs):
    t = pl.program_id(1)

    @pl.when(t == 0)
    def _():
        st_ref[...] = s0_ref[...]
        prev_ref[0:1, :] = br_ref[...]
        prev_ref[1:2, :] = bk_ref[...]
        prev_ref[2:3, :] = bv_ref[...]
        prevl_ref[0:1, :] = bl_ref[...]

    par = par_ref[...]
    prow = lambda i: par[i:i + 1, :]

    def shifted(x, prev_row, mu):
        row_id = lax.broadcasted_iota(jnp.int32, x.shape, 0)
        prev = jnp.where(row_id == 0, prev_row, pltpu.roll(x, 1, 0))
        return x + (prev - x) * mu

    pr, pk, pv, plr = r_ref[...], k_ref[...], v_ref[...], l_ref[...]
    r = shifted(pr, prev_ref[0:1, :], prow(_P_MU_R))
    k = shifted(pk, prev_ref[1:2, :], prow(_P_MU_K))
    v = shifted(pv, prev_ref[2:3, :], prow(_P_MU_V))
    xl = shifted(plr, prevl_ref[0:1, :], mul_ref[...])
    for slot, x in enumerate((pr, pk, pv)):
        prev_ref[slot:slot + 1, :] = x[chunk - 1:chunk, :]
    prevl_ref[0:1, :] = plr[chunk - 1:chunk, :]
    x0 = xl[:, 0:LANES]
    x1 = xl[:, LANES:2 * LANES]
    x2 = xl[:, 2 * LANES:3 * LANES]

    w_lin = prow(_P_W0) + _dot(jnp.tanh(x0).astype(BF16), w2_ref[...])
    z = -w_lin
    softplus = jnp.maximum(z, 0.0) + jnp.log(1.0 + jnp.exp(-jnp.abs(z)))
    lw = -jnp.exp(-softplus - 0.5)
    a_icl = _sigmoid(prow(_P_A0) + _dot(x0.astype(BF16), a2_ref[...]))
    g_out = _dot(_sigmoid(x1).astype(BF16), g2a_ref[...]) + _dot(_sigmoid(x2).astype(BF16), g2b_ref[...])

    lane = lax.broadcasted_iota(jnp.int32, (1, LANES), 1)
    m0 = (lane < RWKV_HEAD).astype(F32)
    m1 = 1.0 - m0
    head_shift = RWKV_HEAD.bit_length() - 1
    hr = lax.broadcasted_iota(jnp.int32, (LANES, LANES), 0) >> head_shift
    hc = lax.broadcasted_iota(jnp.int32, (LANES, LANES), 1) >> head_shift
    head_ones = (hr == hc).astype(BF16)
    tile_of = lambda x, j: x[:, j * LANES:(j + 1) * LANES]

    def headsum(x):
        xs = jnp.concatenate([tile_of(x, j) for j in range(pairs)], axis=0)
        hi, lo = _split(xs)
        ys = _dot(hi, head_ones) + _dot(lo, head_ones)
        return jnp.concatenate([ys[j * chunk:(j + 1) * chunk] for j in range(pairs)], axis=1)

    kk = k * prow(_P_KK)
    kk = kk / jnp.maximum(jnp.sqrt(headsum(kk * kk)), 1e-12)
    k_mod = k * (1.0 + (a_icl - 1.0) * prow(_P_KA))
    a = -kk
    b = kk * a_icl

    ti = lax.broadcasted_iota(jnp.int32, (chunk, chunk), 0)
    tj = lax.broadcasted_iota(jnp.int32, (chunk, chunk), 1)
    tri = (ti >= tj).astype(BF16)
    lw_hi, lw_lo = _split(lw)
    cum = _dot(tri, lw_hi) + _dot(tri, lw_lo)
    cum_last = cum[chunk - 1:chunk, :]
    e_neg = jnp.exp(-cum)
    e_end = jnp.exp(cum_last - cum)
    a_t = a * jnp.exp(cum - lw)
    b_t = b * e_neg
    k_t = k_mod * e_neg
    r_t = r * jnp.exp(cum)
    b_end = b * e_end
    k_end = k_mod * e_end
    g_last = jnp.exp(cum_last)

    n2 = 2 * chunk
    si = lax.broadcasted_iota(jnp.int32, (n2, n2), 0)
    sj = lax.broadcasted_iota(jnp.int32, (n2, n2), 1)
    chunk_shift = chunk.bit_length() - 1
    same_head = (si >> chunk_shift) == (sj >> chunk_shift)
    strict = same_head & ((sj & (chunk - 1)) < (si & (chunk - 1)))
    incl = same_head & ((sj & (chunk - 1)) <= (si & (chunk - 1)))
    eye = (si == sj).astype(F32)
    stack = lambda x: jnp.concatenate([x * m0, x * m1], axis=0)
    stack_b = lambda x, j: stack(tile_of(x, j)).astype(BF16)
    every = range(pairs)

    a_st = [stack_b(a_t, j) for j in every]
    r_st = [stack_b(r_t, j) for j in every]
    v_st = [stack_b(v, j) for j in every]
    bend_st = [stack_b(b_end, j) for j in every]
    kend_st = [stack_b(k_end, j) for j in every]
    sc = [_dot_nt(jnp.concatenate([a_st[j], r_st[j]], axis=0),
                  jnp.concatenate([stack_b(b_t, j), stack_b(k_t, j)], axis=0)) for j in every]
    a_ab = [jnp.where(strict, s[:n2, :n2], 0.0) for s in sc]
    a_xk = [jnp.concatenate([jnp.where(strict, s[:n2, n2:], 0.0), jnp.where(incl, s[n2:, n2:], 0.0)],
                            axis=0).astype(BF16) for s in sc]
    a_rb = [jnp.where(incl, s[n2:, :n2], 0.0).astype(BF16) for s in sc]

    inv = [eye + x for x in a_ab]
    pw = a_ab
    for _ in range(int(math.log2(chunk)) - 1):
        pw = [_dot(x.astype(BF16), x.astype(BF16)) for x in pw]
        inv = [i + _dot(i.astype(BF16), p.astype(BF16)) for i, p in zip(inv, pw)]

    av = [_dot(a_xk[j], v_st[j]) for j in every]
    wu = [_dot(inv[j].astype(BF16),
               jnp.concatenate([a_st[j], av[j][:n2].astype(BF16)], axis=1)).astype(BF16) for j in every]
    qo = [_dot(a_rb[j], wu[j]) + jnp.concatenate([r_st[j].astype(F32), av[j][n2:]], axis=1) for j in every]
    mn = [_dot_tn(wu[j], bend_st[j]) for j in every]
    nn = [mn[j][LANES:] + _dot_tn(v_st[j], kend_st[j]) for j in every]

    outs = []
    for j in every:
        s0 = st_ref[j]
        s0_b = s0.astype(BF16)
        q = (qo[j][:chunk, :LANES] + qo[j][chunk:, :LANES]).astype(BF16)
        outs.append(_dot_nt(q, s0_b) + qo[j][:chunk, LANES:] + qo[j][chunk:, LANES:])
        st_ref[j] = s0 * tile_of(g_last, j) + _dot(s0_b, mn[j][:LANES].astype(BF16)) + nn[j]
    o = jnp.concatenate(outs, axis=1)

    inv_n = 1.0 / RWKV_HEAD
    mean = headsum(o) * inv_n
    dlt = o - mean
    var = headsum(dlt * dlt) * inv_n
    o = dlt * lax.rsqrt(var + RWKV_GN_EPS) * prow(_P_LNG) + prow(_P_LNB)
    bonus = headsum(r * k_mod * prow(_P_RK))
    o_ref[...] = (o + bonus * v) * g_out

    @pl.when(t == pl.num_programs(1) - 1)
    def _():
        sout_ref[...] = st_ref[...]


def _rwkv(p3, buf_rkv, buf_l, par, mul, w2p, a2p, g2a, g2b, s0bd, d, chunk):
    bsz, tlen, _ = p3.shape
    pairs = d // LANES
    nl = 3 * LANES
    kern = functools.partial(_rwkv_kernel, chunk=chunk, pairs=pairs)
    pcol = lambda cb: pl.BlockSpec((None, chunk, d), lambda b, t: (b, t, cb))
    bcol = lambda cb: pl.BlockSpec((None, 1, d), lambda b, t: (b, 0, cb))
    full = lambda a: pl.BlockSpec(a.shape, lambda b, t: (0,) * a.ndim)
    state = pl.BlockSpec((None, pairs, LANES, LANES), lambda b, t: (b, 0, 0, 0))
    return pl.pallas_call(
        kern,
        grid=(bsz, tlen // chunk),
        in_specs=[
            pcol(4), pcol(5), pcol(6),
            pl.BlockSpec((None, chunk, nl), lambda b, t: (b, t, 9 * d // nl)),
            bcol(0), bcol(1), bcol(2),
            pl.BlockSpec((None, 1, nl), lambda b, t: (b, 0, 0)),
            full(par), full(mul), full(w2p), full(a2p), full(g2a), full(g2b), state,
        ],
        out_specs=[pl.BlockSpec((None, chunk, d), lambda b, t: (b, t, 0)), state],
        out_shape=[
            jax.ShapeDtypeStruct((bsz, tlen, d), F32),
            jax.ShapeDtypeStruct((bsz, pairs, LANES, LANES), F32),
        ],
        scratch_shapes=[pltpu.VMEM((pairs, LANES, LANES), F32), pltpu.VMEM((8, d), F32),
                        pltpu.VMEM((8, nl), F32)],
        compiler_params=_cparams(("parallel", "arbitrary")),
        name="rwkv",
    )(p3, p3, p3, p3, buf_rkv, buf_rkv, buf_rkv, buf_l, par, mul, w2p, a2p, g2a, g2b, s0bd)


def _attn_kernel(x_ref, oh_ref, or_ref, ga_ref, gb_ref, mk_ref, mv_ref, wout_ref, nxg_ref, wq_ref, wo_ref,
                 nfg_ref, wrh_ref, wrl_ref, rb_ref, h_ref, u_ref, wf_ref, att_ref, *, bt, tt):
    n = bt * tt
    d = x_ref.shape[-1]
    flat = lambda ref: ref[...].reshape(n, d)
    merged = _sigmoid(flat(ga_ref)) * flat(oh_ref) + _sigmoid(flat(gb_ref)) * flat(or_ref)
    h1 = flat(x_ref) + _dot(merged.astype(BF16), wout_ref[...])

    q = _dot(_rmsnorm(h1, nxg_ref[...]).astype(BF16), wq_ref[...])
    dh = d // X_HEADS
    scale = dh ** -0.5
    for bi in range(bt):
        rows = slice(bi * tt, (bi + 1) * tt)
        for hh in range(X_HEADS):
            cols = slice(hh * dh, (hh + 1) * dh)
            s = _dot_nt(q[rows, cols].astype(BF16), mk_ref[bi, :, cols].astype(BF16)) * scale
            p = jnp.exp(s - jnp.max(s, axis=-1, keepdims=True))
            p = p / jnp.sum(p, axis=-1, keepdims=True)
            att_ref[rows, cols] = _dot(p.astype(BF16), mv_ref[bi, :, cols].astype(BF16)).astype(BF16)
    h2 = h1 + _dot(att_ref[...], wo_ref[...])
    h_ref[...] = h2.reshape(bt, tt, d)

    u = _rmsnorm(h2, nfg_ref[...])
    u_ref[...] = u.astype(BF16).reshape(bt, tt, d)
    u_hi, u_lo = _split(u)
    logits = (_dot(u_hi, wrh_ref[...]) + _dot(u_hi, wrl_ref[...]) + _dot(u_lo, wrh_ref[...])) + rb_ref[...]
    lane = lax.broadcasted_iota(jnp.int32, (n, LANES), 1)
    lane_f = lane.astype(F32)
    is_group = (lane >= N_EXPERTS) & (lane < N_EXPERTS + N_GROUPS)
    lg = jnp.where(is_group, logits, NEG_BIG)
    g_max = jnp.max(lg, axis=-1, keepdims=True)
    g_idx = jnp.min(jnp.where(lg == g_max, lane_f, 1e9), axis=-1, keepdims=True) - N_EXPERTS
    pg = 1.0 / jnp.sum(jnp.exp(lg - g_max), axis=-1, keepdims=True)
    in_group = (lane < N_EXPERTS) & ((lane >> (EXPERTS_PER_GROUP.bit_length() - 1)).astype(F32) == g_idx)
    le = jnp.where(in_group, logits, NEG_BIG)
    m1 = jnp.max(le, axis=-1, keepdims=True)
    i1 = jnp.min(jnp.where(le == m1, lane_f, 1e9), axis=-1, keepdims=True)
    le2 = jnp.where(lane_f == i1, NEG_BIG, le)
    m2 = jnp.max(le2, axis=-1, keepdims=True)
    i2 = jnp.min(jnp.where(le2 == m2, lane_f, 1e9), axis=-1, keepdims=True)
    e2 = jnp.exp(m2 - m1)
    w1 = pg / (1.0 + e2)
    wf = jnp.where(lane_f == i1, w1, 0.0) + jnp.where(lane_f == i2, w1 * e2, 0.0)
    wf = wf + jnp.where(lane == _GID_LANE, g_idx, 0.0)
    wf_ref[...] = wf.reshape(bt, tt, LANES)


def _attn(x3, oh3, or3, p3, mk, mv, wout, nxg, wq, wo, nfg, wrh, wrl, rb, bt, tt, gate_block0):
    bsz, tlen, d = x3.shape
    nmem = mk.shape[1]
    kern = functools.partial(_attn_kernel, bt=bt, tt=tt)
    tok = lambda cb: pl.BlockSpec((bt, tt, d), lambda b, t: (b, t, cb))
    mem = pl.BlockSpec((bt, nmem, d), lambda b, t: (b, 0, 0))
    full = lambda a: pl.BlockSpec(a.shape, lambda b, t: (0,) * a.ndim)
    nxg, nfg = nxg.reshape(1, d), nfg.reshape(1, d)
    return pl.pallas_call(
        kern,
        grid=(bsz // bt, tlen // tt),
        in_specs=[tok(0), tok(0), tok(0), tok(gate_block0), tok(gate_block0 + 1), mem, mem,
                  full(wout), full(nxg), full(wq), full(wo), full(nfg), full(wrh), full(wrl), full(rb)],
        out_specs=[
            pl.BlockSpec((bt, tt, d), lambda b, t: (b, t, 0)),
            pl.BlockSpec((bt, tt, d), lambda b, t: (b, t, 0)),
            pl.BlockSpec((bt, tt, LANES), lambda b, t: (b, t, 0)),
        ],
        out_shape=[
            jax.ShapeDtypeStruct((bsz, tlen, d), F32),
            jax.ShapeDtypeStruct((bsz, tlen, d), BF16),
            jax.ShapeDtypeStruct((bsz, tlen, LANES), F32),
        ],
        scratch_shapes=[pltpu.VMEM((bt * tt, d), BF16)],
        compiler_params=_cparams(("parallel", "parallel")),
        name="attn",
    )(x3, oh3, or3, p3, p3, mk, mv, wout, nxg, wq, wo, nfg, wrh, wrl, rb)


def _segment_dma(src, dst, src_row, dst_row, length, max_len, sem, start):
    size = _SEG
    while size * 2 <= max_len:
        size *= 2
    while size >= _SEG:
        offset = length & (-2 * size)

        @pl.when((length & size) != 0)
        def _(offset=offset, size=size):
            copy = pltpu.make_async_copy(src.at[pl.ds(pl.multiple_of(src_row + offset, _SEG), size)],
                                         dst.at[pl.ds(pl.multiple_of(dst_row + offset, _SEG), size)], sem)
            if start:
                copy.start()
            else:
                copy.wait()

        size //= 2


def _tile_meta(meta_ref, nt):
    i = pl.program_id(0)
    field = lambda k: [meta_ref[(k * nt + i) * N_GROUPS + g] for g in range(N_GROUPS)]
    return field(0), field(1), field(2)


def _moe_sort_kernel(meta_ref, u_ref, wf_ref, xs_in, ws_in, xs_ref, ws_ref, su_ref, sw_ref, sem, *, nt, tm):
    del xs_in, ws_in
    off, cnt, base = _tile_meta(meta_ref, nt)
    slots = su_ref.shape[0]
    wf = wf_ref[...]
    gid = wf.T[_GID_LANE:_GID_LANE + 1, :]
    onehot = lax.broadcasted_iota(jnp.int32, (8, tm), 0).astype(F32) == gid
    earlier = (lax.broadcasted_iota(jnp.int32, (tm, tm), 0)
               < lax.broadcasted_iota(jnp.int32, (tm, tm), 1)).astype(BF16)
    seen = _dot(onehot.astype(BF16), earlier)
    pos = jnp.sum(jnp.where(onehot, seen, 0.0), axis=0, keepdims=True)
    for g in range(N_GROUPS):
        pos = pos + jnp.where(gid == float(g), base[g].astype(F32), 0.0)
    perm = (lax.broadcasted_iota(jnp.int32, (slots, tm), 0).astype(F32) == pos).astype(BF16)
    su_ref[...] = _dot(perm, u_ref[...]).astype(BF16)
    w_hi, w_lo = _split(wf)
    sw_ref[...] = _dot(perm, w_hi) + _dot(perm, w_lo)
    for start in (True, False):
        for g in range(N_GROUPS):
            _segment_dma(su_ref, xs_ref, base[g], off[g], cnt[g], tm, sem.at[0], start)
            _segment_dma(sw_ref, ws_ref, base[g], off[g], cnt[g], tm, sem.at[1], start)


def _moe_expert_kernel(meta_ref, xs_ref, ws_ref, wg_ref, wu_ref, wd_ref, ys_ref, *, n_tiles):
    r = pl.program_id(0)
    used = r < meta_ref[n_tiles]

    @pl.when(used)
    def _():
        first = meta_ref[r] * EXPERTS_PER_GROUP
        x = xs_ref[...]
        ws = ws_ref[...]
        lane = lax.broadcasted_iota(jnp.int32, ws.shape, 1)
        hid = []
        for e in range(EXPERTS_PER_GROUP):
            h = _silu(_dot(x, wg_ref[e])) * _dot(x, wu_ref[e])
            w_col = jnp.sum(jnp.where(lane == first + e, ws, 0.0), axis=-1, keepdims=True)
            hid.append((h * w_col).astype(BF16))
        ys_ref[...] = _dot(jnp.concatenate(hid, axis=1), wd_ref[...])

    @pl.when(jnp.logical_not(used))
    def _():
        ys_ref[...] = jnp.zeros_like(ys_ref)


def _moe_unsort_kernel(meta_ref, wf_ref, h_ref, gf_ref, ys_ref, y_ref, sy_ref, sem, *, nt, tm):
    off, cnt, base = _tile_meta(meta_ref, nt)
    filled = meta_ref[3 * nt * N_GROUPS + pl.program_id(0)]
    slots = sy_ref.shape[0]
    for g in range(N_GROUPS):
        _segment_dma(ys_ref, sy_ref, off[g], base[g], cnt[g], tm, sem.at[0], True)
    gid = wf_ref[...][:, _GID_LANE:_GID_LANE + 1]
    onehot = lax.broadcasted_iota(jnp.int32, (tm, LANES), 1).astype(F32) == gid
    earlier = (lax.broadcasted_iota(jnp.int32, (tm, tm), 1)
               < lax.broadcasted_iota(jnp.int32, (tm, tm), 0)).astype(BF16)
    seen = _dot(earlier, onehot.astype(BF16))
    pos = jnp.sum(jnp.where(onehot, seen, 0.0), axis=1, keepdims=True)
    for g in range(N_GROUPS):
        pos = pos + jnp.where(gid == float(g), base[g].astype(F32), 0.0)
    perm_t = (lax.broadcasted_iota(jnp.int32, (tm, slots), 1).astype(F32) == pos).astype(BF16)
    for g in range(N_GROUPS):
        _segment_dma(ys_ref, sy_ref, off[g], base[g], cnt[g], tm, sem.at[0], False)
    slot_row = lax.broadcasted_iota(jnp.int32, (slots, 1), 0)
    y_hi, y_lo = _split(jnp.where(slot_row < filled, sy_ref[...], 0.0))
    moe = _dot(perm_t, y_hi) + _dot(perm_t, y_lo)
    y_ref[...] = _rmsnorm(h_ref[...] + moe, gf_ref[...])


def _moe(u, wf, h, wg, wu, wd, gf, tm, te):
    n, d = u.shape
    ne, _, f = wg.shape
    nt = n // tm
    slots = tm + LANES
    n_tiles = -(-(n + nt * N_GROUPS * (_SEG - 1) + N_GROUPS * (te - 1)) // te)
    rows = n_tiles * te
    i32 = jnp.int32

    gid = wf[:, _GID_LANE].astype(i32).reshape(nt, tm)
    cnt = jnp.sum((gid[:, :, None] == jnp.arange(N_GROUPS, dtype=i32)).astype(i32), axis=1)
    cnt = (cnt + _SEG - 1) // _SEG * _SEG
    base = jnp.cumsum(cnt, axis=1) - cnt
    region = (jnp.sum(cnt, axis=0) + te - 1) // te * te
    region_end = jnp.cumsum(region)
    off = (region_end - region)[None, :] + jnp.cumsum(cnt, axis=0) - cnt
    meta = jnp.concatenate([off.reshape(-1), cnt.reshape(-1), base.reshape(-1), jnp.sum(cnt, axis=1)]).astype(i32)
    tile_row = jnp.arange(n_tiles, dtype=i32) * te
    tile_group = jnp.minimum(jnp.sum((tile_row[:, None] >= region_end[None, :]).astype(i32), axis=1), N_GROUPS - 1)
    emeta = jnp.concatenate([tile_group, region_end[-1:] // te]).astype(i32)

    anyspec = pl.BlockSpec(memory_space=pl.ANY)
    tok = lambda w: pl.BlockSpec((tm, w), lambda i, m: (i, 0))
    xs, ws = pl.pallas_call(
        functools.partial(_moe_sort_kernel, nt=nt, tm=tm),
        grid_spec=pltpu.PrefetchScalarGridSpec(
            num_scalar_prefetch=1, grid=(nt,),
            in_specs=[tok(d), tok(LANES), anyspec, anyspec],
            out_specs=[anyspec, anyspec],
            scratch_shapes=[pltpu.VMEM((slots, d), BF16), pltpu.VMEM((slots, LANES), F32),
                            pltpu.SemaphoreType.DMA((2,))]),
        out_shape=[jax.ShapeDtypeStruct((rows, d), BF16), jax.ShapeDtypeStruct((rows, LANES), F32)],
        input_output_aliases={3: 0, 4: 1},
        compiler_params=_cparams(("arbitrary",)),
        name="moe_sort",
    )(meta, u, wf, jnp.zeros((rows, d), BF16), jnp.zeros((rows, LANES), F32))

    srt = lambda w: pl.BlockSpec((te, w), lambda r, m: (r, 0))
    grp_w = pl.BlockSpec((EXPERTS_PER_GROUP, d, f), lambda r, m: (m[r], 0, 0))
    ys = pl.pallas_call(
        functools.partial(_moe_expert_kernel, n_tiles=n_tiles),
        grid_spec=pltpu.PrefetchScalarGridSpec(
            num_scalar_prefetch=1, grid=(n_tiles,),
            in_specs=[srt(d), srt(LANES), grp_w, grp_w,
                      pl.BlockSpec((EXPERTS_PER_GROUP * f, d), lambda r, m: (m[r], 0))],
            out_specs=srt(d)),
        out_shape=jax.ShapeDtypeStruct((rows, d), F32),
        compiler_params=_cparams(("parallel",)),
        name="moe_expert",
    )(emeta, xs, ws, wg, wu, wd.reshape(ne * f, d))

    return pl.pallas_call(
        functools.partial(_moe_unsort_kernel, nt=nt, tm=tm),
        grid_spec=pltpu.PrefetchScalarGridSpec(
            num_scalar_prefetch=1, grid=(nt,),
            in_specs=[tok(LANES), tok(d), pl.BlockSpec((1, d), lambda i, m: (0, 0)), anyspec],
            out_specs=tok(d),
            scratch_shapes=[pltpu.VMEM((slots, d), F32), pltpu.SemaphoreType.DMA((1,))]),
        out_shape=jax.ShapeDtypeStruct((n, d), F32),
        compiler_params=_cparams(("arbitrary",)),
        name="moe_unsort",
    )(meta, wf, h, gf.reshape(1, d), ys)


def _pick(n, pref):
    t = min(n, pref)
    while n % t:
        t -= 8
    return t


def _block_diag_pairs(s):
    b, h, n, _ = s.shape
    s = s.reshape(b, h // 2, 2, n, n)
    z = jnp.zeros_like(s[:, :, 0])
    top = jnp.concatenate([s[:, :, 0], z], axis=-1)
    bot = jnp.concatenate([z, s[:, :, 1]], axis=-1)
    return jnp.concatenate([top, bot], axis=-2)


def _pairs_to_heads(sbd):
    b, p, n2, _ = sbd.shape
    n = n2 // 2
    s = jnp.stack([sbd[:, :, :n, :n], sbd[:, :, n:, n:]], axis=2)
    return s.reshape(b, 2 * p, n, n)


def kernel(x_prompt, x_sample, mem_prompt, cache_mem_k, cache_mem_v, state_hgrn, state_rwkv, state_rwkv_shift, norm_mix_g, w_in, hgrn_lower_bounds, hgrn_norm_g, rwkv_mu, rwkv_w0, rwkv_w2, rwkv_a0, rwkv_a2, rwkv_g2, rwkv_k_k, rwkv_k_a, rwkv_r_k, rwkv_ln_g, rwkv_ln_b, w_out, norm_x_g, norm_mem_g, wq_x, wk_x, wv_x, wo_x, norm_ffn_g, router_group_w, router_group_b, router_expert_w, router_expert_b, expert_w_gate, expert_w_up, expert_w_down, norm_final_g):
    assert w_in.shape[0] == 1, "single-layer configuration"
    d = x_prompt.shape[-1]
    n_lora_w, n_lora_a, n_lora_g = rwkv_w2.shape[1], rwkv_a2.shape[1], rwkv_g2.shape[1]
    assert n_lora_w + n_lora_a == LANES and LANES < n_lora_g <= 2 * LANES
    n_lora = n_lora_w + n_lora_a + n_lora_g
    n_rwkv_in = 3 * d + n_lora
    lora_pad = 3 * LANES - n_lora

    w = w_in[0]
    pr0 = 4 * d
    gate0 = pr0 + n_rwkv_in
    w_perm = jnp.concatenate(
        [w[:, :pr0 + 3 * d], w[:, gate0:gate0 + 2 * d], w[:, pr0 + 3 * d:gate0],
         jnp.zeros((d, lora_pad), w.dtype)], axis=1).astype(BF16)
    n_cols = w_perm.shape[1]
    lora_col0 = 9 * d
    gate_block0 = 7
    mu = rwkv_mu[0]
    mul = jnp.pad(mu[3 * d:], (0, lora_pad)).reshape(1, 3 * LANES)
    par = jnp.stack([mu[:d], mu[d:2 * d], mu[2 * d:3 * d], rwkv_w0[0], rwkv_a0[0], rwkv_k_k[0], rwkv_k_a[0],
                     rwkv_r_k[0].reshape(d), rwkv_ln_g[0], rwkv_ln_b[0]])
    par = jnp.pad(par, ((0, _P_ROWS - par.shape[0]), (0, 0)))
    w2p = jnp.pad(rwkv_w2[0], ((0, n_lora_a), (0, 0))).astype(BF16)
    a2p = jnp.pad(rwkv_a2[0], ((n_lora_w, 0), (0, 0))).astype(BF16)
    g2a = rwkv_g2[0][:LANES].astype(BF16)
    g2b = jnp.pad(rwkv_g2[0][LANES:], ((0, 2 * LANES - n_lora_g), (0, 0))).astype(BF16)
    wout_b, wq_b, wo_b = w_out[0].astype(BF16), wq_x[0].astype(BF16), wo_x[0].astype(BF16)
    wr = jnp.concatenate([router_expert_w[0], router_group_w[0],
                          jnp.zeros((d, LANES - N_EXPERTS - N_GROUPS), F32)], axis=1)
    wr_hi, wr_lo = _split(wr)
    rb = jnp.pad(jnp.concatenate([router_expert_b[0].reshape(-1), router_group_b[0]]),
                 (0, LANES - N_EXPERTS - N_GROUPS)).reshape(1, LANES)
    wg_b, wu_b, wd_b = (expert_w_gate[0].astype(BF16), expert_w_up[0].astype(BF16),
                        expert_w_down[0].astype(BF16))

    nb_p, n_mem, _ = mem_prompt.shape
    wkv = jnp.concatenate([wk_x[0], wv_x[0]], axis=1).astype(BF16)
    kv = _norm_matmul(mem_prompt.reshape(nb_p * n_mem, d), norm_mem_g[0], wkv, _pick(nb_p * n_mem, 512), d)
    mem_k_p = kv[:, :d].reshape(nb_p, n_mem, d)
    mem_v_p = kv[:, d:].reshape(nb_p, n_mem, d)

    def group(x3, mem_k, mem_v, s_h, s_r, buf, tile_h, chunk_r, bt, tt, te):
        bsz, tlen, _ = x3.shape
        n = bsz * tlen
        p = _norm_matmul(x3.reshape(n, d), norm_mix_g[0], w_perm, _pick(n, 1024), n_cols // 5)
        p3 = p.reshape(bsz, tlen, n_cols)
        o_h, s_h_new = _hgrn(p3, hgrn_lower_bounds, hgrn_norm_g[0], s_h, tile_h, 32)
        buf_rkv = buf[:, :, :3 * d]
        buf_l = jnp.pad(buf[:, :, 3 * d:], ((0, 0), (0, 0), (0, lora_pad)))
        o_r, s_r_bd = _rwkv(p3, buf_rkv, buf_l, par, mul, w2p, a2p, g2a, g2b, _block_diag_pairs(s_r), d, chunk_r)
        h2, u, wf = _attn(x3, o_h, o_r, p3, mem_k, mem_v, wout_b, norm_x_g[0], wq_b, wo_b, norm_ffn_g[0],
                          wr_hi, wr_lo, rb, bt, tt, gate_block0)
        y = _moe(u.reshape(n, d), wf.reshape(n, LANES), h2.reshape(n, d), wg_b, wu_b, wd_b, norm_final_g,
                 _pick(n, 512), te)
        shift = jnp.concatenate([p3[:, -1:, pr0:pr0 + 3 * d], p3[:, -1:, lora_col0:lora_col0 + n_lora]], axis=-1)
        return y.reshape(bsz, tlen, d), s_h_new, _pairs_to_heads(s_r_bd), shift

    nb_s, t_s, _ = x_sample.shape
    t_p = x_prompt.shape[1]
    zeros_like_state = lambda s: jnp.zeros((nb_p,) + s.shape[2:], s.dtype)
    y_p, sh_p, sr_p, bf_p = group(
        x_prompt, mem_k_p, mem_v_p, zeros_like_state(state_hgrn), zeros_like_state(state_rwkv),
        zeros_like_state(state_rwkv_shift), _pick(t_p, 256), 64, 1, _pick(t_p, 256), 512)
    y_s, sh_s, sr_s, bf_s = group(
        x_sample, cache_mem_k[0].reshape(nb_s, n_mem, d), cache_mem_v[0].reshape(nb_s, n_mem, d),
        state_hgrn[0], state_rwkv[0], state_rwkv_shift[0], t_s, t_s, _pick(nb_s, 4), t_s, 256)

    xh = cache_mem_k.shape[3]
    kv_shape = (1, nb_p, n_mem, xh, d // xh)
    return (y_p, y_s, mem_k_p.reshape(kv_shape), mem_v_p.reshape(kv_shape), sh_p[None], sh_s[None],
            sr_p[None], sr_s[None], bf_p[None], bf_s[None])
```

```python
import functools
import math

import jax
import jax.numpy as jnp
from jax import lax
from jax.experimental import pallas as pl
from jax.experimental.pallas import tpu as pltpu

F32 = jnp.float32
BF16 = jnp.bfloat16

NORM_EPS = 1e-6
RWKV_GN_EPS = 64e-5
LANES = 128
HGRN_HEAD = 128
RWKV_HEAD = 64
X_HEADS = 4
N_GROUPS = 4
EXPERTS_PER_GROUP = 8
N_EXPERTS = N_GROUPS * EXPERTS_PER_GROUP
NEG_BIG = -1e30
VMEM_LIMIT = 52 * 1024 * 1024
_GID_LANE = N_EXPERTS
_SEG = 16


def _dot(a, b):
    return jnp.dot(a, b, preferred_element_type=F32)


def _dot_nt(a, b):
    return lax.dot_general(a, b, (((1,), (1,)), ((), ())), preferred_element_type=F32)


def _dot_tn(a, b):
    return lax.dot_general(a, b, (((0,), (0,)), ((), ())), preferred_element_type=F32)


def _split(x):
    hi = x.astype(BF16)
    lo = (x - hi.astype(F32)).astype(BF16)
    return hi, lo


def _sigmoid(x):
    return 1.0 / (1.0 + jnp.exp(-x))


def _silu(x):
    return x * _sigmoid(x)


def _rmsnorm(x, g):
    return x * lax.rsqrt(jnp.mean(x * x, axis=-1, keepdims=True) + NORM_EPS) * g


def _cparams(sem):
    return pltpu.CompilerParams(dimension_semantics=sem, vmem_limit_bytes=VMEM_LIMIT)


def _norm_matmul_kernel(x_ref, g_ref, w_ref, o_ref, u_ref):
    @pl.when(pl.program_id(1) == 0)
    def _():
        u_ref[...] = _rmsnorm(x_ref[...], g_ref[...]).astype(BF16)

    o_ref[...] = _dot(u_ref[...], w_ref[...])


def _norm_matmul(x, g, w, tm, tn):
    n, d = x.shape
    m = w.shape[1]
    return pl.pallas_call(
        _norm_matmul_kernel,
        grid=(n // tm, m // tn),
        in_specs=[
            pl.BlockSpec((tm, d), lambda i, j: (i, 0)),
            pl.BlockSpec((1, d), lambda i, j: (0, 0)),
            pl.BlockSpec((d, tn), lambda i, j: (0, j)),
        ],
        out_specs=pl.BlockSpec((tm, tn), lambda i, j: (i, j)),
        out_shape=jax.ShapeDtypeStruct((n, m), F32),
        scratch_shapes=[pltpu.VMEM((tm, d), BF16)],
        compiler_params=_cparams(("parallel", "arbitrary")),
        name="norm_matmul",
    )(x, g.reshape(1, d), w)


def _mem_kv_kernel(m_ref, g_ref, wk_ref, wv_ref, k_ref, v_ref):
    u = _rmsnorm(m_ref[...], g_ref[...]).astype(BF16)
    k_ref[...] = _dot(u, wk_ref[...])
    v_ref[...] = _dot(u, wv_ref[...])


def _mem_kv(mem, g, wk, wv):
    bsz, nmem, d = mem.shape
    full = lambda a: pl.BlockSpec(a.shape, lambda b: (0,) * a.ndim)
    out = pl.BlockSpec((None, nmem, d), lambda b: (b, 0, 0))
    g = g.reshape(1, d)
    return pl.pallas_call(
        _mem_kv_kernel,
        grid=(bsz,),
        in_specs=[pl.BlockSpec((None, nmem, d), lambda b: (b, 0, 0)), full(g), full(wk), full(wv)],
        out_specs=[out, out],
        out_shape=[jax.ShapeDtypeStruct((bsz, nmem, d), F32)] * 2,
        compiler_params=_cparams(("parallel",)),
        name="mem_kv",
    )(mem, g, wk, wv)


def _hgrn_kernel(q_ref, f_ref, i_ref, g_ref, lbp_ref, ng_ref, s0_ref, o_ref, sout_ref, st_ref,
                 *, chunk, n_chunks, heads):
    t = pl.program_id(1)

    @pl.when(t == 0)
    def _():
        for h in range(heads):
            st_ref[h] = s0_ref[h].T

    lbp = lbp_ref[...]
    e = jnp.exp(lbp - jnp.max(lbp, axis=0, keepdims=True))
    lb = e[0:1] / jnp.sum(e, axis=0, keepdims=True)
    ng = ng_ref[...]

    row = lax.broadcasted_iota(jnp.int32, (chunk, chunk), 0)
    col = lax.broadcasted_iota(jnp.int32, (chunk, chunk), 1)
    causal = row >= col
    mid = chunk // 2
    to_mid = jnp.where((col >= mid) & (col <= row), 1.0, 0.0) - jnp.where((col > row) & (col < mid), 1.0, 0.0)
    coef = jnp.concatenate([causal.astype(F32), to_mid, (col > row).astype(F32)], axis=0).astype(BF16)

    chunks = range(n_chunks)
    qd, kd, qi, kl, el, vv, gate = [], [], [], [], [], [], []
    for c in chunks:
        rows = slice(c * chunk, (c + 1) * chunk)
        f = lb + (1.0 - lb) * _sigmoid(f_ref[rows, :])
        g_hi, g_lo = _split(jnp.log(f))
        bb = _dot(coef, g_hi) + _dot(coef, g_lo)
        b, b_to_mid, b_to_last = bb[:chunk], bb[chunk:2 * chunk], bb[2 * chunk:]
        q = _silu(q_ref[rows, :])
        k = 1.0 - f
        qd.append((q * jnp.exp(b_to_mid)).astype(BF16))
        kd.append((k * jnp.exp(-b_to_mid)).astype(BF16))
        qi.append((q * jnp.exp(b)).astype(BF16))
        kl.append((k * jnp.exp(b_to_last)).astype(BF16))
        el.append(jnp.exp(b[chunk - 1:chunk, :]))
        vv.append(i_ref[rows, :].astype(BF16))
        gate.append(_silu(g_ref[rows, :]))

    head = lambda x, h: x[:, h * HGRN_HEAD:(h + 1) * HGRN_HEAD]
    items = [(c, h) for c in chunks for h in range(heads)]
    sc = {ch: jnp.where(causal, _dot_nt(head(qd[ch[0]], ch[1]), head(kd[ch[0]], ch[1])), 0.0).astype(BF16)
          for ch in items}
    intra = {ch: _dot(sc[ch], head(vv[ch[0]], ch[1])) for ch in items}
    upd = {ch: _dot_tn(head(vv[ch[0]], ch[1]), head(kl[ch[0]], ch[1])) for ch in items}
    for h in range(heads):
        st = st_ref[h]
        for c in chunks:
            o = intra[c, h] + _dot_nt(head(qi[c], h), st.astype(BF16))
            st = st * head(el[c], h) + upd[c, h]
            o = o * lax.rsqrt(jnp.mean(o * o, axis=-1, keepdims=True) + NORM_EPS)
            o_ref[c * chunk:(c + 1) * chunk, h * HGRN_HEAD:(h + 1) * HGRN_HEAD] = o * ng * head(gate[c], h)
        st_ref[h] = st

    @pl.when(t == pl.num_programs(1) - 1)
    def _():
        for h in range(heads):
            sout_ref[h] = st_ref[h].T


def _hgrn(p3, lbp, ng, s0, tile, chunk):
    bsz, tlen, _ = p3.shape
    heads, dk, dv = s0.shape[1:]
    d = heads * HGRN_HEAD
    kern = functools.partial(_hgrn_kernel, chunk=chunk, n_chunks=tile // chunk, heads=heads)
    pspec = lambda cb: pl.BlockSpec((None, tile, d), lambda b, t: (b, t, cb))
    return pl.pallas_call(
        kern,
        grid=(bsz, tlen // tile),
        in_specs=[
            pspec(0), pspec(1), pspec(2), pspec(3),
            pl.BlockSpec(lbp.shape, lambda b, t: (0, 0)),
            pl.BlockSpec((1, dv), lambda b, t: (0, 0)),
            pl.BlockSpec((None, heads, dk, dv), lambda b, t: (b, 0, 0, 0)),
        ],
        out_specs=[
            pl.BlockSpec((None, tile, d), lambda b, t: (b, t, 0)),
            pl.BlockSpec((None, heads, dk, dv), lambda b, t: (b, 0, 0, 0)),
        ],
        out_shape=[
            jax.ShapeDtypeStruct((bsz, tlen, d), F32),
            jax.ShapeDtypeStruct((bsz, heads, dk, dv), F32),
        ],
        scratch_shapes=[pltpu.VMEM((heads, dv, dk), F32)],
        compiler_params=_cparams(("parallel", "arbitrary")),
        name="hgrn",
    )(p3, p3, p3, p3, lbp, ng.reshape(1, dv), s0)


_P_MU_R, _P_MU_K, _P_MU_V, _P_W0, _P_A0, _P_KK, _P_KA, _P_RK, _P_LNG, _P_LNB = range(10)
_P_ROWS = 16


def _rwkv_kernel(r_ref, k_ref, v_ref, l_ref, br_ref, bk_ref, bv_ref, bl_ref, par_ref, mul_ref,
                 w2_ref, a2_ref, g2a_ref, g2b_ref, s0_ref,
                 o_ref, sout_ref, st_ref, prev_ref, prevl_ref, *, chunk, pairs):
    t = pl.program_id(1)

    @pl.when(t == 0)
    def _():
        st_ref[...] = s0_ref[...]
        prev_ref[0:1, :] = br_ref[...]
        prev_ref[1:2, :] = bk_ref[...]
        prev_ref[2:3, :] = bv_ref[...]
        prevl_ref[0:1, :] = bl_ref[...]

    par = par_ref[...]
    prow = lambda i: par[i:i + 1, :]

    def shifted(x, prev_row, mu):
        row_id = lax.broadcasted_iota(jnp.int32, x.shape, 0)
        prev = jnp.where(row_id == 0, prev_row, pltpu.roll(x, 1, 0))
        return x + (prev - x) * mu

    pr, pk, pv, plr = r_ref[...], k_ref[...], v_ref[...], l_ref[...]
    r = shifted(pr, prev_ref[0:1, :], prow(_P_MU_R))
    k = shifted(pk, prev_ref[1:2, :], prow(_P_MU_K))
    v = shifted(pv, prev_ref[2:3, :], prow(_P_MU_V))
    xl = shifted(plr, prevl_ref[0:1, :], mul_ref[...])
    for slot, x in enumerate((pr, pk, pv)):
        prev_ref[slot:slot + 1, :] = x[chunk - 1:chunk, :]
    prevl_ref[0:1, :] = plr[chunk - 1:chunk, :]
    x0 = xl[:, 0:LANES]
    x1 = xl[:, LANES:2 * LANES]
    x2 = xl[:, 2 * LANES:3 * LANES]

    w_lin = prow(_P_W0) + _dot(jnp.tanh(x0).astype(BF16), w2_ref[...])
    z = -w_lin
    softplus = jnp.maximum(z, 0.0) + jnp.log(1.0 + jnp.exp(-jnp.abs(z)))
    lw = -jnp.exp(-softplus - 0.5)
    a_icl = _sigmoid(prow(_P_A0) + _dot(x0.astype(BF16), a2_ref[...]))
    g_out = _dot(_sigmoid(x1).astype(BF16), g2a_ref[...]) + _dot(_sigmoid(x2).astype(BF16), g2b_ref[...])

    lane = lax.broadcasted_iota(jnp.int32, (1, LANES), 1)
    m0 = (lane < RWKV_HEAD).astype(F32)
    m1 = 1.0 - m0
    head_shift = RWKV_HEAD.bit_length() - 1
    hr = lax.broadcasted_iota(jnp.int32, (LANES, LANES), 0) >> head_shift
    hc = lax.broadcasted_iota(jnp.int32, (LANES, LANES), 1) >> head_shift
    head_ones = (hr == hc).astype(BF16)
    tile_of = lambda x, j: x[:, j * LANES:(j + 1) * LANES]

    def headsum(x, split=False):
        xs = jnp.concatenate([tile_of(x, j) for j in range(pairs)], axis=0)
        if split:
            hi, lo = _split(xs)
            ys = _dot(hi, head_ones) + _dot(lo, head_ones)
        else:
            ys = _dot(xs.astype(BF16), head_ones)
        return jnp.concatenate([ys[j * chunk:(j + 1) * chunk] for j in range(pairs)], axis=1)

    kk = k * prow(_P_KK)
    kk = kk / jnp.maximum(jnp.sqrt(headsum(kk * kk, split=True)), 1e-12)
    k_mod = k * (1.0 + (a_icl - 1.0) * prow(_P_KA))
    a = -kk
    b = kk * a_icl

    ti = lax.broadcasted_iota(jnp.int32, (chunk, chunk), 0)
    tj = lax.broadcasted_iota(jnp.int32, (chunk, chunk), 1)
    tri = (ti >= tj).astype(BF16)
    lw_hi, lw_lo = _split(lw)
    cum = _dot(tri, lw_hi) + _dot(tri, lw_lo)
    cum_last = cum[chunk - 1:chunk, :]
    e_neg = jnp.exp(-cum)
    e_end = jnp.exp(cum_last - cum)
    a_t = a * jnp.exp(cum - lw)
    b_t = b * e_neg
    k_t = k_mod * e_neg
    r_t = r * jnp.exp(cum)
    b_end = b * e_end
    k_end = k_mod * e_end
    g_last = jnp.exp(cum_last)

    n2 = 2 * chunk
    si = lax.broadcasted_iota(jnp.int32, (n2, n2), 0)
    sj = lax.broadcasted_iota(jnp.int32, (n2, n2), 1)
    chunk_shift = chunk.bit_length() - 1
    same_head = (si >> chunk_shift) == (sj >> chunk_shift)
    strict = same_head & ((sj & (chunk - 1)) < (si & (chunk - 1)))
    incl = same_head & ((sj & (chunk - 1)) <= (si & (chunk - 1)))
    eye = (si == sj).astype(F32)
    stack = lambda x: jnp.concatenate([x * m0, x * m1], axis=0)
    stack_b = lambda x, j: stack(tile_of(x, j)).astype(BF16)
    every = range(pairs)

    a_st = [stack_b(a_t, j) for j in every]
    r_st = [stack_b(r_t, j) for j in every]
    v_st = [stack_b(v, j) for j in every]
    bend_st = [stack_b(b_end, j) for j in every]
    kend_st = [stack_b(k_end, j) for j in every]
    tile_b = lambda x, j: tile_of(x, j).astype(BF16)
    sc = [_dot_nt(jnp.concatenate([tile_b(a_t, j), tile_b(r_t, j)], axis=0),
                  jnp.concatenate([stack_b(b_t, j), stack_b(k_t, j)], axis=0)) for j in every]
    twice = lambda x: jnp.concatenate([x, x], axis=0)
    a_ab = [jnp.where(strict, twice(s[:chunk, :n2]), 0.0) for s in sc]
    a_xk = [jnp.concatenate([jnp.where(strict, twice(s[:chunk, n2:]), 0.0),
                             jnp.where(incl, twice(s[chunk:, n2:]), 0.0)], axis=0).astype(BF16) for s in sc]
    a_rb = [jnp.where(incl, twice(s[chunk:, :n2]), 0.0).astype(BF16) for s in sc]

    inv = [eye + x for x in a_ab]
    pw = a_ab
    for _ in range(int(math.log2(chunk)) - 1):
        pw = [_dot(x.astype(BF16), x.astype(BF16)) for x in pw]
        inv = [i + _dot(i.astype(BF16), p.astype(BF16)) for i, p in zip(inv, pw)]

    av = [_dot(a_xk[j], v_st[j]) for j in every]
    wu = [_dot(inv[j].astype(BF16),
               jnp.concatenate([a_st[j], av[j][:n2].astype(BF16)], axis=1)).astype(BF16) for j in every]
    qo = [_dot(a_rb[j], wu[j]) + jnp.concatenate([r_st[j].astype(F32), av[j][n2:]], axis=1) for j in every]
    mn = [_dot_tn(wu[j], bend_st[j]) for j in every]
    nn = [mn[j][LANES:] + _dot_tn(v_st[j], kend_st[j]) for j in every]

    outs = []
    for j in every:
        s0 = st_ref[j]
        s0_b = s0.astype(BF16)
        q = (qo[j][:chunk, :LANES] + qo[j][chunk:, :LANES]).astype(BF16)
        outs.append(_dot_nt(q, s0_b) + qo[j][:chunk, LANES:] + qo[j][chunk:, LANES:])
        st_ref[j] = s0 * tile_of(g_last, j) + _dot(s0_b, mn[j][:LANES].astype(BF16)) + nn[j]
    o = jnp.concatenate(outs, axis=1)

    inv_n = 1.0 / RWKV_HEAD
    mean = headsum(o) * inv_n
    dlt = o - mean
    var = headsum(dlt * dlt) * inv_n
    o = dlt * lax.rsqrt(var + RWKV_GN_EPS) * prow(_P_LNG) + prow(_P_LNB)
    bonus = headsum(r * k_mod * prow(_P_RK))
    o_ref[...] = (o + bonus * v) * g_out

    @pl.when(t == pl.num_programs(1) - 1)
    def _():
        sout_ref[...] = st_ref[...]


def _rwkv(p3, buf_rkv, buf_l, par, mul, w2p, a2p, g2a, g2b, s0bd, d, chunk):
    bsz, tlen, _ = p3.shape
    pairs = d // LANES
    nl = 3 * LANES
    kern = functools.partial(_rwkv_kernel, chunk=chunk, pairs=pairs)
    pcol = lambda cb: pl.BlockSpec((None, chunk, d), lambda b, t: (b, t, cb))
    bcol = lambda cb: pl.BlockSpec((None, 1, d), lambda b, t: (b, 0, cb))
    full = lambda a: pl.BlockSpec(a.shape, lambda b, t: (0,) * a.ndim)
    state = pl.BlockSpec((None, pairs, LANES, LANES), lambda b, t: (b, 0, 0, 0))
    return pl.pallas_call(
        kern,
        grid=(bsz, tlen // chunk),
        in_specs=[
            pcol(4), pcol(5), pcol(6),
            pl.BlockSpec((None, chunk, nl), lambda b, t: (b, t, 9 * d // nl)),
            bcol(0), bcol(1), bcol(2),
            pl.BlockSpec((None, 1, nl), lambda b, t: (b, 0, 0)),
            full(par), full(mul), full(w2p), full(a2p), full(g2a), full(g2b), state,
        ],
        out_specs=[pl.BlockSpec((None, chunk, d), lambda b, t: (b, t, 0)), state],
        out_shape=[
            jax.ShapeDtypeStruct((bsz, tlen, d), F32),
            jax.ShapeDtypeStruct((bsz, pairs, LANES, LANES), F32),
        ],
        scratch_shapes=[pltpu.VMEM((pairs, LANES, LANES), F32), pltpu.VMEM((8, d), F32),
                        pltpu.VMEM((8, nl), F32)],
        compiler_params=_cparams(("parallel", "arbitrary")),
        name="rwkv",
    )(p3, p3, p3, p3, buf_rkv, buf_rkv, buf_rkv, buf_l, par, mul, w2p, a2p, g2a, g2b, s0bd)


def _attn_kernel(x_ref, oh_ref, or_ref, ga_ref, gb_ref, mk_ref, mv_ref, wout_ref, nxg_ref, wq_ref, wo_ref,
                 nfg_ref, wrh_ref, wrl_ref, rb_ref, h_ref, u_ref, wf_ref, att_ref, *, bt, tt):
    n = bt * tt
    d = x_ref.shape[-1]
    flat = lambda ref: ref[...].reshape(n, d)
    merged = _sigmoid(flat(ga_ref)) * flat(oh_ref) + _sigmoid(flat(gb_ref)) * flat(or_ref)
    h1 = flat(x_ref) + _dot(merged.astype(BF16), wout_ref[...])

    q = _dot(_rmsnorm(h1, nxg_ref[...]).astype(BF16), wq_ref[...])
    dh = d // X_HEADS
    scale = dh ** -0.5
    for bi in range(bt):
        rows = slice(bi * tt, (bi + 1) * tt)
        for hh in range(X_HEADS):
            cols = slice(hh * dh, (hh + 1) * dh)
            mem_head = lambda ref: (ref[bi, :, cols] if len(ref.shape) == 3 else ref[bi, :, hh, :]).astype(BF16)
            s = _dot_nt(q[rows, cols].astype(BF16), mem_head(mk_ref)) * scale
            p = jnp.exp(s - jnp.max(s, axis=-1, keepdims=True))
            p = p / jnp.sum(p, axis=-1, keepdims=True)
            att_ref[rows, cols] = _dot(p.astype(BF16), mem_head(mv_ref)).astype(BF16)
    h2 = h1 + _dot(att_ref[...], wo_ref[...])
    h_ref[...] = h2.reshape(bt, tt, d)

    u = _rmsnorm(h2, nfg_ref[...])
    u_ref[...] = u.astype(BF16).reshape(bt, tt, d)
    u_hi, u_lo = _split(u)
    logits = (_dot(u_hi, wrh_ref[...]) + _dot(u_hi, wrl_ref[...]) + _dot(u_lo, wrh_ref[...])) + rb_ref[...]
    lane = lax.broadcasted_iota(jnp.int32, (n, LANES), 1)
    lane_f = lane.astype(F32)
    is_group = (lane >= N_EXPERTS) & (lane < N_EXPERTS + N_GROUPS)
    lg = jnp.where(is_group, logits, NEG_BIG)
    g_max = jnp.max(lg, axis=-1, keepdims=True)
    g_idx = jnp.min(jnp.where(lg == g_max, lane_f, 1e9), axis=-1, keepdims=True) - N_EXPERTS
    pg = 1.0 / jnp.sum(jnp.exp(lg - g_max), axis=-1, keepdims=True)
    in_group = (lane < N_EXPERTS) & ((lane >> (EXPERTS_PER_GROUP.bit_length() - 1)).astype(F32) == g_idx)
    le = jnp.where(in_group, logits, NEG_BIG)
    m1 = jnp.max(le, axis=-1, keepdims=True)
    i1 = jnp.min(jnp.where(le == m1, lane_f, 1e9), axis=-1, keepdims=True)
    le2 = jnp.where(lane_f == i1, NEG_BIG, le)
    m2 = jnp.max(le2, axis=-1, keepdims=True)
    i2 = jnp.min(jnp.where(le2 == m2, lane_f, 1e9), axis=-1, keepdims=True)
    e2 = jnp.exp(m2 - m1)
    w1 = pg / (1.0 + e2)
    wf = jnp.where(lane_f == i1, w1, 0.0) + jnp.where(lane_f == i2, w1 * e2, 0.0)
    wf = wf + jnp.where(lane == _GID_LANE, g_idx, 0.0)
    wf_ref[...] = wf.reshape(bt, tt, LANES)


def _attn(x3, oh3, or3, p3, mk, mv, wout, nxg, wq, wo, nfg, wrh, wrl, rb, bt, tt, gate_block0):
    bsz, tlen, d = x3.shape
    kern = functools.partial(_attn_kernel, bt=bt, tt=tt)
    tok = lambda cb: pl.BlockSpec((bt, tt, d), lambda b, t: (b, t, cb))
    mem = pl.BlockSpec((bt,) + mk.shape[1:], lambda b, t: (b,) + (0,) * (mk.ndim - 1))
    full = lambda a: pl.BlockSpec(a.shape, lambda b, t: (0,) * a.ndim)
    nxg, nfg = nxg.reshape(1, d), nfg.reshape(1, d)
    return pl.pallas_call(
        kern,
        grid=(bsz // bt, tlen // tt),
        in_specs=[tok(0), tok(0), tok(0), tok(gate_block0), tok(gate_block0 + 1), mem, mem,
                  full(wout), full(nxg), full(wq), full(wo), full(nfg), full(wrh), full(wrl), full(rb)],
        out_specs=[
            pl.BlockSpec((bt, tt, d), lambda b, t: (b, t, 0)),
            pl.BlockSpec((bt, tt, d), lambda b, t: (b, t, 0)),
            pl.BlockSpec((bt, tt, LANES), lambda b, t: (b, t, 0)),
        ],
        out_shape=[
            jax.ShapeDtypeStruct((bsz, tlen, d), F32),
            jax.ShapeDtypeStruct((bsz, tlen, d), BF16),
            jax.ShapeDtypeStruct((bsz, tlen, LANES), F32),
        ],
        scratch_shapes=[pltpu.VMEM((bt * tt, d), BF16)],
        compiler_params=_cparams(("parallel", "parallel")),
        name="attn",
    )(x3, oh3, or3, p3, p3, mk, mv, wout, nxg, wq, wo, nfg, wrh, wrl, rb)


def _segment_dma(src, dst, src_row, dst_row, length, max_len, sem, start):
    size = _SEG
    while size * 2 <= max_len:
        size *= 2
    while size >= _SEG:
        offset = length & (-2 * size)

        @pl.when((length & size) != 0)
        def _(offset=offset, size=size):
            copy = pltpu.make_async_copy(src.at[pl.ds(pl.multiple_of(src_row + offset, _SEG), size)],
                                         dst.at[pl.ds(pl.multiple_of(dst_row + offset, _SEG), size)], sem)
            if start:
                copy.start()
            else:
                copy.wait()

        size //= 2


def _tile_meta(meta_ref, nt):
    i = pl.program_id(0)
    field = lambda k: [meta_ref[(k * nt + i) * N_GROUPS + g] for g in range(N_GROUPS)]
    return field(0), field(1), field(2)


def _moe_sort_kernel(meta_ref, u_ref, wf_ref, xs_in, ws_in, xs_ref, ws_ref, su_ref, sw_ref, sem, *, nt, tm):
    del xs_in, ws_in
    off, cnt, base = _tile_meta(meta_ref, nt)
    slots = su_ref.shape[0]
    wf = wf_ref[...]
    gid = wf.T[_GID_LANE:_GID_LANE + 1, :]
    onehot = lax.broadcasted_iota(jnp.int32, (8, tm), 0).astype(F32) == gid
    earlier = (lax.broadcasted_iota(jnp.int32, (tm, tm), 0)
               < lax.broadcasted_iota(jnp.int32, (tm, tm), 1)).astype(BF16)
    seen = _dot(onehot.astype(BF16), earlier)
    pos = jnp.sum(jnp.where(onehot, seen, 0.0), axis=0, keepdims=True)
    for g in range(N_GROUPS):
        pos = pos + jnp.where(gid == float(g), base[g].astype(F32), 0.0)
    perm = (lax.broadcasted_iota(jnp.int32, (slots, tm), 0).astype(F32) == pos).astype(BF16)
    su_ref[...] = _dot(perm, u_ref[...]).astype(BF16)
    w_hi, w_lo = _split(wf)
    sw_ref[...] = _dot(perm, w_hi) + _dot(perm, w_lo)
    for start in (True, False):
        for g in range(N_GROUPS):
            _segment_dma(su_ref, xs_ref, base[g], off[g], cnt[g], tm, sem.at[0], start)
            _segment_dma(sw_ref, ws_ref, base[g], off[g], cnt[g], tm, sem.at[1], start)


def _moe_expert_kernel(meta_ref, xs_ref, ws_ref, wg_ref, wu_ref, wd_ref, ys_ref, *, n_tiles):
    r = pl.program_id(0)
    used = r < meta_ref[n_tiles]

    @pl.when(used)
    def _():
        first = meta_ref[r] * EXPERTS_PER_GROUP
        x = xs_ref[...]
        ws = ws_ref[...]
        lane = lax.broadcasted_iota(jnp.int32, ws.shape, 1)
        hid = []
        for e in range(EXPERTS_PER_GROUP):
            h = _silu(_dot(x, wg_ref[e])) * _dot(x, wu_ref[e])
            w_col = jnp.sum(jnp.where(lane == first + e, ws, 0.0), axis=-1, keepdims=True)
            hid.append((h * w_col).astype(BF16))
        ys_ref[...] = _dot(jnp.concatenate(hid, axis=1), wd_ref[...]).astype(ys_ref.dtype)

    @pl.when(jnp.logical_not(used))
    def _():
        ys_ref[...] = jnp.zeros_like(ys_ref)


def _moe_unsort_kernel(meta_ref, wf_ref, h_ref, gf_ref, ys_ref, y_ref, sy_ref, sem, *, nt, tm):
    off, cnt, base = _tile_meta(meta_ref, nt)
    filled = meta_ref[3 * nt * N_GROUPS + pl.program_id(0)]
    slots = sy_ref.shape[0]
    for g in range(N_GROUPS):
        _segment_dma(ys_ref, sy_ref, off[g], base[g], cnt[g], tm, sem.at[0], True)
    gid = wf_ref[...][:, _GID_LANE:_GID_LANE + 1]
    onehot = lax.broadcasted_iota(jnp.int32, (tm, LANES), 1).astype(F32) == gid
    earlier = (lax.broadcasted_iota(jnp.int32, (tm, tm), 1)
               < lax.broadcasted_iota(jnp.int32, (tm, tm), 0)).astype(BF16)
    seen = _dot(earlier, onehot.astype(BF16))
    pos = jnp.sum(jnp.where(onehot, seen, 0.0), axis=1, keepdims=True)
    for g in range(N_GROUPS):
        pos = pos + jnp.where(gid == float(g), base[g].astype(F32), 0.0)
    perm_t = (lax.broadcasted_iota(jnp.int32, (tm, slots), 1).astype(F32) == pos).astype(BF16)
    for g in range(N_GROUPS):
        _segment_dma(ys_ref, sy_ref, off[g], base[g], cnt[g], tm, sem.at[0], False)
    slot_row = lax.broadcasted_iota(jnp.int32, (slots, 1), 0)
    ys = sy_ref[...]
    moe = _dot(perm_t, jnp.where(slot_row < filled, ys, jnp.zeros_like(ys)))
    y_ref[...] = _rmsnorm(h_ref[...] + moe, gf_ref[...])


def _moe(u, wf, h, wg, wu, wd, gf, tm, te):
    n, d = u.shape
    ne, _, f = wg.shape
    nt = n // tm
    slots = tm + LANES
    n_tiles = -(-(n + nt * N_GROUPS * (_SEG - 1) + N_GROUPS * (te - 1)) // te)
    rows = n_tiles * te
    i32 = jnp.int32

    gid = wf[:, _GID_LANE].astype(i32).reshape(nt, tm)
    cnt = jnp.sum((gid[:, :, None] == jnp.arange(N_GROUPS, dtype=i32)).astype(i32), axis=1)
    cnt = (cnt + _SEG - 1) // _SEG * _SEG
    base = jnp.cumsum(cnt, axis=1) - cnt
    region = (jnp.sum(cnt, axis=0) + te - 1) // te * te
    region_end = jnp.cumsum(region)
    off = (region_end - region)[None, :] + jnp.cumsum(cnt, axis=0) - cnt
    meta = jnp.concatenate([off.reshape(-1), cnt.reshape(-1), base.reshape(-1), jnp.sum(cnt, axis=1)]).astype(i32)
    tile_row = jnp.arange(n_tiles, dtype=i32) * te
    tile_group = jnp.minimum(jnp.sum((tile_row[:, None] >= region_end[None, :]).astype(i32), axis=1), N_GROUPS - 1)
    emeta = jnp.concatenate([tile_group, region_end[-1:] // te]).astype(i32)

    anyspec = pl.BlockSpec(memory_space=pl.ANY)
    tok = lambda w: pl.BlockSpec((tm, w), lambda i, m: (i, 0))
    xs, ws = pl.pallas_call(
        functools.partial(_moe_sort_kernel, nt=nt, tm=tm),
        grid_spec=pltpu.PrefetchScalarGridSpec(
            num_scalar_prefetch=1, grid=(nt,),
            in_specs=[tok(d), tok(LANES), anyspec, anyspec],
            out_specs=[anyspec, anyspec],
            scratch_shapes=[pltpu.VMEM((slots, d), BF16), pltpu.VMEM((slots, LANES), F32),
                            pltpu.SemaphoreType.DMA((2,))]),
        out_shape=[jax.ShapeDtypeStruct((rows, d), BF16), jax.ShapeDtypeStruct((rows, LANES), F32)],
        input_output_aliases={3: 0, 4: 1},
        compiler_params=_cparams(("arbitrary",)),
        name="moe_sort",
    )(meta, u, wf, jnp.zeros((rows, d), BF16), jnp.zeros((rows, LANES), F32))

    srt = lambda w: pl.BlockSpec((te, w), lambda r, m: (r, 0))
    grp_w = pl.BlockSpec((EXPERTS_PER_GROUP, d, f), lambda r, m: (m[r], 0, 0))
    ys = pl.pallas_call(
        functools.partial(_moe_expert_kernel, n_tiles=n_tiles),
        grid_spec=pltpu.PrefetchScalarGridSpec(
            num_scalar_prefetch=1, grid=(n_tiles,),
            in_specs=[srt(d), srt(LANES), grp_w, grp_w,
                      pl.BlockSpec((EXPERTS_PER_GROUP * f, d), lambda r, m: (m[r], 0))],
            out_specs=srt(d)),
        out_shape=jax.ShapeDtypeStruct((rows, d), BF16),
        compiler_params=_cparams(("parallel",)),
        name="moe_expert",
    )(emeta, xs, ws, wg, wu, wd.reshape(ne * f, d))

    return pl.pallas_call(
        functools.partial(_moe_unsort_kernel, nt=nt, tm=tm),
        grid_spec=pltpu.PrefetchScalarGridSpec(
            num_scalar_prefetch=1, grid=(nt,),
            in_specs=[tok(LANES), tok(d), pl.BlockSpec((1, d), lambda i, m: (0, 0)), anyspec],
            out_specs=tok(d),
            scratch_shapes=[pltpu.VMEM((slots, d), BF16), pltpu.SemaphoreType.DMA((1,))]),
        out_shape=jax.ShapeDtypeStruct((n, d), F32),
        compiler_params=_cparams(("arbitrary",)),
        name="moe_unsort",
    )(meta, wf, h, gf.reshape(1, d), ys)


def _pick(n, pref):
    t = min(n, pref)
    while n % t:
        t -= 8
    return t


def _block_diag_pairs(s):
    b, h, n, _ = s.shape
    s = s.reshape(b, h // 2, 2, n, n)
    z = jnp.zeros_like(s[:, :, 0])
    top = jnp.concatenate([s[:, :, 0], z], axis=-1)
    bot = jnp.concatenate([z, s[:, :, 1]], axis=-1)
    return jnp.concatenate([top, bot], axis=-2)


def _pairs_to_heads(sbd):
    b, p, n2, _ = sbd.shape
    n = n2 // 2
    s = jnp.stack([sbd[:, :, :n, :n], sbd[:, :, n:, n:]], axis=2)
    return s.reshape(b, 2 * p, n, n)


def kernel(x_prompt, x_sample, mem_prompt, cache_mem_k, cache_mem_v, state_hgrn, state_rwkv, state_rwkv_shift, norm_mix_g, w_in, hgrn_lower_bounds, hgrn_norm_g, rwkv_mu, rwkv_w0, rwkv_w2, rwkv_a0, rwkv_a2, rwkv_g2, rwkv_k_k, rwkv_k_a, rwkv_r_k, rwkv_ln_g, rwkv_ln_b, w_out, norm_x_g, norm_mem_g, wq_x, wk_x, wv_x, wo_x, norm_ffn_g, router_group_w, router_group_b, router_expert_w, router_expert_b, expert_w_gate, expert_w_up, expert_w_down, norm_final_g):
    assert w_in.shape[0] == 1, "single-layer configuration"
    d = x_prompt.shape[-1]
    n_lora_w, n_lora_a, n_lora_g = rwkv_w2.shape[1], rwkv_a2.shape[1], rwkv_g2.shape[1]
    assert n_lora_w + n_lora_a == LANES and LANES < n_lora_g <= 2 * LANES
    n_lora = n_lora_w + n_lora_a + n_lora_g
    n_rwkv_in = 3 * d + n_lora
    lora_pad = 3 * LANES - n_lora

    w = w_in[0]
    pr0 = 4 * d
    gate0 = pr0 + n_rwkv_in
    w_perm = jnp.concatenate(
        [w[:, :pr0 + 3 * d], w[:, gate0:gate0 + 2 * d], w[:, pr0 + 3 * d:gate0],
         jnp.zeros((d, lora_pad), w.dtype)], axis=1).astype(BF16)
    n_cols = w_perm.shape[1]
    lora_col0 = 9 * d
    gate_block0 = 7
    mu = rwkv_mu[0]
    mul = jnp.pad(mu[3 * d:], (0, lora_pad)).reshape(1, 3 * LANES)
    par = jnp.stack([mu[:d], mu[d:2 * d], mu[2 * d:3 * d], rwkv_w0[0], rwkv_a0[0], rwkv_k_k[0], rwkv_k_a[0],
                     rwkv_r_k[0].reshape(d), rwkv_ln_g[0], rwkv_ln_b[0]])
    par = jnp.pad(par, ((0, _P_ROWS - par.shape[0]), (0, 0)))
    w2p = jnp.pad(rwkv_w2[0], ((0, n_lora_a), (0, 0))).astype(BF16)
    a2p = jnp.pad(rwkv_a2[0], ((n_lora_w, 0), (0, 0))).astype(BF16)
    g2a = rwkv_g2[0][:LANES].astype(BF16)
    g2b = jnp.pad(rwkv_g2[0][LANES:], ((0, 2 * LANES - n_lora_g), (0, 0))).astype(BF16)
    wout_b, wq_b, wo_b = w_out[0].astype(BF16), wq_x[0].astype(BF16), wo_x[0].astype(BF16)
    wr = jnp.concatenate([router_expert_w[0], router_group_w[0],
                          jnp.zeros((d, LANES - N_EXPERTS - N_GROUPS), F32)], axis=1)
    wr_hi, wr_lo = _split(wr)
    rb = jnp.pad(jnp.concatenate([router_expert_b[0].reshape(-1), router_group_b[0]]),
                 (0, LANES - N_EXPERTS - N_GROUPS)).reshape(1, LANES)
    wg_b, wu_b, wd_b = (expert_w_gate[0].astype(BF16), expert_w_up[0].astype(BF16),
                        expert_w_down[0].astype(BF16))

    nb_p = mem_prompt.shape[0]
    mem_k_p, mem_v_p = _mem_kv(mem_prompt, norm_mem_g[0], wk_x[0].astype(BF16), wv_x[0].astype(BF16))

    def group(x3, mem_k, mem_v, s_h, s_r, buf, tile_h, chunk_r, bt, tt, te):
        bsz, tlen, _ = x3.shape
        n = bsz * tlen
        p = _norm_matmul(x3.reshape(n, d), norm_mix_g[0], w_perm, _pick(n, 1024), n_cols // 5)
        p3 = p.reshape(bsz, tlen, n_cols)
        o_h, s_h_new = _hgrn(p3, hgrn_lower_bounds, hgrn_norm_g[0], s_h, tile_h, 32)
        buf_rkv = buf[:, :, :3 * d]
        buf_l = jnp.pad(buf[:, :, 3 * d:], ((0, 0), (0, 0), (0, lora_pad)))
        o_r, s_r_bd = _rwkv(p3, buf_rkv, buf_l, par, mul, w2p, a2p, g2a, g2b, _block_diag_pairs(s_r), d, chunk_r)
        h2, u, wf = _attn(x3, o_h, o_r, p3, mem_k, mem_v, wout_b, norm_x_g[0], wq_b, wo_b, norm_ffn_g[0],
                          wr_hi, wr_lo, rb, bt, tt, gate_block0)
        y = _moe(u.reshape(n, d), wf.reshape(n, LANES), h2.reshape(n, d), wg_b, wu_b, wd_b, norm_final_g,
                 _pick(n, 512), te)
        shift = jnp.concatenate([p3[:, -1:, pr0:pr0 + 3 * d], p3[:, -1:, lora_col0:lora_col0 + n_lora]], axis=-1)
        return y.reshape(bsz, tlen, d), s_h_new, _pairs_to_heads(s_r_bd), shift

    nb_s, t_s, _ = x_sample.shape
    t_p = x_prompt.shape[1]
    zeros_like_state = lambda s: jnp.zeros((nb_p,) + s.shape[2:], s.dtype)
    y_p, sh_p, sr_p, bf_p = group(
        x_prompt, mem_k_p, mem_v_p, zeros_like_state(state_hgrn), zeros_like_state(state_rwkv),
        zeros_like_state(state_rwkv_shift), _pick(t_p, 256), 64, 1, _pick(t_p, 256), 512)
    y_s, sh_s, sr_s, bf_s = group(
        x_sample, cache_mem_k[0], cache_mem_v[0],
        state_hgrn[0], state_rwkv[0], state_rwkv_shift[0], t_s, t_s, _pick(nb_s, 4), t_s, 256)

    kv_shape = (1,) + mem_k_p.shape[:2] + cache_mem_k.shape[3:]
    return (y_p, y_s, mem_k_p.reshape(kv_shape), mem_v_p.reshape(kv_shape), sh_p[None], sh_s[None],
            sr_p[None], sr_s[None], bf_p[None], bf_s[None])
```

```python
import functools
import math

import jax
import jax.numpy as jnp
from jax import lax
from jax.experimental import pallas as pl
from jax.experimental.pallas import tpu as pltpu

F32 = jnp.float32
BF16 = jnp.bfloat16

NORM_EPS = 1e-6
RWKV_GN_EPS = 64e-5
LANES = 128
HGRN_HEAD = 128
RWKV_HEAD = 64
X_HEADS = 4
N_GROUPS = 4
EXPERTS_PER_GROUP = 8
N_EXPERTS = N_GROUPS * EXPERTS_PER_GROUP
NEG_BIG = -1e30
VMEM_LIMIT = 52 * 1024 * 1024
_E1_LANE, _E2_LANE = N_EXPERTS, N_EXPERTS + 1
_SEG = 16


def _dot(a, b):
    return jnp.dot(a, b, preferred_element_type=F32)


def _dot_nt(a, b):
    return lax.dot_general(a, b, (((1,), (1,)), ((), ())), preferred_element_type=F32)


def _dot_tn(a, b):
    return lax.dot_general(a, b, (((0,), (0,)), ((), ())), preferred_element_type=F32)


def _split(x):
    hi = x.astype(BF16)
    lo = (x - hi.astype(F32)).astype(BF16)
    return hi, lo


def _sigmoid(x):
    return 1.0 / (1.0 + jnp.exp(-x))


def _silu(x):
    return x * _sigmoid(x)


def _rmsnorm(x, g):
    return x * lax.rsqrt(jnp.mean(x * x, axis=-1, keepdims=True) + NORM_EPS) * g


def _cparams(sem):
    return pltpu.CompilerParams(dimension_semantics=sem, vmem_limit_bytes=VMEM_LIMIT)


def _norm_matmul_kernel(x_ref, g_ref, w_ref, o_ref, u_ref):
    @pl.when(pl.program_id(1) == 0)
    def _():
        u_ref[...] = _rmsnorm(x_ref[...], g_ref[...]).astype(BF16)

    o_ref[...] = _dot(u_ref[...], w_ref[...])


def _norm_matmul(x, g, w, tm, tn):
    n, d = x.shape
    m = w.shape[1]
    return pl.pallas_call(
        _norm_matmul_kernel,
        grid=(n // tm, m // tn),
        in_specs=[
            pl.BlockSpec((tm, d), lambda i, j: (i, 0)),
            pl.BlockSpec((1, d), lambda i, j: (0, 0)),
            pl.BlockSpec((d, tn), lambda i, j: (0, j)),
        ],
        out_specs=pl.BlockSpec((tm, tn), lambda i, j: (i, j)),
        out_shape=jax.ShapeDtypeStruct((n, m), F32),
        scratch_shapes=[pltpu.VMEM((tm, d), BF16)],
        compiler_params=_cparams(("parallel", "arbitrary")),
        name="norm_matmul",
    )(x, g.reshape(1, d), w)


def _mem_kv_kernel(m_ref, g_ref, wk_ref, wv_ref, k_ref, v_ref):
    u = _rmsnorm(m_ref[...], g_ref[...]).astype(BF16)
    k_ref[...] = _dot(u, wk_ref[...])
    v_ref[...] = _dot(u, wv_ref[...])


def _mem_kv(mem, g, wk, wv):
    bsz, nmem, d = mem.shape
    full = lambda a: pl.BlockSpec(a.shape, lambda b: (0,) * a.ndim)
    out = pl.BlockSpec((None, nmem, d), lambda b: (b, 0, 0))
    g = g.reshape(1, d)
    return pl.pallas_call(
        _mem_kv_kernel,
        grid=(bsz,),
        in_specs=[pl.BlockSpec((None, nmem, d), lambda b: (b, 0, 0)), full(g), full(wk), full(wv)],
        out_specs=[out, out],
        out_shape=[jax.ShapeDtypeStruct((bsz, nmem, d), F32)] * 2,
        compiler_params=_cparams(("parallel",)),
        name="mem_kv",
    )(mem, g, wk, wv)


def _hgrn_kernel(q_ref, f_ref, i_ref, g_ref, lbp_ref, ng_ref, s0_ref, o_ref, sout_ref, st_ref,
                 *, chunk, n_chunks, heads):
    t = pl.program_id(1)

    @pl.when(t == 0)
    def _():
        for h in range(heads):
            st_ref[h] = s0_ref[h].T

    lbp = lbp_ref[...]
    e = jnp.exp(lbp - jnp.max(lbp, axis=0, keepdims=True))
    lb = e[0:1] / jnp.sum(e, axis=0, keepdims=True)
    ng = ng_ref[...]

    row = lax.broadcasted_iota(jnp.int32, (chunk, chunk), 0)
    col = lax.broadcasted_iota(jnp.int32, (chunk, chunk), 1)
    causal = row >= col
    mid = chunk // 2
    to_mid = jnp.where((col >= mid) & (col <= row), 1.0, 0.0) - jnp.where((col > row) & (col < mid), 1.0, 0.0)
    coef = jnp.concatenate([causal.astype(F32), to_mid, (col > row).astype(F32)], axis=0).astype(BF16)

    chunks = range(n_chunks)
    qd, kd, qi, kl, el, vv, gate = [], [], [], [], [], [], []
    for c in chunks:
        rows = slice(c * chunk, (c + 1) * chunk)
        f = lb + (1.0 - lb) * _sigmoid(f_ref[rows, :])
        g_hi, g_lo = _split(jnp.log(f))
        bb = _dot(coef, g_hi) + _dot(coef, g_lo)
        b, b_to_mid, b_to_last = bb[:chunk], bb[chunk:2 * chunk], bb[2 * chunk:]
        q = _silu(q_ref[rows, :])
        k = 1.0 - f
        qd.append((q * jnp.exp(b_to_mid)).astype(BF16))
        kd.append((k * jnp.exp(-b_to_mid)).astype(BF16))
        qi.append((q * jnp.exp(b)).astype(BF16))
        kl.append((k * jnp.exp(b_to_last)).astype(BF16))
        el.append(jnp.exp(b[chunk - 1:chunk, :]))
        vv.append(i_ref[rows, :].astype(BF16))
        gate.append(_silu(g_ref[rows, :]))

    head = lambda x, h: x[:, h * HGRN_HEAD:(h + 1) * HGRN_HEAD]
    items = [(c, h) for c in chunks for h in range(heads)]
    sc = {ch: jnp.where(causal, _dot_nt(head(qd[ch[0]], ch[1]), head(kd[ch[0]], ch[1])), 0.0).astype(BF16)
          for ch in items}
    intra = {ch: _dot(sc[ch], head(vv[ch[0]], ch[1])) for ch in items}
    upd = {ch: _dot_tn(head(vv[ch[0]], ch[1]), head(kl[ch[0]], ch[1])) for ch in items}
    for h in range(heads):
        st = st_ref[h]
        for c in chunks:
            o = intra[c, h] + _dot_nt(head(qi[c], h), st.astype(BF16))
            st = st * head(el[c], h) + upd[c, h]
            o = o * lax.rsqrt(jnp.mean(o * o, axis=-1, keepdims=True) + NORM_EPS)
            o_ref[c * chunk:(c + 1) * chunk, h * HGRN_HEAD:(h + 1) * HGRN_HEAD] = o * ng * head(gate[c], h)
        st_ref[h] = st

    @pl.when(t == pl.num_programs(1) - 1)
    def _():
        for h in range(heads):
            sout_ref[h] = st_ref[h].T


def _hgrn(p3, lbp, ng, s0, tile, chunk):
    bsz, tlen, _ = p3.shape
    heads, dk, dv = s0.shape[1:]
    d = heads * HGRN_HEAD
    kern = functools.partial(_hgrn_kernel, chunk=chunk, n_chunks=tile // chunk, heads=heads)
    pspec = lambda cb: pl.BlockSpec((None, tile, d), lambda b, t: (b, t, cb))
    return pl.pallas_call(
        kern,
        grid=(bsz, tlen // tile),
        in_specs=[
            pspec(0), pspec(1), pspec(2), pspec(3),
            pl.BlockSpec(lbp.shape, lambda b, t: (0, 0)),
            pl.BlockSpec((1, dv), lambda b, t: (0, 0)),
            pl.BlockSpec((None, heads, dk, dv), lambda b, t: (b, 0, 0, 0)),
        ],
        out_specs=[
            pl.BlockSpec((None, tile, d), lambda b, t: (b, t, 0)),
            pl.BlockSpec((None, heads, dk, dv), lambda b, t: (b, 0, 0, 0)),
        ],
        out_shape=[
            jax.ShapeDtypeStruct((bsz, tlen, d), F32),
            jax.ShapeDtypeStruct((bsz, heads, dk, dv), F32),
        ],
        scratch_shapes=[pltpu.VMEM((heads, dv, dk), F32)],
        compiler_params=_cparams(("parallel", "arbitrary")),
        name="hgrn",
    )(p3, p3, p3, p3, lbp, ng.reshape(1, dv), s0)


_P_MU_R, _P_MU_K, _P_MU_V, _P_W0, _P_A0, _P_KK, _P_KA, _P_RK, _P_LNG, _P_LNB = range(10)
_P_ROWS = 16


def _rwkv_kernel(r_ref, k_ref, v_ref, l_ref, br_ref, bk_ref, bv_ref, bl_ref, par_ref, mul_ref,
                 w2_ref, a2_ref, g2a_ref, g2b_ref, s0_ref,
                 o_ref, sout_ref, st_ref, prev_ref, prevl_ref, *, chunk, pairs):
    t = pl.program_id(1)

    @pl.when(t == 0)
    def _():
        st_ref[...] = s0_ref[...]
        prev_ref[0:1, :] = br_ref[...]
        prev_ref[1:2, :] = bk_ref[...]
        prev_ref[2:3, :] = bv_ref[...]
        prevl_ref[0:1, :] = bl_ref[...]

    par = par_ref[...]
    prow = lambda i: par[i:i + 1, :]

    def shifted(x, prev_row, mu):
        row_id = lax.broadcasted_iota(jnp.int32, x.shape, 0)
        prev = jnp.where(row_id == 0, prev_row, pltpu.roll(x, 1, 0))
        return x + (prev - x) * mu

    pr, pk, pv, plr = r_ref[...], k_ref[...], v_ref[...], l_ref[...]
    r = shifted(pr, prev_ref[0:1, :], prow(_P_MU_R))
    k = shifted(pk, prev_ref[1:2, :], prow(_P_MU_K))
    v = shifted(pv, prev_ref[2:3, :], prow(_P_MU_V))
    xl = shifted(plr, prevl_ref[0:1, :], mul_ref[...])
    for slot, x in enumerate((pr, pk, pv)):
        prev_ref[slot:slot + 1, :] = x[chunk - 1:chunk, :]
    prevl_ref[0:1, :] = plr[chunk - 1:chunk, :]
    x0 = xl[:, 0:LANES]
    x1 = xl[:, LANES:2 * LANES]
    x2 = xl[:, 2 * LANES:3 * LANES]

    w_lin = prow(_P_W0) + _dot(jnp.tanh(x0).astype(BF16), w2_ref[...])
    z = -w_lin
    softplus = jnp.maximum(z, 0.0) + jnp.log(1.0 + jnp.exp(-jnp.abs(z)))
    lw = -jnp.exp(-softplus - 0.5)
    a_icl = _sigmoid(prow(_P_A0) + _dot(x0.astype(BF16), a2_ref[...]))
    g_out = _dot(_sigmoid(x1).astype(BF16), g2a_ref[...]) + _dot(_sigmoid(x2).astype(BF16), g2b_ref[...])

    lane = lax.broadcasted_iota(jnp.int32, (1, LANES), 1)
    m0 = (lane < RWKV_HEAD).astype(F32)
    m1 = 1.0 - m0
    head_shift = RWKV_HEAD.bit_length() - 1
    hr = lax.broadcasted_iota(jnp.int32, (LANES, LANES), 0) >> head_shift
    hc = lax.broadcasted_iota(jnp.int32, (LANES, LANES), 1) >> head_shift
    head_ones = (hr == hc).astype(BF16)
    tile_of = lambda x, j: x[:, j * LANES:(j + 1) * LANES]

    def headsum(x, split=False):
        xs = jnp.concatenate([tile_of(x, j) for j in range(pairs)], axis=0)
        if split:
            hi, lo = _split(xs)
            ys = _dot(hi, head_ones) + _dot(lo, head_ones)
        else:
            ys = _dot(xs.astype(BF16), head_ones)
        return jnp.concatenate([ys[j * chunk:(j + 1) * chunk] for j in range(pairs)], axis=1)

    kk = k * prow(_P_KK)
    kk = kk / jnp.maximum(jnp.sqrt(headsum(kk * kk, split=True)), 1e-12)
    k_mod = k * (1.0 + (a_icl - 1.0) * prow(_P_KA))
    a = -kk
    b = kk * a_icl

    ti = lax.broadcasted_iota(jnp.int32, (chunk, chunk), 0)
    tj = lax.broadcasted_iota(jnp.int32, (chunk, chunk), 1)
    tri = (ti >= tj).astype(BF16)
    lw_hi, lw_lo = _split(lw)
    cum = _dot(tri, lw_hi) + _dot(tri, lw_lo)
    cum_last = cum[chunk - 1:chunk, :]
    e_neg = jnp.exp(-cum)
    e_end = jnp.exp(cum_last - cum)
    a_t = a * jnp.exp(cum - lw)
    b_t = b * e_neg
    k_t = k_mod * e_neg
    r_t = r * jnp.exp(cum)
    b_end = b * e_end
    k_end = k_mod * e_end
    g_last = jnp.exp(cum_last)

    n2 = 2 * chunk
    si = lax.broadcasted_iota(jnp.int32, (n2, n2), 0)
    sj = lax.broadcasted_iota(jnp.int32, (n2, n2), 1)
    chunk_shift = chunk.bit_length() - 1
    same_head = (si >> chunk_shift) == (sj >> chunk_shift)
    strict = same_head & ((sj & (chunk - 1)) < (si & (chunk - 1)))
    incl = same_head & ((sj & (chunk - 1)) <= (si & (chunk - 1)))
    eye = (si == sj).astype(F32)
    stack = lambda x: jnp.concatenate([x * m0, x * m1], axis=0)
    stack_b = lambda x, j: stack(tile_of(x, j)).astype(BF16)
    every = range(pairs)

    a_st = [stack_b(a_t, j) for j in every]
    r_st = [stack_b(r_t, j) for j in every]
    v_st = [stack_b(v, j) for j in every]
    bend_st = [stack_b(b_end, j) for j in every]
    kend_st = [stack_b(k_end, j) for j in every]
    tile_b = lambda x, j: tile_of(x, j).astype(BF16)
    sc = [_dot_nt(jnp.concatenate([tile_b(a_t, j), tile_b(r_t, j)], axis=0),
                  jnp.concatenate([stack_b(b_t, j), stack_b(k_t, j)], axis=0)) for j in every]
    twice = lambda x: jnp.concatenate([x, x], axis=0)
    a_ab = [jnp.where(strict, twice(s[:chunk, :n2]), 0.0) for s in sc]
    a_xk = [jnp.concatenate([jnp.where(strict, twice(s[:chunk, n2:]), 0.0),
                             jnp.where(incl, twice(s[chunk:, n2:]), 0.0)], axis=0).astype(BF16) for s in sc]
    a_rb = [jnp.where(incl, twice(s[chunk:, :n2]), 0.0).astype(BF16) for s in sc]

    inv = [eye + x for x in a_ab]
    pw = a_ab
    for _ in range(int(math.log2(chunk)) - 1):
        pw = [_dot(x.astype(BF16), x.astype(BF16)) for x in pw]
        inv = [i + _dot(i.astype(BF16), p.astype(BF16)) for i, p in zip(inv, pw)]

    av = [_dot(a_xk[j], v_st[j]) for j in every]
    wu = [_dot(inv[j].astype(BF16),
               jnp.concatenate([a_st[j], av[j][:n2].astype(BF16)], axis=1)).astype(BF16) for j in every]
    qo = [_dot(a_rb[j], wu[j]) + jnp.concatenate([r_st[j].astype(F32), av[j][n2:]], axis=1) for j in every]
    mn = [_dot_tn(wu[j], bend_st[j]) for j in every]
    nn = [mn[j][LANES:] + _dot_tn(v_st[j], kend_st[j]) for j in every]

    outs = []
    for j in every:
        s0 = st_ref[j]
        s0_b = s0.astype(BF16)
        q = (qo[j][:chunk, :LANES] + qo[j][chunk:, :LANES]).astype(BF16)
        outs.append(_dot_nt(q, s0_b) + qo[j][:chunk, LANES:] + qo[j][chunk:, LANES:])
        st_ref[j] = s0 * tile_of(g_last, j) + _dot(s0_b, mn[j][:LANES].astype(BF16)) + nn[j]
    o = jnp.concatenate(outs, axis=1)

    inv_n = 1.0 / RWKV_HEAD
    mean = headsum(o) * inv_n
    dlt = o - mean
    var = headsum(dlt * dlt) * inv_n
    o = dlt * lax.rsqrt(var + RWKV_GN_EPS) * prow(_P_LNG) + prow(_P_LNB)
    bonus = headsum(r * k_mod * prow(_P_RK))
    o_ref[...] = (o + bonus * v) * g_out

    @pl.when(t == pl.num_programs(1) - 1)
    def _():
        sout_ref[...] = st_ref[...]


def _rwkv(p3, buf_rkv, buf_l, par, mul, w2p, a2p, g2a, g2b, s0bd, d, chunk):
    bsz, tlen, _ = p3.shape
    pairs = d // LANES
    nl = 3 * LANES
    kern = functools.partial(_rwkv_kernel, chunk=chunk, pairs=pairs)
    pcol = lambda cb: pl.BlockSpec((None, chunk, d), lambda b, t: (b, t, cb))
    bcol = lambda cb: pl.BlockSpec((None, 1, d), lambda b, t: (b, 0, cb))
    full = lambda a: pl.BlockSpec(a.shape, lambda b, t: (0,) * a.ndim)
    state = pl.BlockSpec((None, pairs, LANES, LANES), lambda b, t: (b, 0, 0, 0))
    return pl.pallas_call(
        kern,
        grid=(bsz, tlen // chunk),
        in_specs=[
            pcol(4), pcol(5), pcol(6),
            pl.BlockSpec((None, chunk, nl), lambda b, t: (b, t, 9 * d // nl)),
            bcol(0), bcol(1), bcol(2),
            pl.BlockSpec((None, 1, nl), lambda b, t: (b, 0, 0)),
            full(par), full(mul), full(w2p), full(a2p), full(g2a), full(g2b), state,
        ],
        out_specs=[pl.BlockSpec((None, chunk, d), lambda b, t: (b, t, 0)), state],
        out_shape=[
            jax.ShapeDtypeStruct((bsz, tlen, d), F32),
            jax.ShapeDtypeStruct((bsz, pairs, LANES, LANES), F32),
        ],
        scratch_shapes=[pltpu.VMEM((pairs, LANES, LANES), F32), pltpu.VMEM((8, d), F32),
                        pltpu.VMEM((8, nl), F32)],
        compiler_params=_cparams(("parallel", "arbitrary")),
        name="rwkv",
    )(p3, p3, p3, p3, buf_rkv, buf_rkv, buf_rkv, buf_l, par, mul, w2p, a2p, g2a, g2b, s0bd)


def _attn_kernel(x_ref, oh_ref, or_ref, ga_ref, gb_ref, mk_ref, mv_ref, wout_ref, nxg_ref, wq_ref, wo_ref,
                 nfg_ref, wrh_ref, wrl_ref, rb_ref, h_ref, u_ref, wf_ref, att_ref, *, bt, tt):
    n = bt * tt
    d = x_ref.shape[-1]
    flat = lambda ref: ref[...].reshape(n, d)
    merged = _sigmoid(flat(ga_ref)) * flat(oh_ref) + _sigmoid(flat(gb_ref)) * flat(or_ref)
    h1 = flat(x_ref) + _dot(merged.astype(BF16), wout_ref[...])

    q = _dot(_rmsnorm(h1, nxg_ref[...]).astype(BF16), wq_ref[...])
    dh = d // X_HEADS
    scale = dh ** -0.5
    for bi in range(bt):
        rows = slice(bi * tt, (bi + 1) * tt)
        for hh in range(X_HEADS):
            cols = slice(hh * dh, (hh + 1) * dh)
            mem_head = lambda ref: (ref[bi, :, cols] if len(ref.shape) == 3 else ref[bi, :, hh, :]).astype(BF16)
            s = _dot_nt(q[rows, cols].astype(BF16), mem_head(mk_ref)) * scale
            p = jnp.exp(s - jnp.max(s, axis=-1, keepdims=True))
            p = p / jnp.sum(p, axis=-1, keepdims=True)
            att_ref[rows, cols] = _dot(p.astype(BF16), mem_head(mv_ref)).astype(BF16)
    h2 = h1 + _dot(att_ref[...], wo_ref[...])
    h_ref[...] = h2.reshape(bt, tt, d)

    u = _rmsnorm(h2, nfg_ref[...])
    u_ref[...] = u.astype(BF16).reshape(bt, tt, d)
    u_hi, u_lo = _split(u)
    logits = (_dot(u_hi, wrh_ref[...]) + _dot(u_hi, wrl_ref[...]) + _dot(u_lo, wrh_ref[...])) + rb_ref[...]
    lane = lax.broadcasted_iota(jnp.int32, (n, LANES), 1)
    lane_f = lane.astype(F32)
    is_group = (lane >= N_EXPERTS) & (lane < N_EXPERTS + N_GROUPS)
    lg = jnp.where(is_group, logits, NEG_BIG)
    g_max = jnp.max(lg, axis=-1, keepdims=True)
    g_idx = jnp.min(jnp.where(lg == g_max, lane_f, 1e9), axis=-1, keepdims=True) - N_EXPERTS
    pg = 1.0 / jnp.sum(jnp.exp(lg - g_max), axis=-1, keepdims=True)
    in_group = (lane < N_EXPERTS) & ((lane >> (EXPERTS_PER_GROUP.bit_length() - 1)).astype(F32) == g_idx)
    le = jnp.where(in_group, logits, NEG_BIG)
    m1 = jnp.max(le, axis=-1, keepdims=True)
    i1 = jnp.min(jnp.where(le == m1, lane_f, 1e9), axis=-1, keepdims=True)
    le2 = jnp.where(lane_f == i1, NEG_BIG, le)
    m2 = jnp.max(le2, axis=-1, keepdims=True)
    i2 = jnp.min(jnp.where(le2 == m2, lane_f, 1e9), axis=-1, keepdims=True)
    e2 = jnp.exp(m2 - m1)
    w1 = pg / (1.0 + e2)
    wf = jnp.where(lane_f == i1, w1, 0.0) + jnp.where(lane_f == i2, w1 * e2, 0.0)
    wf = wf + jnp.where(lane == _E1_LANE, i1, 0.0) + jnp.where(lane == _E2_LANE, i2, 0.0)
    wf_ref[...] = wf.reshape(bt, tt, LANES)


def _attn(x3, oh3, or3, p3, mk, mv, wout, nxg, wq, wo, nfg, wrh, wrl, rb, bt, tt, gate_block0):
    bsz, tlen, d = x3.shape
    kern = functools.partial(_attn_kernel, bt=bt, tt=tt)
    tok = lambda cb: pl.BlockSpec((bt, tt, d), lambda b, t: (b, t, cb))
    mem = pl.BlockSpec((bt,) + mk.shape[1:], lambda b, t: (b,) + (0,) * (mk.ndim - 1))
    full = lambda a: pl.BlockSpec(a.shape, lambda b, t: (0,) * a.ndim)
    nxg, nfg = nxg.reshape(1, d), nfg.reshape(1, d)
    return pl.pallas_call(
        kern,
        grid=(bsz // bt, tlen // tt),
        in_specs=[tok(0), tok(0), tok(0), tok(gate_block0), tok(gate_block0 + 1), mem, mem,
                  full(wout), full(nxg), full(wq), full(wo), full(nfg), full(wrh), full(wrl), full(rb)],
        out_specs=[
            pl.BlockSpec((bt, tt, d), lambda b, t: (b, t, 0)),
            pl.BlockSpec((bt, tt, d), lambda b, t: (b, t, 0)),
            pl.BlockSpec((bt, tt, LANES), lambda b, t: (b, t, 0)),
        ],
        out_shape=[
            jax.ShapeDtypeStruct((bsz, tlen, d), F32),
            jax.ShapeDtypeStruct((bsz, tlen, d), BF16),
            jax.ShapeDtypeStruct((bsz, tlen, LANES), F32),
        ],
        scratch_shapes=[pltpu.VMEM((bt * tt, d), BF16)],
        compiler_params=_cparams(("parallel", "parallel")),
        name="attn",
    )(x3, oh3, or3, p3, p3, mk, mv, wout, nxg, wq, wo, nfg, wrh, wrl, rb)


def _segment_dma(src, dst, src_row, dst_row, length, max_len, sem, start):
    size = _SEG
    while size * 2 <= max_len:
        size *= 2
    while size >= _SEG:
        offset = length & (-2 * size)

        @pl.when((length & size) != 0)
        def _(offset=offset, size=size):
            copy = pltpu.make_async_copy(src.at[pl.ds(pl.multiple_of(src_row + offset, _SEG), size)],
                                         dst.at[pl.ds(pl.multiple_of(dst_row + offset, _SEG), size)], sem)
            if start:
                copy.start()
            else:
                copy.wait()

        size //= 2


def _for_each_expert(fn):
    def body(e, carry):
        fn(e)
        return carry
    lax.fori_loop(0, N_EXPERTS, body, 0)


def _segment_meta(meta_ref, nt, e):
    i = pl.program_id(0)
    return tuple(meta_ref[(k * nt + i) * N_EXPERTS + e] for k in range(3))


def _moe_sort_kernel(meta_ref, u_ref, wf_ref, base_ref, xs_ref, ws_ref, su_ref, sw_ref, zu_ref, zw_ref, sem,
                     *, nt, tm, te):
    slots = su_ref.shape[0]
    wft = wf_ref[...].T
    e1, e2 = wft[_E1_LANE:_E1_LANE + 1, :], wft[_E2_LANE:_E2_LANE + 1, :]
    expert = lax.broadcasted_iota(jnp.int32, (LANES, tm), 0).astype(F32)
    member = (expert == e1) | (expert == e2)
    earlier = (lax.broadcasted_iota(jnp.int32, (tm, tm), 0)
               < lax.broadcasted_iota(jnp.int32, (tm, tm), 1)).astype(BF16)
    slot = _dot(member.astype(BF16), earlier) + base_ref[...]
    pos1 = jnp.sum(jnp.where(expert == e1, slot, 0.0), axis=0, keepdims=True)
    pos2 = jnp.sum(jnp.where(expert == e2, slot, 0.0), axis=0, keepdims=True)
    slot_id = lax.broadcasted_iota(jnp.int32, (slots, tm), 0).astype(F32)
    perm = ((slot_id == pos1) | (slot_id == pos2)).astype(BF16)
    su_ref[...] = _dot(perm, u_ref[...]).astype(BF16)
    w_hi, w_lo = _split(wf_ref[...])
    sw_ref[...] = _dot(perm, w_hi) + _dot(perm, w_lo)

    def copies(start):
        def one(e):
            off, cnt, base = _segment_meta(meta_ref, nt, e)
            _segment_dma(su_ref, xs_ref, base, off, cnt, tm, sem.at[0], start)
            _segment_dma(sw_ref, ws_ref, base, off, cnt, tm, sem.at[1], start)
        return one

    _for_each_expert(copies(True))
    _for_each_expert(copies(False))

    @pl.when(pl.program_id(0) == nt - 1)
    def _():
        zu_ref[...] = jnp.zeros_like(zu_ref)
        zw_ref[...] = jnp.zeros_like(zw_ref)

        def tails(start):
            def one(e):
                off = meta_ref[3 * nt * N_EXPERTS + nt + e]
                cnt = meta_ref[3 * nt * N_EXPERTS + nt + N_EXPERTS + e]
                _segment_dma(zu_ref, xs_ref, 0, off, cnt, te, sem.at[0], start)
                _segment_dma(zw_ref, ws_ref, 0, off, cnt, te, sem.at[1], start)
            return one

        _for_each_expert(tails(True))
        _for_each_expert(tails(False))

        def unused(start):
            def one(r, carry):
                row = pl.multiple_of(r * te, te)
                for src, dst, s in ((zu_ref, xs_ref, sem.at[0]), (zw_ref, ws_ref, sem.at[1])):
                    copy = pltpu.make_async_copy(src, dst.at[pl.ds(row, te)], s)
                    if start:
                        copy.start()
                    else:
                        copy.wait()
                return carry
            lax.fori_loop(meta_ref[3 * nt * N_EXPERTS + nt + 2 * N_EXPERTS], xs_ref.shape[0] // te, one, 0)

        unused(True)
        unused(False)


def _moe_expert_kernel(meta_ref, xs_ref, ws_ref, wg_ref, wu_ref, wd_ref, ys_ref, *, n_tiles):
    r = pl.program_id(0)
    used = r < meta_ref[2 * n_tiles]

    @pl.when(used)
    def _():
        x = xs_ref[...]
        ws = ws_ref[...]
        lane = lax.broadcasted_iota(jnp.int32, ws.shape, 1)
        w_col = jnp.sum(jnp.where(lane == meta_ref[r], ws, 0.0), axis=-1, keepdims=True)
        hid = _silu(_dot(x, wg_ref[...])) * _dot(x, wu_ref[...])
        ys_ref[...] = _dot((hid * w_col).astype(BF16), wd_ref[...]).astype(ys_ref.dtype)

    @pl.when(jnp.logical_not(used))
    def _():
        ys_ref[...] = jnp.zeros_like(ys_ref)


def _moe_unsort_kernel(meta_ref, wf_ref, h_ref, base_ref, gf_ref, ys_ref, y_ref, sy_ref, sem, *, nt, tm):
    filled = meta_ref[3 * nt * N_EXPERTS + pl.program_id(0)]
    slots = sy_ref.shape[0]

    def copies(start):
        def one(e):
            off, cnt, base = _segment_meta(meta_ref, nt, e)
            _segment_dma(ys_ref, sy_ref, off, base, cnt, tm, sem.at[0], start)
        return one

    _for_each_expert(copies(True))
    wf = wf_ref[...]
    e1, e2 = wf[:, _E1_LANE:_E1_LANE + 1], wf[:, _E2_LANE:_E2_LANE + 1]
    expert = lax.broadcasted_iota(jnp.int32, (tm, LANES), 1).astype(F32)
    member = (expert == e1) | (expert == e2)
    earlier = (lax.broadcasted_iota(jnp.int32, (tm, tm), 1)
               < lax.broadcasted_iota(jnp.int32, (tm, tm), 0)).astype(BF16)
    slot = _dot(earlier, member.astype(BF16)) + base_ref[...]
    pos1 = jnp.sum(jnp.where(expert == e1, slot, 0.0), axis=1, keepdims=True)
    pos2 = jnp.sum(jnp.where(expert == e2, slot, 0.0), axis=1, keepdims=True)
    slot_id = lax.broadcasted_iota(jnp.int32, (tm, slots), 1).astype(F32)
    perm_t = ((slot_id == pos1) | (slot_id == pos2)).astype(BF16)
    _for_each_expert(copies(False))
    slot_row = lax.broadcasted_iota(jnp.int32, (slots, 1), 0)
    ys = sy_ref[...]
    moe = _dot(perm_t, jnp.where(slot_row < filled, ys, jnp.zeros_like(ys)))
    y_ref[...] = _rmsnorm(h_ref[...] + moe, gf_ref[...])


def _moe(u, wf, h, wg, wu, wd, gf, tm, te):
    n, d = u.shape
    ne, _, f = wg.shape
    nt = n // tm
    slots = 2 * tm + ne * _SEG
    n_tiles = -(-(2 * n + nt * ne * (_SEG - 1) + ne * (te - 1)) // te)
    rows = n_tiles * te
    i32 = jnp.int32

    experts = jnp.arange(ne, dtype=i32)
    e1 = wf[:, _E1_LANE].astype(i32).reshape(nt, tm, 1)
    e2 = wf[:, _E2_LANE].astype(i32).reshape(nt, tm, 1)
    cnt = jnp.sum(((e1 == experts) | (e2 == experts)).astype(i32), axis=1)
    cnt = (cnt + _SEG - 1) // _SEG * _SEG
    base = jnp.cumsum(cnt, axis=1) - cnt
    total = jnp.sum(cnt, axis=0)
    region = (total + te - 1) // te * te
    region_end = jnp.cumsum(region)
    region_start = region_end - region
    off = region_start[None, :] + jnp.cumsum(cnt, axis=0) - cnt
    n_used = region_end[-1] // te
    meta = jnp.concatenate([off.reshape(-1), cnt.reshape(-1), base.reshape(-1), jnp.sum(cnt, axis=1),
                            region_start + total, region - total, n_used[None]]).astype(i32)
    tile_row = jnp.arange(n_tiles, dtype=i32) * te
    tile_expert = jnp.minimum(jnp.sum((tile_row[:, None] >= region_end[None, :]).astype(i32), axis=1), ne - 1)
    tile_src = jnp.minimum(jnp.arange(n_tiles, dtype=i32), n_used - 1)
    emeta = jnp.concatenate([tile_expert, tile_src, n_used[None]]).astype(i32)
    base_f = jnp.pad(base.astype(F32), ((0, 0), (0, LANES - ne)))
    base_col, base_row = base_f.reshape(nt, LANES, 1), base_f.reshape(nt, 1, LANES)

    anyspec = pl.BlockSpec(memory_space=pl.ANY)
    tok = lambda w: pl.BlockSpec((tm, w), lambda i, m: (i, 0))
    xs, ws = pl.pallas_call(
        functools.partial(_moe_sort_kernel, nt=nt, tm=tm, te=te),
        grid_spec=pltpu.PrefetchScalarGridSpec(
            num_scalar_prefetch=1, grid=(nt,),
            in_specs=[tok(d), tok(LANES), pl.BlockSpec((None, LANES, 1), lambda i, m: (i, 0, 0))],
            out_specs=[anyspec, anyspec],
            scratch_shapes=[pltpu.VMEM((slots, d), BF16), pltpu.VMEM((slots, LANES), F32),
                            pltpu.VMEM((te, d), BF16), pltpu.VMEM((te, LANES), F32),
                            pltpu.SemaphoreType.DMA((2,))]),
        out_shape=[jax.ShapeDtypeStruct((rows, d), BF16), jax.ShapeDtypeStruct((rows, LANES), F32)],
        compiler_params=_cparams(("arbitrary",)),
        name="moe_sort",
    )(meta, u, wf, base_col)

    srt = lambda w: pl.BlockSpec((te, w), lambda r, m: (m[n_tiles + r], 0))
    ys = pl.pallas_call(
        functools.partial(_moe_expert_kernel, n_tiles=n_tiles),
        grid_spec=pltpu.PrefetchScalarGridSpec(
            num_scalar_prefetch=1, grid=(n_tiles,),
            in_specs=[srt(d), srt(LANES),
                      pl.BlockSpec((None, d, f), lambda r, m: (m[r], 0, 0)),
                      pl.BlockSpec((None, d, f), lambda r, m: (m[r], 0, 0)),
                      pl.BlockSpec((None, f, d), lambda r, m: (m[r], 0, 0))],
            out_specs=pl.BlockSpec((te, d), lambda r, m: (r, 0))),
        out_shape=jax.ShapeDtypeStruct((rows, d), BF16),
        compiler_params=_cparams(("parallel",)),
        name="moe_expert",
    )(emeta, xs, ws, wg, wu, wd)

    return pl.pallas_call(
        functools.partial(_moe_unsort_kernel, nt=nt, tm=tm),
        grid_spec=pltpu.PrefetchScalarGridSpec(
            num_scalar_prefetch=1, grid=(nt,),
            in_specs=[tok(LANES), tok(d), pl.BlockSpec((None, 1, LANES), lambda i, m: (i, 0, 0)),
                      pl.BlockSpec((1, d), lambda i, m: (0, 0)), anyspec],
            out_specs=tok(d),
            scratch_shapes=[pltpu.VMEM((slots, d), BF16), pltpu.SemaphoreType.DMA((1,))]),
        out_shape=jax.ShapeDtypeStruct((n, d), F32),
        compiler_params=_cparams(("arbitrary",)),
        name="moe_unsort",
    )(meta, wf, h, base_row, gf.reshape(1, d), ys)


def _pick(n, pref):
    t = min(n, pref)
    while n % t:
        t -= 8
    return t


def _block_diag_pairs(s):
    b, h, n, _ = s.shape
    s = s.reshape(b, h // 2, 2, n, n)
    z = jnp.zeros_like(s[:, :, 0])
    top = jnp.concatenate([s[:, :, 0], z], axis=-1)
    bot = jnp.concatenate([z, s[:, :, 1]], axis=-1)
    return jnp.concatenate([top, bot], axis=-2)


def _pairs_to_heads(sbd):
    b, p, n2, _ = sbd.shape
    n = n2 // 2
    s = jnp.stack([sbd[:, :, :n, :n], sbd[:, :, n:, n:]], axis=2)
    return s.reshape(b, 2 * p, n, n)


def kernel(x_prompt, x_sample, mem_prompt, cache_mem_k, cache_mem_v, state_hgrn, state_rwkv, state_rwkv_shift, norm_mix_g, w_in, hgrn_lower_bounds, hgrn_norm_g, rwkv_mu, rwkv_w0, rwkv_w2, rwkv_a0, rwkv_a2, rwkv_g2, rwkv_k_k, rwkv_k_a, rwkv_r_k, rwkv_ln_g, rwkv_ln_b, w_out, norm_x_g, norm_mem_g, wq_x, wk_x, wv_x, wo_x, norm_ffn_g, router_group_w, router_group_b, router_expert_w, router_expert_b, expert_w_gate, expert_w_up, expert_w_down, norm_final_g):
    assert w_in.shape[0] == 1, "single-layer configuration"
    d = x_prompt.shape[-1]
    n_lora_w, n_lora_a, n_lora_g = rwkv_w2.shape[1], rwkv_a2.shape[1], rwkv_g2.shape[1]
    assert n_lora_w + n_lora_a == LANES and LANES < n_lora_g <= 2 * LANES
    n_lora = n_lora_w + n_lora_a + n_lora_g
    n_rwkv_in = 3 * d + n_lora
    lora_pad = 3 * LANES - n_lora

    w = w_in[0]
    pr0 = 4 * d
    gate0 = pr0 + n_rwkv_in
    w_perm = jnp.concatenate(
        [w[:, :pr0 + 3 * d], w[:, gate0:gate0 + 2 * d], w[:, pr0 + 3 * d:gate0],
         jnp.zeros((d, lora_pad), w.dtype)], axis=1).astype(BF16)
    n_cols = w_perm.shape[1]
    lora_col0 = 9 * d
    gate_block0 = 7
    mu = rwkv_mu[0]
    mul = jnp.pad(mu[3 * d:], (0, lora_pad)).reshape(1, 3 * LANES)
    par = jnp.stack([mu[:d], mu[d:2 * d], mu[2 * d:3 * d], rwkv_w0[0], rwkv_a0[0], rwkv_k_k[0], rwkv_k_a[0],
                     rwkv_r_k[0].reshape(d), rwkv_ln_g[0], rwkv_ln_b[0]])
    par = jnp.pad(par, ((0, _P_ROWS - par.shape[0]), (0, 0)))
    w2p = jnp.pad(rwkv_w2[0], ((0, n_lora_a), (0, 0))).astype(BF16)
    a2p = jnp.pad(rwkv_a2[0], ((n_lora_w, 0), (0, 0))).astype(BF16)
    g2a = rwkv_g2[0][:LANES].astype(BF16)
    g2b = jnp.pad(rwkv_g2[0][LANES:], ((0, 2 * LANES - n_lora_g), (0, 0))).astype(BF16)
    wout_b, wq_b, wo_b = w_out[0].astype(BF16), wq_x[0].astype(BF16), wo_x[0].astype(BF16)
    wr = jnp.concatenate([router_expert_w[0], router_group_w[0],
                          jnp.zeros((d, LANES - N_EXPERTS - N_GROUPS), F32)], axis=1)
    wr_hi, wr_lo = _split(wr)
    rb = jnp.pad(jnp.concatenate([router_expert_b[0].reshape(-1), router_group_b[0]]),
                 (0, LANES - N_EXPERTS - N_GROUPS)).reshape(1, LANES)
    wg_b, wu_b, wd_b = (expert_w_gate[0].astype(BF16), expert_w_up[0].astype(BF16),
                        expert_w_down[0].astype(BF16))

    nb_p = mem_prompt.shape[0]
    mem_k_p, mem_v_p = _mem_kv(mem_prompt, norm_mem_g[0], wk_x[0].astype(BF16), wv_x[0].astype(BF16))

    def group(x3, mem_k, mem_v, s_h, s_r, buf, tile_h, chunk_r, bt, tt, te):
        bsz, tlen, _ = x3.shape
        n = bsz * tlen
        p = _norm_matmul(x3.reshape(n, d), norm_mix_g[0], w_perm, _pick(n, 1024), n_cols // 5)
        p3 = p.reshape(bsz, tlen, n_cols)
        o_h, s_h_new = _hgrn(p3, hgrn_lower_bounds, hgrn_norm_g[0], s_h, tile_h, 32)
        buf_rkv = buf[:, :, :3 * d]
        buf_l = jnp.pad(buf[:, :, 3 * d:], ((0, 0), (0, 0), (0, lora_pad)))
        o_r, s_r_bd = _rwkv(p3, buf_rkv, buf_l, par, mul, w2p, a2p, g2a, g2b, _block_diag_pairs(s_r), d, chunk_r)
        h2, u, wf = _attn(x3, o_h, o_r, p3, mem_k, mem_v, wout_b, norm_x_g[0], wq_b, wo_b, norm_ffn_g[0],
                          wr_hi, wr_lo, rb, bt, tt, gate_block0)
        y = _moe(u.reshape(n, d), wf.reshape(n, LANES), h2.reshape(n, d), wg_b, wu_b, wd_b, norm_final_g,
                 _pick(n, 512), te)
        shift = jnp.concatenate([p3[:, -1:, pr0:pr0 + 3 * d], p3[:, -1:, lora_col0:lora_col0 + n_lora]], axis=-1)
        return y.reshape(bsz, tlen, d), s_h_new, _pairs_to_heads(s_r_bd), shift

    nb_s, t_s, _ = x_sample.shape
    t_p = x_prompt.shape[1]
    zeros_like_state = lambda s: jnp.zeros((nb_p,) + s.shape[2:], s.dtype)
    y_p, sh_p, sr_p, bf_p = group(
        x_prompt, mem_k_p, mem_v_p, zeros_like_state(state_hgrn), zeros_like_state(state_rwkv),
        zeros_like_state(state_rwkv_shift), _pick(t_p, 256), 64, 1, _pick(t_p, 256), 256)
    y_s, sh_s, sr_s, bf_s = group(
        x_sample, cache_mem_k[0], cache_mem_v[0],
        state_hgrn[0], state_rwkv[0], state_rwkv_shift[0], t_s, t_s, _pick(nb_s, 4), t_s, 128)

    kv_shape = (1,) + mem_k_p.shape[:2] + cache_mem_k.shape[3:]
    return (y_p, y_s, mem_k_p.reshape(kv_shape), mem_v_p.reshape(kv_shape), sh_p[None], sh_s[None],
            sr_p[None], sr_s[None], bf_p[None], bf_s[None])
```

```python
import functools
import math

import jax
import jax.numpy as jnp
from jax import lax
from jax.experimental import pallas as pl
from jax.experimental.pallas import tpu as pltpu

F32 = jnp.float32
BF16 = jnp.bfloat16

NORM_EPS = 1e-6
RWKV_GN_EPS = 64e-5
LANES = 128
HGRN_HEAD = 128
RWKV_HEAD = 64
X_HEADS = 4
N_GROUPS = 4
EXPERTS_PER_GROUP = 8
N_EXPERTS = N_GROUPS * EXPERTS_PER_GROUP
NEG_BIG = -1e30
VMEM_LIMIT = 52 * 1024 * 1024
_E1_LANE, _E2_LANE = N_EXPERTS, N_EXPERTS + 1
_SEG = 16


def _dot(a, b):
    return jnp.dot(a, b, preferred_element_type=F32)


def _dot_nt(a, b):
    return lax.dot_general(a, b, (((1,), (1,)), ((), ())), preferred_element_type=F32)


def _dot_tn(a, b):
    return lax.dot_general(a, b, (((0,), (0,)), ((), ())), preferred_element_type=F32)


def _split(x):
    hi = x.astype(BF16)
    lo = (x - hi.astype(F32)).astype(BF16)
    return hi, lo


def _sigmoid(x):
    return 1.0 / (1.0 + jnp.exp(-x))


def _silu(x):
    return x * _sigmoid(x)


def _rmsnorm(x, g):
    return x * lax.rsqrt(jnp.mean(x * x, axis=-1, keepdims=True) + NORM_EPS) * g


def _cparams(sem):
    return pltpu.CompilerParams(dimension_semantics=sem, vmem_limit_bytes=VMEM_LIMIT)


def _norm_matmul_kernel(x_ref, g_ref, w_ref, o_ref, u_ref):
    @pl.when(pl.program_id(1) == 0)
    def _():
        u_ref[...] = _rmsnorm(x_ref[...], g_ref[...]).astype(BF16)

    o_ref[...] = _dot(u_ref[...], w_ref[...])


def _norm_matmul(x, g, w, tm, tn):
    n, d = x.shape
    m = w.shape[1]
    return pl.pallas_call(
        _norm_matmul_kernel,
        grid=(n // tm, m // tn),
        in_specs=[
            pl.BlockSpec((tm, d), lambda i, j: (i, 0)),
            pl.BlockSpec((1, d), lambda i, j: (0, 0)),
            pl.BlockSpec((d, tn), lambda i, j: (0, j)),
        ],
        out_specs=pl.BlockSpec((tm, tn), lambda i, j: (i, j)),
        out_shape=jax.ShapeDtypeStruct((n, m), F32),
        scratch_shapes=[pltpu.VMEM((tm, d), BF16)],
        compiler_params=_cparams(("parallel", "arbitrary")),
        name="norm_matmul",
    )(x, g.reshape(1, d), w)


def _mem_kv_kernel(m_ref, g_ref, wk_ref, wv_ref, k_ref, v_ref):
    u = _rmsnorm(m_ref[...], g_ref[...]).astype(BF16)
    k_ref[...] = _dot(u, wk_ref[...])
    v_ref[...] = _dot(u, wv_ref[...])


def _mem_kv(mem, g, wk, wv):
    bsz, nmem, d = mem.shape
    full = lambda a: pl.BlockSpec(a.shape, lambda b: (0,) * a.ndim)
    out = pl.BlockSpec((None, nmem, d), lambda b: (b, 0, 0))
    g = g.reshape(1, d)
    return pl.pallas_call(
        _mem_kv_kernel,
        grid=(bsz,),
        in_specs=[pl.BlockSpec((None, nmem, d), lambda b: (b, 0, 0)), full(g), full(wk), full(wv)],
        out_specs=[out, out],
        out_shape=[jax.ShapeDtypeStruct((bsz, nmem, d), F32)] * 2,
        compiler_params=_cparams(("parallel",)),
        name="mem_kv",
    )(mem, g, wk, wv)


def _hgrn_kernel(q_ref, f_ref, i_ref, g_ref, lbp_ref, ng_ref, s0_ref, o_ref, sout_ref, st_ref,
                 *, chunk, n_chunks, heads):
    t = pl.program_id(1)

    @pl.when(t == 0)
    def _():
        for h in range(heads):
            st_ref[h] = s0_ref[h].T

    lbp = lbp_ref[...]
    e = jnp.exp(lbp - jnp.max(lbp, axis=0, keepdims=True))
    lb = e[0:1] / jnp.sum(e, axis=0, keepdims=True)
    ng = ng_ref[...]

    row = lax.broadcasted_iota(jnp.int32, (chunk, chunk), 0)
    col = lax.broadcasted_iota(jnp.int32, (chunk, chunk), 1)
    causal = row >= col
    mid = chunk // 2
    to_mid = jnp.where((col >= mid) & (col <= row), 1.0, 0.0) - jnp.where((col > row) & (col < mid), 1.0, 0.0)
    coef = jnp.concatenate([causal.astype(F32), to_mid, (col > row).astype(F32)], axis=0).astype(BF16)

    chunks = range(n_chunks)
    qd, kd, qi, kl, el, vv, gate = [], [], [], [], [], [], []
    for c in chunks:
        rows = slice(c * chunk, (c + 1) * chunk)
        f = lb + (1.0 - lb) * _sigmoid(f_ref[rows, :])
        g_hi, g_lo = _split(jnp.log(f))
        bb = _dot(coef, g_hi) + _dot(coef, g_lo)
        b, b_to_mid, b_to_last = bb[:chunk], bb[chunk:2 * chunk], bb[2 * chunk:]
        q = _silu(q_ref[rows, :])
        k = 1.0 - f
        qd.append((q * jnp.exp(b_to_mid)).astype(BF16))
        kd.append((k * jnp.exp(-b_to_mid)).astype(BF16))
        qi.append((q * jnp.exp(b)).astype(BF16))
        kl.append((k * jnp.exp(b_to_last)).astype(BF16))
        el.append(jnp.exp(b[chunk - 1:chunk, :]))
        vv.append(i_ref[rows, :].astype(BF16))
        gate.append(_silu(g_ref[rows, :]))

    head = lambda x, h: x[:, h * HGRN_HEAD:(h + 1) * HGRN_HEAD]
    items = [(c, h) for c in chunks for h in range(heads)]
    sc = {ch: jnp.where(causal, _dot_nt(head(qd[ch[0]], ch[1]), head(kd[ch[0]], ch[1])), 0.0).astype(BF16)
          for ch in items}
    intra = {ch: _dot(sc[ch], head(vv[ch[0]], ch[1])) for ch in items}
    upd = {ch: _dot_tn(head(vv[ch[0]], ch[1]), head(kl[ch[0]], ch[1])) for ch in items}
    for h in range(heads):
        st = st_ref[h]
        for c in chunks:
            o = intra[c, h] + _dot_nt(head(qi[c], h), st.astype(BF16))
            st = st * head(el[c], h) + upd[c, h]
            o = o * lax.rsqrt(jnp.mean(o * o, axis=-1, keepdims=True) + NORM_EPS)
            o_ref[c * chunk:(c + 1) * chunk, h * HGRN_HEAD:(h + 1) * HGRN_HEAD] = o * ng * head(gate[c], h)
        st_ref[h] = st

    @pl.when(t == pl.num_programs(1) - 1)
    def _():
        for h in range(heads):
            sout_ref[h] = st_ref[h].T


def _hgrn(p3, lbp, ng, s0, tile, chunk):
    bsz, tlen, _ = p3.shape
    heads, dk, dv = s0.shape[1:]
    d = heads * HGRN_HEAD
    kern = functools.partial(_hgrn_kernel, chunk=chunk, n_chunks=tile // chunk, heads=heads)
    pspec = lambda cb: pl.BlockSpec((None, tile, d), lambda b, t: (b, t, cb))
    return pl.pallas_call(
        kern,
        grid=(bsz, tlen // tile),
        in_specs=[
            pspec(0), pspec(1), pspec(2), pspec(3),
            pl.BlockSpec(lbp.shape, lambda b, t: (0, 0)),
            pl.BlockSpec((1, dv), lambda b, t: (0, 0)),
            pl.BlockSpec((None, heads, dk, dv), lambda b, t: (b, 0, 0, 0)),
        ],
        out_specs=[
            pl.BlockSpec((None, tile, d), lambda b, t: (b, t, 0)),
            pl.BlockSpec((None, heads, dk, dv), lambda b, t: (b, 0, 0, 0)),
        ],
        out_shape=[
            jax.ShapeDtypeStruct((bsz, tlen, d), F32),
            jax.ShapeDtypeStruct((bsz, heads, dk, dv), F32),
        ],
        scratch_shapes=[pltpu.VMEM((heads, dv, dk), F32)],
        compiler_params=_cparams(("parallel", "arbitrary")),
        name="hgrn",
    )(p3, p3, p3, p3, lbp, ng.reshape(1, dv), s0)


_P_MU_R, _P_MU_K, _P_MU_V, _P_W0, _P_A0, _P_KK, _P_KA, _P_RK, _P_LNG, _P_LNB = range(10)
_P_ROWS = 16


def _rwkv_kernel(r_ref, k_ref, v_ref, l_ref, br_ref, bk_ref, bv_ref, bl_ref, par_ref, mul_ref,
                 w2_ref, a2_ref, g2a_ref, g2b_ref, s0_ref,
                 o_ref, sout_ref, st_ref, prev_ref, prevl_ref, *, chunk, pairs):
    t = pl.program_id(1)

    @pl.when(t == 0)
    def _():
        st_ref[...] = s0_ref[...]
        prev_ref[0:1, :] = br_ref[...]
        prev_ref[1:2, :] = bk_ref[...]
        prev_ref[2:3, :] = bv_ref[...]
        prevl_ref[0:1, :] = bl_ref[...]

    par = par_ref[...]
    prow = lambda i: par[i:i + 1, :]

    def shifted(x, prev_row, mu):
        row_id = lax.broadcasted_iota(jnp.int32, x.shape, 0)
        prev = jnp.where(row_id == 0, prev_row, pltpu.roll(x, 1, 0))
        return x + (prev - x) * mu

    pr, pk, pv, plr = r_ref[...], k_ref[...], v_ref[...], l_ref[...]
    r = shifted(pr, prev_ref[0:1, :], prow(_P_MU_R))
    k = shifted(pk, prev_ref[1:2, :], prow(_P_MU_K))
    v = shifted(pv, prev_ref[2:3, :], prow(_P_MU_V))
    xl = shifted(plr, prevl_ref[0:1, :], mul_ref[...])
    for slot, x in enumerate((pr, pk, pv)):
        prev_ref[slot:slot + 1, :] = x[chunk - 1:chunk, :]
    prevl_ref[0:1, :] = plr[chunk - 1:chunk, :]
    x0 = xl[:, 0:LANES]
    x1 = xl[:, LANES:2 * LANES]
    x2 = xl[:, 2 * LANES:3 * LANES]

    w_lin = prow(_P_W0) + _dot(jnp.tanh(x0).astype(BF16), w2_ref[...])
    z = -w_lin
    softplus = jnp.maximum(z, 0.0) + jnp.log(1.0 + jnp.exp(-jnp.abs(z)))
    lw = -jnp.exp(-softplus - 0.5)
    a_icl = _sigmoid(prow(_P_A0) + _dot(x0.astype(BF16), a2_ref[...]))
    g_out = _dot(_sigmoid(x1).astype(BF16), g2a_ref[...]) + _dot(_sigmoid(x2).astype(BF16), g2b_ref[...])

    lane = lax.broadcasted_iota(jnp.int32, (1, LANES), 1)
    m0 = (lane < RWKV_HEAD).astype(F32)
    m1 = 1.0 - m0
    head_shift = RWKV_HEAD.bit_length() - 1
    hr = lax.broadcasted_iota(jnp.int32, (LANES, LANES), 0) >> head_shift
    hc = lax.broadcasted_iota(jnp.int32, (LANES, LANES), 1) >> head_shift
    head_ones = (hr == hc).astype(BF16)
    tile_of = lambda x, j: x[:, j * LANES:(j + 1) * LANES]

    def headsum(x, split=False):
        xs = jnp.concatenate([tile_of(x, j) for j in range(pairs)], axis=0)
        if split:
            hi, lo = _split(xs)
            ys = _dot(hi, head_ones) + _dot(lo, head_ones)
        else:
            ys = _dot(xs.astype(BF16), head_ones)
        return jnp.concatenate([ys[j * chunk:(j + 1) * chunk] for j in range(pairs)], axis=1)

    kk = k * prow(_P_KK)
    kk = kk / jnp.maximum(jnp.sqrt(headsum(kk * kk, split=True)), 1e-12)
    k_mod = k * (1.0 + (a_icl - 1.0) * prow(_P_KA))
    a = -kk
    b = kk * a_icl

    ti = lax.broadcasted_iota(jnp.int32, (chunk, chunk), 0)
    tj = lax.broadcasted_iota(jnp.int32, (chunk, chunk), 1)
    tri = (ti >= tj).astype(BF16)
    lw_hi, lw_lo = _split(lw)
    cum = _dot(tri, lw_hi) + _dot(tri, lw_lo)
    cum_last = cum[chunk - 1:chunk, :]
    e_neg = jnp.exp(-cum)
    e_end = jnp.exp(cum_last - cum)
    a_t = a * jnp.exp(cum - lw)
    b_t = b * e_neg
    k_t = k_mod * e_neg
    r_t = r * jnp.exp(cum)
    b_end = b * e_end
    k_end = k_mod * e_end
    g_last = jnp.exp(cum_last)

    n2 = 2 * chunk
    si = lax.broadcasted_iota(jnp.int32, (n2, n2), 0)
    sj = lax.broadcasted_iota(jnp.int32, (n2, n2), 1)
    chunk_shift = chunk.bit_length() - 1
    same_head = (si >> chunk_shift) == (sj >> chunk_shift)
    strict = same_head & ((sj & (chunk - 1)) < (si & (chunk - 1)))
    incl = same_head & ((sj & (chunk - 1)) <= (si & (chunk - 1)))
    eye = (si == sj).astype(F32)
    stack = lambda x: jnp.concatenate([x * m0, x * m1], axis=0)
    stack_b = lambda x, j: stack(tile_of(x, j)).astype(BF16)
    every = range(pairs)

    a_st = [stack_b(a_t, j) for j in every]
    r_st = [stack_b(r_t, j) for j in every]
    v_st = [stack_b(v, j) for j in every]
    bend_st = [stack_b(b_end, j) for j in every]
    kend_st = [stack_b(k_end, j) for j in every]
    tile_b = lambda x, j: tile_of(x, j).astype(BF16)
    sc = [_dot_nt(jnp.concatenate([tile_b(a_t, j), tile_b(r_t, j)], axis=0),
                  jnp.concatenate([stack_b(b_t, j), stack_b(k_t, j)], axis=0)) for j in every]
    twice = lambda x: jnp.concatenate([x, x], axis=0)
    a_ab = [jnp.where(strict, twice(s[:chunk, :n2]), 0.0) for s in sc]
    a_xk = [jnp.concatenate([jnp.where(strict, twice(s[:chunk, n2:]), 0.0),
                             jnp.where(incl, twice(s[chunk:, n2:]), 0.0)], axis=0).astype(BF16) for s in sc]
    a_rb = [jnp.where(incl, twice(s[chunk:, :n2]), 0.0).astype(BF16) for s in sc]

    inv = [eye + x for x in a_ab]
    pw = [_dot(x.astype(BF16), x.astype(BF16)) for x in a_ab]
    for _ in range(int(math.log2(chunk)) - 2):
        if n2 % LANES == 0:
            both = [_dot(p.astype(BF16), jnp.concatenate([p, i], axis=1).astype(BF16)) for p, i in zip(pw, inv)]
            pw_next = [x[:, :n2] for x in both]
            inv = [i + x[:, n2:] for i, x in zip(inv, both)]
        else:
            pw_next = [_dot(p.astype(BF16), p.astype(BF16)) for p in pw]
            inv = [i + _dot(p.astype(BF16), i.astype(BF16)) for i, p in zip(inv, pw)]
        pw = pw_next
    inv = [i + _dot(p.astype(BF16), i.astype(BF16)) for i, p in zip(inv, pw)]

    av = [_dot(a_xk[j], v_st[j]) for j in every]
    wu = [_dot(inv[j].astype(BF16),
               jnp.concatenate([a_st[j], av[j][:n2].astype(BF16)], axis=1)).astype(BF16) for j in every]
    qo = [_dot(a_rb[j], wu[j]) + jnp.concatenate([r_st[j].astype(F32), av[j][n2:]], axis=1) for j in every]
    mn = [_dot_tn(wu[j], bend_st[j]) for j in every]
    nn = [mn[j][LANES:] + _dot_tn(v_st[j], kend_st[j]) for j in every]

    outs = []
    for j in every:
        s0 = st_ref[j]
        s0_b = s0.astype(BF16)
        q = (qo[j][:chunk, :LANES] + qo[j][chunk:, :LANES]).astype(BF16)
        outs.append(_dot_nt(q, s0_b) + qo[j][:chunk, LANES:] + qo[j][chunk:, LANES:])
        st_ref[j] = s0 * tile_of(g_last, j) + _dot(s0_b, mn[j][:LANES].astype(BF16)) + nn[j]
    o = jnp.concatenate(outs, axis=1)

    inv_n = 1.0 / RWKV_HEAD
    mean = headsum(o) * inv_n
    dlt = o - mean
    var = headsum(dlt * dlt) * inv_n
    o = dlt * lax.rsqrt(var + RWKV_GN_EPS) * prow(_P_LNG) + prow(_P_LNB)
    bonus = headsum(r * k_mod * prow(_P_RK))
    o_ref[...] = (o + bonus * v) * g_out

    @pl.when(t == pl.num_programs(1) - 1)
    def _():
        sout_ref[...] = st_ref[...]


def _rwkv(p3, buf_rkv, buf_l, par, mul, w2p, a2p, g2a, g2b, s0bd, d, chunk):
    bsz, tlen, _ = p3.shape
    pairs = d // LANES
    nl = 3 * LANES
    kern = functools.partial(_rwkv_kernel, chunk=chunk, pairs=pairs)
    pcol = lambda cb: pl.BlockSpec((None, chunk, d), lambda b, t: (b, t, cb))
    bcol = lambda cb: pl.BlockSpec((None, 1, d), lambda b, t: (b, 0, cb))
    full = lambda a: pl.BlockSpec(a.shape, lambda b, t: (0,) * a.ndim)
    state = pl.BlockSpec((None, pairs, LANES, LANES), lambda b, t: (b, 0, 0, 0))
    return pl.pallas_call(
        kern,
        grid=(bsz, tlen // chunk),
        in_specs=[
            pcol(4), pcol(5), pcol(6),
            pl.BlockSpec((None, chunk, nl), lambda b, t: (b, t, 9 * d // nl)),
            bcol(0), bcol(1), bcol(2),
            pl.BlockSpec((None, 1, nl), lambda b, t: (b, 0, 0)),
            full(par), full(mul), full(w2p), full(a2p), full(g2a), full(g2b), state,
        ],
        out_specs=[pl.BlockSpec((None, chunk, d), lambda b, t: (b, t, 0)), state],
        out_shape=[
            jax.ShapeDtypeStruct((bsz, tlen, d), F32),
            jax.ShapeDtypeStruct((bsz, pairs, LANES, LANES), F32),
        ],
        scratch_shapes=[pltpu.VMEM((pairs, LANES, LANES), F32), pltpu.VMEM((8, d), F32),
                        pltpu.VMEM((8, nl), F32)],
        compiler_params=_cparams(("parallel", "arbitrary")),
        name="rwkv",
    )(p3, p3, p3, p3, buf_rkv, buf_rkv, buf_rkv, buf_l, par, mul, w2p, a2p, g2a, g2b, s0bd)


def _attn_kernel(x_ref, oh_ref, or_ref, ga_ref, gb_ref, mk_ref, mv_ref, wout_ref, nxg_ref, wq_ref, wo_ref,
                 nfg_ref, wrh_ref, wrl_ref, rb_ref, h_ref, u_ref, wf_ref, att_ref, *, bt, tt):
    n = bt * tt
    d = x_ref.shape[-1]
    flat = lambda ref: ref[...].reshape(n, d)
    merged = _sigmoid(flat(ga_ref)) * flat(oh_ref) + _sigmoid(flat(gb_ref)) * flat(or_ref)
    h1 = flat(x_ref) + _dot(merged.astype(BF16), wout_ref[...])

    q = _dot(_rmsnorm(h1, nxg_ref[...]).astype(BF16), wq_ref[...])
    dh = d // X_HEADS
    scale = dh ** -0.5
    for bi in range(bt):
        rows = slice(bi * tt, (bi + 1) * tt)
        for hh in range(X_HEADS):
            cols = slice(hh * dh, (hh + 1) * dh)
            mem_head = lambda ref: (ref[bi, :, cols] if len(ref.shape) == 3 else ref[bi, :, hh, :]).astype(BF16)
            s = _dot_nt(q[rows, cols].astype(BF16), mem_head(mk_ref)) * scale
            p = jnp.exp(s - jnp.max(s, axis=-1, keepdims=True))
            p = p / jnp.sum(p, axis=-1, keepdims=True)
            att_ref[rows, cols] = _dot(p.astype(BF16), mem_head(mv_ref)).astype(BF16)
    h2 = h1 + _dot(att_ref[...], wo_ref[...])
    h_ref[...] = h2.reshape(bt, tt, d)

    u = _rmsnorm(h2, nfg_ref[...])
    u_ref[...] = u.astype(BF16).reshape(bt, tt, d)
    u_hi, u_lo = _split(u)
    logits = (_dot(u_hi, wrh_ref[...]) + _dot(u_hi, wrl_ref[...]) + _dot(u_lo, wrh_ref[...])) + rb_ref[...]
    lane = lax.broadcasted_iota(jnp.int32, (n, LANES), 1)
    lane_f = lane.astype(F32)
    is_group = (lane >= N_EXPERTS) & (lane < N_EXPERTS + N_GROUPS)
    lg = jnp.where(is_group, logits, NEG_BIG)
    g_max = jnp.max(lg, axis=-1, keepdims=True)
    g_idx = jnp.min(jnp.where(lg == g_max, lane_f, 1e9), axis=-1, keepdims=True) - N_EXPERTS
    pg = 1.0 / jnp.sum(jnp.exp(lg - g_max), axis=-1, keepdims=True)
    in_group = (lane < N_EXPERTS) & ((lane >> (EXPERTS_PER_GROUP.bit_length() - 1)).astype(F32) == g_idx)
    le = jnp.where(in_group, logits, NEG_BIG)
    m1 = jnp.max(le, axis=-1, keepdims=True)
    i1 = jnp.min(jnp.where(le == m1, lane_f, 1e9), axis=-1, keepdims=True)
    le2 = jnp.where(lane_f == i1, NEG_BIG, le)
    m2 = jnp.max(le2, axis=-1, keepdims=True)
    i2 = jnp.min(jnp.where(le2 == m2, lane_f, 1e9), axis=-1, keepdims=True)
    e2 = jnp.exp(m2 - m1)
    w1 = pg / (1.0 + e2)
    wf = jnp.where(lane_f == i1, w1, 0.0) + jnp.where(lane_f == i2, w1 * e2, 0.0)
    wf = wf + jnp.where(lane == _E1_LANE, i1, 0.0) + jnp.where(lane == _E2_LANE, i2, 0.0)
    wf_ref[...] = wf.reshape(bt, tt, LANES)


def _attn(x3, oh3, or3, p3, mk, mv, wout, nxg, wq, wo, nfg, wrh, wrl, rb, bt, tt, gate_block0):
    bsz, tlen, d = x3.shape
    kern = functools.partial(_attn_kernel, bt=bt, tt=tt)
    tok = lambda cb: pl.BlockSpec((bt, tt, d), lambda b, t: (b, t, cb))
    mem = pl.BlockSpec((bt,) + mk.shape[1:], lambda b, t: (b,) + (0,) * (mk.ndim - 1))
    full = lambda a: pl.BlockSpec(a.shape, lambda b, t: (0,) * a.ndim)
    nxg, nfg = nxg.reshape(1, d), nfg.reshape(1, d)
    return pl.pallas_call(
        kern,
        grid=(bsz // bt, tlen // tt),
        in_specs=[tok(0), tok(0), tok(0), tok(gate_block0), tok(gate_block0 + 1), mem, mem,
                  full(wout), full(nxg), full(wq), full(wo), full(nfg), full(wrh), full(wrl), full(rb)],
        out_specs=[
            pl.BlockSpec((bt, tt, d), lambda b, t: (b, t, 0)),
            pl.BlockSpec((bt, tt, d), lambda b, t: (b, t, 0)),
            pl.BlockSpec((bt, tt, LANES), lambda b, t: (b, t, 0)),
        ],
        out_shape=[
            jax.ShapeDtypeStruct((bsz, tlen, d), F32),
            jax.ShapeDtypeStruct((bsz, tlen, d), BF16),
            jax.ShapeDtypeStruct((bsz, tlen, LANES), F32),
        ],
        scratch_shapes=[pltpu.VMEM((bt * tt, d), BF16)],
        compiler_params=_cparams(("parallel", "parallel")),
        name="attn",
    )(x3, oh3, or3, p3, p3, mk, mv, wout, nxg, wq, wo, nfg, wrh, wrl, rb)


def _segment_dma(src, dst, src_row, dst_row, length, max_len, sem, start):
    size = _SEG
    while size * 2 <= max_len:
        size *= 2
    while size >= _SEG:
        offset = length & (-2 * size)

        @pl.when((length & size) != 0)
        def _(offset=offset, size=size):
            copy = pltpu.make_async_copy(src.at[pl.ds(pl.multiple_of(src_row + offset, _SEG), size)],
                                         dst.at[pl.ds(pl.multiple_of(dst_row + offset, _SEG), size)], sem)
            if start:
                copy.start()
            else:
                copy.wait()

        size //= 2


def _tile_meta(meta_ref, nt, tile):
    field = lambda k: [meta_ref[(k * nt + tile) * N_GROUPS + g] for g in range(N_GROUPS)]
    return field(0), field(1), field(2)


def _group_of(expert_id):
    return jnp.floor(expert_id * (1.0 / EXPERTS_PER_GROUP))


def _moe_sort_kernel(meta_ref, u_ref, wf_ref, xs_ref, ws_ref, su_ref, sw_ref, zu_ref, zw_ref, sem, *, nt, tm, te):
    i = pl.program_id(0)
    buf = lax.rem(i, 2)
    slots = su_ref.shape[1]
    _, _, base = _tile_meta(meta_ref, nt, i)
    wf = wf_ref[...]
    gid = _group_of(wf.T[_E1_LANE:_E1_LANE + 1, :])
    onehot = lax.broadcasted_iota(jnp.int32, (8, tm), 0).astype(F32) == gid
    earlier = (lax.broadcasted_iota(jnp.int32, (tm, tm), 0)
               < lax.broadcasted_iota(jnp.int32, (tm, tm), 1)).astype(BF16)
    seen = _dot(onehot.astype(BF16), earlier)
    pos = jnp.sum(jnp.where(onehot, seen, 0.0), axis=0, keepdims=True)
    for g in range(N_GROUPS):
        pos = pos + jnp.where(gid == float(g), base[g].astype(F32), 0.0)
    perm = (lax.broadcasted_iota(jnp.int32, (slots, tm), 0).astype(F32) == pos).astype(BF16)
    su_ref[buf] = _dot(perm, u_ref[...]).astype(BF16)
    w_hi, w_lo = _split(wf)
    sw_ref[buf] = _dot(perm, w_hi) + _dot(perm, w_lo)

    def copies(tile, b, start):
        off, cnt, base = _tile_meta(meta_ref, nt, tile)
        for g in range(N_GROUPS):
            _segment_dma(su_ref.at[b], xs_ref, base[g], off[g], cnt[g], tm, sem.at[0, b], start)
            _segment_dma(sw_ref.at[b], ws_ref, base[g], off[g], cnt[g], tm, sem.at[1, b], start)

    copies(i, buf, True)

    @pl.when(i > 0)
    def _():
        copies(i - 1, 1 - buf, False)

    @pl.when(i == nt - 1)
    def _():
        copies(i, buf, False)
        zu_ref[...] = jnp.zeros_like(zu_ref)
        zw_ref[...] = jnp.zeros_like(zw_ref)
        tail = 3 * nt * N_GROUPS + nt
        for start in (True, False):
            for g in range(N_GROUPS):
                off, cnt = meta_ref[tail + g], meta_ref[tail + N_GROUPS + g]
                _segment_dma(zu_ref, xs_ref, 0, off, cnt, te, sem.at[0, 0], start)
                _segment_dma(zw_ref, ws_ref, 0, off, cnt, te, sem.at[1, 0], start)

            def unused(r, carry, start=start):
                row = pl.multiple_of(r * te, te)
                for src, dst, s in ((zu_ref, xs_ref, sem.at[0, 1]), (zw_ref, ws_ref, sem.at[1, 1])):
                    copy = pltpu.make_async_copy(src, dst.at[pl.ds(row, te)], s)
                    if start:
                        copy.start()
                    else:
                        copy.wait()
                return carry

            lax.fori_loop(meta_ref[tail + 2 * N_GROUPS], xs_ref.shape[0] // te, unused, 0)


def _moe_expert_kernel(meta_ref, xs_ref, ws_ref, wg_ref, wu_ref, wd_ref, ys_ref, *, n_tiles):
    r = pl.program_id(0)
    used = r < meta_ref[n_tiles]

    @pl.when(used)
    def _():
        first = meta_ref[r] * EXPERTS_PER_GROUP
        x = xs_ref[...]
        ws = ws_ref[...]
        lane = lax.broadcasted_iota(jnp.int32, ws.shape, 1)
        hid = []
        for e in range(EXPERTS_PER_GROUP):
            h = _silu(_dot(x, wg_ref[e])) * _dot(x, wu_ref[e])
            w_col = jnp.sum(jnp.where(lane == first + e, ws, 0.0), axis=-1, keepdims=True)
            hid.append((h * w_col).astype(BF16))
        ys_ref[...] = _dot(jnp.concatenate(hid, axis=1), wd_ref[...]).astype(ys_ref.dtype)

    @pl.when(jnp.logical_not(used))
    def _():
        ys_ref[...] = jnp.zeros_like(ys_ref)


def _moe_unsort_kernel(meta_ref, wf_ref, h_ref, gf_ref, ys_ref, y_ref, sy_ref, sem, *, nt, tm):
    i = pl.program_id(0)
    buf = lax.rem(i, 2)
    slots = sy_ref.shape[1]

    def copies(tile, b, start):
        off, cnt, base = _tile_meta(meta_ref, nt, tile)
        for g in range(N_GROUPS):
            _segment_dma(ys_ref, sy_ref.at[b], off[g], base[g], cnt[g], tm, sem.at[b], start)

    @pl.when(i == 0)
    def _():
        copies(0, 0, True)

    @pl.when(i + 1 < nt)
    def _():
        copies(i + 1, 1 - buf, True)

    _, _, base = _tile_meta(meta_ref, nt, i)
    gid = _group_of(wf_ref[...][:, _E1_LANE:_E1_LANE + 1])
    onehot = lax.broadcasted_iota(jnp.int32, (tm, LANES), 1).astype(F32) == gid
    earlier = (lax.broadcasted_iota(jnp.int32, (tm, tm), 1)
               < lax.broadcasted_iota(jnp.int32, (tm, tm), 0)).astype(BF16)
    seen = _dot(earlier, onehot.astype(BF16))
    pos = jnp.sum(jnp.where(onehot, seen, 0.0), axis=1, keepdims=True)
    for g in range(N_GROUPS):
        pos = pos + jnp.where(gid == float(g), base[g].astype(F32), 0.0)
    perm_t = (lax.broadcasted_iota(jnp.int32, (tm, slots), 1).astype(F32) == pos).astype(BF16)
    copies(i, buf, False)
    slot_row = lax.broadcasted_iota(jnp.int32, (slots, 1), 0)
    ys = sy_ref[buf]
    moe = _dot(perm_t, jnp.where(slot_row < meta_ref[3 * nt * N_GROUPS + i], ys, jnp.zeros_like(ys)))
    y_ref[...] = _rmsnorm(h_ref[...] + moe, gf_ref[...])


def _moe(u, wf, h, wg, wu, wd, gf, tm, te):
    n, d = u.shape
    ne, _, f = wg.shape
    nt = n // tm
    slots = tm + LANES
    n_tiles = -(-(n + nt * N_GROUPS * (_SEG - 1) + N_GROUPS * (te - 1)) // te)
    rows = n_tiles * te
    i32 = jnp.int32

    gid = (wf[:, _E1_LANE].astype(i32) // EXPERTS_PER_GROUP).reshape(nt, tm)
    cnt = jnp.sum((gid[:, :, None] == jnp.arange(N_GROUPS, dtype=i32)).astype(i32), axis=1)
    cnt = (cnt + _SEG - 1) // _SEG * _SEG
    base = jnp.cumsum(cnt, axis=1) - cnt
    total = jnp.sum(cnt, axis=0)
    region = (total + te - 1) // te * te
    region_end = jnp.cumsum(region)
    region_start = region_end - region
    off = region_start[None, :] + jnp.cumsum(cnt, axis=0) - cnt
    n_used = region_end[-1:] // te
    meta = jnp.concatenate([off.reshape(-1), cnt.reshape(-1), base.reshape(-1), jnp.sum(cnt, axis=1),
                            region_start + total, region - total, n_used]).astype(i32)
    tile_row = jnp.arange(n_tiles, dtype=i32) * te
    tile_group = jnp.minimum(jnp.sum((tile_row[:, None] >= region_end[None, :]).astype(i32), axis=1), N_GROUPS - 1)
    emeta = jnp.concatenate([tile_group, n_used]).astype(i32)

    anyspec = pl.BlockSpec(memory_space=pl.ANY)
    tok = lambda w: pl.BlockSpec((tm, w), lambda i, m: (i, 0))
    xs, ws = pl.pallas_call(
        functools.partial(_moe_sort_kernel, nt=nt, tm=tm, te=te),
        grid_spec=pltpu.PrefetchScalarGridSpec(
            num_scalar_prefetch=1, grid=(nt,),
            in_specs=[tok(d), tok(LANES)],
            out_specs=[anyspec, anyspec],
            scratch_shapes=[pltpu.VMEM((2, slots, d), BF16), pltpu.VMEM((2, slots, LANES), F32),
                            pltpu.VMEM((te, d), BF16), pltpu.VMEM((te, LANES), F32),
                            pltpu.SemaphoreType.DMA((2, 2))]),
        out_shape=[jax.ShapeDtypeStruct((rows, d), BF16), jax.ShapeDtypeStruct((rows, LANES), F32)],
        compiler_params=_cparams(("arbitrary",)),
        name="moe_sort",
    )(meta, u, wf)

    srt = lambda w: pl.BlockSpec((te, w), lambda r, m: (r, 0))
    grp_w = pl.BlockSpec((EXPERTS_PER_GROUP, d, f), lambda r, m: (m[r], 0, 0))
    ys = pl.pallas_call(
        functools.partial(_moe_expert_kernel, n_tiles=n_tiles),
        grid_spec=pltpu.PrefetchScalarGridSpec(
            num_scalar_prefetch=1, grid=(n_tiles,),
            in_specs=[srt(d), srt(LANES), grp_w, grp_w,
                      pl.BlockSpec((EXPERTS_PER_GROUP * f, d), lambda r, m: (m[r], 0))],
            out_specs=srt(d)),
        out_shape=jax.ShapeDtypeStruct((rows, d), BF16),
        compiler_params=_cparams(("parallel",)),
        name="moe_expert",
    )(emeta, xs, ws, wg, wu, wd.reshape(ne * f, d))

    return pl.pallas_call(
        functools.partial(_moe_unsort_kernel, nt=nt, tm=tm),
        grid_spec=pltpu.PrefetchScalarGridSpec(
            num_scalar_prefetch=1, grid=(nt,),
            in_specs=[tok(LANES), tok(d), pl.BlockSpec((1, d), lambda i, m: (0, 0)), anyspec],
            out_specs=tok(d),
            scratch_shapes=[pltpu.VMEM((2, slots, d), BF16), pltpu.SemaphoreType.DMA((2,))]),
        out_shape=jax.ShapeDtypeStruct((n, d), F32),
        compiler_params=_cparams(("arbitrary",)),
        name="moe_unsort",
    )(meta, wf, h, gf.reshape(1, d), ys)


def _pick(n, pref):
    t = min(n, pref)
    while n % t:
        t -= 8
    return t


def _block_diag_pairs(s):
    b, h, n, _ = s.shape
    s = s.reshape(b, h // 2, 2, n, n)
    z = jnp.zeros_like(s[:, :, 0])
    top = jnp.concatenate([s[:, :, 0], z], axis=-1)
    bot = jnp.concatenate([z, s[:, :, 1]], axis=-1)
    return jnp.concatenate([top, bot], axis=-2)


def _pairs_to_heads(sbd):
    b, p, n2, _ = sbd.shape
    n = n2 // 2
    s = jnp.stack([sbd[:, :, :n, :n], sbd[:, :, n:, n:]], axis=2)
    return s.reshape(b, 2 * p, n, n)


def kernel(x_prompt, x_sample, mem_prompt, cache_mem_k, cache_mem_v, state_hgrn, state_rwkv, state_rwkv_shift, norm_mix_g, w_in, hgrn_lower_bounds, hgrn_norm_g, rwkv_mu, rwkv_w0, rwkv_w2, rwkv_a0, rwkv_a2, rwkv_g2, rwkv_k_k, rwkv_k_a, rwkv_r_k, rwkv_ln_g, rwkv_ln_b, w_out, norm_x_g, norm_mem_g, wq_x, wk_x, wv_x, wo_x, norm_ffn_g, router_group_w, router_group_b, router_expert_w, router_expert_b, expert_w_gate, expert_w_up, expert_w_down, norm_final_g):
    assert w_in.shape[0] == 1, "single-layer configuration"
    d = x_prompt.shape[-1]
    n_lora_w, n_lora_a, n_lora_g = rwkv_w2.shape[1], rwkv_a2.shape[1], rwkv_g2.shape[1]
    assert n_lora_w + n_lora_a == LANES and LANES < n_lora_g <= 2 * LANES
    n_lora = n_lora_w + n_lora_a + n_lora_g
    n_rwkv_in = 3 * d + n_lora
    lora_pad = 3 * LANES - n_lora

    w = w_in[0]
    pr0 = 4 * d
    gate0 = pr0 + n_rwkv_in
    w_perm = jnp.concatenate(
        [w[:, :pr0 + 3 * d], w[:, gate0:gate0 + 2 * d], w[:, pr0 + 3 * d:gate0],
         jnp.zeros((d, lora_pad), w.dtype)], axis=1).astype(BF16)
    n_cols = w_perm.shape[1]
    lora_col0 = 9 * d
    gate_block0 = 7
    mu = rwkv_mu[0]
    mul = jnp.pad(mu[3 * d:], (0, lora_pad)).reshape(1, 3 * LANES)
    par = jnp.stack([mu[:d], mu[d:2 * d], mu[2 * d:3 * d], rwkv_w0[0], rwkv_a0[0], rwkv_k_k[0], rwkv_k_a[0],
                     rwkv_r_k[0].reshape(d), rwkv_ln_g[0], rwkv_ln_b[0]])
    par = jnp.pad(par, ((0, _P_ROWS - par.shape[0]), (0, 0)))
    w2p = jnp.pad(rwkv_w2[0], ((0, n_lora_a), (0, 0))).astype(BF16)
    a2p = jnp.pad(rwkv_a2[0], ((n_lora_w, 0), (0, 0))).astype(BF16)
    g2a = rwkv_g2[0][:LANES].astype(BF16)
    g2b = jnp.pad(rwkv_g2[0][LANES:], ((0, 2 * LANES - n_lora_g), (0, 0))).astype(BF16)
    wout_b, wq_b, wo_b = w_out[0].astype(BF16), wq_x[0].astype(BF16), wo_x[0].astype(BF16)
    wr = jnp.concatenate([router_expert_w[0], router_group_w[0],
                          jnp.zeros((d, LANES - N_EXPERTS - N_GROUPS), F32)], axis=1)
    wr_hi, wr_lo = _split(wr)
    rb = jnp.pad(jnp.concatenate([router_expert_b[0].reshape(-1), router_group_b[0]]),
                 (0, LANES - N_EXPERTS - N_GROUPS)).reshape(1, LANES)
    wg_b, wu_b, wd_b = (expert_w_gate[0].astype(BF16), expert_w_up[0].astype(BF16),
                        expert_w_down[0].astype(BF16))

    nb_p = mem_prompt.shape[0]
    mem_k_p, mem_v_p = _mem_kv(mem_prompt, norm_mem_g[0], wk_x[0].astype(BF16), wv_x[0].astype(BF16))

    def group(x3, mem_k, mem_v, s_h, s_r, buf, tile_h, chunk_r, bt, tt, te):
        bsz, tlen, _ = x3.shape
        n = bsz * tlen
        p = _norm_matmul(x3.reshape(n, d), norm_mix_g[0], w_perm, _pick(n, 1024), n_cols // 5)
        p3 = p.reshape(bsz, tlen, n_cols)
        o_h, s_h_new = _hgrn(p3, hgrn_lower_bounds, hgrn_norm_g[0], s_h, tile_h, 32)
        buf_rkv = buf[:, :, :3 * d]
        buf_l = jnp.pad(buf[:, :, 3 * d:], ((0, 0), (0, 0), (0, lora_pad)))
        o_r, s_r_bd = _rwkv(p3, buf_rkv, buf_l, par, mul, w2p, a2p, g2a, g2b, _block_diag_pairs(s_r), d, chunk_r)
        h2, u, wf = _attn(x3, o_h, o_r, p3, mem_k, mem_v, wout_b, norm_x_g[0], wq_b, wo_b, norm_ffn_g[0],
                          wr_hi, wr_lo, rb, bt, tt, gate_block0)
        y = _moe(u.reshape(n, d), wf.reshape(n, LANES), h2.reshape(n, d), wg_b, wu_b, wd_b, norm_final_g,
                 _pick(n, 512), te)
        shift = jnp.concatenate([p3[:, -1:, pr0:pr0 + 3 * d], p3[:, -1:, lora_col0:lora_col0 + n_lora]], axis=-1)
        return y.reshape(bsz, tlen, d), s_h_new, _pairs_to_heads(s_r_bd), shift

    nb_s, t_s, _ = x_sample.shape
    t_p = x_prompt.shape[1]
    zeros_like_state = lambda s: jnp.zeros((nb_p,) + s.shape[2:], s.dtype)
    y_p, sh_p, sr_p, bf_p = group(
        x_prompt, mem_k_p, mem_v_p, zeros_like_state(state_hgrn), zeros_like_state(state_rwkv),
        zeros_like_state(state_rwkv_shift), _pick(t_p, 256), 64, 1, _pick(t_p, 256), 512)
    y_s, sh_s, sr_s, bf_s = group(
        x_sample, cache_mem_k[0], cache_mem_v[0],
        state_hgrn[0], state_rwkv[0], state_rwkv_shift[0], t_s, t_s, _pick(nb_s, 4), t_s, 256)

    kv_shape = (1,) + mem_k_p.shape[:2] + cache_mem_k.shape[3:]
    return (y_p, y_s, mem_k_p.reshape(kv_shape), mem_v_p.reshape(kv_shape), sh_p[None], sh_s[None],
            sr_p[None], sr_s[None], bf_p[None], bf_s[None])
```

```python
import functools
import math

import jax
import jax.numpy as jnp
from jax import lax
from jax.experimental import pallas as pl
from jax.experimental.pallas import tpu as pltpu

F32 = jnp.float32
BF16 = jnp.bfloat16

NORM_EPS = 1e-6
RWKV_GN_EPS = 64e-5
LANES = 128
HGRN_HEAD = 128
RWKV_HEAD = 64
X_HEADS = 4
N_GROUPS = 4
EXPERTS_PER_GROUP = 8
N_EXPERTS = N_GROUPS * EXPERTS_PER_GROUP
NEG_BIG = -1e30
VMEM_LIMIT = 52 * 1024 * 1024
_E1_LANE, _E2_LANE = N_EXPERTS, N_EXPERTS + 1
_SEG = 16


def _dot(a, b):
    return jnp.dot(a, b, preferred_element_type=F32)


def _dot_nt(a, b):
    return lax.dot_general(a, b, (((1,), (1,)), ((), ())), preferred_element_type=F32)


def _dot_tn(a, b):
    return lax.dot_general(a, b, (((0,), (0,)), ((), ())), preferred_element_type=F32)


def _split(x):
    hi = x.astype(BF16)
    lo = (x - hi.astype(F32)).astype(BF16)
    return hi, lo


def _sigmoid(x):
    return 0.5 * jnp.tanh(0.5 * x) + 0.5


def _silu(x):
    return x * _sigmoid(x)


def _rmsnorm(x, g):
    return x * lax.rsqrt(jnp.mean(x * x, axis=-1, keepdims=True) + NORM_EPS) * g


def _cparams(sem):
    return pltpu.CompilerParams(dimension_semantics=sem, vmem_limit_bytes=VMEM_LIMIT)


def _norm_matmul_kernel(x_ref, g_ref, w_ref, o_ref, u_ref):
    @pl.when(pl.program_id(1) == 0)
    def _():
        u_ref[...] = _rmsnorm(x_ref[...], g_ref[...]).astype(BF16)

    o_ref[...] = _dot(u_ref[...], w_ref[...])


def _norm_matmul(x, g, w, tm, tn):
    n, d = x.shape
    m = w.shape[1]
    return pl.pallas_call(
        _norm_matmul_kernel,
        grid=(n // tm, m // tn),
        in_specs=[
            pl.BlockSpec((tm, d), lambda i, j: (i, 0)),
            pl.BlockSpec((1, d), lambda i, j: (0, 0)),
            pl.BlockSpec((d, tn), lambda i, j: (0, j)),
        ],
        out_specs=pl.BlockSpec((tm, tn), lambda i, j: (i, j)),
        out_shape=jax.ShapeDtypeStruct((n, m), F32),
        scratch_shapes=[pltpu.VMEM((tm, d), BF16)],
        compiler_params=_cparams(("parallel", "arbitrary")),
        name="norm_matmul",
    )(x, g.reshape(1, d), w)


def _mem_kv_kernel(m_ref, g_ref, wk_ref, wv_ref, k_ref, v_ref):
    u = _rmsnorm(m_ref[...], g_ref[...]).astype(BF16)
    k_ref[...] = _dot(u, wk_ref[...])
    v_ref[...] = _dot(u, wv_ref[...])


def _mem_kv(mem, g, wk, wv):
    bsz, nmem, d = mem.shape
    full = lambda a: pl.BlockSpec(a.shape, lambda b: (0,) * a.ndim)
    out = pl.BlockSpec((None, nmem, d), lambda b: (b, 0, 0))
    g = g.reshape(1, d)
    return pl.pallas_call(
        _mem_kv_kernel,
        grid=(bsz,),
        in_specs=[pl.BlockSpec((None, nmem, d), lambda b: (b, 0, 0)), full(g), full(wk), full(wv)],
        out_specs=[out, out],
        out_shape=[jax.ShapeDtypeStruct((bsz, nmem, d), F32)] * 2,
        compiler_params=_cparams(("parallel",)),
        name="mem_kv",
    )(mem, g, wk, wv)


def _hgrn_kernel(q_ref, f_ref, i_ref, g_ref, lbp_ref, ng_ref, s0_ref, o_ref, sout_ref, st_ref,
                 *, chunk, n_chunks, heads):
    t = pl.program_id(1)

    @pl.when(t == 0)
    def _():
        for h in range(heads):
            st_ref[h] = s0_ref[h].T

    lbp = lbp_ref[...]
    e = jnp.exp(lbp - jnp.max(lbp, axis=0, keepdims=True))
    lb = e[0:1] / jnp.sum(e, axis=0, keepdims=True)
    ng = ng_ref[...]

    row = lax.broadcasted_iota(jnp.int32, (chunk, chunk), 0)
    col = lax.broadcasted_iota(jnp.int32, (chunk, chunk), 1)
    causal = row >= col
    mid = chunk // 2
    to_mid = jnp.where((col >= mid) & (col <= row), 1.0, 0.0) - jnp.where((col > row) & (col < mid), 1.0, 0.0)
    coef = jnp.concatenate([causal.astype(F32), to_mid, (col > row).astype(F32)], axis=0).astype(BF16)

    chunks = range(n_chunks)
    qd, kd, qi, kl, el, vv, gate = [], [], [], [], [], [], []
    for c in chunks:
        rows = slice(c * chunk, (c + 1) * chunk)
        f = lb + (1.0 - lb) * _sigmoid(f_ref[rows, :])
        g_hi, g_lo = _split(jnp.log(f))
        bb = _dot(coef, g_hi) + _dot(coef, g_lo)
        b, b_to_mid, b_to_last = bb[:chunk], bb[chunk:2 * chunk], bb[2 * chunk:]
        q = _silu(q_ref[rows, :])
        k = 1.0 - f
        qd.append((q * jnp.exp(b_to_mid)).astype(BF16))
        kd.append((k * jnp.exp(-b_to_mid)).astype(BF16))
        qi.append((q * jnp.exp(b)).astype(BF16))
        kl.append((k * jnp.exp(b_to_last)).astype(BF16))
        el.append(jnp.exp(b[chunk - 1:chunk, :]))
        vv.append(i_ref[rows, :].astype(BF16))
        gate.append(_silu(g_ref[rows, :]))

    head = lambda x, h: x[:, h * HGRN_HEAD:(h + 1) * HGRN_HEAD]
    items = [(c, h) for c in chunks for h in range(heads)]
    sc = {ch: jnp.where(causal, _dot_nt(head(qd[ch[0]], ch[1]), head(kd[ch[0]], ch[1])), 0.0).astype(BF16)
          for ch in items}
    intra = {ch: _dot(sc[ch], head(vv[ch[0]], ch[1])) for ch in items}
    upd = {ch: _dot_tn(head(vv[ch[0]], ch[1]), head(kl[ch[0]], ch[1])) for ch in items}
    for h in range(heads):
        st = st_ref[h]
        for c in chunks:
            o = intra[c, h] + _dot_nt(head(qi[c], h), st.astype(BF16))
            st = st * head(el[c], h) + upd[c, h]
            o = o * lax.rsqrt(jnp.mean(o * o, axis=-1, keepdims=True) + NORM_EPS)
            o_ref[c * chunk:(c + 1) * chunk, h * HGRN_HEAD:(h + 1) * HGRN_HEAD] = o * ng * head(gate[c], h)
        st_ref[h] = st

    @pl.when(t == pl.num_programs(1) - 1)
    def _():
        for h in range(heads):
            sout_ref[h] = st_ref[h].T


def _hgrn(p3, lbp, ng, s0, tile, chunk):
    bsz, tlen, _ = p3.shape
    heads, dk, dv = s0.shape[1:]
    d = heads * HGRN_HEAD
    kern = functools.partial(_hgrn_kernel, chunk=chunk, n_chunks=tile // chunk, heads=heads)
    pspec = lambda cb: pl.BlockSpec((None, tile, d), lambda b, t: (b, t, cb))
    return pl.pallas_call(
        kern,
        grid=(bsz, tlen // tile),
        in_specs=[
            pspec(0), pspec(1), pspec(2), pspec(3),
            pl.BlockSpec(lbp.shape, lambda b, t: (0, 0)),
            pl.BlockSpec((1, dv), lambda b, t: (0, 0)),
            pl.BlockSpec((None, heads, dk, dv), lambda b, t: (b, 0, 0, 0)),
        ],
        out_specs=[
            pl.BlockSpec((None, tile, d), lambda b, t: (b, t, 0)),
            pl.BlockSpec((None, heads, dk, dv), lambda b, t: (b, 0, 0, 0)),
        ],
        out_shape=[
            jax.ShapeDtypeStruct((bsz, tlen, d), F32),
            jax.ShapeDtypeStruct((bsz, heads, dk, dv), F32),
        ],
        scratch_shapes=[pltpu.VMEM((heads, dv, dk), F32)],
        compiler_params=_cparams(("parallel", "arbitrary")),
        name="hgrn",
    )(p3, p3, p3, p3, lbp, ng.reshape(1, dv), s0)


_P_MU_R, _P_MU_K, _P_MU_V, _P_W0, _P_A0, _P_KK, _P_KA, _P_RK, _P_LNG, _P_LNB = range(10)
_P_ROWS = 16


def _rwkv_kernel(r_ref, k_ref, v_ref, l_ref, br_ref, bk_ref, bv_ref, bl_ref, par_ref, mul_ref,
                 w2_ref, a2_ref, g2a_ref, g2b_ref, s0_ref,
                 o_ref, sout_ref, st_ref, prev_ref, prevl_ref, *, chunk, pairs):
    t = pl.program_id(1)

    @pl.when(t == 0)
    def _():
        st_ref[...] = s0_ref[...]
        prev_ref[0:1, :] = br_ref[...]
        prev_ref[1:2, :] = bk_ref[...]
        prev_ref[2:3, :] = bv_ref[...]
        prevl_ref[0:1, :] = bl_ref[...]

    par = par_ref[...]
    prow = lambda i: par[i:i + 1, :]

    def shifted(x, prev_row, mu):
        row_id = lax.broadcasted_iota(jnp.int32, x.shape, 0)
        prev = jnp.where(row_id == 0, prev_row, pltpu.roll(x, 1, 0))
        return x + (prev - x) * mu

    pr, pk, pv, plr = r_ref[...], k_ref[...], v_ref[...], l_ref[...]
    r = shifted(pr, prev_ref[0:1, :], prow(_P_MU_R))
    k = shifted(pk, prev_ref[1:2, :], prow(_P_MU_K))
    v = shifted(pv, prev_ref[2:3, :], prow(_P_MU_V))
    xl = shifted(plr, prevl_ref[0:1, :], mul_ref[...])
    for slot, x in enumerate((pr, pk, pv)):
        prev_ref[slot:slot + 1, :] = x[chunk - 1:chunk, :]
    prevl_ref[0:1, :] = plr[chunk - 1:chunk, :]
    x0 = xl[:, 0:LANES]
    x1 = xl[:, LANES:2 * LANES]
    x2 = xl[:, 2 * LANES:3 * LANES]

    w_lin = prow(_P_W0) + _dot(jnp.tanh(x0).astype(BF16), w2_ref[...])
    z = -w_lin
    softplus = jnp.maximum(z, 0.0) + jnp.log(1.0 + jnp.exp(-jnp.abs(z)))
    lw = -jnp.exp(-softplus - 0.5)
    a_icl = _sigmoid(prow(_P_A0) + _dot(x0.astype(BF16), a2_ref[...]))
    g_out = _dot(_sigmoid(x1).astype(BF16), g2a_ref[...]) + _dot(_sigmoid(x2).astype(BF16), g2b_ref[...])

    lane = lax.broadcasted_iota(jnp.int32, (1, LANES), 1)
    m0 = (lane < RWKV_HEAD).astype(F32)
    m1 = 1.0 - m0
    head_shift = RWKV_HEAD.bit_length() - 1
    hr = lax.broadcasted_iota(jnp.int32, (LANES, LANES), 0) >> head_shift
    hc = lax.broadcasted_iota(jnp.int32, (LANES, LANES), 1) >> head_shift
    head_ones = (hr == hc).astype(BF16)
    tile_of = lambda x, j: x[:, j * LANES:(j + 1) * LANES]

    def headsum(x):
        xs = jnp.concatenate([tile_of(x, j) for j in range(pairs)], axis=0)
        ys = _dot(xs.astype(BF16), head_ones)
        return jnp.concatenate([ys[j * chunk:(j + 1) * chunk] for j in range(pairs)], axis=1)

    kk = k * prow(_P_KK)
    kk = kk * lax.rsqrt(jnp.maximum(headsum(kk * kk), 1e-24))
    k_mod = k * (1.0 + (a_icl - 1.0) * prow(_P_KA))
    a = -kk
    b = kk * a_icl

    ti = lax.broadcasted_iota(jnp.int32, (chunk, chunk), 0)
    tj = lax.broadcasted_iota(jnp.int32, (chunk, chunk), 1)
    tri = (ti >= tj).astype(BF16)
    lw_hi, lw_lo = _split(lw)
    cum = _dot(tri, lw_hi) + _dot(tri, lw_lo)
    cum_last = cum[chunk - 1:chunk, :]
    e_neg = jnp.exp(-cum)
    e_end = jnp.exp(cum_last - cum)
    a_t = a * jnp.exp(cum - lw)
    b_t = b * e_neg
    k_t = k_mod * e_neg
    r_t = r * jnp.exp(cum)
    b_end = b * e_end
    k_end = k_mod * e_end
    g_last = jnp.exp(cum_last)

    n2 = 2 * chunk
    si = lax.broadcasted_iota(jnp.int32, (n2, n2), 0)
    sj = lax.broadcasted_iota(jnp.int32, (n2, n2), 1)
    chunk_shift = chunk.bit_length() - 1
    same_head = (si >> chunk_shift) == (sj >> chunk_shift)
    strict = same_head & ((sj & (chunk - 1)) < (si & (chunk - 1)))
    incl = same_head & ((sj & (chunk - 1)) <= (si & (chunk - 1)))
    eye = (si == sj).astype(F32)
    m0_b, m1_b = m0.astype(BF16), m1.astype(BF16)
    tile_b = lambda x, j: tile_of(x, j).astype(BF16)

    def stack_b(x, j):
        xb = tile_b(x, j)
        return jnp.concatenate([xb * m0_b, xb * m1_b], axis=0)

    every = range(pairs)

    a_st = [stack_b(a_t, j) for j in every]
    r_st = [stack_b(r_t, j) for j in every]
    v_st = [stack_b(v, j) for j in every]
    bend_st = [stack_b(b_end, j) for j in every]
    kend_st = [stack_b(k_end, j) for j in every]
    sc = [_dot_nt(jnp.concatenate([tile_b(a_t, j), tile_b(r_t, j)], axis=0),
                  jnp.concatenate([stack_b(b_t, j), stack_b(k_t, j)], axis=0)) for j in every]
    twice = lambda x: jnp.concatenate([x, x], axis=0)
    a_ab = [jnp.where(strict, twice(s[:chunk, :n2]), 0.0) for s in sc]
    a_xk = [jnp.concatenate([jnp.where(strict, twice(s[:chunk, n2:]), 0.0),
                             jnp.where(incl, twice(s[chunk:, n2:]), 0.0)], axis=0).astype(BF16) for s in sc]
    a_rb = [jnp.where(incl, twice(s[chunk:, :n2]), 0.0).astype(BF16) for s in sc]

    inv = [eye + x for x in a_ab]
    pw = [_dot(x.astype(BF16), x.astype(BF16)) for x in a_ab]
    for _ in range(int(math.log2(chunk)) - 2):
        if n2 % LANES == 0:
            both = [_dot(p.astype(BF16), jnp.concatenate([p, i], axis=1).astype(BF16)) for p, i in zip(pw, inv)]
            pw_next = [x[:, :n2] for x in both]
            inv = [i + x[:, n2:] for i, x in zip(inv, both)]
        else:
            pw_next = [_dot(p.astype(BF16), p.astype(BF16)) for p in pw]
            inv = [i + _dot(p.astype(BF16), i.astype(BF16)) for i, p in zip(inv, pw)]
        pw = pw_next
    inv = [i + _dot(p.astype(BF16), i.astype(BF16)) for i, p in zip(inv, pw)]

    av = [_dot(a_xk[j], v_st[j]) for j in every]
    wu = [_dot(inv[j].astype(BF16),
               jnp.concatenate([a_st[j], av[j][:n2].astype(BF16)], axis=1)).astype(BF16) for j in every]
    qo = [_dot(a_rb[j], wu[j]) + jnp.concatenate([r_st[j].astype(F32), av[j][n2:]], axis=1) for j in every]
    mn = [_dot_tn(wu[j], bend_st[j]) for j in every]
    nn = [mn[j][LANES:] + _dot_tn(v_st[j], kend_st[j]) for j in every]

    outs = []
    for j in every:
        s0 = st_ref[j]
        s0_b = s0.astype(BF16)
        q = (qo[j][:chunk, :LANES] + qo[j][chunk:, :LANES]).astype(BF16)
        outs.append(_dot_nt(q, s0_b) + qo[j][:chunk, LANES:] + qo[j][chunk:, LANES:])
        st_ref[j] = s0 * tile_of(g_last, j) + _dot(s0_b, mn[j][:LANES].astype(BF16)) + nn[j]
    o = jnp.concatenate(outs, axis=1)

    inv_n = 1.0 / RWKV_HEAD
    mean = headsum(o) * inv_n
    dlt = o - mean
    var = headsum(dlt * dlt) * inv_n
    o = dlt * lax.rsqrt(var + RWKV_GN_EPS) * prow(_P_LNG) + prow(_P_LNB)
    bonus = headsum(r * k_mod * prow(_P_RK))
    o_ref[...] = (o + bonus * v) * g_out

    @pl.when(t == pl.num_programs(1) - 1)
    def _():
        sout_ref[...] = st_ref[...]


def _rwkv(p3, buf_rkv, buf_l, par, mul, w2p, a2p, g2a, g2b, s0bd, d, chunk):
    bsz, tlen, _ = p3.shape
    pairs = d // LANES
    nl = 3 * LANES
    kern = functools.partial(_rwkv_kernel, chunk=chunk, pairs=pairs)
    pcol = lambda cb: pl.BlockSpec((None, chunk, d), lambda b, t: (b, t, cb))
    bcol = lambda cb: pl.BlockSpec((None, 1, d), lambda b, t: (b, 0, cb))
    full = lambda a: pl.BlockSpec(a.shape, lambda b, t: (0,) * a.ndim)
    state = pl.BlockSpec((None, pairs, LANES, LANES), lambda b, t: (b, 0, 0, 0))
    return pl.pallas_call(
        kern,
        grid=(bsz, tlen // chunk),
        in_specs=[
            pcol(4), pcol(5), pcol(6),
            pl.BlockSpec((None, chunk, nl), lambda b, t: (b, t, 9 * d // nl)),
            bcol(0), bcol(1), bcol(2),
            pl.BlockSpec((None, 1, nl), lambda b, t: (b, 0, 0)),
            full(par), full(mul), full(w2p), full(a2p), full(g2a), full(g2b), state,
        ],
        out_specs=[pl.BlockSpec((None, chunk, d), lambda b, t: (b, t, 0)), state],
        out_shape=[
            jax.ShapeDtypeStruct((bsz, tlen, d), F32),
            jax.ShapeDtypeStruct((bsz, pairs, LANES, LANES), F32),
        ],
        scratch_shapes=[pltpu.VMEM((pairs, LANES, LANES), F32), pltpu.VMEM((8, d), F32),
                        pltpu.VMEM((8, nl), F32)],
        compiler_params=_cparams(("parallel", "arbitrary")),
        name="rwkv",
    )(p3, p3, p3, p3, buf_rkv, buf_rkv, buf_rkv, buf_l, par, mul, w2p, a2p, g2a, g2b, s0bd)


def _attn_kernel(x_ref, oh_ref, or_ref, ga_ref, gb_ref, mk_ref, mv_ref, wout_ref, nxg_ref, wq_ref, wo_ref,
                 nfg_ref, wrh_ref, wrl_ref, rb_ref, h_ref, u_ref, wf_ref, att_ref, *, bt, tt):
    n = bt * tt
    d = x_ref.shape[-1]
    flat = lambda ref: ref[...].reshape(n, d)
    merged = _sigmoid(flat(ga_ref)) * flat(oh_ref) + _sigmoid(flat(gb_ref)) * flat(or_ref)
    h1 = flat(x_ref) + _dot(merged.astype(BF16), wout_ref[...])

    q = _dot(_rmsnorm(h1, nxg_ref[...]).astype(BF16), wq_ref[...])
    dh = d // X_HEADS
    scale = dh ** -0.5
    for bi in range(bt):
        rows = slice(bi * tt, (bi + 1) * tt)
        for hh in range(X_HEADS):
            cols = slice(hh * dh, (hh + 1) * dh)
            mem_head = lambda ref: (ref[bi, :, cols] if len(ref.shape) == 3 else ref[bi, :, hh, :]).astype(BF16)
            s = _dot_nt(q[rows, cols].astype(BF16), mem_head(mk_ref)) * scale
            p = jnp.exp(s - jnp.max(s, axis=-1, keepdims=True))
            p = p / jnp.sum(p, axis=-1, keepdims=True)
            att_ref[rows, cols] = _dot(p.astype(BF16), mem_head(mv_ref)).astype(BF16)
    h2 = h1 + _dot(att_ref[...], wo_ref[...])
    h_ref[...] = h2.reshape(bt, tt, d)

    u = _rmsnorm(h2, nfg_ref[...])
    u_ref[...] = u.astype(BF16).reshape(bt, tt, d)
    u_hi, u_lo = _split(u)
    logits = (_dot(u_hi, wrh_ref[...]) + _dot(u_hi, wrl_ref[...]) + _dot(u_lo, wrh_ref[...])) + rb_ref[...]
    lane = lax.broadcasted_iota(jnp.int32, (n, LANES), 1)
    lane_f = lane.astype(F32)
    is_group = (lane >= N_EXPERTS) & (lane < N_EXPERTS + N_GROUPS)
    lg = jnp.where(is_group, logits, NEG_BIG)
    g_max = jnp.max(lg, axis=-1, keepdims=True)
    g_idx = jnp.min(jnp.where(lg == g_max, lane_f, 1e9), axis=-1, keepdims=True) - N_EXPERTS
    pg = 1.0 / jnp.sum(jnp.exp(lg - g_max), axis=-1, keepdims=True)
    in_group = (lane < N_EXPERTS) & ((lane >> (EXPERTS_PER_GROUP.bit_length() - 1)).astype(F32) == g_idx)
    le = jnp.where(in_group, logits, NEG_BIG)
    m1 = jnp.max(le, axis=-1, keepdims=True)
    i1 = jnp.min(jnp.where(le == m1, lane_f, 1e9), axis=-1, keepdims=True)
    le2 = jnp.where(lane_f == i1, NEG_BIG, le)
    m2 = jnp.max(le2, axis=-1, keepdims=True)
    i2 = jnp.min(jnp.where(le2 == m2, lane_f, 1e9), axis=-1, keepdims=True)
    e2 = jnp.exp(m2 - m1)
    w1 = pg / (1.0 + e2)
    wf = jnp.where(lane_f == i1, w1, 0.0) + jnp.where(lane_f == i2, w1 * e2, 0.0)
    wf = wf + jnp.where(lane == _E1_LANE, i1, 0.0) + jnp.where(lane == _E2_LANE, i2, 0.0)
    wf_ref[...] = wf.reshape(bt, tt, LANES)


def _attn(x3, oh3, or3, p3, mk, mv, wout, nxg, wq, wo, nfg, wrh, wrl, rb, bt, tt, gate_block0):
    bsz, tlen, d = x3.shape
    kern = functools.partial(_attn_kernel, bt=bt, tt=tt)
    tok = lambda cb: pl.BlockSpec((bt, tt, d), lambda b, t: (b, t, cb))
    mem = pl.BlockSpec((bt,) + mk.shape[1:], lambda b, t: (b,) + (0,) * (mk.ndim - 1))
    full = lambda a: pl.BlockSpec(a.shape, lambda b, t: (0,) * a.ndim)
    nxg, nfg = nxg.reshape(1, d), nfg.reshape(1, d)
    return pl.pallas_call(
        kern,
        grid=(bsz // bt, tlen // tt),
        in_specs=[tok(0), tok(0), tok(0), tok(gate_block0), tok(gate_block0 + 1), mem, mem,
                  full(wout), full(nxg), full(wq), full(wo), full(nfg), full(wrh), full(wrl), full(rb)],
        out_specs=[
            pl.BlockSpec((bt, tt, d), lambda b, t: (b, t, 0)),
            pl.BlockSpec((bt, tt, d), lambda b, t: (b, t, 0)),
            pl.BlockSpec((bt, tt, LANES), lambda b, t: (b, t, 0)),
        ],
        out_shape=[
            jax.ShapeDtypeStruct((bsz, tlen, d), F32),
            jax.ShapeDtypeStruct((bsz, tlen, d), BF16),
            jax.ShapeDtypeStruct((bsz, tlen, LANES), F32),
        ],
        scratch_shapes=[pltpu.VMEM((bt * tt, d), BF16)],
        compiler_params=_cparams(("parallel", "parallel")),
        name="attn",
    )(x3, oh3, or3, p3, p3, mk, mv, wout, nxg, wq, wo, nfg, wrh, wrl, rb)


def _segment_dma(src, dst, src_row, dst_row, length, max_len, sem, start):
    size = _SEG
    while size * 2 <= max_len:
        size *= 2
    while size >= _SEG:
        offset = length & (-2 * size)

        @pl.when((length & size) != 0)
        def _(offset=offset, size=size):
            copy = pltpu.make_async_copy(src.at[pl.ds(pl.multiple_of(src_row + offset, _SEG), size)],
                                         dst.at[pl.ds(pl.multiple_of(dst_row + offset, _SEG), size)], sem)
            if start:
                copy.start()
            else:
                copy.wait()

        size //= 2


def _tile_meta(meta_ref, nt, tile):
    field = lambda k: [meta_ref[(k * nt + tile) * N_GROUPS + g] for g in range(N_GROUPS)]
    return field(0), field(1), field(2)


def _group_of(expert_id):
    return jnp.floor(expert_id * (1.0 / EXPERTS_PER_GROUP))


def _moe_sort_kernel(meta_ref, u_ref, wf_ref, xs_ref, ws_ref, su_ref, sw_ref, zu_ref, zw_ref, sem, *, nt, tm, te):
    i = pl.program_id(0)
    buf = lax.rem(i, 2)
    slots = su_ref.shape[1]
    _, _, base = _tile_meta(meta_ref, nt, i)
    wf = wf_ref[...]
    gid = _group_of(wf.T[_E1_LANE:_E1_LANE + 1, :])
    onehot = lax.broadcasted_iota(jnp.int32, (8, tm), 0).astype(F32) == gid
    earlier = (lax.broadcasted_iota(jnp.int32, (tm, tm), 0)
               < lax.broadcasted_iota(jnp.int32, (tm, tm), 1)).astype(BF16)
    seen = _dot(onehot.astype(BF16), earlier)
    pos = jnp.sum(jnp.where(onehot, seen, 0.0), axis=0, keepdims=True)
    for g in range(N_GROUPS):
        pos = pos + jnp.where(gid == float(g), base[g].astype(F32), 0.0)
    perm = (lax.broadcasted_iota(jnp.int32, (slots, tm), 0).astype(F32) == pos).astype(BF16)
    su_ref[buf] = _dot(perm, u_ref[...]).astype(BF16)
    w_hi, w_lo = _split(wf)
    sw_ref[buf] = _dot(perm, w_hi) + _dot(perm, w_lo)

    def copies(tile, b, start):
        off, cnt, base = _tile_meta(meta_ref, nt, tile)
        for g in range(N_GROUPS):
            _segment_dma(su_ref.at[b], xs_ref, base[g], off[g], cnt[g], tm, sem.at[0, b], start)
            _segment_dma(sw_ref.at[b], ws_ref, base[g], off[g], cnt[g], tm, sem.at[1, b], start)

    copies(i, buf, True)

    @pl.when(i > 0)
    def _():
        copies(i - 1, 1 - buf, False)

    @pl.when(i == nt - 1)
    def _():
        copies(i, buf, False)
        zu_ref[...] = jnp.zeros_like(zu_ref)
        zw_ref[...] = jnp.zeros_like(zw_ref)
        tail = 3 * nt * N_GROUPS + nt
        for start in (True, False):
            for g in range(N_GROUPS):
                off, cnt = meta_ref[tail + g], meta_ref[tail + N_GROUPS + g]
                _segment_dma(zu_ref, xs_ref, 0, off, cnt, te, sem.at[0, 0], start)
                _segment_dma(zw_ref, ws_ref, 0, off, cnt, te, sem.at[1, 0], start)

            def unused(r, carry, start=start):
                row = pl.multiple_of(r * te, te)
                for src, dst, s in ((zu_ref, xs_ref, sem.at[0, 1]), (zw_ref, ws_ref, sem.at[1, 1])):
                    copy = pltpu.make_async_copy(src, dst.at[pl.ds(row, te)], s)
                    if start:
                        copy.start()
                    else:
                        copy.wait()
                return carry

            lax.fori_loop(meta_ref[tail + 2 * N_GROUPS], xs_ref.shape[0] // te, unused, 0)


def _moe_expert_kernel(meta_ref, xs_ref, ws_ref, wg_ref, wu_ref, wd_ref, ys_ref, *, n_tiles):
    r = pl.program_id(0)
    used = r < meta_ref[n_tiles]

    @pl.when(used)
    def _():
        first = meta_ref[r] * EXPERTS_PER_GROUP
        x = xs_ref[...]
        ws = ws_ref[...]
        lane = lax.broadcasted_iota(jnp.int32, ws.shape, 1)
        hid = []
        for e in range(EXPERTS_PER_GROUP):
            h = _silu(_dot(x, wg_ref[e])) * _dot(x, wu_ref[e])
            w_col = jnp.sum(jnp.where(lane == first + e, ws, 0.0), axis=-1, keepdims=True)
            hid.append((h * w_col).astype(BF16))
        ys_ref[...] = _dot(jnp.concatenate(hid, axis=1), wd_ref[...]).astype(ys_ref.dtype)

    @pl.when(jnp.logical_not(used))
    def _():
        ys_ref[...] = jnp.zeros_like(ys_ref)


def _moe_unsort_kernel(meta_ref, wf_ref, h_ref, gf_ref, ys_ref, y_ref, sy_ref, sem, *, nt, tm):
    i = pl.program_id(0)
    buf = lax.rem(i, 2)
    slots = sy_ref.shape[1]

    def copies(tile, b, start):
        off, cnt, base = _tile_meta(meta_ref, nt, tile)
        for g in range(N_GROUPS):
            _segment_dma(ys_ref, sy_ref.at[b], off[g], base[g], cnt[g], tm, sem.at[b], start)

    @pl.when(i == 0)
    def _():
        copies(0, 0, True)

    @pl.when(i + 1 < nt)
    def _():
        copies(i + 1, 1 - buf, True)

    _, _, base = _tile_meta(meta_ref, nt, i)
    gid = _group_of(wf_ref[...][:, _E1_LANE:_E1_LANE + 1])
    onehot = lax.broadcasted_iota(jnp.int32, (tm, LANES), 1).astype(F32) == gid
    earlier = (lax.broadcasted_iota(jnp.int32, (tm, tm), 1)
               < lax.broadcasted_iota(jnp.int32, (tm, tm), 0)).astype(BF16)
    seen = _dot(earlier, onehot.astype(BF16))
    pos = jnp.sum(jnp.where(onehot, seen, 0.0), axis=1, keepdims=True)
    for g in range(N_GROUPS):
        pos = pos + jnp.where(gid == float(g), base[g].astype(F32), 0.0)
    perm_t = (lax.broadcasted_iota(jnp.int32, (tm, slots), 1).astype(F32) == pos).astype(BF16)
    copies(i, buf, False)
    slot_row = lax.broadcasted_iota(jnp.int32, (slots, 1), 0)
    ys = sy_ref[buf]
    moe = _dot(perm_t, jnp.where(slot_row < meta_ref[3 * nt * N_GROUPS + i], ys, jnp.zeros_like(ys)))
    y_ref[...] = _rmsnorm(h_ref[...] + moe, gf_ref[...])


def _moe(u, wf, h, wg, wu, wd, gf, tm, te):
    n, d = u.shape
    ne, _, f = wg.shape
    nt = n // tm
    slots = tm + LANES
    n_tiles = -(-(n + nt * N_GROUPS * (_SEG - 1) + N_GROUPS * (te - 1)) // te)
    rows = n_tiles * te
    i32 = jnp.int32

    gid = (wf[:, _E1_LANE].astype(i32) // EXPERTS_PER_GROUP).reshape(nt, tm)
    cnt = jnp.sum((gid[:, :, None] == jnp.arange(N_GROUPS, dtype=i32)).astype(i32), axis=1)
    cnt = (cnt + _SEG - 1) // _SEG * _SEG
    base = jnp.cumsum(cnt, axis=1) - cnt
    total = jnp.sum(cnt, axis=0)
    region = (total + te - 1) // te * te
    region_end = jnp.cumsum(region)
    region_start = region_end - region
    off = region_start[None, :] + jnp.cumsum(cnt, axis=0) - cnt
    n_used = region_end[-1:] // te
    meta = jnp.concatenate([off.reshape(-1), cnt.reshape(-1), base.reshape(-1), jnp.sum(cnt, axis=1),
                            region_start + total, region - total, n_used]).astype(i32)
    tile_row = jnp.arange(n_tiles, dtype=i32) * te
    tile_group = jnp.minimum(jnp.sum((tile_row[:, None] >= region_end[None, :]).astype(i32), axis=1), N_GROUPS - 1)
    emeta = jnp.concatenate([tile_group, n_used]).astype(i32)

    anyspec = pl.BlockSpec(memory_space=pl.ANY)
    tok = lambda w: pl.BlockSpec((tm, w), lambda i, m: (i, 0))
    xs, ws = pl.pallas_call(
        functools.partial(_moe_sort_kernel, nt=nt, tm=tm, te=te),
        grid_spec=pltpu.PrefetchScalarGridSpec(
            num_scalar_prefetch=1, grid=(nt,),
            in_specs=[tok(d), tok(LANES)],
            out_specs=[anyspec, anyspec],
            scratch_shapes=[pltpu.VMEM((2, slots, d), BF16), pltpu.VMEM((2, slots, LANES), F32),
                            pltpu.VMEM((te, d), BF16), pltpu.VMEM((te, LANES), F32),
                            pltpu.SemaphoreType.DMA((2, 2))]),
        out_shape=[jax.ShapeDtypeStruct((rows, d), BF16), jax.ShapeDtypeStruct((rows, LANES), F32)],
        compiler_params=_cparams(("arbitrary",)),
        name="moe_sort",
    )(meta, u, wf)

    srt = lambda w: pl.BlockSpec((te, w), lambda r, m: (r, 0))
    grp_w = pl.BlockSpec((EXPERTS_PER_GROUP, d, f), lambda r, m: (m[r], 0, 0))
    ys = pl.pallas_call(
        functools.partial(_moe_expert_kernel, n_tiles=n_tiles),
        grid_spec=pltpu.PrefetchScalarGridSpec(
            num_scalar_prefetch=1, grid=(n_tiles,),
            in_specs=[srt(d), srt(LANES), grp_w, grp_w,
                      pl.BlockSpec((EXPERTS_PER_GROUP * f, d), lambda r, m: (m[r], 0))],
            out_specs=srt(d)),
        out_shape=jax.ShapeDtypeStruct((rows, d), BF16),
        compiler_params=_cparams(("parallel",)),
        name="moe_expert",
    )(emeta, xs, ws, wg, wu, wd.reshape(ne * f, d))

    return pl.pallas_call(
        functools.partial(_moe_unsort_kernel, nt=nt, tm=tm),
        grid_spec=pltpu.PrefetchScalarGridSpec(
            num_scalar_prefetch=1, grid=(nt,),
            in_specs=[tok(LANES), tok(d), pl.BlockSpec((1, d), lambda i, m: (0, 0)), anyspec],
            out_specs=tok(d),
            scratch_shapes=[pltpu.VMEM((2, slots, d), BF16), pltpu.SemaphoreType.DMA((2,))]),
        out_shape=jax.ShapeDtypeStruct((n, d), F32),
        compiler_params=_cparams(("arbitrary",)),
        name="moe_unsort",
    )(meta, wf, h, gf.reshape(1, d), ys)


def _pick(n, pref):
    t = min(n, pref)
    while n % t:
        t -= 8
    return t


def _block_diag_pairs(s):
    b, h, n, _ = s.shape
    s = s.reshape(b, h // 2, 2, n, n)
    z = jnp.zeros_like(s[:, :, 0])
    top = jnp.concatenate([s[:, :, 0], z], axis=-1)
    bot = jnp.concatenate([z, s[:, :, 1]], axis=-1)
    return jnp.concatenate([top, bot], axis=-2)


def _pairs_to_heads(sbd):
    b, p, n2, _ = sbd.shape
    n = n2 // 2
    s = jnp.stack([sbd[:, :, :n, :n], sbd[:, :, n:, n:]], axis=2)
    return s.reshape(b, 2 * p, n, n)


def kernel(x_prompt, x_sample, mem_prompt, cache_mem_k, cache_mem_v, state_hgrn, state_rwkv, state_rwkv_shift, norm_mix_g, w_in, hgrn_lower_bounds, hgrn_norm_g, rwkv_mu, rwkv_w0, rwkv_w2, rwkv_a0, rwkv_a2, rwkv_g2, rwkv_k_k, rwkv_k_a, rwkv_r_k, rwkv_ln_g, rwkv_ln_b, w_out, norm_x_g, norm_mem_g, wq_x, wk_x, wv_x, wo_x, norm_ffn_g, router_group_w, router_group_b, router_expert_w, router_expert_b, expert_w_gate, expert_w_up, expert_w_down, norm_final_g):
    assert w_in.shape[0] == 1, "single-layer configuration"
    d = x_prompt.shape[-1]
    n_lora_w, n_lora_a, n_lora_g = rwkv_w2.shape[1], rwkv_a2.shape[1], rwkv_g2.shape[1]
    assert n_lora_w + n_lora_a == LANES and LANES < n_lora_g <= 2 * LANES
    n_lora = n_lora_w + n_lora_a + n_lora_g
    n_rwkv_in = 3 * d + n_lora
    lora_pad = 3 * LANES - n_lora

    w = w_in[0]
    pr0 = 4 * d
    gate0 = pr0 + n_rwkv_in
    w_perm = jnp.concatenate(
        [w[:, :pr0 + 3 * d], w[:, gate0:gate0 + 2 * d], w[:, pr0 + 3 * d:gate0],
         jnp.zeros((d, lora_pad), w.dtype)], axis=1).astype(BF16)
    n_cols = w_perm.shape[1]
    lora_col0 = 9 * d
    gate_block0 = 7
    mu = rwkv_mu[0]
    mul = jnp.pad(mu[3 * d:], (0, lora_pad)).reshape(1, 3 * LANES)
    par = jnp.stack([mu[:d], mu[d:2 * d], mu[2 * d:3 * d], rwkv_w0[0], rwkv_a0[0], rwkv_k_k[0], rwkv_k_a[0],
                     rwkv_r_k[0].reshape(d), rwkv_ln_g[0], rwkv_ln_b[0]])
    par = jnp.pad(par, ((0, _P_ROWS - par.shape[0]), (0, 0)))
    w2p = jnp.pad(rwkv_w2[0], ((0, n_lora_a), (0, 0))).astype(BF16)
    a2p = jnp.pad(rwkv_a2[0], ((n_lora_w, 0), (0, 0))).astype(BF16)
    g2a = rwkv_g2[0][:LANES].astype(BF16)
    g2b = jnp.pad(rwkv_g2[0][LANES:], ((0, 2 * LANES - n_lora_g), (0, 0))).astype(BF16)
    wout_b, wq_b, wo_b = w_out[0].astype(BF16), wq_x[0].astype(BF16), wo_x[0].astype(BF16)
    wr = jnp.concatenate([router_expert_w[0], router_group_w[0],
                          jnp.zeros((d, LANES - N_EXPERTS - N_GROUPS), F32)], axis=1)
    wr_hi, wr_lo = _split(wr)
    rb = jnp.pad(jnp.concatenate([router_expert_b[0].reshape(-1), router_group_b[0]]),
                 (0, LANES - N_EXPERTS - N_GROUPS)).reshape(1, LANES)
    wg_b, wu_b, wd_b = (expert_w_gate[0].astype(BF16), expert_w_up[0].astype(BF16),
                        expert_w_down[0].astype(BF16))

    nb_p = mem_prompt.shape[0]
    mem_k_p, mem_v_p = _mem_kv(mem_prompt, norm_mem_g[0], wk_x[0].astype(BF16), wv_x[0].astype(BF16))

    def group(x3, mem_k, mem_v, s_h, s_r, buf, tile_h, chunk_r, bt, tt, te):
        bsz, tlen, _ = x3.shape
        n = bsz * tlen
        p = _norm_matmul(x3.reshape(n, d), norm_mix_g[0], w_perm, _pick(n, 1024), n_cols // 5)
        p3 = p.reshape(bsz, tlen, n_cols)
        o_h, s_h_new = _hgrn(p3, hgrn_lower_bounds, hgrn_norm_g[0], s_h, tile_h, 32)
        buf_rkv = buf[:, :, :3 * d]
        buf_l = jnp.pad(buf[:, :, 3 * d:], ((0, 0), (0, 0), (0, lora_pad)))
        o_r, s_r_bd = _rwkv(p3, buf_rkv, buf_l, par, mul, w2p, a2p, g2a, g2b, _block_diag_pairs(s_r), d, chunk_r)
        h2, u, wf = _attn(x3, o_h, o_r, p3, mem_k, mem_v, wout_b, norm_x_g[0], wq_b, wo_b, norm_ffn_g[0],
                          wr_hi, wr_lo, rb, bt, tt, gate_block0)
        y = _moe(u.reshape(n, d), wf.reshape(n, LANES), h2.reshape(n, d), wg_b, wu_b, wd_b, norm_final_g,
                 _pick(n, 512), te)
        shift = jnp.concatenate([p3[:, -1:, pr0:pr0 + 3 * d], p3[:, -1:, lora_col0:lora_col0 + n_lora]], axis=-1)
        return y.reshape(bsz, tlen, d), s_h_new, _pairs_to_heads(s_r_bd), shift

    nb_s, t_s, _ = x_sample.shape
    t_p = x_prompt.shape[1]
    zeros_like_state = lambda s: jnp.zeros((nb_p,) + s.shape[2:], s.dtype)
    y_p, sh_p, sr_p, bf_p = group(
        x_prompt, mem_k_p, mem_v_p, zeros_like_state(state_hgrn), zeros_like_state(state_rwkv),
        zeros_like_state(state_rwkv_shift), _pick(t_p, 256), 64, 1, _pick(t_p, 256), 512)
    y_s, sh_s, sr_s, bf_s = group(
        x_sample, cache_mem_k[0], cache_mem_v[0],
        state_hgrn[0], state_rwkv[0], state_rwkv_shift[0], t_s, t_s, _pick(nb_s, 4), t_s, 256)

    kv_shape = (1,) + mem_k_p.shape[:2] + cache_mem_k.shape[3:]
    return (y_p, y_s, mem_k_p.reshape(kv_shape), mem_v_p.reshape(kv_shape), sh_p[None], sh_s[None],
            sr_p[None], sr_s[None], bf_p[None], bf_s[None])
```

```python
import functools
import math

import jax
import jax.numpy as jnp
from jax import lax
from jax.experimental import pallas as pl
from jax.experimental.pallas import tpu as pltpu

F32 = jnp.float32
BF16 = jnp.bfloat16

NORM_EPS = 1e-6
RWKV_GN_EPS = 64e-5
LANES = 128
HGRN_HEAD = 128
RWKV_HEAD = 64
X_HEADS = 4
N_GROUPS = 4
EXPERTS_PER_GROUP = 8
N_EXPERTS = N_GROUPS * EXPERTS_PER_GROUP
NEG_BIG = -1e30
VMEM_LIMIT = 52 * 1024 * 1024
_E1_LANE, _E2_LANE = N_EXPERTS, N_EXPERTS + 1
_SEG = 16


def _dot(a, b):
    return jnp.dot(a, b, preferred_element_type=F32)


def _dot_nt(a, b):
    return lax.dot_general(a, b, (((1,), (1,)), ((), ())), preferred_element_type=F32)


def _dot_tn(a, b):
    return lax.dot_general(a, b, (((0,), (0,)), ((), ())), preferred_element_type=F32)


def _split(x):
    hi = x.astype(BF16)
    lo = (x - hi.astype(F32)).astype(BF16)
    return hi, lo


def _sigmoid(x):
    return 0.5 * jnp.tanh(0.5 * x) + 0.5


def _silu(x):
    return x * _sigmoid(x)


def _rmsnorm(x, g):
    return x * lax.rsqrt(jnp.mean(x * x, axis=-1, keepdims=True) + NORM_EPS) * g


def _cparams(sem):
    return pltpu.CompilerParams(dimension_semantics=sem, vmem_limit_bytes=VMEM_LIMIT)


def _norm_matmul_kernel(x_ref, g_ref, w_ref, o_ref, u_ref):
    @pl.when(pl.program_id(1) == 0)
    def _():
        u_ref[...] = _rmsnorm(x_ref[...], g_ref[...]).astype(BF16)

    o_ref[...] = _dot(u_ref[...], w_ref[...])


def _norm_matmul(x, g, w, tm, tn):
    n, d = x.shape
    m = w.shape[1]
    return pl.pallas_call(
        _norm_matmul_kernel,
        grid=(n // tm, m // tn),
        in_specs=[
            pl.BlockSpec((tm, d), lambda i, j: (i, 0)),
            pl.BlockSpec((1, d), lambda i, j: (0, 0)),
            pl.BlockSpec((d, tn), lambda i, j: (0, j)),
        ],
        out_specs=pl.BlockSpec((tm, tn), lambda i, j: (i, j)),
        out_shape=jax.ShapeDtypeStruct((n, m), F32),
        scratch_shapes=[pltpu.VMEM((tm, d), BF16)],
        compiler_params=_cparams(("parallel", "arbitrary")),
        name="norm_matmul",
    )(x, g.reshape(1, d), w)


def _mem_kv_kernel(m_ref, g_ref, wk_ref, wv_ref, k_ref, v_ref):
    u = _rmsnorm(m_ref[...], g_ref[...]).astype(BF16)
    k_ref[...] = _dot(u, wk_ref[...])
    v_ref[...] = _dot(u, wv_ref[...])


def _mem_kv(mem, g, wk, wv):
    bsz, nmem, d = mem.shape
    full = lambda a: pl.BlockSpec(a.shape, lambda b: (0,) * a.ndim)
    out = pl.BlockSpec((None, nmem, d), lambda b: (b, 0, 0))
    g = g.reshape(1, d)
    return pl.pallas_call(
        _mem_kv_kernel,
        grid=(bsz,),
        in_specs=[pl.BlockSpec((None, nmem, d), lambda b: (b, 0, 0)), full(g), full(wk), full(wv)],
        out_specs=[out, out],
        out_shape=[jax.ShapeDtypeStruct((bsz, nmem, d), F32)] * 2,
        compiler_params=_cparams(("parallel",)),
        name="mem_kv",
    )(mem, g, wk, wv)


def _hgrn_kernel(q_ref, f_ref, i_ref, g_ref, lbp_ref, ng_ref, s0_ref, o_ref, sout_ref, st_ref,
                 *, chunk, n_chunks, heads):
    t = pl.program_id(1)

    @pl.when(t == 0)
    def _():
        for h in range(heads):
            st_ref[h] = s0_ref[h].T

    lbp = lbp_ref[...]
    e = jnp.exp(lbp - jnp.max(lbp, axis=0, keepdims=True))
    lb = e[0:1] / jnp.sum(e, axis=0, keepdims=True)
    ng = ng_ref[...]

    row = lax.broadcasted_iota(jnp.int32, (chunk, chunk), 0)
    col = lax.broadcasted_iota(jnp.int32, (chunk, chunk), 1)
    causal = row >= col
    mid = chunk // 2
    to_mid = jnp.where((col >= mid) & (col <= row), 1.0, 0.0) - jnp.where((col > row) & (col < mid), 1.0, 0.0)
    coef = jnp.concatenate([causal.astype(F32), to_mid, (col > row).astype(F32)], axis=0).astype(BF16)

    chunks = range(n_chunks)
    qd, kd, qi, kl, el, vv, gate = [], [], [], [], [], [], []
    for c in chunks:
        rows = slice(c * chunk, (c + 1) * chunk)
        f = lb + (1.0 - lb) * _sigmoid(f_ref[rows, :])
        g_hi, g_lo = _split(jnp.log(f))
        bb = _dot(coef, g_hi) + _dot(coef, g_lo)
        b, b_to_mid, b_to_last = bb[:chunk], bb[chunk:2 * chunk], bb[2 * chunk:]
        q = _silu(q_ref[rows, :])
        k = 1.0 - f
        qd.append((q * jnp.exp(b_to_mid)).astype(BF16))
        kd.append((k * jnp.exp(-b_to_mid)).astype(BF16))
        qi.append((q * jnp.exp(b)).astype(BF16))
        kl.append((k * jnp.exp(b_to_last)).astype(BF16))
        el.append(jnp.exp(b[chunk - 1:chunk, :]))
        vv.append(i_ref[rows, :].astype(BF16))
        gate.append(_silu(g_ref[rows, :]))

    head = lambda x, h: x[:, h * HGRN_HEAD:(h + 1) * HGRN_HEAD]
    items = [(c, h) for c in chunks for h in range(heads)]
    sc = {ch: jnp.where(causal, _dot_nt(head(qd[ch[0]], ch[1]), head(kd[ch[0]], ch[1])), 0.0).astype(BF16)
          for ch in items}
    intra = {ch: _dot(sc[ch], head(vv[ch[0]], ch[1])) for ch in items}
    upd = {ch: _dot_tn(head(vv[ch[0]], ch[1]), head(kl[ch[0]], ch[1])) for ch in items}
    for h in range(heads):
        st = st_ref[h]
        for c in chunks:
            o = intra[c, h] + _dot_nt(head(qi[c], h), st.astype(BF16))
            st = st * head(el[c], h) + upd[c, h]
            o = o * lax.rsqrt(jnp.mean(o * o, axis=-1, keepdims=True) + NORM_EPS)
            o_ref[c * chunk:(c + 1) * chunk, h * HGRN_HEAD:(h + 1) * HGRN_HEAD] = o * ng * head(gate[c], h)
        st_ref[h] = st

    @pl.when(t == pl.num_programs(1) - 1)
    def _():
        for h in range(heads):
            sout_ref[h] = st_ref[h].T


def _hgrn(p3, lbp, ng, s0, tile, chunk):
    bsz, tlen, _ = p3.shape
    heads, dk, dv = s0.shape[1:]
    d = heads * HGRN_HEAD
    kern = functools.partial(_hgrn_kernel, chunk=chunk, n_chunks=tile // chunk, heads=heads)
    pspec = lambda cb: pl.BlockSpec((None, tile, d), lambda b, t: (b, t, cb))
    return pl.pallas_call(
        kern,
        grid=(bsz, tlen // tile),
        in_specs=[
            pspec(0), pspec(1), pspec(2), pspec(3),
            pl.BlockSpec(lbp.shape, lambda b, t: (0, 0)),
            pl.BlockSpec((1, dv), lambda b, t: (0, 0)),
            pl.BlockSpec((None, heads, dk, dv), lambda b, t: (b, 0, 0, 0)),
        ],
        out_specs=[
            pl.BlockSpec((None, tile, d), lambda b, t: (b, t, 0)),
            pl.BlockSpec((None, heads, dk, dv), lambda b, t: (b, 0, 0, 0)),
        ],
        out_shape=[
            jax.ShapeDtypeStruct((bsz, tlen, d), F32),
            jax.ShapeDtypeStruct((bsz, heads, dk, dv), F32),
        ],
        scratch_shapes=[pltpu.VMEM((heads, dv, dk), F32)],
        compiler_params=_cparams(("parallel", "arbitrary")),
        name="hgrn",
    )(p3, p3, p3, p3, lbp, ng.reshape(1, dv), s0)


_P_MU_R, _P_MU_K, _P_MU_V, _P_W0, _P_A0, _P_KK, _P_KA, _P_RK, _P_LNG, _P_LNB = range(10)
_P_ROWS = 16


def _rwkv_kernel(r_ref, k_ref, v_ref, l_ref, br_ref, bk_ref, bv_ref, bl_ref, par_ref, mul_ref,
                 w2_ref, a2_ref, g2a_ref, g2b_ref, s0_ref,
                 o_ref, sout_ref, st_ref, prev_ref, prevl_ref, *, chunk, n_chunks, pairs):
    t = pl.program_id(1)
    tile = chunk * n_chunks

    @pl.when(t == 0)
    def _():
        st_ref[...] = s0_ref[...]
        prev_ref[0:1, :] = br_ref[...]
        prev_ref[1:2, :] = bk_ref[...]
        prev_ref[2:3, :] = bv_ref[...]
        prevl_ref[0:1, :] = bl_ref[...]

    par = par_ref[...]
    prow = lambda i: par[i:i + 1, :]

    def shifted(x, prev_row, mu):
        row_id = lax.broadcasted_iota(jnp.int32, x.shape, 0)
        prev = jnp.where(row_id == 0, prev_row, pltpu.roll(x, 1, 0))
        return x + (prev - x) * mu

    pr, pk, pv, plr = r_ref[...], k_ref[...], v_ref[...], l_ref[...]
    r = shifted(pr, prev_ref[0:1, :], prow(_P_MU_R))
    k = shifted(pk, prev_ref[1:2, :], prow(_P_MU_K))
    v = shifted(pv, prev_ref[2:3, :], prow(_P_MU_V))
    xl = shifted(plr, prevl_ref[0:1, :], mul_ref[...])
    for slot, x in enumerate((pr, pk, pv)):
        prev_ref[slot:slot + 1, :] = x[tile - 1:tile, :]
    prevl_ref[0:1, :] = plr[tile - 1:tile, :]
    x0 = xl[:, 0:LANES]
    x1 = xl[:, LANES:2 * LANES]
    x2 = xl[:, 2 * LANES:3 * LANES]

    w_lin = prow(_P_W0) + _dot(jnp.tanh(x0).astype(BF16), w2_ref[...])
    z = -w_lin
    softplus = jnp.maximum(z, 0.0) + jnp.log(1.0 + jnp.exp(-jnp.abs(z)))
    lw = -jnp.exp(-softplus - 0.5)
    a_icl = _sigmoid(prow(_P_A0) + _dot(x0.astype(BF16), a2_ref[...]))
    g_out = _dot(_sigmoid(x1).astype(BF16), g2a_ref[...]) + _dot(_sigmoid(x2).astype(BF16), g2b_ref[...])

    lane = lax.broadcasted_iota(jnp.int32, (1, LANES), 1)
    m0 = (lane < RWKV_HEAD).astype(F32)
    m1 = 1.0 - m0
    head_shift = RWKV_HEAD.bit_length() - 1
    hr = lax.broadcasted_iota(jnp.int32, (LANES, LANES), 0) >> head_shift
    hc = lax.broadcasted_iota(jnp.int32, (LANES, LANES), 1) >> head_shift
    head_ones = (hr == hc).astype(BF16)
    tile_of = lambda x, j: x[:, j * LANES:(j + 1) * LANES]

    def headsum(x):
        xs = jnp.concatenate([tile_of(x, j) for j in range(pairs)], axis=0)
        ys = _dot(xs.astype(BF16), head_ones)
        return jnp.concatenate([ys[j * tile:(j + 1) * tile] for j in range(pairs)], axis=1)

    kk = k * prow(_P_KK)
    kk = kk * lax.rsqrt(jnp.maximum(headsum(kk * kk), 1e-24))
    k_mod = k * (1.0 + (a_icl - 1.0) * prow(_P_KA))
    a = -kk
    b = kk * a_icl

    ti = lax.broadcasted_iota(jnp.int32, (chunk, chunk), 0)
    tj = lax.broadcasted_iota(jnp.int32, (chunk, chunk), 1)
    tri = (ti >= tj).astype(BF16)
    n2 = 2 * chunk
    si = lax.broadcasted_iota(jnp.int32, (n2, n2), 0)
    sj = lax.broadcasted_iota(jnp.int32, (n2, n2), 1)
    chunk_shift = chunk.bit_length() - 1
    same_head = (si >> chunk_shift) == (sj >> chunk_shift)
    strict = same_head & ((sj & (chunk - 1)) < (si & (chunk - 1)))
    incl = same_head & ((sj & (chunk - 1)) <= (si & (chunk - 1)))
    eye = (si == sj).astype(F32)
    m0_b, m1_b = m0.astype(BF16), m1.astype(BF16)
    tile_b = lambda x, j: tile_of(x, j).astype(BF16)
    twice = lambda x: jnp.concatenate([x, x], axis=0)
    every = range(pairs)

    def stack_b(x, j):
        xb = tile_b(x, j)
        return jnp.concatenate([xb * m0_b, xb * m1_b], axis=0)

    def state_free(rows):
        lw_c, a_c, b_c, k_c, r_c, v_c = lw[rows], a[rows], b[rows], k_mod[rows], r[rows], v[rows]
        lw_hi, lw_lo = _split(lw_c)
        cum = _dot(tri, lw_hi) + _dot(tri, lw_lo)
        cum_last = cum[chunk - 1:chunk, :]
        e_neg = jnp.exp(-cum)
        e_end = jnp.exp(cum_last - cum)
        a_t = a_c * jnp.exp(cum - lw_c)
        b_t = b_c * e_neg
        k_t = k_c * e_neg
        r_t = r_c * jnp.exp(cum)
        b_end = b_c * e_end
        k_end = k_c * e_end

        a_st = [stack_b(a_t, j) for j in every]
        r_st = [stack_b(r_t, j) for j in every]
        v_st = [stack_b(v_c, j) for j in every]
        bend_st = [stack_b(b_end, j) for j in every]
        kend_st = [stack_b(k_end, j) for j in every]
        sc = [_dot_nt(jnp.concatenate([tile_b(a_t, j), tile_b(r_t, j)], axis=0),
                      jnp.concatenate([stack_b(b_t, j), stack_b(k_t, j)], axis=0)) for j in every]
        a_ab = [jnp.where(strict, twice(s[:chunk, :n2]), 0.0) for s in sc]
        a_xk = [jnp.concatenate([jnp.where(strict, twice(s[:chunk, n2:]), 0.0),
                                 jnp.where(incl, twice(s[chunk:, n2:]), 0.0)], axis=0).astype(BF16) for s in sc]
        a_rb = [jnp.where(incl, twice(s[chunk:, :n2]), 0.0).astype(BF16) for s in sc]

        inv = [eye + x for x in a_ab]
        pw = [_dot(x.astype(BF16), x.astype(BF16)) for x in a_ab]
        for _ in range(int(math.log2(chunk)) - 2):
            if n2 % LANES == 0:
                both = [_dot(p.astype(BF16), jnp.concatenate([p, i], axis=1).astype(BF16)) for p, i in zip(pw, inv)]
                pw_next = [x[:, :n2] for x in both]
                inv = [i + x[:, n2:] for i, x in zip(inv, both)]
            else:
                pw_next = [_dot(p.astype(BF16), p.astype(BF16)) for p in pw]
                inv = [i + _dot(p.astype(BF16), i.astype(BF16)) for i, p in zip(inv, pw)]
            pw = pw_next
        inv = [i + _dot(p.astype(BF16), i.astype(BF16)) for i, p in zip(inv, pw)]

        av = [_dot(a_xk[j], v_st[j]) for j in every]
        wu = [_dot(inv[j].astype(BF16), jnp.concatenate([a_st[j], av[j][:n2].astype(BF16)], axis=1)).astype(BF16)
              for j in every]
        qo = [_dot(a_rb[j], wu[j]) + jnp.concatenate([r_st[j].astype(F32), av[j][n2:]], axis=1) for j in every]
        mn = [_dot_tn(wu[j], bend_st[j]) for j in every]
        nn = [mn[j][LANES:] + _dot_tn(v_st[j], kend_st[j]) for j in every]
        return qo, mn, nn, jnp.exp(cum_last)

    parts = [state_free(slice(c * chunk, (c + 1) * chunk)) for c in range(n_chunks)]
    outs = []
    for qo, mn, nn, g_last in parts:
        row = []
        for j in every:
            s0 = st_ref[j]
            s0_b = s0.astype(BF16)
            q = (qo[j][:chunk, :LANES] + qo[j][chunk:, :LANES]).astype(BF16)
            row.append(_dot_nt(q, s0_b) + qo[j][:chunk, LANES:] + qo[j][chunk:, LANES:])
            st_ref[j] = s0 * tile_of(g_last, j) + _dot(s0_b, mn[j][:LANES].astype(BF16)) + nn[j]
        outs.append(jnp.concatenate(row, axis=1))
    o = outs[0] if n_chunks == 1 else jnp.concatenate(outs, axis=0)

    inv_n = 1.0 / RWKV_HEAD
    mean = headsum(o) * inv_n
    dlt = o - mean
    var = headsum(dlt * dlt) * inv_n
    o = dlt * lax.rsqrt(var + RWKV_GN_EPS) * prow(_P_LNG) + prow(_P_LNB)
    bonus = headsum(r * k_mod * prow(_P_RK))
    o_ref[...] = (o + bonus * v) * g_out

    @pl.when(t == pl.num_programs(1) - 1)
    def _():
        sout_ref[...] = st_ref[...]


def _rwkv(p3, buf_rkv, buf_l, par, mul, w2p, a2p, g2a, g2b, s0bd, d, tile, chunk):
    bsz, tlen, _ = p3.shape
    pairs = d // LANES
    nl = 3 * LANES
    kern = functools.partial(_rwkv_kernel, chunk=chunk, n_chunks=tile // chunk, pairs=pairs)
    pcol = lambda cb: pl.BlockSpec((None, tile, d), lambda b, t: (b, t, cb))
    bcol = lambda cb: pl.BlockSpec((None, 1, d), lambda b, t: (b, 0, cb))
    full = lambda a: pl.BlockSpec(a.shape, lambda b, t: (0,) * a.ndim)
    state = pl.BlockSpec((None, pairs, LANES, LANES), lambda b, t: (b, 0, 0, 0))
    return pl.pallas_call(
        kern,
        grid=(bsz, tlen // tile),
        in_specs=[
            pcol(4), pcol(5), pcol(6),
            pl.BlockSpec((None, tile, nl), lambda b, t: (b, t, 9 * d // nl)),
            bcol(0), bcol(1), bcol(2),
            pl.BlockSpec((None, 1, nl), lambda b, t: (b, 0, 0)),
            full(par), full(mul), full(w2p), full(a2p), full(g2a), full(g2b), state,
        ],
        out_specs=[pl.BlockSpec((None, tile, d), lambda b, t: (b, t, 0)), state],
        out_shape=[
            jax.ShapeDtypeStruct((bsz, tlen, d), F32),
            jax.ShapeDtypeStruct((bsz, pairs, LANES, LANES), F32),
        ],
        scratch_shapes=[pltpu.VMEM((pairs, LANES, LANES), F32), pltpu.VMEM((8, d), F32),
                        pltpu.VMEM((8, nl), F32)],
        compiler_params=_cparams(("parallel", "arbitrary")),
        name="rwkv",
    )(p3, p3, p3, p3, buf_rkv, buf_rkv, buf_rkv, buf_l, par, mul, w2p, a2p, g2a, g2b, s0bd)


def _attn_kernel(x_ref, oh_ref, or_ref, ga_ref, gb_ref, mk_ref, mv_ref, wout_ref, nxg_ref, wq_ref, wo_ref,
                 nfg_ref, wrh_ref, wrl_ref, rb_ref, h_ref, u_ref, wf_ref, att_ref, *, bt, tt):
    n = bt * tt
    d = x_ref.shape[-1]
    flat = lambda ref: ref[...].reshape(n, d)
    merged = _sigmoid(flat(ga_ref)) * flat(oh_ref) + _sigmoid(flat(gb_ref)) * flat(or_ref)
    h1 = flat(x_ref) + _dot(merged.astype(BF16), wout_ref[...])

    q = _dot(_rmsnorm(h1, nxg_ref[...]).astype(BF16), wq_ref[...])
    dh = d // X_HEADS
    scale = dh ** -0.5
    for bi in range(bt):
        rows = slice(bi * tt, (bi + 1) * tt)
        for hh in range(X_HEADS):
            cols = slice(hh * dh, (hh + 1) * dh)
            mem_head = lambda ref: (ref[bi, :, cols] if len(ref.shape) == 3 else ref[bi, :, hh, :]).astype(BF16)
            s = _dot_nt(q[rows, cols].astype(BF16), mem_head(mk_ref)) * scale
            p = jnp.exp(s - jnp.max(s, axis=-1, keepdims=True))
            p = p / jnp.sum(p, axis=-1, keepdims=True)
            att_ref[rows, cols] = _dot(p.astype(BF16), mem_head(mv_ref)).astype(BF16)
    h2 = h1 + _dot(att_ref[...], wo_ref[...])
    h_ref[...] = h2.reshape(bt, tt, d)

    u = _rmsnorm(h2, nfg_ref[...])
    u_ref[...] = u.astype(BF16).reshape(bt, tt, d)
    u_hi, u_lo = _split(u)
    logits = (_dot(u_hi, wrh_ref[...]) + _dot(u_hi, wrl_ref[...]) + _dot(u_lo, wrh_ref[...])) + rb_ref[...]
    lane = lax.broadcasted_iota(jnp.int32, (n, LANES), 1)
    lane_f = lane.astype(F32)
    is_group = (lane >= N_EXPERTS) & (lane < N_EXPERTS + N_GROUPS)
    lg = jnp.where(is_group, logits, NEG_BIG)
    g_max = jnp.max(lg, axis=-1, keepdims=True)
    g_idx = jnp.min(jnp.where(lg == g_max, lane_f, 1e9), axis=-1, keepdims=True) - N_EXPERTS
    pg = 1.0 / jnp.sum(jnp.exp(lg - g_max), axis=-1, keepdims=True)
    in_group = (lane < N_EXPERTS) & ((lane >> (EXPERTS_PER_GROUP.bit_length() - 1)).astype(F32) == g_idx)
    le = jnp.where(in_group, logits, NEG_BIG)
    m1 = jnp.max(le, axis=-1, keepdims=True)
    i1 = jnp.min(jnp.where(le == m1, lane_f, 1e9), axis=-1, keepdims=True)
    le2 = jnp.where(lane_f == i1, NEG_BIG, le)
    m2 = jnp.max(le2, axis=-1, keepdims=True)
    i2 = jnp.min(jnp.where(le2 == m2, lane_f, 1e9), axis=-1, keepdims=True)
    e2 = jnp.exp(m2 - m1)
    w1 = pg / (1.0 + e2)
    wf = jnp.where(lane_f == i1, w1, 0.0) + jnp.where(lane_f == i2, w1 * e2, 0.0)
    wf = wf + jnp.where(lane == _E1_LANE, i1, 0.0) + jnp.where(lane == _E2_LANE, i2, 0.0)
    wf_ref[...] = wf.reshape(bt, tt, LANES)


def _attn(x3, oh3, or3, p3, mk, mv, wout, nxg, wq, wo, nfg, wrh, wrl, rb, bt, tt, gate_block0):
    bsz, tlen, d = x3.shape
    kern = functools.partial(_attn_kernel, bt=bt, tt=tt)
    tok = lambda cb: pl.BlockSpec((bt, tt, d), lambda b, t: (b, t, cb))
    mem = pl.BlockSpec((bt,) + mk.shape[1:], lambda b, t: (b,) + (0,) * (mk.ndim - 1))
    full = lambda a: pl.BlockSpec(a.shape, lambda b, t: (0,) * a.ndim)
    nxg, nfg = nxg.reshape(1, d), nfg.reshape(1, d)
    return pl.pallas_call(
        kern,
        grid=(bsz // bt, tlen // tt),
        in_specs=[tok(0), tok(0), tok(0), tok(gate_block0), tok(gate_block0 + 1), mem, mem,
                  full(wout), full(nxg), full(wq), full(wo), full(nfg), full(wrh), full(wrl), full(rb)],
        out_specs=[
            pl.BlockSpec((bt, tt, d), lambda b, t: (b, t, 0)),
            pl.BlockSpec((bt, tt, d), lambda b, t: (b, t, 0)),
            pl.BlockSpec((bt, tt, LANES), lambda b, t: (b, t, 0)),
        ],
        out_shape=[
            jax.ShapeDtypeStruct((bsz, tlen, d), F32),
            jax.ShapeDtypeStruct((bsz, tlen, d), BF16),
            jax.ShapeDtypeStruct((bsz, tlen, LANES), F32),
        ],
        scratch_shapes=[pltpu.VMEM((bt * tt, d), BF16)],
        compiler_params=_cparams(("parallel", "parallel")),
        name="attn",
    )(x3, oh3, or3, p3, p3, mk, mv, wout, nxg, wq, wo, nfg, wrh, wrl, rb)


def _segment_dma(src, dst, src_row, dst_row, length, max_len, sem, start):
    size = _SEG
    while size * 2 <= max_len:
        size *= 2
    while size >= _SEG:
        offset = length & (-2 * size)

        @pl.when((length & size) != 0)
        def _(offset=offset, size=size):
            copy = pltpu.make_async_copy(src.at[pl.ds(pl.multiple_of(src_row + offset, _SEG), size)],
                                         dst.at[pl.ds(pl.multiple_of(dst_row + offset, _SEG), size)], sem)
            if start:
                copy.start()
            else:
                copy.wait()

        size //= 2


def _tile_meta(meta_ref, nt, tile):
    field = lambda k: [meta_ref[(k * nt + tile) * N_GROUPS + g] for g in range(N_GROUPS)]
    return field(0), field(1), field(2)


def _group_of(expert_id):
    return jnp.floor(expert_id * (1.0 / EXPERTS_PER_GROUP))


def _moe_sort_kernel(meta_ref, u_ref, wf_ref, xs_ref, ws_ref, su_ref, sw_ref, zu_ref, zw_ref, sem, *, nt, tm, te):
    i = pl.program_id(0)
    buf = lax.rem(i, 2)
    slots = su_ref.shape[1]
    _, _, base = _tile_meta(meta_ref, nt, i)
    wf = wf_ref[...]
    gid = _group_of(wf.T[_E1_LANE:_E1_LANE + 1, :])
    onehot = lax.broadcasted_iota(jnp.int32, (8, tm), 0).astype(F32) == gid
    earlier = (lax.broadcasted_iota(jnp.int32, (tm, tm), 0)
               < lax.broadcasted_iota(jnp.int32, (tm, tm), 1)).astype(BF16)
    seen = _dot(onehot.astype(BF16), earlier)
    pos = jnp.sum(jnp.where(onehot, seen, 0.0), axis=0, keepdims=True)
    for g in range(N_GROUPS):
        pos = pos + jnp.where(gid == float(g), base[g].astype(F32), 0.0)
    perm = (lax.broadcasted_iota(jnp.int32, (slots, tm), 0).astype(F32) == pos).astype(BF16)
    su_ref[buf] = _dot(perm, u_ref[...]).astype(BF16)
    w_hi, w_lo = _split(wf)
    sw_ref[buf] = _dot(perm, w_hi) + _dot(perm, w_lo)

    def copies(tile, b, start):
        off, cnt, base = _tile_meta(meta_ref, nt, tile)
        for g in range(N_GROUPS):
            _segment_dma(su_ref.at[b], xs_ref, base[g], off[g], cnt[g], tm, sem.at[0, b], start)
            _segment_dma(sw_ref.at[b], ws_ref, base[g], off[g], cnt[g], tm, sem.at[1, b], start)

    copies(i, buf, True)

    @pl.when(i > 0)
    def _():
        copies(i - 1, 1 - buf, False)

    @pl.when(i == nt - 1)
    def _():
        copies(i, buf, False)
        zu_ref[...] = jnp.zeros_like(zu_ref)
        zw_ref[...] = jnp.zeros_like(zw_ref)
        tail = 3 * nt * N_GROUPS + nt
        for start in (True, False):
            for g in range(N_GROUPS):
                off, cnt = meta_ref[tail + g], meta_ref[tail + N_GROUPS + g]
                _segment_dma(zu_ref, xs_ref, 0, off, cnt, te, sem.at[0, 0], start)
                _segment_dma(zw_ref, ws_ref, 0, off, cnt, te, sem.at[1, 0], start)

            def unused(r, carry, start=start):
                row = pl.multiple_of(r * te, te)
                for src, dst, s in ((zu_ref, xs_ref, sem.at[0, 1]), (zw_ref, ws_ref, sem.at[1, 1])):
                    copy = pltpu.make_async_copy(src, dst.at[pl.ds(row, te)], s)
                    if start:
                        copy.start()
                    else:
                        copy.wait()
                return carry

            lax.fori_loop(meta_ref[tail + 2 * N_GROUPS], xs_ref.shape[0] // te, unused, 0)


def _moe_expert_kernel(meta_ref, xs_ref, ws_ref, wg_ref, wu_ref, wd_ref, ys_ref, *, n_tiles):
    r = pl.program_id(0)
    used = r < meta_ref[n_tiles]

    @pl.when(used)
    def _():
        first = meta_ref[r] * EXPERTS_PER_GROUP
        x = xs_ref[...]
        ws = ws_ref[...]
        lane = lax.broadcasted_iota(jnp.int32, ws.shape, 1)
        hid = []
        for e in range(EXPERTS_PER_GROUP):
            h = _silu(_dot(x, wg_ref[e])) * _dot(x, wu_ref[e])
            w_col = jnp.sum(jnp.where(lane == first + e, ws, 0.0), axis=-1, keepdims=True)
            hid.append((h * w_col).astype(BF16))
        ys_ref[...] = _dot(jnp.concatenate(hid, axis=1), wd_ref[...]).astype(ys_ref.dtype)

    @pl.when(jnp.logical_not(used))
    def _():
        ys_ref[...] = jnp.zeros_like(ys_ref)


def _moe_unsort_kernel(meta_ref, wf_ref, h_ref, gf_ref, ys_ref, y_ref, sy_ref, sem, *, nt, tm):
    i = pl.program_id(0)
    buf = lax.rem(i, 2)
    slots = sy_ref.shape[1]

    def copies(tile, b, start):
        off, cnt, base = _tile_meta(meta_ref, nt, tile)
        for g in range(N_GROUPS):
            _segment_dma(ys_ref, sy_ref.at[b], off[g], base[g], cnt[g], tm, sem.at[b], start)

    @pl.when(i == 0)
    def _():
        copies(0, 0, True)

    @pl.when(i + 1 < nt)
    def _():
        copies(i + 1, 1 - buf, True)

    _, _, base = _tile_meta(meta_ref, nt, i)
    gid = _group_of(wf_ref[...][:, _E1_LANE:_E1_LANE + 1])
    onehot = lax.broadcasted_iota(jnp.int32, (tm, LANES), 1).astype(F32) == gid
    earlier = (lax.broadcasted_iota(jnp.int32, (tm, tm), 1)
               < lax.broadcasted_iota(jnp.int32, (tm, tm), 0)).astype(BF16)
    seen = _dot(earlier, onehot.astype(BF16))
    pos = jnp.sum(jnp.where(onehot, seen, 0.0), axis=1, keepdims=True)
    for g in range(N_GROUPS):
        pos = pos + jnp.where(gid == float(g), base[g].astype(F32), 0.0)
    perm_t = (lax.broadcasted_iota(jnp.int32, (tm, slots), 1).astype(F32) == pos).astype(BF16)
    copies(i, buf, False)
    slot_row = lax.broadcasted_iota(jnp.int32, (slots, 1), 0)
    ys = sy_ref[buf]
    moe = _dot(perm_t, jnp.where(slot_row < meta_ref[3 * nt * N_GROUPS + i], ys, jnp.zeros_like(ys)))
    y_ref[...] = _rmsnorm(h_ref[...] + moe, gf_ref[...])


def _moe(u, wf, h, wg, wu, wd, gf, tm, te):
    n, d = u.shape
    ne, _, f = wg.shape
    nt = n // tm
    slots = tm + LANES
    n_tiles = -(-(n + nt * N_GROUPS * (_SEG - 1) + N_GROUPS * (te - 1)) // te)
    rows = n_tiles * te
    i32 = jnp.int32

    gid = (wf[:, _E1_LANE].astype(i32) // EXPERTS_PER_GROUP).reshape(nt, tm)
    cnt = jnp.sum((gid[:, :, None] == jnp.arange(N_GROUPS, dtype=i32)).astype(i32), axis=1)
    cnt = (cnt + _SEG - 1) // _SEG * _SEG
    base = jnp.cumsum(cnt, axis=1) - cnt
    total = jnp.sum(cnt, axis=0)
    region = (total + te - 1) // te * te
    region_end = jnp.cumsum(region)
    region_start = region_end - region
    off = region_start[None, :] + jnp.cumsum(cnt, axis=0) - cnt
    n_used = region_end[-1:] // te
    meta = jnp.concatenate([off.reshape(-1), cnt.reshape(-1), base.reshape(-1), jnp.sum(cnt, axis=1),
                            region_start + total, region - total, n_used]).astype(i32)
    tile_row = jnp.arange(n_tiles, dtype=i32) * te
    tile_group = jnp.minimum(jnp.sum((tile_row[:, None] >= region_end[None, :]).astype(i32), axis=1), N_GROUPS - 1)
    emeta = jnp.concatenate([tile_group, n_used]).astype(i32)

    anyspec = pl.BlockSpec(memory_space=pl.ANY)
    tok = lambda w: pl.BlockSpec((tm, w), lambda i, m: (i, 0))
    xs, ws = pl.pallas_call(
        functools.partial(_moe_sort_kernel, nt=nt, tm=tm, te=te),
        grid_spec=pltpu.PrefetchScalarGridSpec(
            num_scalar_prefetch=1, grid=(nt,),
            in_specs=[tok(d), tok(LANES)],
            out_specs=[anyspec, anyspec],
            scratch_shapes=[pltpu.VMEM((2, slots, d), BF16), pltpu.VMEM((2, slots, LANES), F32),
                            pltpu.VMEM((te, d), BF16), pltpu.VMEM((te, LANES), F32),
                            pltpu.SemaphoreType.DMA((2, 2))]),
        out_shape=[jax.ShapeDtypeStruct((rows, d), BF16), jax.ShapeDtypeStruct((rows, LANES), F32)],
        compiler_params=_cparams(("arbitrary",)),
        name="moe_sort",
    )(meta, u, wf)

    srt = lambda w: pl.BlockSpec((te, w), lambda r, m: (r, 0))
    grp_w = pl.BlockSpec((EXPERTS_PER_GROUP, d, f), lambda r, m: (m[r], 0, 0))
    ys = pl.pallas_call(
        functools.partial(_moe_expert_kernel, n_tiles=n_tiles),
        grid_spec=pltpu.PrefetchScalarGridSpec(
            num_scalar_prefetch=1, grid=(n_tiles,),
            in_specs=[srt(d), srt(LANES), grp_w, grp_w,
                      pl.BlockSpec((EXPERTS_PER_GROUP * f, d), lambda r, m: (m[r], 0))],
            out_specs=srt(d)),
        out_shape=jax.ShapeDtypeStruct((rows, d), BF16),
        compiler_params=_cparams(("parallel",)),
        name="moe_expert",
    )(emeta, xs, ws, wg, wu, wd.reshape(ne * f, d))

    return pl.pallas_call(
        functools.partial(_moe_unsort_kernel, nt=nt, tm=tm),
        grid_spec=pltpu.PrefetchScalarGridSpec(
            num_scalar_prefetch=1, grid=(nt,),
            in_specs=[tok(LANES), tok(d), pl.BlockSpec((1, d), lambda i, m: (0, 0)), anyspec],
            out_specs=tok(d),
            scratch_shapes=[pltpu.VMEM((2, slots, d), BF16), pltpu.SemaphoreType.DMA((2,))]),
        out_shape=jax.ShapeDtypeStruct((n, d), F32),
        compiler_params=_cparams(("arbitrary",)),
        name="moe_unsort",
    )(meta, wf, h, gf.reshape(1, d), ys)


def _pick(n, pref):
    t = min(n, pref)
    while n % t:
        t -= 8
    return t


def _block_diag_pairs(s):
    b, h, n, _ = s.shape
    s = s.reshape(b, h // 2, 2, n, n)
    z = jnp.zeros_like(s[:, :, 0])
    top = jnp.concatenate([s[:, :, 0], z], axis=-1)
    bot = jnp.concatenate([z, s[:, :, 1]], axis=-1)
    return jnp.concatenate([top, bot], axis=-2)


def _pairs_to_heads(sbd):
    b, p, n2, _ = sbd.shape
    n = n2 // 2
    s = jnp.stack([sbd[:, :, :n, :n], sbd[:, :, n:, n:]], axis=2)
    return s.reshape(b, 2 * p, n, n)


def kernel(x_prompt, x_sample, mem_prompt, cache_mem_k, cache_mem_v, state_hgrn, state_rwkv, state_rwkv_shift, norm_mix_g, w_in, hgrn_lower_bounds, hgrn_norm_g, rwkv_mu, rwkv_w0, rwkv_w2, rwkv_a0, rwkv_a2, rwkv_g2, rwkv_k_k, rwkv_k_a, rwkv_r_k, rwkv_ln_g, rwkv_ln_b, w_out, norm_x_g, norm_mem_g, wq_x, wk_x, wv_x, wo_x, norm_ffn_g, router_group_w, router_group_b, router_expert_w, router_expert_b, expert_w_gate, expert_w_up, expert_w_down, norm_final_g):
    assert w_in.shape[0] == 1, "single-layer configuration"
    d = x_prompt.shape[-1]
    n_lora_w, n_lora_a, n_lora_g = rwkv_w2.shape[1], rwkv_a2.shape[1], rwkv_g2.shape[1]
    assert n_lora_w + n_lora_a == LANES and LANES < n_lora_g <= 2 * LANES
    n_lora = n_lora_w + n_lora_a + n_lora_g
    n_rwkv_in = 3 * d + n_lora
    lora_pad = 3 * LANES - n_lora

    w = w_in[0]
    pr0 = 4 * d
    gate0 = pr0 + n_rwkv_in
    w_perm = jnp.concatenate(
        [w[:, :pr0 + 3 * d], w[:, gate0:gate0 + 2 * d], w[:, pr0 + 3 * d:gate0],
         jnp.zeros((d, lora_pad), w.dtype)], axis=1).astype(BF16)
    n_cols = w_perm.shape[1]
    lora_col0 = 9 * d
    gate_block0 = 7
    mu = rwkv_mu[0]
    mul = jnp.pad(mu[3 * d:], (0, lora_pad)).reshape(1, 3 * LANES)
    par = jnp.stack([mu[:d], mu[d:2 * d], mu[2 * d:3 * d], rwkv_w0[0], rwkv_a0[0], rwkv_k_k[0], rwkv_k_a[0],
                     rwkv_r_k[0].reshape(d), rwkv_ln_g[0], rwkv_ln_b[0]])
    par = jnp.pad(par, ((0, _P_ROWS - par.shape[0]), (0, 0)))
    w2p = jnp.pad(rwkv_w2[0], ((0, n_lora_a), (0, 0))).astype(BF16)
    a2p = jnp.pad(rwkv_a2[0], ((n_lora_w, 0), (0, 0))).astype(BF16)
    g2a = rwkv_g2[0][:LANES].astype(BF16)
    g2b = jnp.pad(rwkv_g2[0][LANES:], ((0, 2 * LANES - n_lora_g), (0, 0))).astype(BF16)
    wout_b, wq_b, wo_b = w_out[0].astype(BF16), wq_x[0].astype(BF16), wo_x[0].astype(BF16)
    wr = jnp.concatenate([router_expert_w[0], router_group_w[0],
                          jnp.zeros((d, LANES - N_EXPERTS - N_GROUPS), F32)], axis=1)
    wr_hi, wr_lo = _split(wr)
    rb = jnp.pad(jnp.concatenate([router_expert_b[0].reshape(-1), router_group_b[0]]),
                 (0, LANES - N_EXPERTS - N_GROUPS)).reshape(1, LANES)
    wg_b, wu_b, wd_b = (expert_w_gate[0].astype(BF16), expert_w_up[0].astype(BF16),
                        expert_w_down[0].astype(BF16))

    nb_p = mem_prompt.shape[0]
    mem_k_p, mem_v_p = _mem_kv(mem_prompt, norm_mem_g[0], wk_x[0].astype(BF16), wv_x[0].astype(BF16))

    def group(x3, mem_k, mem_v, s_h, s_r, buf, tile_h, tile_r, chunk_r, bt, tt, te):
        bsz, tlen, _ = x3.shape
        n = bsz * tlen
        p = _norm_matmul(x3.reshape(n, d), norm_mix_g[0], w_perm, _pick(n, 1024), n_cols // 5)
        p3 = p.reshape(bsz, tlen, n_cols)
        o_h, s_h_new = _hgrn(p3, hgrn_lower_bounds, hgrn_norm_g[0], s_h, tile_h, 32)
        buf_rkv = buf[:, :, :3 * d]
        buf_l = jnp.pad(buf[:, :, 3 * d:], ((0, 0), (0, 0), (0, lora_pad)))
        o_r, s_r_bd = _rwkv(p3, buf_rkv, buf_l, par, mul, w2p, a2p, g2a, g2b, _block_diag_pairs(s_r), d, tile_r, chunk_r)
        h2, u, wf = _attn(x3, o_h, o_r, p3, mem_k, mem_v, wout_b, norm_x_g[0], wq_b, wo_b, norm_ffn_g[0],
                          wr_hi, wr_lo, rb, bt, tt, gate_block0)
        y = _moe(u.reshape(n, d), wf.reshape(n, LANES), h2.reshape(n, d), wg_b, wu_b, wd_b, norm_final_g,
                 _pick(n, 512), te)
        shift = jnp.concatenate([p3[:, -1:, pr0:pr0 + 3 * d], p3[:, -1:, lora_col0:lora_col0 + n_lora]], axis=-1)
        return y.reshape(bsz, tlen, d), s_h_new, _pairs_to_heads(s_r_bd), shift

    nb_s, t_s, _ = x_sample.shape
    t_p = x_prompt.shape[1]
    zeros_like_state = lambda s: jnp.zeros((nb_p,) + s.shape[2:], s.dtype)
    y_p, sh_p, sr_p, bf_p = group(
        x_prompt, mem_k_p, mem_v_p, zeros_like_state(state_hgrn), zeros_like_state(state_rwkv),
        zeros_like_state(state_rwkv_shift), _pick(t_p, 256), _pick(t_p, 256), 64, 1, _pick(t_p, 256), 512)
    y_s, sh_s, sr_s, bf_s = group(
        x_sample, cache_mem_k[0], cache_mem_v[0],
        state_hgrn[0], state_rwkv[0], state_rwkv_shift[0], t_s, t_s, t_s, _pick(nb_s, 4), t_s, 256)

    kv_shape = (1,) + mem_k_p.shape[:2] + cache_mem_k.shape[3:]
    return (y_p, y_s, mem_k_p.reshape(kv_shape), mem_v_p.reshape(kv_shape), sh_p[None], sh_s[None],
            sr_p[None], sr_s[None], bf_p[None], bf_s[None])
```

```python
import functools
import math

import jax
import jax.numpy as jnp
from jax import lax
from jax.experimental import pallas as pl
from jax.experimental.pallas import tpu as pltpu

F32 = jnp.float32
BF16 = jnp.bfloat16

NORM_EPS = 1e-6
RWKV_GN_EPS = 64e-5
LANES = 128
HGRN_HEAD = 128
RWKV_HEAD = 64
X_HEADS = 4
N_GROUPS = 4
EXPERTS_PER_GROUP = 8
N_EXPERTS = N_GROUPS * EXPERTS_PER_GROUP
NEG_BIG = -1e30
VMEM_LIMIT = 52 * 1024 * 1024
_E1_LANE, _E2_LANE = N_EXPERTS, N_EXPERTS + 1
_SEG = 16


def _dot(a, b):
    return jnp.dot(a, b, preferred_element_type=F32)


def _dot_nt(a, b):
    return lax.dot_general(a, b, (((1,), (1,)), ((), ())), preferred_element_type=F32)


def _dot_tn(a, b):
    return lax.dot_general(a, b, (((0,), (0,)), ((), ())), preferred_element_type=F32)


def _split(x):
    hi = x.astype(BF16)
    lo = (x - hi.astype(F32)).astype(BF16)
    return hi, lo


def _sigmoid(x):
    return 0.5 * jnp.tanh(0.5 * x) + 0.5


def _silu(x):
    return x * _sigmoid(x)


def _rmsnorm(x, g):
    return x * lax.rsqrt(jnp.mean(x * x, axis=-1, keepdims=True) + NORM_EPS) * g


def _cparams(sem):
    return pltpu.CompilerParams(dimension_semantics=sem, vmem_limit_bytes=VMEM_LIMIT)


def _norm_matmul_kernel(x_ref, g_ref, w_ref, o_ref, u_ref):
    @pl.when(pl.program_id(1) == 0)
    def _():
        u_ref[...] = _rmsnorm(x_ref[...], g_ref[...]).astype(BF16)

    o_ref[...] = _dot(u_ref[...], w_ref[...])


def _norm_matmul(x, g, w, tm, tn):
    n, d = x.shape
    m = w.shape[1]
    return pl.pallas_call(
        _norm_matmul_kernel,
        grid=(n // tm, m // tn),
        in_specs=[
            pl.BlockSpec((tm, d), lambda i, j: (i, 0)),
            pl.BlockSpec((1, d), lambda i, j: (0, 0)),
            pl.BlockSpec((d, tn), lambda i, j: (0, j)),
        ],
        out_specs=pl.BlockSpec((tm, tn), lambda i, j: (i, j)),
        out_shape=jax.ShapeDtypeStruct((n, m), F32),
        scratch_shapes=[pltpu.VMEM((tm, d), BF16)],
        compiler_params=_cparams(("parallel", "arbitrary")),
        name="norm_matmul",
    )(x, g.reshape(1, d), w)


def _permute_columns_kernel(w_ref, o_ref, *, pieces):
    col = 0
    for src, width in pieces:
        o_ref[:, col:col + width] = w_ref[:, src:src + width].astype(o_ref.dtype)
        col += width
    if col < o_ref.shape[1]:
        o_ref[:, col:] = jnp.zeros((o_ref.shape[0], o_ref.shape[1] - col), o_ref.dtype)


def _permute_columns(w, pieces, n_cols, tr):
    rows, cols = w.shape
    return pl.pallas_call(
        functools.partial(_permute_columns_kernel, pieces=pieces),
        grid=(rows // tr,),
        in_specs=[pl.BlockSpec((tr, cols), lambda i: (i, 0))],
        out_specs=pl.BlockSpec((tr, n_cols), lambda i: (i, 0)),
        out_shape=jax.ShapeDtypeStruct((rows, n_cols), BF16),
        compiler_params=_cparams(("parallel",)),
        name="permute_columns",
    )(w)


def _mem_kv_kernel(m_ref, g_ref, wk_ref, wv_ref, k_ref, v_ref):
    u = _rmsnorm(m_ref[...], g_ref[...]).astype(BF16)
    k_ref[...] = _dot(u, wk_ref[...])
    v_ref[...] = _dot(u, wv_ref[...])


def _mem_kv(mem, g, wk, wv):
    bsz, nmem, d = mem.shape
    full = lambda a: pl.BlockSpec(a.shape, lambda b: (0,) * a.ndim)
    out = pl.BlockSpec((None, nmem, d), lambda b: (b, 0, 0))
    g = g.reshape(1, d)
    return pl.pallas_call(
        _mem_kv_kernel,
        grid=(bsz,),
        in_specs=[pl.BlockSpec((None, nmem, d), lambda b: (b, 0, 0)), full(g), full(wk), full(wv)],
        out_specs=[out, out],
        out_shape=[jax.ShapeDtypeStruct((bsz, nmem, d), F32)] * 2,
        compiler_params=_cparams(("parallel",)),
        name="mem_kv",
    )(mem, g, wk, wv)


def _hgrn_kernel(q_ref, f_ref, i_ref, g_ref, lbp_ref, ng_ref, s0_ref, o_ref, sout_ref, st_ref,
                 *, chunk, n_chunks, heads):
    t = pl.program_id(1)

    @pl.when(t == 0)
    def _():
        for h in range(heads):
            st_ref[h] = s0_ref[h].T

    lbp = lbp_ref[...]
    e = jnp.exp(lbp - jnp.max(lbp, axis=0, keepdims=True))
    lb = e[0:1] / jnp.sum(e, axis=0, keepdims=True)
    ng = ng_ref[...]

    row = lax.broadcasted_iota(jnp.int32, (chunk, chunk), 0)
    col = lax.broadcasted_iota(jnp.int32, (chunk, chunk), 1)
    causal = row >= col
    mid = chunk // 2
    to_mid = jnp.where((col >= mid) & (col <= row), 1.0, 0.0) - jnp.where((col > row) & (col < mid), 1.0, 0.0)
    coef = jnp.concatenate([causal.astype(F32), to_mid, (col > row).astype(F32)], axis=0).astype(BF16)

    chunks = range(n_chunks)
    qd, kd, qi, kl, el, vv, gate = [], [], [], [], [], [], []
    for c in chunks:
        rows = slice(c * chunk, (c + 1) * chunk)
        f = lb + (1.0 - lb) * _sigmoid(f_ref[rows, :])
        g_hi, g_lo = _split(jnp.log(f))
        bb = _dot(coef, g_hi) + _dot(coef, g_lo)
        b, b_to_mid, b_to_last = bb[:chunk], bb[chunk:2 * chunk], bb[2 * chunk:]
        q = _silu(q_ref[rows, :])
        k = 1.0 - f
        qd.append((q * jnp.exp(b_to_mid)).astype(BF16))
        kd.append((k * jnp.exp(-b_to_mid)).astype(BF16))
        qi.append((q * jnp.exp(b)).astype(BF16))
        kl.append((k * jnp.exp(b_to_last)).astype(BF16))
        el.append(jnp.exp(b[chunk - 1:chunk, :]))
        vv.append(i_ref[rows, :].astype(BF16))
        gate.append(_silu(g_ref[rows, :]))

    head = lambda x, h: x[:, h * HGRN_HEAD:(h + 1) * HGRN_HEAD]
    items = [(c, h) for c in chunks for h in range(heads)]
    sc = {ch: jnp.where(causal, _dot_nt(head(qd[ch[0]], ch[1]), head(kd[ch[0]], ch[1])), 0.0).astype(BF16)
          for ch in items}
    intra = {ch: _dot(sc[ch], head(vv[ch[0]], ch[1])) for ch in items}
    upd = {ch: _dot_tn(head(vv[ch[0]], ch[1]), head(kl[ch[0]], ch[1])) for ch in items}
    for h in range(heads):
        st = st_ref[h]
        for c in chunks:
            o = intra[c, h] + _dot_nt(head(qi[c], h), st.astype(BF16))
            st = st * head(el[c], h) + upd[c, h]
            o = o * lax.rsqrt(jnp.mean(o * o, axis=-1, keepdims=True) + NORM_EPS)
            o_ref[c * chunk:(c + 1) * chunk, h * HGRN_HEAD:(h + 1) * HGRN_HEAD] = o * ng * head(gate[c], h)
        st_ref[h] = st

    @pl.when(t == pl.num_programs(1) - 1)
    def _():
        for h in range(heads):
            sout_ref[h] = st_ref[h].T


def _hgrn(p3, lbp, ng, s0, tile, chunk):
    bsz, tlen, _ = p3.shape
    heads, dk, dv = s0.shape[1:]
    d = heads * HGRN_HEAD
    kern = functools.partial(_hgrn_kernel, chunk=chunk, n_chunks=tile // chunk, heads=heads)
    pspec = lambda cb: pl.BlockSpec((None, tile, d), lambda b, t: (b, t, cb))
    return pl.pallas_call(
        kern,
        grid=(bsz, tlen // tile),
        in_specs=[
            pspec(0), pspec(1), pspec(2), pspec(3),
            pl.BlockSpec(lbp.shape, lambda b, t: (0, 0)),
            pl.BlockSpec((1, dv), lambda b, t: (0, 0)),
            pl.BlockSpec((None, heads, dk, dv), lambda b, t: (b, 0, 0, 0)),
        ],
        out_specs=[
            pl.BlockSpec((None, tile, d), lambda b, t: (b, t, 0)),
            pl.BlockSpec((None, heads, dk, dv), lambda b, t: (b, 0, 0, 0)),
        ],
        out_shape=[
            jax.ShapeDtypeStruct((bsz, tlen, d), F32),
            jax.ShapeDtypeStruct((bsz, heads, dk, dv), F32),
        ],
        scratch_shapes=[pltpu.VMEM((heads, dv, dk), F32)],
        compiler_params=_cparams(("parallel", "arbitrary")),
        name="hgrn",
    )(p3, p3, p3, p3, lbp, ng.reshape(1, dv), s0)


_P_MU_R, _P_MU_K, _P_MU_V, _P_W0, _P_A0, _P_KK, _P_KA, _P_RK, _P_LNG, _P_LNB = range(10)
_P_ROWS = 16


def _rwkv_kernel(r_ref, k_ref, v_ref, l_ref, br_ref, bk_ref, bv_ref, bl_ref, par_ref, mul_ref,
                 w2_ref, a2_ref, g2a_ref, g2b_ref, s0_ref,
                 o_ref, sout_ref, st_ref, prev_ref, prevl_ref, *, chunk, n_chunks, pairs):
    t = pl.program_id(1)
    tile = chunk * n_chunks

    @pl.when(t == 0)
    def _():
        st_ref[...] = s0_ref[...]
        prev_ref[0:1, :] = br_ref[...]
        prev_ref[1:2, :] = bk_ref[...]
        prev_ref[2:3, :] = bv_ref[...]
        prevl_ref[0:1, :] = bl_ref[...]

    par = par_ref[...]
    prow = lambda i: par[i:i + 1, :]

    def shifted(x, prev_row, mu):
        row_id = lax.broadcasted_iota(jnp.int32, x.shape, 0)
        prev = jnp.where(row_id == 0, prev_row, pltpu.roll(x, 1, 0))
        return x + (prev - x) * mu

    pr, pk, pv, plr = r_ref[...], k_ref[...], v_ref[...], l_ref[...]
    r = shifted(pr, prev_ref[0:1, :], prow(_P_MU_R))
    k = shifted(pk, prev_ref[1:2, :], prow(_P_MU_K))
    v = shifted(pv, prev_ref[2:3, :], prow(_P_MU_V))
    xl = shifted(plr, prevl_ref[0:1, :], mul_ref[...])
    for slot, x in enumerate((pr, pk, pv)):
        prev_ref[slot:slot + 1, :] = x[tile - 1:tile, :]
    prevl_ref[0:1, :] = plr[tile - 1:tile, :]
    x0 = xl[:, 0:LANES]
    x1 = xl[:, LANES:2 * LANES]
    x2 = xl[:, 2 * LANES:3 * LANES]

    w_lin = prow(_P_W0) + _dot(jnp.tanh(x0).astype(BF16), w2_ref[...])
    z = -w_lin
    softplus = jnp.maximum(z, 0.0) + jnp.log(1.0 + jnp.exp(-jnp.abs(z)))
    lw = -jnp.exp(-softplus - 0.5)
    a_icl = _sigmoid(prow(_P_A0) + _dot(x0.astype(BF16), a2_ref[...]))
    g_out = _dot(_sigmoid(x1).astype(BF16), g2a_ref[...]) + _dot(_sigmoid(x2).astype(BF16), g2b_ref[...])

    lane = lax.broadcasted_iota(jnp.int32, (1, LANES), 1)
    m0 = (lane < RWKV_HEAD).astype(F32)
    m1 = 1.0 - m0
    head_shift = RWKV_HEAD.bit_length() - 1
    hr = lax.broadcasted_iota(jnp.int32, (LANES, LANES), 0) >> head_shift
    hc = lax.broadcasted_iota(jnp.int32, (LANES, LANES), 1) >> head_shift
    head_ones = (hr == hc).astype(BF16)
    tile_of = lambda x, j: x[:, j * LANES:(j + 1) * LANES]

    def headsum(x):
        xs = jnp.concatenate([tile_of(x, j) for j in range(pairs)], axis=0)
        ys = _dot(xs.astype(BF16), head_ones)
        return jnp.concatenate([ys[j * tile:(j + 1) * tile] for j in range(pairs)], axis=1)

    kk = k * prow(_P_KK)
    kk = kk * lax.rsqrt(jnp.maximum(headsum(kk * kk), 1e-24))
    k_mod = k * (1.0 + (a_icl - 1.0) * prow(_P_KA))
    a = -kk
    b = kk * a_icl

    ti = lax.broadcasted_iota(jnp.int32, (chunk, chunk), 0)
    tj = lax.broadcasted_iota(jnp.int32, (chunk, chunk), 1)
    tri = (ti >= tj).astype(BF16)
    n2 = 2 * chunk
    si = lax.broadcasted_iota(jnp.int32, (n2, n2), 0)
    sj = lax.broadcasted_iota(jnp.int32, (n2, n2), 1)
    chunk_shift = chunk.bit_length() - 1
    same_head = (si >> chunk_shift) == (sj >> chunk_shift)
    strict = same_head & ((sj & (chunk - 1)) < (si & (chunk - 1)))
    incl = same_head & ((sj & (chunk - 1)) <= (si & (chunk - 1)))
    eye = (si == sj).astype(F32)
    m0_b, m1_b = m0.astype(BF16), m1.astype(BF16)
    tile_b = lambda x, j: tile_of(x, j).astype(BF16)
    twice = lambda x: jnp.concatenate([x, x], axis=0)
    every = range(pairs)

    def stack_b(x, j):
        xb = tile_b(x, j)
        return jnp.concatenate([xb * m0_b, xb * m1_b], axis=0)

    def state_free(rows):
        lw_c, a_c, b_c, k_c, r_c, v_c = lw[rows], a[rows], b[rows], k_mod[rows], r[rows], v[rows]
        lw_hi, lw_lo = _split(lw_c)
        cum = _dot(tri, lw_hi) + _dot(tri, lw_lo)
        cum_last = cum[chunk - 1:chunk, :]
        e_neg = jnp.exp(-cum)
        e_end = jnp.exp(cum_last - cum)
        a_t = a_c * jnp.exp(cum - lw_c)
        b_t = b_c * e_neg
        k_t = k_c * e_neg
        r_t = r_c * jnp.exp(cum)
        b_end = b_c * e_end
        k_end = k_c * e_end

        a_st = [stack_b(a_t, j) for j in every]
        r_st = [stack_b(r_t, j) for j in every]
        v_st = [stack_b(v_c, j) for j in every]
        bend_st = [stack_b(b_end, j) for j in every]
        kend_st = [stack_b(k_end, j) for j in every]
        sc = [_dot_nt(jnp.concatenate([tile_b(a_t, j), tile_b(r_t, j)], axis=0),
                      jnp.concatenate([stack_b(b_t, j), stack_b(k_t, j)], axis=0)) for j in every]
        a_ab = [jnp.where(strict, twice(s[:chunk, :n2]), 0.0) for s in sc]
        a_xk = [jnp.concatenate([jnp.where(strict, twice(s[:chunk, n2:]), 0.0),
                                 jnp.where(incl, twice(s[chunk:, n2:]), 0.0)], axis=0).astype(BF16) for s in sc]
        a_rb = [jnp.where(incl, twice(s[chunk:, :n2]), 0.0).astype(BF16) for s in sc]

        inv = [eye + x for x in a_ab]
        pw = [_dot(x.astype(BF16), x.astype(BF16)) for x in a_ab]
        for _ in range(int(math.log2(chunk)) - 2):
            if n2 % LANES == 0:
                both = [_dot(p.astype(BF16), jnp.concatenate([p, i], axis=1).astype(BF16)) for p, i in zip(pw, inv)]
                pw_next = [x[:, :n2] for x in both]
                inv = [i + x[:, n2:] for i, x in zip(inv, both)]
            else:
                pw_next = [_dot(p.astype(BF16), p.astype(BF16)) for p in pw]
                inv = [i + _dot(p.astype(BF16), i.astype(BF16)) for i, p in zip(inv, pw)]
            pw = pw_next
        inv = [i + _dot(p.astype(BF16), i.astype(BF16)) for i, p in zip(inv, pw)]

        av = [_dot(a_xk[j], v_st[j]) for j in every]
        wu = [_dot(inv[j].astype(BF16), jnp.concatenate([a_st[j], av[j][:n2].astype(BF16)], axis=1)).astype(BF16)
              for j in every]
        qo = [_dot(a_rb[j], wu[j]) + jnp.concatenate([r_st[j].astype(F32), av[j][n2:]], axis=1) for j in every]
        mn = [_dot_tn(wu[j], bend_st[j]) for j in every]
        nn = [mn[j][LANES:] + _dot_tn(v_st[j], kend_st[j]) for j in every]
        return qo, mn, nn, jnp.exp(cum_last)

    parts = [state_free(slice(c * chunk, (c + 1) * chunk)) for c in range(n_chunks)]
    outs = []
    for qo, mn, nn, g_last in parts:
        row = []
        for j in every:
            s0 = st_ref[j]
            s0_b = s0.astype(BF16)
            q = (qo[j][:chunk, :LANES] + qo[j][chunk:, :LANES]).astype(BF16)
            row.append(_dot_nt(q, s0_b) + qo[j][:chunk, LANES:] + qo[j][chunk:, LANES:])
            st_ref[j] = s0 * tile_of(g_last, j) + _dot(s0_b, mn[j][:LANES].astype(BF16)) + nn[j]
        outs.append(jnp.concatenate(row, axis=1))
    o = outs[0] if n_chunks == 1 else jnp.concatenate(outs, axis=0)

    inv_n = 1.0 / RWKV_HEAD
    mean = headsum(o) * inv_n
    dlt = o - mean
    var = headsum(dlt * dlt) * inv_n
    o = dlt * lax.rsqrt(var + RWKV_GN_EPS) * prow(_P_LNG) + prow(_P_LNB)
    bonus = headsum(r * k_mod * prow(_P_RK))
    o_ref[...] = (o + bonus * v) * g_out

    @pl.when(t == pl.num_programs(1) - 1)
    def _():
        sout_ref[...] = st_ref[...]


def _rwkv(p3, buf_rkv, buf_l, par, mul, w2p, a2p, g2a, g2b, s0bd, d, tile, chunk):
    bsz, tlen, _ = p3.shape
    pairs = d // LANES
    nl = 3 * LANES
    kern = functools.partial(_rwkv_kernel, chunk=chunk, n_chunks=tile // chunk, pairs=pairs)
    pcol = lambda cb: pl.BlockSpec((None, tile, d), lambda b, t: (b, t, cb))
    bcol = lambda cb: pl.BlockSpec((None, 1, d), lambda b, t: (b, 0, cb))
    full = lambda a: pl.BlockSpec(a.shape, lambda b, t: (0,) * a.ndim)
    state = pl.BlockSpec((None, pairs, LANES, LANES), lambda b, t: (b, 0, 0, 0))
    return pl.pallas_call(
        kern,
        grid=(bsz, tlen // tile),
        in_specs=[
            pcol(4), pcol(5), pcol(6),
            pl.BlockSpec((None, tile, nl), lambda b, t: (b, t, 9 * d // nl)),
            bcol(0), bcol(1), bcol(2),
            pl.BlockSpec((None, 1, nl), lambda b, t: (b, 0, 0)),
            full(par), full(mul), full(w2p), full(a2p), full(g2a), full(g2b), state,
        ],
        out_specs=[pl.BlockSpec((None, tile, d), lambda b, t: (b, t, 0)), state],
        out_shape=[
            jax.ShapeDtypeStruct((bsz, tlen, d), F32),
            jax.ShapeDtypeStruct((bsz, pairs, LANES, LANES), F32),
        ],
        scratch_shapes=[pltpu.VMEM((pairs, LANES, LANES), F32), pltpu.VMEM((8, d), F32),
                        pltpu.VMEM((8, nl), F32)],
        compiler_params=_cparams(("parallel", "arbitrary")),
        name="rwkv",
    )(p3, p3, p3, p3, buf_rkv, buf_rkv, buf_rkv, buf_l, par, mul, w2p, a2p, g2a, g2b, s0bd)


def _attn_kernel(x_ref, oh_ref, or_ref, ga_ref, gb_ref, mk_ref, mv_ref, wout_ref, nxg_ref, wq_ref, wo_ref,
                 nfg_ref, wrh_ref, wrl_ref, rb_ref, h_ref, u_ref, wf_ref, att_ref, *, bt, tt):
    n = bt * tt
    d = x_ref.shape[-1]
    flat = lambda ref: ref[...].reshape(n, d)
    merged = _sigmoid(flat(ga_ref)) * flat(oh_ref) + _sigmoid(flat(gb_ref)) * flat(or_ref)
    h1 = flat(x_ref) + _dot(merged.astype(BF16), wout_ref[...])

    q = _dot(_rmsnorm(h1, nxg_ref[...]).astype(BF16), wq_ref[...])
    dh = d // X_HEADS
    scale = dh ** -0.5
    for bi in range(bt):
        rows = slice(bi * tt, (bi + 1) * tt)
        for hh in range(X_HEADS):
            cols = slice(hh * dh, (hh + 1) * dh)
            mem_head = lambda ref: (ref[bi, :, cols] if len(ref.shape) == 3 else ref[bi, :, hh, :]).astype(BF16)
            s = _dot_nt(q[rows, cols].astype(BF16), mem_head(mk_ref)) * scale
            p = jnp.exp(s - jnp.max(s, axis=-1, keepdims=True))
            p = p / jnp.sum(p, axis=-1, keepdims=True)
            att_ref[rows, cols] = _dot(p.astype(BF16), mem_head(mv_ref)).astype(BF16)
    h2 = h1 + _dot(att_ref[...], wo_ref[...])
    h_ref[...] = h2.reshape(bt, tt, d)

    u = _rmsnorm(h2, nfg_ref[...])
    u_ref[...] = u.astype(BF16).reshape(bt, tt, d)
    u_hi, u_lo = _split(u)
    logits = (_dot(u_hi, wrh_ref[...]) + _dot(u_hi, wrl_ref[...]) + _dot(u_lo, wrh_ref[...])) + rb_ref[...]
    lane = lax.broadcasted_iota(jnp.int32, (n, LANES), 1)
    lane_f = lane.astype(F32)
    is_group = (lane >= N_EXPERTS) & (lane < N_EXPERTS + N_GROUPS)
    lg = jnp.where(is_group, logits, NEG_BIG)
    g_max = jnp.max(lg, axis=-1, keepdims=True)
    g_idx = jnp.min(jnp.where(lg == g_max, lane_f, 1e9), axis=-1, keepdims=True) - N_EXPERTS
    pg = 1.0 / jnp.sum(jnp.exp(lg - g_max), axis=-1, keepdims=True)
    in_group = (lane < N_EXPERTS) & ((lane >> (EXPERTS_PER_GROUP.bit_length() - 1)).astype(F32) == g_idx)
    le = jnp.where(in_group, logits, NEG_BIG)
    m1 = jnp.max(le, axis=-1, keepdims=True)
    i1 = jnp.min(jnp.where(le == m1, lane_f, 1e9), axis=-1, keepdims=True)
    le2 = jnp.where(lane_f == i1, NEG_BIG, le)
    m2 = jnp.max(le2, axis=-1, keepdims=True)
    i2 = jnp.min(jnp.where(le2 == m2, lane_f, 1e9), axis=-1, keepdims=True)
    e2 = jnp.exp(m2 - m1)
    w1 = pg / (1.0 + e2)
    wf = jnp.where(lane_f == i1, w1, 0.0) + jnp.where(lane_f == i2, w1 * e2, 0.0)
    wf = wf + jnp.where(lane == _E1_LANE, i1, 0.0) + jnp.where(lane == _E2_LANE, i2, 0.0)
    wf_ref[...] = wf.reshape(bt, tt, LANES)


def _attn(x3, oh3, or3, p3, mk, mv, wout, nxg, wq, wo, nfg, wrh, wrl, rb, bt, tt, gate_block0):
    bsz, tlen, d = x3.shape
    kern = functools.partial(_attn_kernel, bt=bt, tt=tt)
    tok = lambda cb: pl.BlockSpec((bt, tt, d), lambda b, t: (b, t, cb))
    mem = pl.BlockSpec((bt,) + mk.shape[1:], lambda b, t: (b,) + (0,) * (mk.ndim - 1))
    full = lambda a: pl.BlockSpec(a.shape, lambda b, t: (0,) * a.ndim)
    nxg, nfg = nxg.reshape(1, d), nfg.reshape(1, d)
    return pl.pallas_call(
        kern,
        grid=(bsz // bt, tlen // tt),
        in_specs=[tok(0), tok(0), tok(0), tok(gate_block0), tok(gate_block0 + 1), mem, mem,
                  full(wout), full(nxg), full(wq), full(wo), full(nfg), full(wrh), full(wrl), full(rb)],
        out_specs=[
            pl.BlockSpec((bt, tt, d), lambda b, t: (b, t, 0)),
            pl.BlockSpec((bt, tt, d), lambda b, t: (b, t, 0)),
            pl.BlockSpec((bt, tt, LANES), lambda b, t: (b, t, 0)),
        ],
        out_shape=[
            jax.ShapeDtypeStruct((bsz, tlen, d), F32),
            jax.ShapeDtypeStruct((bsz, tlen, d), BF16),
            jax.ShapeDtypeStruct((bsz, tlen, LANES), F32),
        ],
        scratch_shapes=[pltpu.VMEM((bt * tt, d), BF16)],
        compiler_params=_cparams(("parallel", "parallel")),
        name="attn",
    )(x3, oh3, or3, p3, p3, mk, mv, wout, nxg, wq, wo, nfg, wrh, wrl, rb)


def _segment_dma(src, dst, src_row, dst_row, length, max_len, sem, start):
    size = _SEG
    while size * 2 <= max_len:
        size *= 2
    while size >= _SEG:
        offset = length & (-2 * size)

        @pl.when((length & size) != 0)
        def _(offset=offset, size=size):
            copy = pltpu.make_async_copy(src.at[pl.ds(pl.multiple_of(src_row + offset, _SEG), size)],
                                         dst.at[pl.ds(pl.multiple_of(dst_row + offset, _SEG), size)], sem)
            if start:
                copy.start()
            else:
                copy.wait()

        size //= 2


def _tile_meta(meta_ref, nt, tile):
    field = lambda k: [meta_ref[(k * nt + tile) * N_GROUPS + g] for g in range(N_GROUPS)]
    return field(0), field(1), field(2)


def _group_of(expert_id):
    return jnp.floor(expert_id * (1.0 / EXPERTS_PER_GROUP))


def _moe_sort_kernel(meta_ref, u_ref, wf_ref, xs_ref, ws_ref, su_ref, sw_ref, zu_ref, zw_ref, sem, *, nt, tm, te):
    i = pl.program_id(0)
    buf = lax.rem(i, 2)
    slots = su_ref.shape[1]
    _, _, base = _tile_meta(meta_ref, nt, i)
    wf = wf_ref[...]
    gid = _group_of(wf.T[_E1_LANE:_E1_LANE + 1, :])
    onehot = lax.broadcasted_iota(jnp.int32, (8, tm), 0).astype(F32) == gid
    earlier = (lax.broadcasted_iota(jnp.int32, (tm, tm), 0)
               < lax.broadcasted_iota(jnp.int32, (tm, tm), 1)).astype(BF16)
    seen = _dot(onehot.astype(BF16), earlier)
    pos = jnp.sum(jnp.where(onehot, seen, 0.0), axis=0, keepdims=True)
    for g in range(N_GROUPS):
        pos = pos + jnp.where(gid == float(g), base[g].astype(F32), 0.0)
    perm = (lax.broadcasted_iota(jnp.int32, (slots, tm), 0).astype(F32) == pos).astype(BF16)
    su_ref[buf] = _dot(perm, u_ref[...]).astype(BF16)
    w_hi, w_lo = _split(wf)
    sw_ref[buf] = _dot(perm, w_hi) + _dot(perm, w_lo)

    def copies(tile, b, start):
        off, cnt, base = _tile_meta(meta_ref, nt, tile)
        for g in range(N_GROUPS):
            _segment_dma(su_ref.at[b], xs_ref, base[g], off[g], cnt[g], tm, sem.at[0, b], start)
            _segment_dma(sw_ref.at[b], ws_ref, base[g], off[g], cnt[g], tm, sem.at[1, b], start)

    copies(i, buf, True)

    @pl.when(i > 0)
    def _():
        copies(i - 1, 1 - buf, False)

    @pl.when(i == nt - 1)
    def _():
        copies(i, buf, False)
        zu_ref[...] = jnp.zeros_like(zu_ref)
        zw_ref[...] = jnp.zeros_like(zw_ref)
        tail = 3 * nt * N_GROUPS + nt
        for start in (True, False):
            for g in range(N_GROUPS):
                off, cnt = meta_ref[tail + g], meta_ref[tail + N_GROUPS + g]
                _segment_dma(zu_ref, xs_ref, 0, off, cnt, te, sem.at[0, 0], start)
                _segment_dma(zw_ref, ws_ref, 0, off, cnt, te, sem.at[1, 0], start)

            def unused(r, carry, start=start):
                row = pl.multiple_of(r * te, te)
                for src, dst, s in ((zu_ref, xs_ref, sem.at[0, 1]), (zw_ref, ws_ref, sem.at[1, 1])):
                    copy = pltpu.make_async_copy(src, dst.at[pl.ds(row, te)], s)
                    if start:
                        copy.start()
                    else:
                        copy.wait()
                return carry

            lax.fori_loop(meta_ref[tail + 2 * N_GROUPS], xs_ref.shape[0] // te, unused, 0)


def _moe_expert_kernel(meta_ref, xs_ref, ws_ref, wg_ref, wu_ref, wd_ref, ys_ref, *, n_tiles):
    r = pl.program_id(0)
    used = r < meta_ref[n_tiles]

    @pl.when(used)
    def _():
        first = meta_ref[r] * EXPERTS_PER_GROUP
        x = xs_ref[...]
        ws = ws_ref[...]
        lane = lax.broadcasted_iota(jnp.int32, ws.shape, 1)
        hid = []
        for e in range(EXPERTS_PER_GROUP):
            h = _silu(_dot(x, wg_ref[e])) * _dot(x, wu_ref[e])
            w_col = jnp.sum(jnp.where(lane == first + e, ws, 0.0), axis=-1, keepdims=True)
            hid.append((h * w_col).astype(BF16))
        ys_ref[...] = _dot(jnp.concatenate(hid, axis=1), wd_ref[...]).astype(ys_ref.dtype)

    @pl.when(jnp.logical_not(used))
    def _():
        ys_ref[...] = jnp.zeros_like(ys_ref)


def _moe_unsort_kernel(meta_ref, wf_ref, h_ref, gf_ref, ys_ref, y_ref, sy_ref, sem, *, nt, tm):
    i = pl.program_id(0)
    buf = lax.rem(i, 2)
    slots = sy_ref.shape[1]

    def copies(tile, b, start):
        off, cnt, base = _tile_meta(meta_ref, nt, tile)
        for g in range(N_GROUPS):
            _segment_dma(ys_ref, sy_ref.at[b], off[g], base[g], cnt[g], tm, sem.at[b], start)

    @pl.when(i == 0)
    def _():
        sy_ref[...] = jnp.zeros_like(sy_ref)
        copies(0, 0, True)

    @pl.when(i + 1 < nt)
    def _():
        copies(i + 1, 1 - buf, True)

    _, _, base = _tile_meta(meta_ref, nt, i)
    gid = _group_of(wf_ref[...][:, _E1_LANE:_E1_LANE + 1])
    onehot = lax.broadcasted_iota(jnp.int32, (tm, LANES), 1).astype(F32) == gid
    earlier = (lax.broadcasted_iota(jnp.int32, (tm, tm), 1)
               < lax.broadcasted_iota(jnp.int32, (tm, tm), 0)).astype(BF16)
    seen = _dot(earlier, onehot.astype(BF16))
    pos = jnp.sum(jnp.where(onehot, seen, 0.0), axis=1, keepdims=True)
    for g in range(N_GROUPS):
        pos = pos + jnp.where(gid == float(g), base[g].astype(F32), 0.0)
    perm_t = (lax.broadcasted_iota(jnp.int32, (tm, slots), 1).astype(F32) == pos).astype(BF16)
    copies(i, buf, False)
    slot_row = lax.broadcasted_iota(jnp.int32, (slots, 1), 0)
    ys = sy_ref[buf]
    moe = _dot(perm_t, jnp.where(slot_row < meta_ref[3 * nt * N_GROUPS + i], ys, jnp.zeros_like(ys)))
    y_ref[...] = _rmsnorm(h_ref[...] + moe, gf_ref[...])


def _moe(u, wf, h, wg, wu, wd, gf, tm, te):
    n, d = u.shape
    ne, _, f = wg.shape
    nt = n // tm
    slots = tm + LANES
    n_tiles = -(-(n + nt * N_GROUPS * (_SEG - 1) + N_GROUPS * (te - 1)) // te)
    rows = n_tiles * te
    i32 = jnp.int32

    gid = (wf[:, _E1_LANE].astype(i32) // EXPERTS_PER_GROUP).reshape(nt, tm)
    cnt = jnp.sum((gid[:, :, None] == jnp.arange(N_GROUPS, dtype=i32)).astype(i32), axis=1)
    cnt = (cnt + _SEG - 1) // _SEG * _SEG
    base = jnp.cumsum(cnt, axis=1) - cnt
    total = jnp.sum(cnt, axis=0)
    region = (total + te - 1) // te * te
    region_end = jnp.cumsum(region)
    region_start = region_end - region
    off = region_start[None, :] + jnp.cumsum(cnt, axis=0) - cnt
    n_used = region_end[-1:] // te
    meta = jnp.concatenate([off.reshape(-1), cnt.reshape(-1), base.reshape(-1), jnp.sum(cnt, axis=1),
                            region_start + total, region - total, n_used]).astype(i32)
    tile_row = jnp.arange(n_tiles, dtype=i32) * te
    tile_group = jnp.minimum(jnp.sum((tile_row[:, None] >= region_end[None, :]).astype(i32), axis=1), N_GROUPS - 1)
    emeta = jnp.concatenate([tile_group, n_used]).astype(i32)

    anyspec = pl.BlockSpec(memory_space=pl.ANY)
    tok = lambda w: pl.BlockSpec((tm, w), lambda i, m: (i, 0))
    xs, ws = pl.pallas_call(
        functools.partial(_moe_sort_kernel, nt=nt, tm=tm, te=te),
        grid_spec=pltpu.PrefetchScalarGridSpec(
            num_scalar_prefetch=1, grid=(nt,),
            in_specs=[tok(d), tok(LANES)],
            out_specs=[anyspec, anyspec],
            scratch_shapes=[pltpu.VMEM((2, slots, d), BF16), pltpu.VMEM((2, slots, LANES), F32),
                            pltpu.VMEM((te, d), BF16), pltpu.VMEM((te, LANES), F32),
                            pltpu.SemaphoreType.DMA((2, 2))]),
        out_shape=[jax.ShapeDtypeStruct((rows, d), BF16), jax.ShapeDtypeStruct((rows, LANES), F32)],
        compiler_params=_cparams(("arbitrary",)),
        name="moe_sort",
    )(meta, u, wf)

    srt = lambda w: pl.BlockSpec((te, w), lambda r, m: (r, 0))
    grp_w = pl.BlockSpec((EXPERTS_PER_GROUP, d, f), lambda r, m: (m[r], 0, 0))
    ys = pl.pallas_call(
        functools.partial(_moe_expert_kernel, n_tiles=n_tiles),
        grid_spec=pltpu.PrefetchScalarGridSpec(
            num_scalar_prefetch=1, grid=(n_tiles,),
            in_specs=[srt(d), srt(LANES), grp_w, grp_w,
                      pl.BlockSpec((EXPERTS_PER_GROUP * f, d), lambda r, m: (m[r], 0))],
            out_specs=srt(d)),
        out_shape=jax.ShapeDtypeStruct((rows, d), BF16),
        compiler_params=_cparams(("parallel",)),
        name="moe_expert",
    )(emeta, xs, ws, wg, wu, wd.reshape(ne * f, d))

    return pl.pallas_call(
        functools.partial(_moe_unsort_kernel, nt=nt, tm=tm),
        grid_spec=pltpu.PrefetchScalarGridSpec(
            num_scalar_prefetch=1, grid=(nt,),
            in_specs=[tok(LANES), tok(d), pl.BlockSpec((1, d), lambda i, m: (0, 0)), anyspec],
            out_specs=tok(d),
            scratch_shapes=[pltpu.VMEM((2, slots, d), BF16), pltpu.SemaphoreType.DMA((2,))]),
        out_shape=jax.ShapeDtypeStruct((n, d), F32),
        compiler_params=_cparams(("arbitrary",)),
        name="moe_unsort",
    )(meta, wf, h, gf.reshape(1, d), ys)


def _pick(n, pref):
    t = min(n, pref)
    while n % t:
        t -= 8
    return t


def _block_diag_pairs(s):
    b, h, n, _ = s.shape
    s = s.reshape(b, h // 2, 2, n, n)
    z = jnp.zeros_like(s[:, :, 0])
    top = jnp.concatenate([s[:, :, 0], z], axis=-1)
    bot = jnp.concatenate([z, s[:, :, 1]], axis=-1)
    return jnp.concatenate([top, bot], axis=-2)


def _pairs_to_heads(sbd):
    b, p, n2, _ = sbd.shape
    n = n2 // 2
    s = jnp.stack([sbd[:, :, :n, :n], sbd[:, :, n:, n:]], axis=2)
    return s.reshape(b, 2 * p, n, n)


def kernel(x_prompt, x_sample, mem_prompt, cache_mem_k, cache_mem_v, state_hgrn, state_rwkv, state_rwkv_shift, norm_mix_g, w_in, hgrn_lower_bounds, hgrn_norm_g, rwkv_mu, rwkv_w0, rwkv_w2, rwkv_a0, rwkv_a2, rwkv_g2, rwkv_k_k, rwkv_k_a, rwkv_r_k, rwkv_ln_g, rwkv_ln_b, w_out, norm_x_g, norm_mem_g, wq_x, wk_x, wv_x, wo_x, norm_ffn_g, router_group_w, router_group_b, router_expert_w, router_expert_b, expert_w_gate, expert_w_up, expert_w_down, norm_final_g):
    assert w_in.shape[0] == 1, "single-layer configuration"
    d = x_prompt.shape[-1]
    n_lora_w, n_lora_a, n_lora_g = rwkv_w2.shape[1], rwkv_a2.shape[1], rwkv_g2.shape[1]
    assert n_lora_w + n_lora_a == LANES and LANES < n_lora_g <= 2 * LANES
    n_lora = n_lora_w + n_lora_a + n_lora_g
    n_rwkv_in = 3 * d + n_lora
    lora_pad = 3 * LANES - n_lora

    w = w_in[0]
    pr0 = 4 * d
    gate0 = pr0 + n_rwkv_in
    n_cols = w.shape[1] + lora_pad
    w_perm = _permute_columns(w, [(0, pr0 + 3 * d), (gate0, 2 * d), (pr0 + 3 * d, n_lora)], n_cols, _pick(d, 128))
    lora_col0 = 9 * d
    gate_block0 = 7
    mu = rwkv_mu[0]
    mul = jnp.pad(mu[3 * d:], (0, lora_pad)).reshape(1, 3 * LANES)
    par = jnp.stack([mu[:d], mu[d:2 * d], mu[2 * d:3 * d], rwkv_w0[0], rwkv_a0[0], rwkv_k_k[0], rwkv_k_a[0],
                     rwkv_r_k[0].reshape(d), rwkv_ln_g[0], rwkv_ln_b[0]])
    par = jnp.pad(par, ((0, _P_ROWS - par.shape[0]), (0, 0)))
    w2p = jnp.pad(rwkv_w2[0], ((0, n_lora_a), (0, 0))).astype(BF16)
    a2p = jnp.pad(rwkv_a2[0], ((n_lora_w, 0), (0, 0))).astype(BF16)
    g2a = rwkv_g2[0][:LANES].astype(BF16)
    g2b = jnp.pad(rwkv_g2[0][LANES:], ((0, 2 * LANES - n_lora_g), (0, 0))).astype(BF16)
    wout_b, wq_b, wo_b = w_out[0].astype(BF16), wq_x[0].astype(BF16), wo_x[0].astype(BF16)
    wr = jnp.concatenate([router_expert_w[0], router_group_w[0],
                          jnp.zeros((d, LANES - N_EXPERTS - N_GROUPS), F32)], axis=1)
    wr_hi, wr_lo = _split(wr)
    rb = jnp.pad(jnp.concatenate([router_expert_b[0].reshape(-1), router_group_b[0]]),
                 (0, LANES - N_EXPERTS - N_GROUPS)).reshape(1, LANES)
    wg_b, wu_b, wd_b = (expert_w_gate[0].astype(BF16), expert_w_up[0].astype(BF16),
                        expert_w_down[0].astype(BF16))

    nb_p = mem_prompt.shape[0]
    mem_k_p, mem_v_p = _mem_kv(mem_prompt, norm_mem_g[0], wk_x[0].astype(BF16), wv_x[0].astype(BF16))

    def group(x3, mem_k, mem_v, s_h, s_r, buf, tile_h, tile_r, chunk_r, bt, tt, te):
        bsz, tlen, _ = x3.shape
        n = bsz * tlen
        p = _norm_matmul(x3.reshape(n, d), norm_mix_g[0], w_perm, _pick(n, 1024), n_cols // 5)
        p3 = p.reshape(bsz, tlen, n_cols)
        o_h, s_h_new = _hgrn(p3, hgrn_lower_bounds, hgrn_norm_g[0], s_h, tile_h, 32)
        buf_rkv = buf[:, :, :3 * d]
        buf_l = jnp.pad(buf[:, :, 3 * d:], ((0, 0), (0, 0), (0, lora_pad)))
        o_r, s_r_bd = _rwkv(p3, buf_rkv, buf_l, par, mul, w2p, a2p, g2a, g2b, _block_diag_pairs(s_r), d, tile_r, chunk_r)
        h2, u, wf = _attn(x3, o_h, o_r, p3, mem_k, mem_v, wout_b, norm_x_g[0], wq_b, wo_b, norm_ffn_g[0],
                          wr_hi, wr_lo, rb, bt, tt, gate_block0)
        y = _moe(u.reshape(n, d), wf.reshape(n, LANES), h2.reshape(n, d), wg_b, wu_b, wd_b, norm_final_g,
                 _pick(n, 512), te)
        shift = jnp.concatenate([p3[:, -1:, pr0:pr0 + 3 * d], p3[:, -1:, lora_col0:lora_col0 + n_lora]], axis=-1)
        return y.reshape(bsz, tlen, d), s_h_new, _pairs_to_heads(s_r_bd), shift

    nb_s, t_s, _ = x_sample.shape
    t_p = x_prompt.shape[1]
    zeros_like_state = lambda s: jnp.zeros((nb_p,) + s.shape[2:], s.dtype)
    y_p, sh_p, sr_p, bf_p = group(
        x_prompt, mem_k_p, mem_v_p, zeros_like_state(state_hgrn), zeros_like_state(state_rwkv),
        zeros_like_state(state_rwkv_shift), _pick(t_p, 256), _pick(t_p, 256), 64, 1, _pick(t_p, 256), 512)
    y_s, sh_s, sr_s, bf_s = group(
        x_sample, cache_mem_k[0], cache_mem_v[0],
        state_hgrn[0], state_rwkv[0], state_rwkv_shift[0], t_s, t_s, t_s, _pick(nb_s, 4), t_s, 256)

    kv_shape = (1,) + mem_k_p.shape[:2] + cache_mem_k.shape[3:]
    return (y_p, y_s, mem_k_p.reshape(kv_shape), mem_v_p.reshape(kv_shape), sh_p[None], sh_s[None],
            sr_p[None], sr_s[None], bf_p[None], bf_s[None])
```

```python
import functools
import math

import jax
import jax.numpy as jnp
from jax import lax
from jax.experimental import pallas as pl
from jax.experimental.pallas import tpu as pltpu

F32 = jnp.float32
BF16 = jnp.bfloat16

NORM_EPS = 1e-6
RWKV_GN_EPS = 64e-5
LANES = 128
HGRN_HEAD = 128
RWKV_HEAD = 64
X_HEADS = 4
N_GROUPS = 4
EXPERTS_PER_GROUP = 8
N_EXPERTS = N_GROUPS * EXPERTS_PER_GROUP
NEG_BIG = -1e30
VMEM_LIMIT = 52 * 1024 * 1024
_E1_LANE, _E2_LANE = N_EXPERTS, N_EXPERTS + 1
_SEG = 16


def _dot(a, b):
    return jnp.dot(a, b, preferred_element_type=F32)


def _dot_nt(a, b):
    return lax.dot_general(a, b, (((1,), (1,)), ((), ())), preferred_element_type=F32)


def _dot_tn(a, b):
    return lax.dot_general(a, b, (((0,), (0,)), ((), ())), preferred_element_type=F32)


def _split(x):
    hi = x.astype(BF16)
    lo = (x - hi.astype(F32)).astype(BF16)
    return hi, lo


def _sigmoid(x):
    return 0.5 * jnp.tanh(0.5 * x) + 0.5


def _silu(x):
    return x * _sigmoid(x)


def _rmsnorm(x, g):
    return x * lax.rsqrt(jnp.mean(x * x, axis=-1, keepdims=True) + NORM_EPS) * g


def _cparams(sem):
    return pltpu.CompilerParams(dimension_semantics=sem, vmem_limit_bytes=VMEM_LIMIT)


def _norm_matmul_kernel(x_ref, g_ref, w_ref, o_ref, u_ref):
    @pl.when(pl.program_id(1) == 0)
    def _():
        u_ref[...] = _rmsnorm(x_ref[...], g_ref[...]).astype(BF16)

    o_ref[...] = _dot(u_ref[...], w_ref[...])


def _norm_matmul(x, g, w, tm, tn):
    n, d = x.shape
    m = w.shape[1]
    return pl.pallas_call(
        _norm_matmul_kernel,
        grid=(n // tm, m // tn),
        in_specs=[
            pl.BlockSpec((tm, d), lambda i, j: (i, 0)),
            pl.BlockSpec((1, d), lambda i, j: (0, 0)),
            pl.BlockSpec((d, tn), lambda i, j: (0, j)),
        ],
        out_specs=pl.BlockSpec((tm, tn), lambda i, j: (i, j)),
        out_shape=jax.ShapeDtypeStruct((n, m), F32),
        scratch_shapes=[pltpu.VMEM((tm, d), BF16)],
        compiler_params=_cparams(("parallel", "arbitrary")),
        name="norm_matmul",
    )(x, g.reshape(1, d), w)


def _permute_columns_kernel(w_ref, o_ref, *, pieces):
    col = 0
    for src, width in pieces:
        o_ref[:, col:col + width] = w_ref[:, src:src + width].astype(o_ref.dtype)
        col += width
    if col < o_ref.shape[1]:
        o_ref[:, col:] = jnp.zeros((o_ref.shape[0], o_ref.shape[1] - col), o_ref.dtype)


def _permute_columns(w, pieces, n_cols, tr):
    _, rows, cols = w.shape
    return pl.pallas_call(
        functools.partial(_permute_columns_kernel, pieces=pieces),
        grid=(rows // tr,),
        in_specs=[pl.BlockSpec((None, tr, cols), lambda i: (0, i, 0))],
        out_specs=pl.BlockSpec((tr, n_cols), lambda i: (i, 0)),
        out_shape=jax.ShapeDtypeStruct((rows, n_cols), BF16),
        compiler_params=_cparams(("parallel",)),
        name="permute_columns",
    )(w)


def _mem_kv_kernel(m_ref, g_ref, wk_ref, wv_ref, k_ref, v_ref):
    u = _rmsnorm(m_ref[...], g_ref[...]).astype(BF16)
    k_ref[...] = _dot(u, wk_ref[...])
    v_ref[...] = _dot(u, wv_ref[...])


def _mem_kv(mem, g, wk, wv):
    bsz, nmem, d = mem.shape
    full = lambda a: pl.BlockSpec(a.shape, lambda b: (0,) * a.ndim)
    out = pl.BlockSpec((None, nmem, d), lambda b: (b, 0, 0))
    g = g.reshape(1, d)
    return pl.pallas_call(
        _mem_kv_kernel,
        grid=(bsz,),
        in_specs=[pl.BlockSpec((None, nmem, d), lambda b: (b, 0, 0)), full(g), full(wk), full(wv)],
        out_specs=[out, out],
        out_shape=[jax.ShapeDtypeStruct((bsz, nmem, d), F32)] * 2,
        compiler_params=_cparams(("parallel",)),
        name="mem_kv",
    )(mem, g, wk, wv)


def _hgrn_kernel(q_ref, f_ref, i_ref, g_ref, lbp_ref, ng_ref, s0_ref, o_ref, sout_ref, st_ref,
                 *, chunk, n_chunks, heads):
    t = pl.program_id(1)

    @pl.when(t == 0)
    def _():
        for h in range(heads):
            st_ref[h] = s0_ref[h].T

    lbp = lbp_ref[...]
    e = jnp.exp(lbp - jnp.max(lbp, axis=0, keepdims=True))
    lb = e[0:1] / jnp.sum(e, axis=0, keepdims=True)
    ng = ng_ref[...]

    row = lax.broadcasted_iota(jnp.int32, (chunk, chunk), 0)
    col = lax.broadcasted_iota(jnp.int32, (chunk, chunk), 1)
    causal = row >= col
    mid = chunk // 2
    to_mid = jnp.where((col >= mid) & (col <= row), 1.0, 0.0) - jnp.where((col > row) & (col < mid), 1.0, 0.0)
    coef = jnp.concatenate([causal.astype(F32), to_mid, (col > row).astype(F32)], axis=0).astype(BF16)

    chunks = range(n_chunks)
    qd, kd, qi, kl, el, vv, gate = [], [], [], [], [], [], []
    for c in chunks:
        rows = slice(c * chunk, (c + 1) * chunk)
        f = lb + (1.0 - lb) * _sigmoid(f_ref[rows, :])
        g_hi, g_lo = _split(jnp.log(f))
        bb = _dot(coef, g_hi) + _dot(coef, g_lo)
        b, b_to_mid, b_to_last = bb[:chunk], bb[chunk:2 * chunk], bb[2 * chunk:]
        q = _silu(q_ref[rows, :])
        k = 1.0 - f
        qd.append((q * jnp.exp(b_to_mid)).astype(BF16))
        kd.append((k * jnp.exp(-b_to_mid)).astype(BF16))
        qi.append((q * jnp.exp(b)).astype(BF16))
        kl.append((k * jnp.exp(b_to_last)).astype(BF16))
        el.append(jnp.exp(b[chunk - 1:chunk, :]))
        vv.append(i_ref[rows, :].astype(BF16))
        gate.append(_silu(g_ref[rows, :]))

    head = lambda x, h: x[:, h * HGRN_HEAD:(h + 1) * HGRN_HEAD]
    items = [(c, h) for c in chunks for h in range(heads)]
    sc = {ch: jnp.where(causal, _dot_nt(head(qd[ch[0]], ch[1]), head(kd[ch[0]], ch[1])), 0.0).astype(BF16)
          for ch in items}
    intra = {ch: _dot(sc[ch], head(vv[ch[0]], ch[1])) for ch in items}
    upd = {ch: _dot_tn(head(vv[ch[0]], ch[1]), head(kl[ch[0]], ch[1])) for ch in items}
    for h in range(heads):
        st = st_ref[h]
        for c in chunks:
            o = intra[c, h] + _dot_nt(head(qi[c], h), st.astype(BF16))
            st = st * head(el[c], h) + upd[c, h]
            o = o * lax.rsqrt(jnp.mean(o * o, axis=-1, keepdims=True) + NORM_EPS)
            o_ref[c * chunk:(c + 1) * chunk, h * HGRN_HEAD:(h + 1) * HGRN_HEAD] = o * ng * head(gate[c], h)
        st_ref[h] = st

    @pl.when(t == pl.num_programs(1) - 1)
    def _():
        for h in range(heads):
            sout_ref[h] = st_ref[h].T


def _hgrn(p3, lbp, ng, s0, tile, chunk):
    bsz, tlen, _ = p3.shape
    heads, dk, dv = s0.shape[1:]
    d = heads * HGRN_HEAD
    kern = functools.partial(_hgrn_kernel, chunk=chunk, n_chunks=tile // chunk, heads=heads)
    pspec = lambda cb: pl.BlockSpec((None, tile, d), lambda b, t: (b, t, cb))
    return pl.pallas_call(
        kern,
        grid=(bsz, tlen // tile),
        in_specs=[
            pspec(0), pspec(1), pspec(2), pspec(3),
            pl.BlockSpec(lbp.shape, lambda b, t: (0, 0)),
            pl.BlockSpec((1, dv), lambda b, t: (0, 0)),
            pl.BlockSpec((None, heads, dk, dv), lambda b, t: (b, 0, 0, 0)),
        ],
        out_specs=[
            pl.BlockSpec((None, tile, d), lambda b, t: (b, t, 0)),
            pl.BlockSpec((None, heads, dk, dv), lambda b, t: (b, 0, 0, 0)),
        ],
        out_shape=[
            jax.ShapeDtypeStruct((bsz, tlen, d), F32),
            jax.ShapeDtypeStruct((bsz, heads, dk, dv), F32),
        ],
        scratch_shapes=[pltpu.VMEM((heads, dv, dk), F32)],
        compiler_params=_cparams(("parallel", "arbitrary")),
        name="hgrn",
    )(p3, p3, p3, p3, lbp, ng.reshape(1, dv), s0)


_P_MU_R, _P_MU_K, _P_MU_V, _P_W0, _P_A0, _P_KK, _P_KA, _P_RK, _P_LNG, _P_LNB = range(10)
_P_ROWS = 16


def _rwkv_kernel(r_ref, k_ref, v_ref, l_ref, br_ref, bk_ref, bv_ref, bl_ref, par_ref, mul_ref,
                 w2_ref, a2_ref, g2a_ref, g2b_ref, s0_ref,
                 o_ref, sout_ref, st_ref, prev_ref, prevl_ref, *, chunk, n_chunks, pairs):
    t = pl.program_id(1)
    tile = chunk * n_chunks

    @pl.when(t == 0)
    def _():
        st_ref[...] = s0_ref[...]
        prev_ref[0:1, :] = br_ref[...]
        prev_ref[1:2, :] = bk_ref[...]
        prev_ref[2:3, :] = bv_ref[...]
        prevl_ref[0:1, :] = bl_ref[...]

    par = par_ref[...]
    prow = lambda i: par[i:i + 1, :]

    def shifted(x, prev_row, mu):
        row_id = lax.broadcasted_iota(jnp.int32, x.shape, 0)
        prev = jnp.where(row_id == 0, prev_row, pltpu.roll(x, 1, 0))
        return x + (prev - x) * mu

    pr, pk, pv, plr = r_ref[...], k_ref[...], v_ref[...], l_ref[...]
    r = shifted(pr, prev_ref[0:1, :], prow(_P_MU_R))
    k = shifted(pk, prev_ref[1:2, :], prow(_P_MU_K))
    v = shifted(pv, prev_ref[2:3, :], prow(_P_MU_V))
    xl = shifted(plr, prevl_ref[0:1, :], mul_ref[...])
    for slot, x in enumerate((pr, pk, pv)):
        prev_ref[slot:slot + 1, :] = x[tile - 1:tile, :]
    prevl_ref[0:1, :] = plr[tile - 1:tile, :]
    x0 = xl[:, 0:LANES]
    x1 = xl[:, LANES:2 * LANES]
    x2 = xl[:, 2 * LANES:3 * LANES]

    w_lin = prow(_P_W0) + _dot(jnp.tanh(x0).astype(BF16), w2_ref[...])
    z = -w_lin
    softplus = jnp.maximum(z, 0.0) + jnp.log(1.0 + jnp.exp(-jnp.abs(z)))
    lw = -jnp.exp(-softplus - 0.5)
    a_icl = _sigmoid(prow(_P_A0) + _dot(x0.astype(BF16), a2_ref[...]))
    g_out = _dot(_sigmoid(x1).astype(BF16), g2a_ref[...]) + _dot(_sigmoid(x2).astype(BF16), g2b_ref[...])

    lane = lax.broadcasted_iota(jnp.int32, (1, LANES), 1)
    m0 = (lane < RWKV_HEAD).astype(F32)
    m1 = 1.0 - m0
    head_shift = RWKV_HEAD.bit_length() - 1
    hr = lax.broadcasted_iota(jnp.int32, (LANES, LANES), 0) >> head_shift
    hc = lax.broadcasted_iota(jnp.int32, (LANES, LANES), 1) >> head_shift
    head_ones = (hr == hc).astype(BF16)
    tile_of = lambda x, j: x[:, j * LANES:(j + 1) * LANES]

    def headsum(x):
        xs = jnp.concatenate([tile_of(x, j) for j in range(pairs)], axis=0)
        ys = _dot(xs.astype(BF16), head_ones)
        return jnp.concatenate([ys[j * tile:(j + 1) * tile] for j in range(pairs)], axis=1)

    kk = k * prow(_P_KK)
    kk = kk * lax.rsqrt(jnp.maximum(headsum(kk * kk), 1e-24))
    k_mod = k * (1.0 + (a_icl - 1.0) * prow(_P_KA))
    a = -kk
    b = kk * a_icl

    ti = lax.broadcasted_iota(jnp.int32, (chunk, chunk), 0)
    tj = lax.broadcasted_iota(jnp.int32, (chunk, chunk), 1)
    tri = (ti >= tj).astype(BF16)
    n2 = 2 * chunk
    si = lax.broadcasted_iota(jnp.int32, (n2, n2), 0)
    sj = lax.broadcasted_iota(jnp.int32, (n2, n2), 1)
    chunk_shift = chunk.bit_length() - 1
    same_head = (si >> chunk_shift) == (sj >> chunk_shift)
    strict = same_head & ((sj & (chunk - 1)) < (si & (chunk - 1)))
    incl = same_head & ((sj & (chunk - 1)) <= (si & (chunk - 1)))
    eye = (si == sj).astype(F32)
    m0_b, m1_b = m0.astype(BF16), m1.astype(BF16)
    tile_b = lambda x, j: tile_of(x, j).astype(BF16)
    twice = lambda x: jnp.concatenate([x, x], axis=0)
    every = range(pairs)

    def stack_b(x, j):
        xb = tile_b(x, j)
        return jnp.concatenate([xb * m0_b, xb * m1_b], axis=0)

    def state_free(rows):
        lw_c, a_c, b_c, k_c, r_c, v_c = lw[rows], a[rows], b[rows], k_mod[rows], r[rows], v[rows]
        lw_hi, lw_lo = _split(lw_c)
        cum = _dot(tri, lw_hi) + _dot(tri, lw_lo)
        cum_last = cum[chunk - 1:chunk, :]
        e_neg = jnp.exp(-cum)
        e_end = jnp.exp(cum_last - cum)
        a_t = a_c * jnp.exp(cum - lw_c)
        b_t = b_c * e_neg
        k_t = k_c * e_neg
        r_t = r_c * jnp.exp(cum)
        b_end = b_c * e_end
        k_end = k_c * e_end

        a_st = [stack_b(a_t, j) for j in every]
        r_st = [stack_b(r_t, j) for j in every]
        v_st = [stack_b(v_c, j) for j in every]
        bend_st = [stack_b(b_end, j) for j in every]
        kend_st = [stack_b(k_end, j) for j in every]
        sc = [_dot_nt(jnp.concatenate([tile_b(a_t, j), tile_b(r_t, j)], axis=0),
                      jnp.concatenate([stack_b(b_t, j), stack_b(k_t, j)], axis=0)) for j in every]
        a_ab = [jnp.where(strict, twice(s[:chunk, :n2]), 0.0) for s in sc]
        a_xk = [jnp.concatenate([jnp.where(strict, twice(s[:chunk, n2:]), 0.0),
                                 jnp.where(incl, twice(s[chunk:, n2:]), 0.0)], axis=0).astype(BF16) for s in sc]
        a_rb = [jnp.where(incl, twice(s[chunk:, :n2]), 0.0).astype(BF16) for s in sc]

        inv = [eye + x for x in a_ab]
        pw = [_dot(x.astype(BF16), x.astype(BF16)) for x in a_ab]
        for _ in range(int(math.log2(chunk)) - 2):
            if n2 % LANES == 0:
                both = [_dot(p.astype(BF16), jnp.concatenate([p, i], axis=1).astype(BF16)) for p, i in zip(pw, inv)]
                pw_next = [x[:, :n2] for x in both]
                inv = [i + x[:, n2:] for i, x in zip(inv, both)]
            else:
                pw_next = [_dot(p.astype(BF16), p.astype(BF16)) for p in pw]
                inv = [i + _dot(p.astype(BF16), i.astype(BF16)) for i, p in zip(inv, pw)]
            pw = pw_next
        inv = [i + _dot(p.astype(BF16), i.astype(BF16)) for i, p in zip(inv, pw)]

        av = [_dot(a_xk[j], v_st[j]) for j in every]
        wu = [_dot(inv[j].astype(BF16), jnp.concatenate([a_st[j], av[j][:n2].astype(BF16)], axis=1)).astype(BF16)
              for j in every]
        qo = [_dot(a_rb[j], wu[j]) + jnp.concatenate([r_st[j].astype(F32), av[j][n2:]], axis=1) for j in every]
        mn = [_dot_tn(wu[j], bend_st[j]) for j in every]
        nn = [mn[j][LANES:] + _dot_tn(v_st[j], kend_st[j]) for j in every]
        return qo, mn, nn, jnp.exp(cum_last)

    parts = [state_free(slice(c * chunk, (c + 1) * chunk)) for c in range(n_chunks)]
    outs = []
    for qo, mn, nn, g_last in parts:
        row = []
        for j in every:
            s0 = st_ref[j]
            s0_b = s0.astype(BF16)
            q = (qo[j][:chunk, :LANES] + qo[j][chunk:, :LANES]).astype(BF16)
            row.append(_dot_nt(q, s0_b) + qo[j][:chunk, LANES:] + qo[j][chunk:, LANES:])
            st_ref[j] = s0 * tile_of(g_last, j) + _dot(s0_b, mn[j][:LANES].astype(BF16)) + nn[j]
        outs.append(jnp.concatenate(row, axis=1))
    o = outs[0] if n_chunks == 1 else jnp.concatenate(outs, axis=0)

    inv_n = 1.0 / RWKV_HEAD
    mean = headsum(o) * inv_n
    dlt = o - mean
    var = headsum(dlt * dlt) * inv_n
    o = dlt * lax.rsqrt(var + RWKV_GN_EPS) * prow(_P_LNG) + prow(_P_LNB)
    bonus = headsum(r * k_mod * prow(_P_RK))
    o_ref[...] = (o + bonus * v) * g_out

    @pl.when(t == pl.num_programs(1) - 1)
    def _():
        sout_ref[...] = st_ref[...]


def _rwkv(p3, buf_rkv, buf_l, par, mul, w2p, a2p, g2a, g2b, s0bd, d, tile, chunk):
    bsz, tlen, _ = p3.shape
    pairs = d // LANES
    nl = 3 * LANES
    kern = functools.partial(_rwkv_kernel, chunk=chunk, n_chunks=tile // chunk, pairs=pairs)
    pcol = lambda cb: pl.BlockSpec((None, tile, d), lambda b, t: (b, t, cb))
    bcol = lambda cb: pl.BlockSpec((None, 1, d), lambda b, t: (b, 0, cb))
    full = lambda a: pl.BlockSpec(a.shape, lambda b, t: (0,) * a.ndim)
    state = pl.BlockSpec((None, pairs, LANES, LANES), lambda b, t: (b, 0, 0, 0))
    return pl.pallas_call(
        kern,
        grid=(bsz, tlen // tile),
        in_specs=[
            pcol(4), pcol(5), pcol(6),
            pl.BlockSpec((None, tile, nl), lambda b, t: (b, t, 9 * d // nl)),
            bcol(0), bcol(1), bcol(2),
            pl.BlockSpec((None, 1, nl), lambda b, t: (b, 0, 0)),
            full(par), full(mul), full(w2p), full(a2p), full(g2a), full(g2b), state,
        ],
        out_specs=[pl.BlockSpec((None, tile, d), lambda b, t: (b, t, 0)), state],
        out_shape=[
            jax.ShapeDtypeStruct((bsz, tlen, d), F32),
            jax.ShapeDtypeStruct((bsz, pairs, LANES, LANES), F32),
        ],
        scratch_shapes=[pltpu.VMEM((pairs, LANES, LANES), F32), pltpu.VMEM((8, d), F32),
                        pltpu.VMEM((8, nl), F32)],
        compiler_params=_cparams(("parallel", "arbitrary")),
        name="rwkv",
    )(p3, p3, p3, p3, buf_rkv, buf_rkv, buf_rkv, buf_l, par, mul, w2p, a2p, g2a, g2b, s0bd)


def _attn_kernel(x_ref, oh_ref, or_ref, ga_ref, gb_ref, mk_ref, mv_ref, wout_ref, nxg_ref, wq_ref, wo_ref,
                 nfg_ref, wrh_ref, wrl_ref, rb_ref, h_ref, u_ref, wf_ref, att_ref, *, bt, tt):
    n = bt * tt
    d = x_ref.shape[-1]
    flat = lambda ref: ref[...].reshape(n, d)
    merged = _sigmoid(flat(ga_ref)) * flat(oh_ref) + _sigmoid(flat(gb_ref)) * flat(or_ref)
    h1 = flat(x_ref) + _dot(merged.astype(BF16), wout_ref[...])

    q = _dot(_rmsnorm(h1, nxg_ref[...]).astype(BF16), wq_ref[...])
    dh = d // X_HEADS
    scale = dh ** -0.5
    for bi in range(bt):
        rows = slice(bi * tt, (bi + 1) * tt)
        if len(mk_ref.shape) == 3:
            mem_heads = lambda ref: [ref[bi, :, hh * dh:(hh + 1) * dh].astype(BF16) for hh in range(X_HEADS)]
        else:
            def mem_heads(ref):
                by_head = pltpu.einshape("mhd->hmd", ref[bi]).astype(BF16)
                return [by_head[hh] for hh in range(X_HEADS)]
        k_heads, v_heads = mem_heads(mk_ref), mem_heads(mv_ref)
        for hh in range(X_HEADS):
            cols = slice(hh * dh, (hh + 1) * dh)
            s = _dot_nt(q[rows, cols].astype(BF16), k_heads[hh]) * scale
            p = jnp.exp(s - jnp.max(s, axis=-1, keepdims=True))
            p = p / jnp.sum(p, axis=-1, keepdims=True)
            att_ref[rows, cols] = _dot(p.astype(BF16), v_heads[hh]).astype(BF16)
    h2 = h1 + _dot(att_ref[...], wo_ref[...])
    h_ref[...] = h2.reshape(bt, tt, d)

    u = _rmsnorm(h2, nfg_ref[...])
    u_ref[...] = u.astype(BF16).reshape(bt, tt, d)
    u_hi, u_lo = _split(u)
    logits = (_dot(u_hi, wrh_ref[...]) + _dot(u_hi, wrl_ref[...]) + _dot(u_lo, wrh_ref[...])) + rb_ref[...]
    lane = lax.broadcasted_iota(jnp.int32, (n, LANES), 1)
    lane_f = lane.astype(F32)
    is_group = (lane >= N_EXPERTS) & (lane < N_EXPERTS + N_GROUPS)
    lg = jnp.where(is_group, logits, NEG_BIG)
    g_max = jnp.max(lg, axis=-1, keepdims=True)
    g_idx = jnp.min(jnp.where(lg == g_max, lane_f, 1e9), axis=-1, keepdims=True) - N_EXPERTS
    pg = 1.0 / jnp.sum(jnp.exp(lg - g_max), axis=-1, keepdims=True)
    in_group = (lane < N_EXPERTS) & ((lane >> (EXPERTS_PER_GROUP.bit_length() - 1)).astype(F32) == g_idx)
    le = jnp.where(in_group, logits, NEG_BIG)
    m1 = jnp.max(le, axis=-1, keepdims=True)
    i1 = jnp.min(jnp.where(le == m1, lane_f, 1e9), axis=-1, keepdims=True)
    le2 = jnp.where(lane_f == i1, NEG_BIG, le)
    m2 = jnp.max(le2, axis=-1, keepdims=True)
    i2 = jnp.min(jnp.where(le2 == m2, lane_f, 1e9), axis=-1, keepdims=True)
    e2 = jnp.exp(m2 - m1)
    w1 = pg / (1.0 + e2)
    wf = jnp.where(lane_f == i1, w1, 0.0) + jnp.where(lane_f == i2, w1 * e2, 0.0)
    wf = wf + jnp.where(lane == _E1_LANE, i1, 0.0) + jnp.where(lane == _E2_LANE, i2, 0.0)
    wf_ref[...] = wf.reshape(bt, tt, LANES)


def _attn(x3, oh3, or3, p3, mk, mv, wout, nxg, wq, wo, nfg, wrh, wrl, rb, bt, tt, gate_block0):
    bsz, tlen, d = x3.shape
    kern = functools.partial(_attn_kernel, bt=bt, tt=tt)
    tok = lambda cb: pl.BlockSpec((bt, tt, d), lambda b, t: (b, t, cb))
    mem = pl.BlockSpec((bt,) + mk.shape[1:], lambda b, t: (b,) + (0,) * (mk.ndim - 1))
    full = lambda a: pl.BlockSpec(a.shape, lambda b, t: (0,) * a.ndim)
    nxg, nfg = nxg.reshape(1, d), nfg.reshape(1, d)
    return pl.pallas_call(
        kern,
        grid=(bsz // bt, tlen // tt),
        in_specs=[tok(0), tok(0), tok(0), tok(gate_block0), tok(gate_block0 + 1), mem, mem,
                  full(wout), full(nxg), full(wq), full(wo), full(nfg), full(wrh), full(wrl), full(rb)],
        out_specs=[
            pl.BlockSpec((bt, tt, d), lambda b, t: (b, t, 0)),
            pl.BlockSpec((bt, tt, d), lambda b, t: (b, t, 0)),
            pl.BlockSpec((bt, tt, LANES), lambda b, t: (b, t, 0)),
        ],
        out_shape=[
            jax.ShapeDtypeStruct((bsz, tlen, d), F32),
            jax.ShapeDtypeStruct((bsz, tlen, d), BF16),
            jax.ShapeDtypeStruct((bsz, tlen, LANES), F32),
        ],
        scratch_shapes=[pltpu.VMEM((bt * tt, d), BF16)],
        compiler_params=_cparams(("parallel", "parallel")),
        name="attn",
    )(x3, oh3, or3, p3, p3, mk, mv, wout, nxg, wq, wo, nfg, wrh, wrl, rb)


def _segment_dma(src, dst, src_row, dst_row, length, max_len, sem, start):
    size = _SEG
    while size * 2 <= max_len:
        size *= 2
    while size >= _SEG:
        offset = length & (-2 * size)

        @pl.when((length & size) != 0)
        def _(offset=offset, size=size):
            copy = pltpu.make_async_copy(src.at[pl.ds(pl.multiple_of(src_row + offset, _SEG), size)],
                                         dst.at[pl.ds(pl.multiple_of(dst_row + offset, _SEG), size)], sem)
            if start:
                copy.start()
            else:
                copy.wait()

        size //= 2


def _tile_meta(meta_ref, nt, tile):
    field = lambda k: [meta_ref[(k * nt + tile) * N_GROUPS + g] for g in range(N_GROUPS)]
    return field(0), field(1), field(2)


def _group_of(expert_id):
    return jnp.floor(expert_id * (1.0 / EXPERTS_PER_GROUP))


def _moe_sort_kernel(meta_ref, u_ref, wf_ref, xs_ref, ws_ref, su_ref, sw_ref, zu_ref, zw_ref, sem, *, nt, tm, te):
    i = pl.program_id(0)
    buf = lax.rem(i, 2)
    slots = su_ref.shape[1]
    _, _, base = _tile_meta(meta_ref, nt, i)
    wf = wf_ref[...]
    gid = _group_of(wf.T[_E1_LANE:_E1_LANE + 1, :])
    onehot = lax.broadcasted_iota(jnp.int32, (8, tm), 0).astype(F32) == gid
    earlier = (lax.broadcasted_iota(jnp.int32, (tm, tm), 0)
               < lax.broadcasted_iota(jnp.int32, (tm, tm), 1)).astype(BF16)
    seen = _dot(onehot.astype(BF16), earlier)
    pos = jnp.sum(jnp.where(onehot, seen, 0.0), axis=0, keepdims=True)
    for g in range(N_GROUPS):
        pos = pos + jnp.where(gid == float(g), base[g].astype(F32), 0.0)
    perm = (lax.broadcasted_iota(jnp.int32, (slots, tm), 0).astype(F32) == pos).astype(BF16)
    su_ref[buf] = _dot(perm, u_ref[...]).astype(BF16)
    w_hi, w_lo = _split(wf)
    sw_ref[buf] = _dot(perm, w_hi) + _dot(perm, w_lo)

    def copies(tile, b, start):
        off, cnt, base = _tile_meta(meta_ref, nt, tile)
        for g in range(N_GROUPS):
            _segment_dma(su_ref.at[b], xs_ref, base[g], off[g], cnt[g], tm, sem.at[0, b], start)
            _segment_dma(sw_ref.at[b], ws_ref, base[g], off[g], cnt[g], tm, sem.at[1, b], start)

    copies(i, buf, True)

    @pl.when(i > 0)
    def _():
        copies(i - 1, 1 - buf, False)

    @pl.when(i == nt - 1)
    def _():
        copies(i, buf, False)
        zu_ref[...] = jnp.zeros_like(zu_ref)
        zw_ref[...] = jnp.zeros_like(zw_ref)
        tail = 3 * nt * N_GROUPS + nt
        for start in (True, False):
            for g in range(N_GROUPS):
                off, cnt = meta_ref[tail + g], meta_ref[tail + N_GROUPS + g]
                _segment_dma(zu_ref, xs_ref, 0, off, cnt, te, sem.at[0, 0], start)
                _segment_dma(zw_ref, ws_ref, 0, off, cnt, te, sem.at[1, 0], start)

            def unused(r, carry, start=start):
                row = pl.multiple_of(r * te, te)
                for src, dst, s in ((zu_ref, xs_ref, sem.at[0, 1]), (zw_ref, ws_ref, sem.at[1, 1])):
                    copy = pltpu.make_async_copy(src, dst.at[pl.ds(row, te)], s)
                    if start:
                        copy.start()
                    else:
                        copy.wait()
                return carry

            lax.fori_loop(meta_ref[tail + 2 * N_GROUPS], xs_ref.shape[0] // te, unused, 0)


def _moe_expert_kernel(meta_ref, xs_ref, ws_ref, wg_ref, wu_ref, wd_ref, ys_ref, *, n_tiles):
    r = pl.program_id(0)
    used = r < meta_ref[n_tiles]

    @pl.when(used)
    def _():
        first = meta_ref[r] * EXPERTS_PER_GROUP
        x = xs_ref[...]
        ws = ws_ref[...]
        lane = lax.broadcasted_iota(jnp.int32, ws.shape, 1)
        hid = []
        for e in range(EXPERTS_PER_GROUP):
            h = _silu(_dot(x, wg_ref[e])) * _dot(x, wu_ref[e])
            w_col = jnp.sum(jnp.where(lane == first + e, ws, 0.0), axis=-1, keepdims=True)
            hid.append((h * w_col).astype(BF16))
        ys_ref[...] = _dot(jnp.concatenate(hid, axis=1), wd_ref[...]).astype(ys_ref.dtype)

    @pl.when(jnp.logical_not(used))
    def _():
        ys_ref[...] = jnp.zeros_like(ys_ref)


def _moe_unsort_kernel(meta_ref, wf_ref, h_ref, gf_ref, ys_ref, y_ref, sy_ref, sem, *, nt, tm):
    i = pl.program_id(0)
    buf = lax.rem(i, 2)
    slots = sy_ref.shape[1]

    def copies(tile, b, start):
        off, cnt, base = _tile_meta(meta_ref, nt, tile)
        for g in range(N_GROUPS):
            _segment_dma(ys_ref, sy_ref.at[b], off[g], base[g], cnt[g], tm, sem.at[b], start)

    @pl.when(i == 0)
    def _():
        sy_ref[...] = jnp.zeros_like(sy_ref)
        copies(0, 0, True)

    @pl.when(i + 1 < nt)
    def _():
        copies(i + 1, 1 - buf, True)

    _, _, base = _tile_meta(meta_ref, nt, i)
    gid = _group_of(wf_ref[...][:, _E1_LANE:_E1_LANE + 1])
    onehot = lax.broadcasted_iota(jnp.int32, (tm, LANES), 1).astype(F32) == gid
    earlier = (lax.broadcasted_iota(jnp.int32, (tm, tm), 1)
               < lax.broadcasted_iota(jnp.int32, (tm, tm), 0)).astype(BF16)
    seen = _dot(earlier, onehot.astype(BF16))
    pos = jnp.sum(jnp.where(onehot, seen, 0.0), axis=1, keepdims=True)
    for g in range(N_GROUPS):
        pos = pos + jnp.where(gid == float(g), base[g].astype(F32), 0.0)
    perm_t = (lax.broadcasted_iota(jnp.int32, (tm, slots), 1).astype(F32) == pos).astype(BF16)
    copies(i, buf, False)
    slot_row = lax.broadcasted_iota(jnp.int32, (slots, 1), 0)
    ys = sy_ref[buf]
    moe = _dot(perm_t, jnp.where(slot_row < meta_ref[3 * nt * N_GROUPS + i], ys, jnp.zeros_like(ys)))
    y_ref[...] = _rmsnorm(h_ref[...] + moe, gf_ref[...])


def _moe(u, wf, h, wg, wu, wd, gf, tm, te):
    n, d = u.shape
    ne, _, f = wg.shape
    nt = n // tm
    slots = tm + LANES
    n_tiles = -(-(n + nt * N_GROUPS * (_SEG - 1) + N_GROUPS * (te - 1)) // te)
    rows = n_tiles * te
    i32 = jnp.int32

    gid = (wf[:, _E1_LANE].astype(i32) // EXPERTS_PER_GROUP).reshape(nt, tm)
    cnt = jnp.sum((gid[:, :, None] == jnp.arange(N_GROUPS, dtype=i32)).astype(i32), axis=1)
    cnt = (cnt + _SEG - 1) // _SEG * _SEG
    base = jnp.cumsum(cnt, axis=1) - cnt
    total = jnp.sum(cnt, axis=0)
    region = (total + te - 1) // te * te
    region_end = jnp.cumsum(region)
    region_start = region_end - region
    off = region_start[None, :] + jnp.cumsum(cnt, axis=0) - cnt
    n_used = region_end[-1:] // te
    meta = jnp.concatenate([off.reshape(-1), cnt.reshape(-1), base.reshape(-1), jnp.sum(cnt, axis=1),
                            region_start + total, region - total, n_used]).astype(i32)
    tile_row = jnp.arange(n_tiles, dtype=i32) * te
    tile_group = jnp.minimum(jnp.sum((tile_row[:, None] >= region_end[None, :]).astype(i32), axis=1), N_GROUPS - 1)
    emeta = jnp.concatenate([tile_group, n_used]).astype(i32)

    anyspec = pl.BlockSpec(memory_space=pl.ANY)
    tok = lambda w: pl.BlockSpec((tm, w), lambda i, m: (i, 0))
    xs, ws = pl.pallas_call(
        functools.partial(_moe_sort_kernel, nt=nt, tm=tm, te=te),
        grid_spec=pltpu.PrefetchScalarGridSpec(
            num_scalar_prefetch=1, grid=(nt,),
            in_specs=[tok(d), tok(LANES)],
            out_specs=[anyspec, anyspec],
            scratch_shapes=[pltpu.VMEM((2, slots, d), BF16), pltpu.VMEM((2, slots, LANES), F32),
                            pltpu.VMEM((te, d), BF16), pltpu.VMEM((te, LANES), F32),
                            pltpu.SemaphoreType.DMA((2, 2))]),
        out_shape=[jax.ShapeDtypeStruct((rows, d), BF16), jax.ShapeDtypeStruct((rows, LANES), F32)],
        compiler_params=_cparams(("arbitrary",)),
        name="moe_sort",
    )(meta, u, wf)

    srt = lambda w: pl.BlockSpec((te, w), lambda r, m: (r, 0))
    grp_w = pl.BlockSpec((EXPERTS_PER_GROUP, d, f), lambda r, m: (m[r], 0, 0))
    ys = pl.pallas_call(
        functools.partial(_moe_expert_kernel, n_tiles=n_tiles),
        grid_spec=pltpu.PrefetchScalarGridSpec(
            num_scalar_prefetch=1, grid=(n_tiles,),
            in_specs=[srt(d), srt(LANES), grp_w, grp_w,
                      pl.BlockSpec((EXPERTS_PER_GROUP * f, d), lambda r, m: (m[r], 0))],
            out_specs=srt(d)),
        out_shape=jax.ShapeDtypeStruct((rows, d), BF16),
        compiler_params=_cparams(("parallel",)),
        name="moe_expert",
    )(emeta, xs, ws, wg, wu, wd.reshape(ne * f, d))

    return pl.pallas_call(
        functools.partial(_moe_unsort_kernel, nt=nt, tm=tm),
        grid_spec=pltpu.PrefetchScalarGridSpec(
            num_scalar_prefetch=1, grid=(nt,),
            in_specs=[tok(LANES), tok(d), pl.BlockSpec((1, d), lambda i, m: (0, 0)), anyspec],
            out_specs=tok(d),
            scratch_shapes=[pltpu.VMEM((2, slots, d), BF16), pltpu.SemaphoreType.DMA((2,))]),
        out_shape=jax.ShapeDtypeStruct((n, d), F32),
        compiler_params=_cparams(("arbitrary",)),
        name="moe_unsort",
    )(meta, wf, h, gf.reshape(1, d), ys)


def _pick(n, pref):
    t = min(n, pref)
    while n % t:
        t -= 8
    return t


def _block_diag_pairs(s):
    b, h, n, _ = s.shape
    s = s.reshape(b, h // 2, 2, n, n)
    z = jnp.zeros_like(s[:, :, 0])
    top = jnp.concatenate([s[:, :, 0], z], axis=-1)
    bot = jnp.concatenate([z, s[:, :, 1]], axis=-1)
    return jnp.concatenate([top, bot], axis=-2)


def _pairs_to_heads(sbd):
    b, p, n2, _ = sbd.shape
    n = n2 // 2
    s = jnp.stack([sbd[:, :, :n, :n], sbd[:, :, n:, n:]], axis=2)
    return s.reshape(b, 2 * p, n, n)


def kernel(x_prompt, x_sample, mem_prompt, cache_mem_k, cache_mem_v, state_hgrn, state_rwkv, state_rwkv_shift, norm_mix_g, w_in, hgrn_lower_bounds, hgrn_norm_g, rwkv_mu, rwkv_w0, rwkv_w2, rwkv_a0, rwkv_a2, rwkv_g2, rwkv_k_k, rwkv_k_a, rwkv_r_k, rwkv_ln_g, rwkv_ln_b, w_out, norm_x_g, norm_mem_g, wq_x, wk_x, wv_x, wo_x, norm_ffn_g, router_group_w, router_group_b, router_expert_w, router_expert_b, expert_w_gate, expert_w_up, expert_w_down, norm_final_g):
    assert w_in.shape[0] == 1, "single-layer configuration"
    d = x_prompt.shape[-1]
    n_lora_w, n_lora_a, n_lora_g = rwkv_w2.shape[1], rwkv_a2.shape[1], rwkv_g2.shape[1]
    assert n_lora_w + n_lora_a == LANES and LANES < n_lora_g <= 2 * LANES
    n_lora = n_lora_w + n_lora_a + n_lora_g
    n_rwkv_in = 3 * d + n_lora
    lora_pad = 3 * LANES - n_lora

    pr0 = 4 * d
    gate0 = pr0 + n_rwkv_in
    n_cols = w_in.shape[2] + lora_pad
    w_perm = _permute_columns(w_in, [(0, pr0 + 3 * d), (gate0, 2 * d), (pr0 + 3 * d, n_lora)], n_cols,
                              _pick(d, 128))
    lora_col0 = 9 * d
    gate_block0 = 7
    mu = rwkv_mu[0]
    mul = jnp.pad(mu[3 * d:], (0, lora_pad)).reshape(1, 3 * LANES)
    par = jnp.stack([mu[:d], mu[d:2 * d], mu[2 * d:3 * d], rwkv_w0[0], rwkv_a0[0], rwkv_k_k[0], rwkv_k_a[0],
                     rwkv_r_k[0].reshape(d), rwkv_ln_g[0], rwkv_ln_b[0]])
    par = jnp.pad(par, ((0, _P_ROWS - par.shape[0]), (0, 0)))
    w2p = jnp.pad(rwkv_w2[0], ((0, n_lora_a), (0, 0))).astype(BF16)
    a2p = jnp.pad(rwkv_a2[0], ((n_lora_w, 0), (0, 0))).astype(BF16)
    g2a = rwkv_g2[0][:LANES].astype(BF16)
    g2b = jnp.pad(rwkv_g2[0][LANES:], ((0, 2 * LANES - n_lora_g), (0, 0))).astype(BF16)
    wout_b, wq_b, wo_b = w_out[0].astype(BF16), wq_x[0].astype(BF16), wo_x[0].astype(BF16)
    wr = jnp.concatenate([router_expert_w[0], router_group_w[0],
                          jnp.zeros((d, LANES - N_EXPERTS - N_GROUPS), F32)], axis=1)
    wr_hi, wr_lo = _split(wr)
    rb = jnp.pad(jnp.concatenate([router_expert_b[0].reshape(-1), router_group_b[0]]),
                 (0, LANES - N_EXPERTS - N_GROUPS)).reshape(1, LANES)
    wg_b, wu_b, wd_b = (expert_w_gate[0].astype(BF16), expert_w_up[0].astype(BF16),
                        expert_w_down[0].astype(BF16))

    nb_p = mem_prompt.shape[0]
    mem_k_p, mem_v_p = _mem_kv(mem_prompt, norm_mem_g[0], wk_x[0].astype(BF16), wv_x[0].astype(BF16))

    def group(x3, mem_k, mem_v, s_h, s_r, buf, tile_h, tile_r, chunk_r, bt, tt, te):
        bsz, tlen, _ = x3.shape
        n = bsz * tlen
        p = _norm_matmul(x3.reshape(n, d), norm_mix_g[0], w_perm, _pick(n, 1024), n_cols // 5)
        p3 = p.reshape(bsz, tlen, n_cols)
        o_h, s_h_new = _hgrn(p3, hgrn_lower_bounds, hgrn_norm_g[0], s_h, tile_h, 32)
        buf_rkv = buf[:, :, :3 * d]
        buf_l = jnp.pad(buf[:, :, 3 * d:], ((0, 0), (0, 0), (0, lora_pad)))
        o_r, s_r_bd = _rwkv(p3, buf_rkv, buf_l, par, mul, w2p, a2p, g2a, g2b, _block_diag_pairs(s_r), d, tile_r, chunk_r)
        h2, u, wf = _attn(x3, o_h, o_r, p3, mem_k, mem_v, wout_b, norm_x_g[0], wq_b, wo_b, norm_ffn_g[0],
                          wr_hi, wr_lo, rb, bt, tt, gate_block0)
        y = _moe(u.reshape(n, d), wf.reshape(n, LANES), h2.reshape(n, d), wg_b, wu_b, wd_b, norm_final_g,
                 _pick(n, 512), te)
        shift = jnp.concatenate([p3[:, -1:, pr0:pr0 + 3 * d], p3[:, -1:, lora_col0:lora_col0 + n_lora]], axis=-1)
        return y.reshape(bsz, tlen, d), s_h_new, _pairs_to_heads(s_r_bd), shift

    nb_s, t_s, _ = x_sample.shape
    t_p = x_prompt.shape[1]
    zeros_like_state = lambda s: jnp.zeros((nb_p,) + s.shape[2:], s.dtype)
    y_p, sh_p, sr_p, bf_p = group(
        x_prompt, mem_k_p, mem_v_p, zeros_like_state(state_hgrn), zeros_like_state(state_rwkv),
        zeros_like_state(state_rwkv_shift), _pick(t_p, 256), _pick(t_p, 256), 64, 1, _pick(t_p, 256), 512)
    y_s, sh_s, sr_s, bf_s = group(
        x_sample, cache_mem_k[0], cache_mem_v[0],
        state_hgrn[0], state_rwkv[0], state_rwkv_shift[0], t_s, t_s, t_s, _pick(nb_s, 4), t_s, 256)

    kv_shape = (1,) + mem_k_p.shape[:2] + cache_mem_k.shape[3:]
    return (y_p, y_s, mem_k_p.reshape(kv_shape), mem_v_p.reshape(kv_shape), sh_p[None], sh_s[None],
            sr_p[None], sr_s[None], bf_p[None], bf_s[None])
```

```python
import functools
import math

import jax
import jax.numpy as jnp
from jax import lax
from jax.experimental import pallas as pl
from jax.experimental.pallas import tpu as pltpu

F32 = jnp.float32
BF16 = jnp.bfloat16

NORM_EPS = 1e-6
RWKV_GN_EPS = 64e-5
LANES = 128
HGRN_HEAD = 128
RWKV_HEAD = 64
X_HEADS = 4
N_GROUPS = 4
EXPERTS_PER_GROUP = 8
N_EXPERTS = N_GROUPS * EXPERTS_PER_GROUP
NEG_BIG = -1e30
VMEM_LIMIT = 52 * 1024 * 1024
_E1_LANE, _E2_LANE = N_EXPERTS, N_EXPERTS + 1
_SEG = 16


def _dot(a, b):
    return jnp.dot(a, b, preferred_element_type=F32)


def _dot_nt(a, b):
    return lax.dot_general(a, b, (((1,), (1,)), ((), ())), preferred_element_type=F32)


def _dot_tn(a, b):
    return lax.dot_general(a, b, (((0,), (0,)), ((), ())), preferred_element_type=F32)


def _split(x):
    hi = x.astype(BF16)
    lo = (x - hi.astype(F32)).astype(BF16)
    return hi, lo


def _sigmoid(x):
    return 0.5 * jnp.tanh(0.5 * x) + 0.5


def _silu(x):
    return x * _sigmoid(x)


def _rmsnorm(x, g):
    return x * lax.rsqrt(jnp.mean(x * x, axis=-1, keepdims=True) + NORM_EPS) * g


def _cparams(sem):
    return pltpu.CompilerParams(dimension_semantics=sem, vmem_limit_bytes=VMEM_LIMIT)


def _norm_matmul_kernel(x_ref, g_ref, w_ref, o_ref, u_ref):
    @pl.when(pl.program_id(1) == 0)
    def _():
        u_ref[...] = _rmsnorm(x_ref[...], g_ref[...]).astype(BF16)

    o_ref[...] = _dot(u_ref[...], w_ref[...])


def _norm_matmul(x, g, w, tm, tn):
    n, d = x.shape
    m = w.shape[1]
    return pl.pallas_call(
        _norm_matmul_kernel,
        grid=(n // tm, m // tn),
        in_specs=[
            pl.BlockSpec((tm, d), lambda i, j: (i, 0)),
            pl.BlockSpec((1, d), lambda i, j: (0, 0)),
            pl.BlockSpec((d, tn), lambda i, j: (0, j)),
        ],
        out_specs=pl.BlockSpec((tm, tn), lambda i, j: (i, j)),
        out_shape=jax.ShapeDtypeStruct((n, m), F32),
        scratch_shapes=[pltpu.VMEM((tm, d), BF16)],
        compiler_params=_cparams(("parallel", "arbitrary")),
        name="norm_matmul",
    )(x, g.reshape(1, d), w)


def _permute_columns_kernel(w_ref, o_ref, *, pieces):
    col = 0
    for src, width in pieces:
        o_ref[:, col:col + width] = w_ref[:, src:src + width].astype(o_ref.dtype)
        col += width
    if col < o_ref.shape[1]:
        o_ref[:, col:] = jnp.zeros((o_ref.shape[0], o_ref.shape[1] - col), o_ref.dtype)


def _permute_columns(w, pieces, n_cols, tr):
    _, rows, cols = w.shape
    return pl.pallas_call(
        functools.partial(_permute_columns_kernel, pieces=pieces),
        grid=(rows // tr,),
        in_specs=[pl.BlockSpec((None, tr, cols), lambda i: (0, i, 0))],
        out_specs=pl.BlockSpec((tr, n_cols), lambda i: (i, 0)),
        out_shape=jax.ShapeDtypeStruct((rows, n_cols), BF16),
        compiler_params=_cparams(("parallel",)),
        name="permute_columns",
    )(w)


def _mem_kv_kernel(m_ref, g_ref, wk_ref, wv_ref, k_ref, v_ref):
    u = _rmsnorm(m_ref[...], g_ref[...]).astype(BF16)
    k_ref[...] = _dot(u, wk_ref[...])
    v_ref[...] = _dot(u, wv_ref[...])


def _mem_kv(mem, g, wk, wv):
    bsz, nmem, d = mem.shape
    full = lambda a: pl.BlockSpec(a.shape, lambda b: (0,) * a.ndim)
    out = pl.BlockSpec((None, nmem, d), lambda b: (b, 0, 0))
    g = g.reshape(1, d)
    return pl.pallas_call(
        _mem_kv_kernel,
        grid=(bsz,),
        in_specs=[pl.BlockSpec((None, nmem, d), lambda b: (b, 0, 0)), full(g), full(wk), full(wv)],
        out_specs=[out, out],
        out_shape=[jax.ShapeDtypeStruct((bsz, nmem, d), F32)] * 2,
        compiler_params=_cparams(("parallel",)),
        name="mem_kv",
    )(mem, g, wk, wv)


def _hgrn_kernel(q_ref, f_ref, i_ref, g_ref, lbp_ref, ng_ref, s0_ref, o_ref, sout_ref, st_ref,
                 *, chunk, n_chunks, heads):
    t = pl.program_id(1)

    @pl.when(t == 0)
    def _():
        for h in range(heads):
            st_ref[h] = s0_ref[h].T

    lbp = lbp_ref[...]
    e = jnp.exp(lbp - jnp.max(lbp, axis=0, keepdims=True))
    lb = e[0:1] / jnp.sum(e, axis=0, keepdims=True)
    ng = ng_ref[...]

    row = lax.broadcasted_iota(jnp.int32, (chunk, chunk), 0)
    col = lax.broadcasted_iota(jnp.int32, (chunk, chunk), 1)
    causal = row >= col
    mid = chunk // 2
    to_mid = jnp.where((col >= mid) & (col <= row), 1.0, 0.0) - jnp.where((col > row) & (col < mid), 1.0, 0.0)
    coef = jnp.concatenate([causal.astype(F32), to_mid, (col > row).astype(F32)], axis=0).astype(BF16)

    chunks = range(n_chunks)
    qd, kd, qi, kl, el, vv, gate = [], [], [], [], [], [], []
    for c in chunks:
        rows = slice(c * chunk, (c + 1) * chunk)
        f = lb + (1.0 - lb) * _sigmoid(f_ref[rows, :])
        g_hi, g_lo = _split(jnp.log(f))
        bb = _dot(coef, g_hi) + _dot(coef, g_lo)
        b, b_to_mid, b_to_last = bb[:chunk], bb[chunk:2 * chunk], bb[2 * chunk:]
        q = _silu(q_ref[rows, :])
        k = 1.0 - f
        qd.append((q * jnp.exp(b_to_mid)).astype(BF16))
        kd.append((k * jnp.exp(-b_to_mid)).astype(BF16))
        qi.append((q * jnp.exp(b)).astype(BF16))
        kl.append((k * jnp.exp(b_to_last)).astype(BF16))
        el.append(jnp.exp(b[chunk - 1:chunk, :]))
        vv.append(i_ref[rows, :].astype(BF16))
        gate.append(_silu(g_ref[rows, :]))

    head = lambda x, h: x[:, h * HGRN_HEAD:(h + 1) * HGRN_HEAD]
    items = [(c, h) for c in chunks for h in range(heads)]
    sc = {ch: jnp.where(causal, _dot_nt(head(qd[ch[0]], ch[1]), head(kd[ch[0]], ch[1])), 0.0).astype(BF16)
          for ch in items}
    intra = {ch: _dot(sc[ch], head(vv[ch[0]], ch[1])) for ch in items}
    upd = {ch: _dot_tn(head(vv[ch[0]], ch[1]), head(kl[ch[0]], ch[1])) for ch in items}
    for h in range(heads):
        st = st_ref[h]
        for c in chunks:
            o = intra[c, h] + _dot_nt(head(qi[c], h), st.astype(BF16))
            st = st * head(el[c], h) + upd[c, h]
            o = o * lax.rsqrt(jnp.mean(o * o, axis=-1, keepdims=True) + NORM_EPS)
            o_ref[c * chunk:(c + 1) * chunk, h * HGRN_HEAD:(h + 1) * HGRN_HEAD] = o * ng * head(gate[c], h)
        st_ref[h] = st

    @pl.when(t == pl.num_programs(1) - 1)
    def _():
        for h in range(heads):
            sout_ref[h] = st_ref[h].T


def _hgrn(p3, lbp, ng, s0, tile, chunk):
    bsz, tlen, _ = p3.shape
    heads, dk, dv = s0.shape[1:]
    d = heads * HGRN_HEAD
    kern = functools.partial(_hgrn_kernel, chunk=chunk, n_chunks=tile // chunk, heads=heads)
    pspec = lambda cb: pl.BlockSpec((None, tile, d), lambda b, t: (b, t, cb))
    return pl.pallas_call(
        kern,
        grid=(bsz, tlen // tile),
        in_specs=[
            pspec(0), pspec(1), pspec(2), pspec(3),
            pl.BlockSpec(lbp.shape, lambda b, t: (0, 0)),
            pl.BlockSpec((1, dv), lambda b, t: (0, 0)),
            pl.BlockSpec((None, heads, dk, dv), lambda b, t: (b, 0, 0, 0)),
        ],
        out_specs=[
            pl.BlockSpec((None, tile, d), lambda b, t: (b, t, 0)),
            pl.BlockSpec((None, heads, dk, dv), lambda b, t: (b, 0, 0, 0)),
        ],
        out_shape=[
            jax.ShapeDtypeStruct((bsz, tlen, d), F32),
            jax.ShapeDtypeStruct((bsz, heads, dk, dv), F32),
        ],
        scratch_shapes=[pltpu.VMEM((heads, dv, dk), F32)],
        compiler_params=_cparams(("parallel", "arbitrary")),
        name="hgrn",
    )(p3, p3, p3, p3, lbp, ng.reshape(1, dv), s0)


_P_MU_R, _P_MU_K, _P_MU_V, _P_W0, _P_A0, _P_KK, _P_KA, _P_RK, _P_LNG, _P_LNB = range(10)
_P_ROWS = 16


def _rwkv_kernel(r_ref, k_ref, v_ref, l_ref, br_ref, bk_ref, bv_ref, bl_ref, par_ref, mul_ref,
                 w2_ref, a2_ref, g2a_ref, g2b_ref, s0_ref,
                 o_ref, sout_ref, st_ref, prev_ref, prevl_ref, *, chunk, n_chunks, pairs):
    t = pl.program_id(1)
    tile = chunk * n_chunks

    @pl.when(t == 0)
    def _():
        st_ref[...] = s0_ref[...]
        prev_ref[0:1, :] = br_ref[...]
        prev_ref[1:2, :] = bk_ref[...]
        prev_ref[2:3, :] = bv_ref[...]
        prevl_ref[0:1, :] = bl_ref[...]

    par = par_ref[...]
    prow = lambda i: par[i:i + 1, :]

    def shifted(x, prev_row, mu):
        row_id = lax.broadcasted_iota(jnp.int32, x.shape, 0)
        prev = jnp.where(row_id == 0, prev_row, pltpu.roll(x, 1, 0))
        return x + (prev - x) * mu

    pr, pk, pv, plr = r_ref[...], k_ref[...], v_ref[...], l_ref[...]
    r = shifted(pr, prev_ref[0:1, :], prow(_P_MU_R))
    k = shifted(pk, prev_ref[1:2, :], prow(_P_MU_K))
    v = shifted(pv, prev_ref[2:3, :], prow(_P_MU_V))
    xl = shifted(plr, prevl_ref[0:1, :], mul_ref[...])
    for slot, x in enumerate((pr, pk, pv)):
        prev_ref[slot:slot + 1, :] = x[tile - 1:tile, :]
    prevl_ref[0:1, :] = plr[tile - 1:tile, :]
    x0 = xl[:, 0:LANES]
    x1 = xl[:, LANES:2 * LANES]
    x2 = xl[:, 2 * LANES:3 * LANES]

    w_lin = prow(_P_W0) + _dot(jnp.tanh(x0).astype(BF16), w2_ref[...])
    z = -w_lin
    softplus = jnp.maximum(z, 0.0) + jnp.log(1.0 + jnp.exp(-jnp.abs(z)))
    lw = -jnp.exp(-softplus - 0.5)
    a_icl = _sigmoid(prow(_P_A0) + _dot(x0.astype(BF16), a2_ref[...]))
    g_out = _dot(_sigmoid(x1).astype(BF16), g2a_ref[...]) + _dot(_sigmoid(x2).astype(BF16), g2b_ref[...])

    lane = lax.broadcasted_iota(jnp.int32, (1, LANES), 1)
    m0 = (lane < RWKV_HEAD).astype(F32)
    m1 = 1.0 - m0
    head_shift = RWKV_HEAD.bit_length() - 1
    hr = lax.broadcasted_iota(jnp.int32, (LANES, LANES), 0) >> head_shift
    hc = lax.broadcasted_iota(jnp.int32, (LANES, LANES), 1) >> head_shift
    head_ones = (hr == hc).astype(BF16)
    tile_of = lambda x, j: x[:, j * LANES:(j + 1) * LANES]

    def headsum(x):
        xs = jnp.concatenate([tile_of(x, j) for j in range(pairs)], axis=0)
        ys = _dot(xs.astype(BF16), head_ones)
        return jnp.concatenate([ys[j * tile:(j + 1) * tile] for j in range(pairs)], axis=1)

    kk = k * prow(_P_KK)
    kk = kk * lax.rsqrt(jnp.maximum(headsum(kk * kk), 1e-24))
    k_mod = k * (1.0 + (a_icl - 1.0) * prow(_P_KA))
    a = -kk
    b = kk * a_icl

    ti = lax.broadcasted_iota(jnp.int32, (chunk, chunk), 0)
    tj = lax.broadcasted_iota(jnp.int32, (chunk, chunk), 1)
    tri = (ti >= tj).astype(BF16)
    n2 = 2 * chunk
    si = lax.broadcasted_iota(jnp.int32, (n2, n2), 0)
    sj = lax.broadcasted_iota(jnp.int32, (n2, n2), 1)
    chunk_shift = chunk.bit_length() - 1
    same_head = (si >> chunk_shift) == (sj >> chunk_shift)
    strict = same_head & ((sj & (chunk - 1)) < (si & (chunk - 1)))
    incl = same_head & ((sj & (chunk - 1)) <= (si & (chunk - 1)))
    eye = (si == sj).astype(F32)
    m0_b, m1_b = m0.astype(BF16), m1.astype(BF16)
    tile_b = lambda x, j: tile_of(x, j).astype(BF16)
    twice = lambda x: jnp.concatenate([x, x], axis=0)
    every = range(pairs)

    def stack_b(x, j):
        xb = tile_b(x, j)
        return jnp.concatenate([xb * m0_b, xb * m1_b], axis=0)

    def state_free(rows):
        lw_c, a_c, b_c, k_c, r_c, v_c = lw[rows], a[rows], b[rows], k_mod[rows], r[rows], v[rows]
        lw_hi, lw_lo = _split(lw_c)
        cum = _dot(tri, lw_hi) + _dot(tri, lw_lo)
        cum_last = cum[chunk - 1:chunk, :]
        e_neg = jnp.exp(-cum)
        e_end = jnp.exp(cum_last - cum)
        a_t = a_c * jnp.exp(cum - lw_c)
        b_t = b_c * e_neg
        k_t = k_c * e_neg
        r_t = r_c * jnp.exp(cum)
        b_end = b_c * e_end
        k_end = k_c * e_end

        a_st = [stack_b(a_t, j) for j in every]
        r_st = [stack_b(r_t, j) for j in every]
        v_st = [stack_b(v_c, j) for j in every]
        bend_st = [stack_b(b_end, j) for j in every]
        kend_st = [stack_b(k_end, j) for j in every]
        sc = [_dot_nt(jnp.concatenate([tile_b(a_t, j), tile_b(r_t, j)], axis=0),
                      jnp.concatenate([stack_b(b_t, j), stack_b(k_t, j)], axis=0)) for j in every]
        a_ab = [jnp.where(strict, twice(s[:chunk, :n2]), 0.0) for s in sc]
        a_xk = [jnp.concatenate([jnp.where(strict, twice(s[:chunk, n2:]), 0.0),
                                 jnp.where(incl, twice(s[chunk:, n2:]), 0.0)], axis=0).astype(BF16) for s in sc]
        a_rb = [jnp.where(incl, twice(s[chunk:, :n2]), 0.0).astype(BF16) for s in sc]

        inv = [eye + x for x in a_ab]
        pw = [_dot(x.astype(BF16), x.astype(BF16)) for x in a_ab]
        for _ in range(int(math.log2(chunk)) - 2):
            if n2 % LANES == 0:
                both = [_dot(p.astype(BF16), jnp.concatenate([p, i], axis=1).astype(BF16)) for p, i in zip(pw, inv)]
                pw_next = [x[:, :n2] for x in both]
                inv = [i + x[:, n2:] for i, x in zip(inv, both)]
            else:
                pw_next = [_dot(p.astype(BF16), p.astype(BF16)) for p in pw]
                inv = [i + _dot(p.astype(BF16), i.astype(BF16)) for i, p in zip(inv, pw)]
            pw = pw_next
        inv = [i + _dot(p.astype(BF16), i.astype(BF16)) for i, p in zip(inv, pw)]

        av = [_dot(a_xk[j], v_st[j]) for j in every]
        wu = [_dot(inv[j].astype(BF16), jnp.concatenate([a_st[j], av[j][:n2].astype(BF16)], axis=1)).astype(BF16)
              for j in every]
        qo = [_dot(a_rb[j], wu[j]) + jnp.concatenate([r_st[j].astype(F32), av[j][n2:]], axis=1) for j in every]
        mn = [_dot_tn(wu[j], bend_st[j]) for j in every]
        nn = [mn[j][LANES:] + _dot_tn(v_st[j], kend_st[j]) for j in every]
        return qo, mn, nn, jnp.exp(cum_last)

    parts = [state_free(slice(c * chunk, (c + 1) * chunk)) for c in range(n_chunks)]
    outs = []
    for qo, mn, nn, g_last in parts:
        row = []
        for j in every:
            s0 = st_ref[j]
            s0_b = s0.astype(BF16)
            q = (qo[j][:chunk, :LANES] + qo[j][chunk:, :LANES]).astype(BF16)
            row.append(_dot_nt(q, s0_b) + qo[j][:chunk, LANES:] + qo[j][chunk:, LANES:])
            st_ref[j] = s0 * tile_of(g_last, j) + _dot(s0_b, mn[j][:LANES].astype(BF16)) + nn[j]
        outs.append(jnp.concatenate(row, axis=1))
    o = outs[0] if n_chunks == 1 else jnp.concatenate(outs, axis=0)

    inv_n = 1.0 / RWKV_HEAD
    mean = headsum(o) * inv_n
    dlt = o - mean
    var = headsum(dlt * dlt) * inv_n
    o = dlt * lax.rsqrt(var + RWKV_GN_EPS) * prow(_P_LNG) + prow(_P_LNB)
    bonus = headsum(r * k_mod * prow(_P_RK))
    o_ref[...] = (o + bonus * v) * g_out

    @pl.when(t == pl.num_programs(1) - 1)
    def _():
        sout_ref[...] = st_ref[...]


def _rwkv(p3, buf_rkv, buf_l, par, mul, w2p, a2p, g2a, g2b, s0bd, d, tile, chunk):
    bsz, tlen, _ = p3.shape
    pairs = d // LANES
    nl = 3 * LANES
    kern = functools.partial(_rwkv_kernel, chunk=chunk, n_chunks=tile // chunk, pairs=pairs)
    pcol = lambda cb: pl.BlockSpec((None, tile, d), lambda b, t: (b, t, cb))
    bcol = lambda cb: pl.BlockSpec((None, 1, d), lambda b, t: (b, 0, cb))
    full = lambda a: pl.BlockSpec(a.shape, lambda b, t: (0,) * a.ndim)
    state = pl.BlockSpec((None, pairs, LANES, LANES), lambda b, t: (b, 0, 0, 0))
    return pl.pallas_call(
        kern,
        grid=(bsz, tlen // tile),
        in_specs=[
            pcol(4), pcol(5), pcol(6),
            pl.BlockSpec((None, tile, nl), lambda b, t: (b, t, 9 * d // nl)),
            bcol(0), bcol(1), bcol(2),
            pl.BlockSpec((None, 1, nl), lambda b, t: (b, 0, 0)),
            full(par), full(mul), full(w2p), full(a2p), full(g2a), full(g2b), state,
        ],
        out_specs=[pl.BlockSpec((None, tile, d), lambda b, t: (b, t, 0)), state],
        out_shape=[
            jax.ShapeDtypeStruct((bsz, tlen, d), F32),
            jax.ShapeDtypeStruct((bsz, pairs, LANES, LANES), F32),
        ],
        scratch_shapes=[pltpu.VMEM((pairs, LANES, LANES), F32), pltpu.VMEM((8, d), F32),
                        pltpu.VMEM((8, nl), F32)],
        compiler_params=_cparams(("parallel", "arbitrary")),
        name="rwkv",
    )(p3, p3, p3, p3, buf_rkv, buf_rkv, buf_rkv, buf_l, par, mul, w2p, a2p, g2a, g2b, s0bd)


def _attn_kernel(x_ref, oh_ref, or_ref, ga_ref, gb_ref, mk_ref, mv_ref, wout_ref, nxg_ref, wq_ref, wo_ref,
                 nfg_ref, wrh_ref, wrl_ref, rb_ref, h_ref, u_ref, wf_ref, att_ref, *, bt, tt):
    n = bt * tt
    d = x_ref.shape[-1]
    flat = lambda ref: ref[...].reshape(n, d)
    merged = _sigmoid(flat(ga_ref)) * flat(oh_ref) + _sigmoid(flat(gb_ref)) * flat(or_ref)
    h1 = flat(x_ref) + _dot(merged.astype(BF16), wout_ref[...])

    q = _dot(_rmsnorm(h1, nxg_ref[...]).astype(BF16), wq_ref[...])
    dh = d // X_HEADS
    scale = dh ** -0.5
    for bi in range(bt):
        rows = slice(bi * tt, (bi + 1) * tt)
        if len(mk_ref.shape) == 3:
            mem_heads = lambda ref: [ref[bi, :, hh * dh:(hh + 1) * dh].astype(BF16) for hh in range(X_HEADS)]
        else:
            def mem_heads(ref):
                by_head = jnp.transpose(ref[bi], (1, 0, 2)).astype(BF16)
                return [by_head[hh] for hh in range(X_HEADS)]
        k_heads, v_heads = mem_heads(mk_ref), mem_heads(mv_ref)
        for hh in range(X_HEADS):
            cols = slice(hh * dh, (hh + 1) * dh)
            s = _dot_nt(q[rows, cols].astype(BF16), k_heads[hh]) * scale
            p = jnp.exp(s - jnp.max(s, axis=-1, keepdims=True))
            p = p / jnp.sum(p, axis=-1, keepdims=True)
            att_ref[rows, cols] = _dot(p.astype(BF16), v_heads[hh]).astype(BF16)
    h2 = h1 + _dot(att_ref[...], wo_ref[...])
    h_ref[...] = h2.reshape(bt, tt, d)

    u = _rmsnorm(h2, nfg_ref[...])
    u_ref[...] = u.astype(BF16).reshape(bt, tt, d)
    u_hi, u_lo = _split(u)
    logits = (_dot(u_hi, wrh_ref[...]) + _dot(u_hi, wrl_ref[...]) + _dot(u_lo, wrh_ref[...])) + rb_ref[...]
    lane = lax.broadcasted_iota(jnp.int32, (n, LANES), 1)
    lane_f = lane.astype(F32)
    is_group = (lane >= N_EXPERTS) & (lane < N_EXPERTS + N_GROUPS)
    lg = jnp.where(is_group, logits, NEG_BIG)
    g_max = jnp.max(lg, axis=-1, keepdims=True)
    g_idx = jnp.min(jnp.where(lg == g_max, lane_f, 1e9), axis=-1, keepdims=True) - N_EXPERTS
    pg = 1.0 / jnp.sum(jnp.exp(lg - g_max), axis=-1, keepdims=True)
    in_group = (lane < N_EXPERTS) & ((lane >> (EXPERTS_PER_GROUP.bit_length() - 1)).astype(F32) == g_idx)
    le = jnp.where(in_group, logits, NEG_BIG)
    m1 = jnp.max(le, axis=-1, keepdims=True)
    i1 = jnp.min(jnp.where(le == m1, lane_f, 1e9), axis=-1, keepdims=True)
    le2 = jnp.where(lane_f == i1, NEG_BIG, le)
    m2 = jnp.max(le2, axis=-1, keepdims=True)
    i2 = jnp.min(jnp.where(le2 == m2, lane_f, 1e9), axis=-1, keepdims=True)
    e2 = jnp.exp(m2 - m1)
    w1 = pg / (1.0 + e2)
    wf = jnp.where(lane_f == i1, w1, 0.0) + jnp.where(lane_f == i2, w1 * e2, 0.0)
    wf = wf + jnp.where(lane == _E1_LANE, i1, 0.0) + jnp.where(lane == _E2_LANE, i2, 0.0)
    wf_ref[...] = wf.reshape(bt, tt, LANES)


def _attn(x3, oh3, or3, p3, mk, mv, wout, nxg, wq, wo, nfg, wrh, wrl, rb, bt, tt, gate_block0):
    bsz, tlen, d = x3.shape
    kern = functools.partial(_attn_kernel, bt=bt, tt=tt)
    tok = lambda cb: pl.BlockSpec((bt, tt, d), lambda b, t: (b, t, cb))
    mem = pl.BlockSpec((bt,) + mk.shape[1:], lambda b, t: (b,) + (0,) * (mk.ndim - 1))
    full = lambda a: pl.BlockSpec(a.shape, lambda b, t: (0,) * a.ndim, pipeline_mode=pl.Buffered(1))
    nxg, nfg = nxg.reshape(1, d), nfg.reshape(1, d)
    return pl.pallas_call(
        kern,
        grid=(bsz // bt, tlen // tt),
        in_specs=[tok(0), tok(0), tok(0), tok(gate_block0), tok(gate_block0 + 1), mem, mem,
                  full(wout), full(nxg), full(wq), full(wo), full(nfg), full(wrh), full(wrl), full(rb)],
        out_specs=[
            pl.BlockSpec((bt, tt, d), lambda b, t: (b, t, 0)),
            pl.BlockSpec((bt, tt, d), lambda b, t: (b, t, 0)),
            pl.BlockSpec((bt, tt, LANES), lambda b, t: (b, t, 0)),
        ],
        out_shape=[
            jax.ShapeDtypeStruct((bsz, tlen, d), F32),
            jax.ShapeDtypeStruct((bsz, tlen, d), BF16),
            jax.ShapeDtypeStruct((bsz, tlen, LANES), F32),
        ],
        scratch_shapes=[pltpu.VMEM((bt * tt, d), BF16)],
        compiler_params=_cparams(("parallel", "parallel")),
        name="attn",
    )(x3, oh3, or3, p3, p3, mk, mv, wout, nxg, wq, wo, nfg, wrh, wrl, rb)


def _segment_dma(src, dst, src_row, dst_row, length, max_len, sem, start):
    size = _SEG
    while size * 2 <= max_len:
        size *= 2
    while size >= _SEG:
        offset = length & (-2 * size)

        @pl.when((length & size) != 0)
        def _(offset=offset, size=size):
            copy = pltpu.make_async_copy(src.at[pl.ds(pl.multiple_of(src_row + offset, _SEG), size)],
                                         dst.at[pl.ds(pl.multiple_of(dst_row + offset, _SEG), size)], sem)
            if start:
                copy.start()
            else:
                copy.wait()

        size //= 2


def _tile_meta(meta_ref, nt, tile):
    field = lambda k: [meta_ref[(k * nt + tile) * N_GROUPS + g] for g in range(N_GROUPS)]
    return field(0), field(1), field(2)


def _group_of(expert_id):
    return jnp.floor(expert_id * (1.0 / EXPERTS_PER_GROUP))


def _moe_sort_kernel(meta_ref, u_ref, wf_ref, xs_ref, ws_ref, su_ref, sw_ref, zu_ref, zw_ref, sem, *, nt, tm, te):
    i = pl.program_id(0)
    buf = lax.rem(i, 2)
    slots = su_ref.shape[1]
    _, _, base = _tile_meta(meta_ref, nt, i)
    wf = wf_ref[...]
    gid = _group_of(wf.T[_E1_LANE:_E1_LANE + 1, :])
    onehot = lax.broadcasted_iota(jnp.int32, (8, tm), 0).astype(F32) == gid
    earlier = (lax.broadcasted_iota(jnp.int32, (tm, tm), 0)
               < lax.broadcasted_iota(jnp.int32, (tm, tm), 1)).astype(BF16)
    seen = _dot(onehot.astype(BF16), earlier)
    pos = jnp.sum(jnp.where(onehot, seen, 0.0), axis=0, keepdims=True)
    for g in range(N_GROUPS):
        pos = pos + jnp.where(gid == float(g), base[g].astype(F32), 0.0)
    perm = (lax.broadcasted_iota(jnp.int32, (slots, tm), 0).astype(F32) == pos).astype(BF16)
    su_ref[buf] = _dot(perm, u_ref[...]).astype(BF16)
    w_hi, w_lo = _split(wf)
    sw_ref[buf] = _dot(perm, w_hi) + _dot(perm, w_lo)

    def copies(tile, b, start):
        off, cnt, base = _tile_meta(meta_ref, nt, tile)
        for g in range(N_GROUPS):
            _segment_dma(su_ref.at[b], xs_ref, base[g], off[g], cnt[g], tm, sem.at[0, b], start)
            _segment_dma(sw_ref.at[b], ws_ref, base[g], off[g], cnt[g], tm, sem.at[1, b], start)

    copies(i, buf, True)

    @pl.when(i > 0)
    def _():
        copies(i - 1, 1 - buf, False)

    @pl.when(i == nt - 1)
    def _():
        copies(i, buf, False)
        zu_ref[...] = jnp.zeros_like(zu_ref)
        zw_ref[...] = jnp.zeros_like(zw_ref)
        tail = 3 * nt * N_GROUPS + nt
        for start in (True, False):
            for g in range(N_GROUPS):
                off, cnt = meta_ref[tail + g], meta_ref[tail + N_GROUPS + g]
                _segment_dma(zu_ref, xs_ref, 0, off, cnt, te, sem.at[0, 0], start)
                _segment_dma(zw_ref, ws_ref, 0, off, cnt, te, sem.at[1, 0], start)

            def unused(r, carry, start=start):
                row = pl.multiple_of(r * te, te)
                for src, dst, s in ((zu_ref, xs_ref, sem.at[0, 1]), (zw_ref, ws_ref, sem.at[1, 1])):
                    copy = pltpu.make_async_copy(src, dst.at[pl.ds(row, te)], s)
                    if start:
                        copy.start()
                    else:
                        copy.wait()
                return carry

            lax.fori_loop(meta_ref[tail + 2 * N_GROUPS], xs_ref.shape[0] // te, unused, 0)


def _moe_expert_kernel(meta_ref, xs_ref, ws_ref, wg_ref, wu_ref, wd_ref, ys_ref, *, n_tiles):
    r = pl.program_id(0)
    used = r < meta_ref[n_tiles]

    @pl.when(used)
    def _():
        first = meta_ref[r] * EXPERTS_PER_GROUP
        x = xs_ref[...]
        ws = ws_ref[...]
        lane = lax.broadcasted_iota(jnp.int32, ws.shape, 1)
        hid = []
        for e in range(EXPERTS_PER_GROUP):
            h = _silu(_dot(x, wg_ref[e])) * _dot(x, wu_ref[e])
            w_col = jnp.sum(jnp.where(lane == first + e, ws, 0.0), axis=-1, keepdims=True)
            hid.append((h * w_col).astype(BF16))
        ys_ref[...] = _dot(jnp.concatenate(hid, axis=1), wd_ref[...]).astype(ys_ref.dtype)

    @pl.when(jnp.logical_not(used))
    def _():
        ys_ref[...] = jnp.zeros_like(ys_ref)


def _moe_unsort_kernel(meta_ref, wf_ref, h_ref, gf_ref, ys_ref, y_ref, sy_ref, sem, *, nt, tm):
    i = pl.program_id(0)
    buf = lax.rem(i, 2)
    slots = sy_ref.shape[1]

    def copies(tile, b, start):
        off, cnt, base = _tile_meta(meta_ref, nt, tile)
        for g in range(N_GROUPS):
            _segment_dma(ys_ref, sy_ref.at[b], off[g], base[g], cnt[g], tm, sem.at[b], start)

    @pl.when(i == 0)
    def _():
        sy_ref[...] = jnp.zeros_like(sy_ref)
        copies(0, 0, True)

    @pl.when(i + 1 < nt)
    def _():
        copies(i + 1, 1 - buf, True)

    _, _, base = _tile_meta(meta_ref, nt, i)
    gid = _group_of(wf_ref[...][:, _E1_LANE:_E1_LANE + 1])
    onehot = lax.broadcasted_iota(jnp.int32, (tm, LANES), 1).astype(F32) == gid
    earlier = (lax.broadcasted_iota(jnp.int32, (tm, tm), 1)
               < lax.broadcasted_iota(jnp.int32, (tm, tm), 0)).astype(BF16)
    seen = _dot(earlier, onehot.astype(BF16))
    pos = jnp.sum(jnp.where(onehot, seen, 0.0), axis=1, keepdims=True)
    for g in range(N_GROUPS):
        pos = pos + jnp.where(gid == float(g), base[g].astype(F32), 0.0)
    perm_t = (lax.broadcasted_iota(jnp.int32, (tm, slots), 1).astype(F32) == pos).astype(BF16)
    copies(i, buf, False)
    slot_row = lax.broadcasted_iota(jnp.int32, (slots, 1), 0)
    ys = sy_ref[buf]
    moe = _dot(perm_t, jnp.where(slot_row < meta_ref[3 * nt * N_GROUPS + i], ys, jnp.zeros_like(ys)))
    y_ref[...] = _rmsnorm(h_ref[...] + moe, gf_ref[...])


def _moe(u, wf, h, wg, wu, wd, gf, tm, te):
    n, d = u.shape
    ne, _, f = wg.shape
    nt = n // tm
    slots = tm + LANES
    n_tiles = -(-(n + nt * N_GROUPS * (_SEG - 1) + N_GROUPS * (te - 1)) // te)
    rows = n_tiles * te
    i32 = jnp.int32

    gid = (wf[:, _E1_LANE].astype(i32) // EXPERTS_PER_GROUP).reshape(nt, tm)
    cnt = jnp.sum((gid[:, :, None] == jnp.arange(N_GROUPS, dtype=i32)).astype(i32), axis=1)
    cnt = (cnt + _SEG - 1) // _SEG * _SEG
    base = jnp.cumsum(cnt, axis=1) - cnt
    total = jnp.sum(cnt, axis=0)
    region = (total + te - 1) // te * te
    region_end = jnp.cumsum(region)
    region_start = region_end - region
    off = region_start[None, :] + jnp.cumsum(cnt, axis=0) - cnt
    n_used = region_end[-1:] // te
    meta = jnp.concatenate([off.reshape(-1), cnt.reshape(-1), base.reshape(-1), jnp.sum(cnt, axis=1),
                            region_start + total, region - total, n_used]).astype(i32)
    tile_row = jnp.arange(n_tiles, dtype=i32) * te
    tile_group = jnp.minimum(jnp.sum((tile_row[:, None] >= region_end[None, :]).astype(i32), axis=1), N_GROUPS - 1)
    emeta = jnp.concatenate([tile_group, n_used]).astype(i32)

    anyspec = pl.BlockSpec(memory_space=pl.ANY)
    tok = lambda w: pl.BlockSpec((tm, w), lambda i, m: (i, 0))
    xs, ws = pl.pallas_call(
        functools.partial(_moe_sort_kernel, nt=nt, tm=tm, te=te),
        grid_spec=pltpu.PrefetchScalarGridSpec(
            num_scalar_prefetch=1, grid=(nt,),
            in_specs=[tok(d), tok(LANES)],
            out_specs=[anyspec, anyspec],
            scratch_shapes=[pltpu.VMEM((2, slots, d), BF16), pltpu.VMEM((2, slots, LANES), F32),
                            pltpu.VMEM((te, d), BF16), pltpu.VMEM((te, LANES), F32),
                            pltpu.SemaphoreType.DMA((2, 2))]),
        out_shape=[jax.ShapeDtypeStruct((rows, d), BF16), jax.ShapeDtypeStruct((rows, LANES), F32)],
        compiler_params=_cparams(("arbitrary",)),
        name="moe_sort",
    )(meta, u, wf)

    srt = lambda w: pl.BlockSpec((te, w), lambda r, m: (r, 0))
    grp_w = pl.BlockSpec((EXPERTS_PER_GROUP, d, f), lambda r, m: (m[r], 0, 0))
    ys = pl.pallas_call(
        functools.partial(_moe_expert_kernel, n_tiles=n_tiles),
        grid_spec=pltpu.PrefetchScalarGridSpec(
            num_scalar_prefetch=1, grid=(n_tiles,),
            in_specs=[srt(d), srt(LANES), grp_w, grp_w,
                      pl.BlockSpec((EXPERTS_PER_GROUP * f, d), lambda r, m: (m[r], 0))],
            out_specs=srt(d)),
        out_shape=jax.ShapeDtypeStruct((rows, d), BF16),
        compiler_params=_cparams(("parallel",)),
        name="moe_expert",
    )(emeta, xs, ws, wg, wu, wd.reshape(ne * f, d))

    return pl.pallas_call(
        functools.partial(_moe_unsort_kernel, nt=nt, tm=tm),
        grid_spec=pltpu.PrefetchScalarGridSpec(
            num_scalar_prefetch=1, grid=(nt,),
            in_specs=[tok(LANES), tok(d), pl.BlockSpec((1, d), lambda i, m: (0, 0)), anyspec],
            out_specs=tok(d),
            scratch_shapes=[pltpu.VMEM((2, slots, d), BF16), pltpu.SemaphoreType.DMA((2,))]),
        out_shape=jax.ShapeDtypeStruct((n, d), F32),
        compiler_params=_cparams(("arbitrary",)),
        name="moe_unsort",
    )(meta, wf, h, gf.reshape(1, d), ys)


def _pick(n, pref):
    t = min(n, pref)
    while n % t:
        t -= 8
    return t


def _block_diag_pairs(s):
    b, h, n, _ = s.shape
    s = s.reshape(b, h // 2, 2, n, n)
    z = jnp.zeros_like(s[:, :, 0])
    top = jnp.concatenate([s[:, :, 0], z], axis=-1)
    bot = jnp.concatenate([z, s[:, :, 1]], axis=-1)
    return jnp.concatenate([top, bot], axis=-2)


def _pairs_to_heads(sbd):
    b, p, n2, _ = sbd.shape
    n = n2 // 2
    s = jnp.stack([sbd[:, :, :n, :n], sbd[:, :, n:, n:]], axis=2)
    return s.reshape(b, 2 * p, n, n)


def kernel(x_prompt, x_sample, mem_prompt, cache_mem_k, cache_mem_v, state_hgrn, state_rwkv, state_rwkv_shift, norm_mix_g, w_in, hgrn_lower_bounds, hgrn_norm_g, rwkv_mu, rwkv_w0, rwkv_w2, rwkv_a0, rwkv_a2, rwkv_g2, rwkv_k_k, rwkv_k_a, rwkv_r_k, rwkv_ln_g, rwkv_ln_b, w_out, norm_x_g, norm_mem_g, wq_x, wk_x, wv_x, wo_x, norm_ffn_g, router_group_w, router_group_b, router_expert_w, router_expert_b, expert_w_gate, expert_w_up, expert_w_down, norm_final_g):
    assert w_in.shape[0] == 1, "single-layer configuration"
    d = x_prompt.shape[-1]
    n_lora_w, n_lora_a, n_lora_g = rwkv_w2.shape[1], rwkv_a2.shape[1], rwkv_g2.shape[1]
    assert n_lora_w + n_lora_a == LANES and LANES < n_lora_g <= 2 * LANES
    n_lora = n_lora_w + n_lora_a + n_lora_g
    n_rwkv_in = 3 * d + n_lora
    lora_pad = 3 * LANES - n_lora

    pr0 = 4 * d
    gate0 = pr0 + n_rwkv_in
    n_cols = w_in.shape[2] + lora_pad
    w_perm = _permute_columns(w_in, [(0, pr0 + 3 * d), (gate0, 2 * d), (pr0 + 3 * d, n_lora)], n_cols,
                              _pick(d, 128))
    lora_col0 = 9 * d
    gate_block0 = 7
    mu = rwkv_mu[0]
    mul = jnp.pad(mu[3 * d:], (0, lora_pad)).reshape(1, 3 * LANES)
    par = jnp.stack([mu[:d], mu[d:2 * d], mu[2 * d:3 * d], rwkv_w0[0], rwkv_a0[0], rwkv_k_k[0], rwkv_k_a[0],
                     rwkv_r_k[0].reshape(d), rwkv_ln_g[0], rwkv_ln_b[0]])
    par = jnp.pad(par, ((0, _P_ROWS - par.shape[0]), (0, 0)))
    w2p = jnp.pad(rwkv_w2[0], ((0, n_lora_a), (0, 0))).astype(BF16)
    a2p = jnp.pad(rwkv_a2[0], ((n_lora_w, 0), (0, 0))).astype(BF16)
    g2a = rwkv_g2[0][:LANES].astype(BF16)
    g2b = jnp.pad(rwkv_g2[0][LANES:], ((0, 2 * LANES - n_lora_g), (0, 0))).astype(BF16)
    wout_b, wq_b, wo_b = w_out[0].astype(BF16), wq_x[0].astype(BF16), wo_x[0].astype(BF16)
    wr = jnp.concatenate([router_expert_w[0], router_group_w[0],
                          jnp.zeros((d, LANES - N_EXPERTS - N_GROUPS), F32)], axis=1)
    wr_hi, wr_lo = _split(wr)
    rb = jnp.pad(jnp.concatenate([router_expert_b[0].reshape(-1), router_group_b[0]]),
                 (0, LANES - N_EXPERTS - N_GROUPS)).reshape(1, LANES)
    wg_b, wu_b, wd_b = (expert_w_gate[0].astype(BF16), expert_w_up[0].astype(BF16),
                        expert_w_down[0].astype(BF16))

    nb_p = mem_prompt.shape[0]
    mem_k_p, mem_v_p = _mem_kv(mem_prompt, norm_mem_g[0], wk_x[0].astype(BF16), wv_x[0].astype(BF16))

    def group(x3, mem_k, mem_v, s_h, s_r, buf, tile_h, tile_r, chunk_r, bt, tt, te):
        bsz, tlen, _ = x3.shape
        n = bsz * tlen
        p = _norm_matmul(x3.reshape(n, d), norm_mix_g[0], w_perm, _pick(n, 1024), n_cols // 5)
        p3 = p.reshape(bsz, tlen, n_cols)
        o_h, s_h_new = _hgrn(p3, hgrn_lower_bounds, hgrn_norm_g[0], s_h, tile_h, 32)
        buf_rkv = buf[:, :, :3 * d]
        buf_l = jnp.pad(buf[:, :, 3 * d:], ((0, 0), (0, 0), (0, lora_pad)))
        o_r, s_r_bd = _rwkv(p3, buf_rkv, buf_l, par, mul, w2p, a2p, g2a, g2b, _block_diag_pairs(s_r), d, tile_r, chunk_r)
        h2, u, wf = _attn(x3, o_h, o_r, p3, mem_k, mem_v, wout_b, norm_x_g[0], wq_b, wo_b, norm_ffn_g[0],
                          wr_hi, wr_lo, rb, bt, tt, gate_block0)
        y = _moe(u.reshape(n, d), wf.reshape(n, LANES), h2.reshape(n, d), wg_b, wu_b, wd_b, norm_final_g,
                 _pick(n, 512), te)
        shift = jnp.concatenate([p3[:, -1:, pr0:pr0 + 3 * d], p3[:, -1:, lora_col0:lora_col0 + n_lora]], axis=-1)
        return y.reshape(bsz, tlen, d), s_h_new, _pairs_to_heads(s_r_bd), shift

    nb_s, t_s, _ = x_sample.shape
    t_p = x_prompt.shape[1]
    zeros_like_state = lambda s: jnp.zeros((nb_p,) + s.shape[2:], s.dtype)
    y_p, sh_p, sr_p, bf_p = group(
        x_prompt, mem_k_p, mem_v_p, zeros_like_state(state_hgrn), zeros_like_state(state_rwkv),
        zeros_like_state(state_rwkv_shift), _pick(t_p, 256), _pick(t_p, 256), 64, 1, _pick(t_p, 512), 512)
    y_s, sh_s, sr_s, bf_s = group(
        x_sample, cache_mem_k[0], cache_mem_v[0],
        state_hgrn[0], state_rwkv[0], state_rwkv_shift[0], t_s, t_s, t_s, _pick(nb_s, 4), t_s, 256)

    kv_shape = (1,) + mem_k_p.shape[:2] + cache_mem_k.shape[3:]
    return (y_p, y_s, mem_k_p.reshape(kv_shape), mem_v_p.reshape(kv_shape), sh_p[None], sh_s[None],
            sr_p[None], sr_s[None], bf_p[None], bf_s[None])
```

```python
import functools
import math

import jax
import jax.numpy as jnp
from jax import lax
from jax.experimental import pallas as pl
from jax.experimental.pallas import tpu as pltpu

F32 = jnp.float32
BF16 = jnp.bfloat16

NORM_EPS = 1e-6
RWKV_GN_EPS = 64e-5
LANES = 128
HGRN_HEAD = 128
RWKV_HEAD = 64
X_HEADS = 4
N_GROUPS = 4
EXPERTS_PER_GROUP = 8
N_EXPERTS = N_GROUPS * EXPERTS_PER_GROUP
NEG_BIG = -1e30
VMEM_LIMIT = 52 * 1024 * 1024
_E1_LANE, _E2_LANE = N_EXPERTS, N_EXPERTS + 1
_SEG = 16


def _dot(a, b):
    return jnp.dot(a, b, preferred_element_type=F32)


def _dot_nt(a, b):
    return lax.dot_general(a, b, (((1,), (1,)), ((), ())), preferred_element_type=F32)


def _dot_tn(a, b):
    return lax.dot_general(a, b, (((0,), (0,)), ((), ())), preferred_element_type=F32)


def _split(x):
    hi = x.astype(BF16)
    lo = (x - hi.astype(F32)).astype(BF16)
    return hi, lo


def _sigmoid(x):
    return 0.5 * jnp.tanh(0.5 * x) + 0.5


def _silu(x):
    return x * _sigmoid(x)


def _rmsnorm(x, g):
    return x * lax.rsqrt(jnp.mean(x * x, axis=-1, keepdims=True) + NORM_EPS) * g


def _cparams(sem):
    return pltpu.CompilerParams(dimension_semantics=sem, vmem_limit_bytes=VMEM_LIMIT)


def _norm_matmul_kernel(x_ref, g_ref, w_ref, o_ref, u_ref):
    @pl.when(pl.program_id(1) == 0)
    def _():
        u_ref[...] = _rmsnorm(x_ref[...], g_ref[...]).astype(BF16)

    o_ref[...] = _dot(u_ref[...], w_ref[...])


def _norm_matmul(x, g, w, tm, tn):
    n, d = x.shape
    m = w.shape[1]
    return pl.pallas_call(
        _norm_matmul_kernel,
        grid=(n // tm, m // tn),
        in_specs=[
            pl.BlockSpec((tm, d), lambda i, j: (i, 0)),
            pl.BlockSpec((1, d), lambda i, j: (0, 0)),
            pl.BlockSpec((d, tn), lambda i, j: (0, j)),
        ],
        out_specs=pl.BlockSpec((tm, tn), lambda i, j: (i, j)),
        out_shape=jax.ShapeDtypeStruct((n, m), F32),
        scratch_shapes=[pltpu.VMEM((tm, d), BF16)],
        compiler_params=_cparams(("parallel", "arbitrary")),
        name="norm_matmul",
    )(x, g.reshape(1, d), w)


def _permute_columns_kernel(w_ref, o_ref, *, pieces):
    col = 0
    for src, width in pieces:
        o_ref[:, col:col + width] = w_ref[:, src:src + width].astype(o_ref.dtype)
        col += width
    if col < o_ref.shape[1]:
        o_ref[:, col:] = jnp.zeros((o_ref.shape[0], o_ref.shape[1] - col), o_ref.dtype)


def _permute_columns(w, pieces, n_cols, tr):
    _, rows, cols = w.shape
    return pl.pallas_call(
        functools.partial(_permute_columns_kernel, pieces=pieces),
        grid=(rows // tr,),
        in_specs=[pl.BlockSpec((None, tr, cols), lambda i: (0, i, 0))],
        out_specs=pl.BlockSpec((tr, n_cols), lambda i: (i, 0)),
        out_shape=jax.ShapeDtypeStruct((rows, n_cols), BF16),
        compiler_params=_cparams(("parallel",)),
        name="permute_columns",
    )(w)


def _mem_kv_kernel(m_ref, g_ref, wk_ref, wv_ref, k_ref, v_ref):
    u = _rmsnorm(m_ref[...], g_ref[...]).astype(BF16)
    k_ref[...] = _dot(u, wk_ref[...])
    v_ref[...] = _dot(u, wv_ref[...])


def _mem_kv(mem, g, wk, wv):
    bsz, nmem, d = mem.shape
    full = lambda a: pl.BlockSpec(a.shape, lambda b: (0,) * a.ndim)
    out = pl.BlockSpec((None, nmem, d), lambda b: (b, 0, 0))
    g = g.reshape(1, d)
    return pl.pallas_call(
        _mem_kv_kernel,
        grid=(bsz,),
        in_specs=[pl.BlockSpec((None, nmem, d), lambda b: (b, 0, 0)), full(g), full(wk), full(wv)],
        out_specs=[out, out],
        out_shape=[jax.ShapeDtypeStruct((bsz, nmem, d), F32)] * 2,
        compiler_params=_cparams(("parallel",)),
        name="mem_kv",
    )(mem, g, wk, wv)


def _hgrn_kernel(q_ref, f_ref, i_ref, g_ref, lbp_ref, ng_ref, s0_ref, o_ref, sout_ref, st_ref,
                 *, chunk, n_chunks, heads):
    t = pl.program_id(1)

    @pl.when(t == 0)
    def _():
        for h in range(heads):
            st_ref[h] = s0_ref[h].T

    lbp = lbp_ref[...]
    e = jnp.exp(lbp - jnp.max(lbp, axis=0, keepdims=True))
    lb = e[0:1] / jnp.sum(e, axis=0, keepdims=True)
    ng = ng_ref[...]

    row = lax.broadcasted_iota(jnp.int32, (chunk, chunk), 0)
    col = lax.broadcasted_iota(jnp.int32, (chunk, chunk), 1)
    causal = row >= col
    mid = chunk // 2
    to_mid = jnp.where((col >= mid) & (col <= row), 1.0, 0.0) - jnp.where((col > row) & (col < mid), 1.0, 0.0)
    coef = jnp.concatenate([causal.astype(F32), to_mid, (col > row).astype(F32)], axis=0).astype(BF16)

    chunks = range(n_chunks)
    qd, kd, qi, kl, el, vv, gate = [], [], [], [], [], [], []
    for c in chunks:
        rows = slice(c * chunk, (c + 1) * chunk)
        f = lb + (1.0 - lb) * _sigmoid(f_ref[rows, :])
        g_hi, g_lo = _split(jnp.log(f))
        bb = _dot(coef, g_hi) + _dot(coef, g_lo)
        b, b_to_mid, b_to_last = bb[:chunk], bb[chunk:2 * chunk], bb[2 * chunk:]
        q = _silu(q_ref[rows, :])
        k = 1.0 - f
        qd.append((q * jnp.exp(b_to_mid)).astype(BF16))
        kd.append((k * jnp.exp(-b_to_mid)).astype(BF16))
        qi.append((q * jnp.exp(b)).astype(BF16))
        kl.append((k * jnp.exp(b_to_last)).astype(BF16))
        el.append(jnp.exp(b[chunk - 1:chunk, :]))
        vv.append(i_ref[rows, :].astype(BF16))
        gate.append(_silu(g_ref[rows, :]))

    head = lambda x, h: x[:, h * HGRN_HEAD:(h + 1) * HGRN_HEAD]
    items = [(c, h) for c in chunks for h in range(heads)]
    sc = {ch: jnp.where(causal, _dot_nt(head(qd[ch[0]], ch[1]), head(kd[ch[0]], ch[1])), 0.0).astype(BF16)
          for ch in items}
    intra = {ch: _dot(sc[ch], head(vv[ch[0]], ch[1])) for ch in items}
    upd = {ch: _dot_tn(head(vv[ch[0]], ch[1]), head(kl[ch[0]], ch[1])) for ch in items}
    for h in range(heads):
        st = st_ref[h]
        for c in chunks:
            o = intra[c, h] + _dot_nt(head(qi[c], h), st.astype(BF16))
            st = st * head(el[c], h) + upd[c, h]
            o = o * lax.rsqrt(jnp.mean(o * o, axis=-1, keepdims=True) + NORM_EPS)
            o_ref[c * chunk:(c + 1) * chunk, h * HGRN_HEAD:(h + 1) * HGRN_HEAD] = o * ng * head(gate[c], h)
        st_ref[h] = st

    @pl.when(t == pl.num_programs(1) - 1)
    def _():
        for h in range(heads):
            sout_ref[h] = st_ref[h].T


def _hgrn(p3, lbp, ng, s0, tile, chunk):
    bsz, tlen, _ = p3.shape
    heads, dk, dv = s0.shape[1:]
    d = heads * HGRN_HEAD
    kern = functools.partial(_hgrn_kernel, chunk=chunk, n_chunks=tile // chunk, heads=heads)
    pspec = lambda cb: pl.BlockSpec((None, tile, d), lambda b, t: (b, t, cb))
    return pl.pallas_call(
        kern,
        grid=(bsz, tlen // tile),
        in_specs=[
            pspec(0), pspec(1), pspec(2), pspec(3),
            pl.BlockSpec(lbp.shape, lambda b, t: (0, 0)),
            pl.BlockSpec((1, dv), lambda b, t: (0, 0)),
            pl.BlockSpec((None, heads, dk, dv), lambda b, t: (b, 0, 0, 0)),
        ],
        out_specs=[
            pl.BlockSpec((None, tile, d), lambda b, t: (b, t, 0)),
            pl.BlockSpec((None, heads, dk, dv), lambda b, t: (b, 0, 0, 0)),
        ],
        out_shape=[
            jax.ShapeDtypeStruct((bsz, tlen, d), F32),
            jax.ShapeDtypeStruct((bsz, heads, dk, dv), F32),
        ],
        scratch_shapes=[pltpu.VMEM((heads, dv, dk), F32)],
        compiler_params=_cparams(("parallel", "arbitrary")),
        name="hgrn",
    )(p3, p3, p3, p3, lbp, ng.reshape(1, dv), s0)


_P_MU_R, _P_MU_K, _P_MU_V, _P_W0, _P_A0, _P_KK, _P_KA, _P_RK, _P_LNG, _P_LNB = range(10)
_P_ROWS = 16


def _rwkv_kernel(r_ref, k_ref, v_ref, l_ref, br_ref, bk_ref, bv_ref, bl_ref, par_ref, mul_ref,
                 w2_ref, a2_ref, g2a_ref, g2b_ref, s0_ref,
                 o_ref, sout_ref, st_ref, prev_ref, prevl_ref, *, chunk, n_chunks, pairs):
    t = pl.program_id(1)
    tile = chunk * n_chunks

    @pl.when(t == 0)
    def _():
        st_ref[...] = s0_ref[...]
        prev_ref[0:1, :] = br_ref[...]
        prev_ref[1:2, :] = bk_ref[...]
        prev_ref[2:3, :] = bv_ref[...]
        prevl_ref[0:1, :] = bl_ref[...]

    par = par_ref[...]
    prow = lambda i: par[i:i + 1, :]

    def shifted(x, prev_row, mu):
        row_id = lax.broadcasted_iota(jnp.int32, x.shape, 0)
        prev = jnp.where(row_id == 0, prev_row, pltpu.roll(x, 1, 0))
        return x + (prev - x) * mu

    pr, pk, pv, plr = r_ref[...], k_ref[...], v_ref[...], l_ref[...]
    r = shifted(pr, prev_ref[0:1, :], prow(_P_MU_R))
    k = shifted(pk, prev_ref[1:2, :], prow(_P_MU_K))
    v = shifted(pv, prev_ref[2:3, :], prow(_P_MU_V))
    xl = shifted(plr, prevl_ref[0:1, :], mul_ref[...])
    for slot, x in enumerate((pr, pk, pv)):
        prev_ref[slot:slot + 1, :] = x[tile - 1:tile, :]
    prevl_ref[0:1, :] = plr[tile - 1:tile, :]
    x0 = xl[:, 0:LANES]
    x1 = xl[:, LANES:2 * LANES]
    x2 = xl[:, 2 * LANES:3 * LANES]

    w_lin = prow(_P_W0) + _dot(jnp.tanh(x0).astype(BF16), w2_ref[...])
    z = -w_lin
    softplus = jnp.maximum(z, 0.0) + jnp.log(1.0 + jnp.exp(-jnp.abs(z)))
    lw = -jnp.exp(-softplus - 0.5)
    a_icl = _sigmoid(prow(_P_A0) + _dot(x0.astype(BF16), a2_ref[...]))
    g_out = _dot(_sigmoid(x1).astype(BF16), g2a_ref[...]) + _dot(_sigmoid(x2).astype(BF16), g2b_ref[...])

    lane = lax.broadcasted_iota(jnp.int32, (1, LANES), 1)
    m0 = (lane < RWKV_HEAD).astype(F32)
    m1 = 1.0 - m0
    head_shift = RWKV_HEAD.bit_length() - 1
    hr = lax.broadcasted_iota(jnp.int32, (LANES, LANES), 0) >> head_shift
    hc = lax.broadcasted_iota(jnp.int32, (LANES, LANES), 1) >> head_shift
    head_ones = (hr == hc).astype(BF16)
    tile_of = lambda x, j: x[:, j * LANES:(j + 1) * LANES]

    def headsum(x):
        xs = jnp.concatenate([tile_of(x, j) for j in range(pairs)], axis=0)
        ys = _dot(xs.astype(BF16), head_ones)
        return jnp.concatenate([ys[j * tile:(j + 1) * tile] for j in range(pairs)], axis=1)

    kk = k * prow(_P_KK)
    kk = kk * lax.rsqrt(jnp.maximum(headsum(kk * kk), 1e-24))
    k_mod = k * (1.0 + (a_icl - 1.0) * prow(_P_KA))
    a = -kk
    b = kk * a_icl

    ti = lax.broadcasted_iota(jnp.int32, (chunk, chunk), 0)
    tj = lax.broadcasted_iota(jnp.int32, (chunk, chunk), 1)
    tri = (ti >= tj).astype(BF16)
    n2 = 2 * chunk
    si = lax.broadcasted_iota(jnp.int32, (n2, n2), 0)
    sj = lax.broadcasted_iota(jnp.int32, (n2, n2), 1)
    chunk_shift = chunk.bit_length() - 1
    same_head = (si >> chunk_shift) == (sj >> chunk_shift)
    strict = same_head & ((sj & (chunk - 1)) < (si & (chunk - 1)))
    incl = same_head & ((sj & (chunk - 1)) <= (si & (chunk - 1)))
    eye = (si == sj).astype(F32)
    m0_b, m1_b = m0.astype(BF16), m1.astype(BF16)
    tile_b = lambda x, j: tile_of(x, j).astype(BF16)
    twice = lambda x: jnp.concatenate([x, x], axis=0)
    every = range(pairs)

    def stack_b(x, j):
        xb = tile_b(x, j)
        return jnp.concatenate([xb * m0_b, xb * m1_b], axis=0)

    def state_free(rows):
        lw_c, a_c, b_c, k_c, r_c, v_c = lw[rows], a[rows], b[rows], k_mod[rows], r[rows], v[rows]
        lw_hi, lw_lo = _split(lw_c)
        cum = _dot(tri, lw_hi) + _dot(tri, lw_lo)
        cum_last = cum[chunk - 1:chunk, :]
        e_neg = jnp.exp(-cum)
        e_end = jnp.exp(cum_last - cum)
        a_t = a_c * jnp.exp(cum - lw_c)
        b_t = b_c * e_neg
        k_t = k_c * e_neg
        r_t = r_c * jnp.exp(cum)
        b_end = b_c * e_end
        k_end = k_c * e_end

        a_st = [stack_b(a_t, j) for j in every]
        r_st = [stack_b(r_t, j) for j in every]
        v_st = [stack_b(v_c, j) for j in every]
        bend_st = [stack_b(b_end, j) for j in every]
        kend_st = [stack_b(k_end, j) for j in every]
        sc = [_dot_nt(jnp.concatenate([tile_b(a_t, j), tile_b(r_t, j)], axis=0),
                      jnp.concatenate([stack_b(b_t, j), stack_b(k_t, j)], axis=0)) for j in every]
        a_ab = [jnp.where(strict, twice(s[:chunk, :n2]), 0.0) for s in sc]
        a_xk = [jnp.concatenate([jnp.where(strict, twice(s[:chunk, n2:]), 0.0),
                                 jnp.where(incl, twice(s[chunk:, n2:]), 0.0)], axis=0).astype(BF16) for s in sc]
        a_rb = [jnp.where(incl, twice(s[chunk:, :n2]), 0.0).astype(BF16) for s in sc]

        inv = [eye + x for x in a_ab]
        pw = [_dot(x.astype(BF16), x.astype(BF16)) for x in a_ab]
        for _ in range(int(math.log2(chunk)) - 2):
            if n2 % LANES == 0:
                both = [_dot(p.astype(BF16), jnp.concatenate([p, i], axis=1).astype(BF16)) for p, i in zip(pw, inv)]
                pw_next = [x[:, :n2] for x in both]
                inv = [i + x[:, n2:] for i, x in zip(inv, both)]
            else:
                pw_next = [_dot(p.astype(BF16), p.astype(BF16)) for p in pw]
                inv = [i + _dot(p.astype(BF16), i.astype(BF16)) for i, p in zip(inv, pw)]
            pw = pw_next
        inv = [i + _dot(p.astype(BF16), i.astype(BF16)) for i, p in zip(inv, pw)]

        av = [_dot(a_xk[j], v_st[j]) for j in every]
        wu = [_dot(inv[j].astype(BF16), jnp.concatenate([a_st[j], av[j][:n2].astype(BF16)], axis=1)).astype(BF16)
              for j in every]
        qo = [_dot(a_rb[j], wu[j]) + jnp.concatenate([r_st[j].astype(F32), av[j][n2:]], axis=1) for j in every]
        mn = [_dot_tn(wu[j], bend_st[j]) for j in every]
        nn = [mn[j][LANES:] + _dot_tn(v_st[j], kend_st[j]) for j in every]
        return qo, mn, nn, jnp.exp(cum_last)

    parts = [state_free(slice(c * chunk, (c + 1) * chunk)) for c in range(n_chunks)]
    outs = []
    for qo, mn, nn, g_last in parts:
        row = []
        for j in every:
            s0 = st_ref[j]
            s0_b = s0.astype(BF16)
            q = (qo[j][:chunk, :LANES] + qo[j][chunk:, :LANES]).astype(BF16)
            row.append(_dot_nt(q, s0_b) + qo[j][:chunk, LANES:] + qo[j][chunk:, LANES:])
            st_ref[j] = s0 * tile_of(g_last, j) + _dot(s0_b, mn[j][:LANES].astype(BF16)) + nn[j]
        outs.append(jnp.concatenate(row, axis=1))
    o = outs[0] if n_chunks == 1 else jnp.concatenate(outs, axis=0)

    inv_n = 1.0 / RWKV_HEAD
    mean = headsum(o) * inv_n
    dlt = o - mean
    var = headsum(dlt * dlt) * inv_n
    o = dlt * lax.rsqrt(var + RWKV_GN_EPS) * prow(_P_LNG) + prow(_P_LNB)
    bonus = headsum(r * k_mod * prow(_P_RK))
    o_ref[...] = (o + bonus * v) * g_out

    @pl.when(t == pl.num_programs(1) - 1)
    def _():
        sout_ref[...] = st_ref[...]


def _rwkv(p3, buf_rkv, buf_l, par, mul, w2p, a2p, g2a, g2b, s0bd, d, tile, chunk):
    bsz, tlen, _ = p3.shape
    pairs = d // LANES
    nl = 3 * LANES
    kern = functools.partial(_rwkv_kernel, chunk=chunk, n_chunks=tile // chunk, pairs=pairs)
    pcol = lambda cb: pl.BlockSpec((None, tile, d), lambda b, t: (b, t, cb))
    bcol = lambda cb: pl.BlockSpec((None, 1, d), lambda b, t: (b, 0, cb))
    full = lambda a: pl.BlockSpec(a.shape, lambda b, t: (0,) * a.ndim)
    state = pl.BlockSpec((None, pairs, LANES, LANES), lambda b, t: (b, 0, 0, 0))
    return pl.pallas_call(
        kern,
        grid=(bsz, tlen // tile),
        in_specs=[
            pcol(4), pcol(5), pcol(6),
            pl.BlockSpec((None, tile, nl), lambda b, t: (b, t, 9 * d // nl)),
            bcol(0), bcol(1), bcol(2),
            pl.BlockSpec((None, 1, nl), lambda b, t: (b, 0, 0)),
            full(par), full(mul), full(w2p), full(a2p), full(g2a), full(g2b), state,
        ],
        out_specs=[pl.BlockSpec((None, tile, d), lambda b, t: (b, t, 0)), state],
        out_shape=[
            jax.ShapeDtypeStruct((bsz, tlen, d), F32),
            jax.ShapeDtypeStruct((bsz, pairs, LANES, LANES), F32),
        ],
        scratch_shapes=[pltpu.VMEM((pairs, LANES, LANES), F32), pltpu.VMEM((8, d), F32),
                        pltpu.VMEM((8, nl), F32)],
        compiler_params=_cparams(("parallel", "arbitrary")),
        name="rwkv",
    )(p3, p3, p3, p3, buf_rkv, buf_rkv, buf_rkv, buf_l, par, mul, w2p, a2p, g2a, g2b, s0bd)


def _attn_kernel(x_ref, oh_ref, or_ref, ga_ref, gb_ref, mk_ref, mv_ref, wout_ref, nxg_ref, wq_ref, wo_ref,
                 nfg_ref, wrh_ref, wrl_ref, rb_ref, h_ref, u_ref, wf_ref, att_ref, *, bt, tt):
    n = bt * tt
    d = x_ref.shape[-1]
    flat = lambda ref: ref[...].reshape(n, d)
    merged = _sigmoid(flat(ga_ref)) * flat(oh_ref) + _sigmoid(flat(gb_ref)) * flat(or_ref)
    h1 = flat(x_ref) + _dot(merged.astype(BF16), wout_ref[...])

    q = _dot(_rmsnorm(h1, nxg_ref[...]).astype(BF16), wq_ref[...])
    dh = d // X_HEADS
    scale = dh ** -0.5
    for bi in range(bt):
        rows = slice(bi * tt, (bi + 1) * tt)
        if len(mk_ref.shape) == 3:
            mem_heads = lambda ref: [ref[bi, :, hh * dh:(hh + 1) * dh].astype(BF16) for hh in range(X_HEADS)]
        else:
            def mem_heads(ref):
                by_head = jnp.transpose(ref[bi], (1, 0, 2)).astype(BF16)
                return [by_head[hh] for hh in range(X_HEADS)]
        k_heads, v_heads = mem_heads(mk_ref), mem_heads(mv_ref)
        for hh in range(X_HEADS):
            cols = slice(hh * dh, (hh + 1) * dh)
            s = _dot_nt(q[rows, cols].astype(BF16), k_heads[hh]) * scale
            p = jnp.exp(s - jnp.max(s, axis=-1, keepdims=True))
            p = p / jnp.sum(p, axis=-1, keepdims=True)
            att_ref[rows, cols] = _dot(p.astype(BF16), v_heads[hh]).astype(BF16)
    h2 = h1 + _dot(att_ref[...], wo_ref[...])
    h_ref[...] = h2.reshape(bt, tt, d)

    u = _rmsnorm(h2, nfg_ref[...])
    u_ref[...] = u.astype(BF16).reshape(bt, tt, d)
    u_hi, u_lo = _split(u)
    logits = (_dot(u_hi, wrh_ref[...]) + _dot(u_hi, wrl_ref[...]) + _dot(u_lo, wrh_ref[...])) + rb_ref[...]
    lane = lax.broadcasted_iota(jnp.int32, (n, LANES), 1)
    lane_f = lane.astype(F32)
    is_group = (lane >= N_EXPERTS) & (lane < N_EXPERTS + N_GROUPS)
    lg = jnp.where(is_group, logits, NEG_BIG)
    g_max = jnp.max(lg, axis=-1, keepdims=True)
    g_idx = jnp.min(jnp.where(lg == g_max, lane_f, 1e9), axis=-1, keepdims=True) - N_EXPERTS
    pg = 1.0 / jnp.sum(jnp.exp(lg - g_max), axis=-1, keepdims=True)
    in_group = (lane < N_EXPERTS) & ((lane >> (EXPERTS_PER_GROUP.bit_length() - 1)).astype(F32) == g_idx)
    le = jnp.where(in_group, logits, NEG_BIG)
    m1 = jnp.max(le, axis=-1, keepdims=True)
    i1 = jnp.min(jnp.where(le == m1, lane_f, 1e9), axis=-1, keepdims=True)
    le2 = jnp.where(lane_f == i1, NEG_BIG, le)
    m2 = jnp.max(le2, axis=-1, keepdims=True)
    i2 = jnp.min(jnp.where(le2 == m2, lane_f, 1e9), axis=-1, keepdims=True)
    e2 = jnp.exp(m2 - m1)
    w1 = pg / (1.0 + e2)
    wf = jnp.where(lane_f == i1, w1, 0.0) + jnp.where(lane_f == i2, w1 * e2, 0.0)
    wf = wf + jnp.where(lane == _E1_LANE, i1, 0.0) + jnp.where(lane == _E2_LANE, i2, 0.0)
    wf_ref[...] = wf.reshape(bt, tt, LANES)


def _attn(x3, oh3, or3, p3, mk, mv, wout, nxg, wq, wo, nfg, wrh, wrl, rb, bt, tt, gate_block0):
    bsz, tlen, d = x3.shape
    kern = functools.partial(_attn_kernel, bt=bt, tt=tt)
    tok = lambda cb: pl.BlockSpec((bt, tt, d), lambda b, t: (b, t, cb))
    mem = pl.BlockSpec((bt,) + mk.shape[1:], lambda b, t: (b,) + (0,) * (mk.ndim - 1))
    full = lambda a: pl.BlockSpec(a.shape, lambda b, t: (0,) * a.ndim, pipeline_mode=pl.Buffered(1))
    nxg, nfg = nxg.reshape(1, d), nfg.reshape(1, d)
    return pl.pallas_call(
        kern,
        grid=(bsz // bt, tlen // tt),
        in_specs=[tok(0), tok(0), tok(0), tok(gate_block0), tok(gate_block0 + 1), mem, mem,
                  full(wout), full(nxg), full(wq), full(wo), full(nfg), full(wrh), full(wrl), full(rb)],
        out_specs=[
            pl.BlockSpec((bt, tt, d), lambda b, t: (b, t, 0)),
            pl.BlockSpec((bt, tt, d), lambda b, t: (b, t, 0)),
            pl.BlockSpec((bt, tt, LANES), lambda b, t: (b, t, 0)),
        ],
        out_shape=[
            jax.ShapeDtypeStruct((bsz, tlen, d), F32),
            jax.ShapeDtypeStruct((bsz, tlen, d), BF16),
            jax.ShapeDtypeStruct((bsz, tlen, LANES), F32),
        ],
        scratch_shapes=[pltpu.VMEM((bt * tt, d), BF16)],
        compiler_params=_cparams(("parallel", "parallel")),
        name="attn",
    )(x3, oh3, or3, p3, p3, mk, mv, wout, nxg, wq, wo, nfg, wrh, wrl, rb)


def _segment_dma(src, dst, src_row, dst_row, length, max_len, sem, start):
    size = _SEG
    while size * 2 <= max_len:
        size *= 2
    while size >= _SEG:
        offset = length & (-2 * size)

        @pl.when((length & size) != 0)
        def _(offset=offset, size=size):
            copy = pltpu.make_async_copy(src.at[pl.ds(pl.multiple_of(src_row + offset, _SEG), size)],
                                         dst.at[pl.ds(pl.multiple_of(dst_row + offset, _SEG), size)], sem)
            if start:
                copy.start()
            else:
                copy.wait()

        size //= 2


def _tile_meta(meta_ref, nt, tile):
    field = lambda k: [meta_ref[(k * nt + tile) * N_GROUPS + g] for g in range(N_GROUPS)]
    return field(0), field(1), field(2)


def _group_of(expert_id):
    return jnp.floor(expert_id * (1.0 / EXPERTS_PER_GROUP))


def _moe_sort_kernel(meta_ref, u_ref, wf_ref, xs_ref, ws_ref, su_ref, sw_ref, zu_ref, zw_ref, sem, *, nt, tm, te):
    i = pl.program_id(0)
    buf = lax.rem(i, 2)
    slots = su_ref.shape[1]
    _, _, base = _tile_meta(meta_ref, nt, i)
    wf = wf_ref[...]
    gid = _group_of(wf.T[_E1_LANE:_E1_LANE + 1, :])
    onehot = lax.broadcasted_iota(jnp.int32, (8, tm), 0).astype(F32) == gid
    earlier = (lax.broadcasted_iota(jnp.int32, (tm, tm), 0)
               < lax.broadcasted_iota(jnp.int32, (tm, tm), 1)).astype(BF16)
    seen = _dot(onehot.astype(BF16), earlier)
    pos = jnp.sum(jnp.where(onehot, seen, 0.0), axis=0, keepdims=True)
    for g in range(N_GROUPS):
        pos = pos + jnp.where(gid == float(g), base[g].astype(F32), 0.0)
    perm = (lax.broadcasted_iota(jnp.int32, (slots, tm), 0).astype(F32) == pos).astype(BF16)
    su_ref[buf] = _dot(perm, u_ref[...]).astype(BF16)
    w_hi, w_lo = _split(wf)
    sw_ref[buf] = _dot(perm, w_hi) + _dot(perm, w_lo)

    def copies(tile, b, start):
        off, cnt, base = _tile_meta(meta_ref, nt, tile)
        for g in range(N_GROUPS):
            _segment_dma(su_ref.at[b], xs_ref, base[g], off[g], cnt[g], tm, sem.at[0, b], start)
            _segment_dma(sw_ref.at[b], ws_ref, base[g], off[g], cnt[g], tm, sem.at[1, b], start)

    copies(i, buf, True)

    @pl.when(i > 0)
    def _():
        copies(i - 1, 1 - buf, False)

    @pl.when(i == nt - 1)
    def _():
        copies(i, buf, False)
        zu_ref[...] = jnp.zeros_like(zu_ref)
        zw_ref[...] = jnp.zeros_like(zw_ref)
        tail = 3 * nt * N_GROUPS + nt
        for start in (True, False):
            for g in range(N_GROUPS):
                off, cnt = meta_ref[tail + g], meta_ref[tail + N_GROUPS + g]
                _segment_dma(zu_ref, xs_ref, 0, off, cnt, te, sem.at[0, 0], start)
                _segment_dma(zw_ref, ws_ref, 0, off, cnt, te, sem.at[1, 0], start)

            def unused(r, carry, start=start):
                row = pl.multiple_of(r * te, te)
                for src, dst, s in ((zu_ref, xs_ref, sem.at[0, 1]), (zw_ref, ws_ref, sem.at[1, 1])):
                    copy = pltpu.make_async_copy(src, dst.at[pl.ds(row, te)], s)
                    if start:
                        copy.start()
                    else:
                        copy.wait()
                return carry

            lax.fori_loop(meta_ref[tail + 2 * N_GROUPS], xs_ref.shape[0] // te, unused, 0)


def _moe_expert_kernel(meta_ref, xs_ref, ws_ref, wg_ref, wu_ref, wd_ref, ys_ref, *, n_tiles):
    r = pl.program_id(0)
    used = r < meta_ref[n_tiles]

    @pl.when(used)
    def _():
        first = meta_ref[r] * EXPERTS_PER_GROUP
        x = xs_ref[...]
        ws = ws_ref[...]
        lane = lax.broadcasted_iota(jnp.int32, ws.shape, 1)
        hid = []
        for e in range(EXPERTS_PER_GROUP):
            h = _silu(_dot(x, wg_ref[e])) * _dot(x, wu_ref[e])
            w_col = jnp.sum(jnp.where(lane == first + e, ws, 0.0), axis=-1, keepdims=True)
            hid.append((h * w_col).astype(BF16))
        ys_ref[...] = _dot(jnp.concatenate(hid, axis=1), wd_ref[...]).astype(ys_ref.dtype)

    @pl.when(jnp.logical_not(used))
    def _():
        ys_ref[...] = jnp.zeros_like(ys_ref)


def _moe_unsort_kernel(meta_ref, wf_ref, h_ref, gf_ref, ys_ref, y_ref, sy_ref, sem, *, nt, tm):
    i = pl.program_id(0)
    buf = lax.rem(i, 2)
    slots = sy_ref.shape[1]

    def copies(tile, b, start):
        off, cnt, base = _tile_meta(meta_ref, nt, tile)
        for g in range(N_GROUPS):
            _segment_dma(ys_ref, sy_ref.at[b], off[g], base[g], cnt[g], tm, sem.at[b], start)

    @pl.when(i == 0)
    def _():
        sy_ref[...] = jnp.zeros_like(sy_ref)
        copies(0, 0, True)

    @pl.when(i + 1 < nt)
    def _():
        copies(i + 1, 1 - buf, True)

    _, _, base = _tile_meta(meta_ref, nt, i)
    gid = _group_of(wf_ref[...][:, _E1_LANE:_E1_LANE + 1])
    onehot = lax.broadcasted_iota(jnp.int32, (tm, LANES), 1).astype(F32) == gid
    earlier = (lax.broadcasted_iota(jnp.int32, (tm, tm), 1)
               < lax.broadcasted_iota(jnp.int32, (tm, tm), 0)).astype(BF16)
    seen = _dot(earlier, onehot.astype(BF16))
    pos = jnp.sum(jnp.where(onehot, seen, 0.0), axis=1, keepdims=True)
    for g in range(N_GROUPS):
        pos = pos + jnp.where(gid == float(g), base[g].astype(F32), 0.0)
    perm_t = (lax.broadcasted_iota(jnp.int32, (tm, slots), 1).astype(F32) == pos).astype(BF16)
    copies(i, buf, False)
    slot_row = lax.broadcasted_iota(jnp.int32, (slots, 1), 0)
    ys = sy_ref[buf]
    moe = _dot(perm_t, jnp.where(slot_row < meta_ref[3 * nt * N_GROUPS + i], ys, jnp.zeros_like(ys)))
    y_ref[...] = _rmsnorm(h_ref[...] + moe, gf_ref[...])


def _moe(u, wf, h, wg, wu, wd, gf, tm, te):
    n, d = u.shape
    ne, _, f = wg.shape
    nt = n // tm
    slots = tm + LANES
    n_tiles = -(-(n + nt * N_GROUPS * (_SEG - 1) + N_GROUPS * (te - 1)) // te)
    rows = n_tiles * te
    i32 = jnp.int32

    gid = (wf[:, _E1_LANE].astype(i32) // EXPERTS_PER_GROUP).reshape(nt, tm)
    cnt = jnp.sum((gid[:, :, None] == jnp.arange(N_GROUPS, dtype=i32)).astype(i32), axis=1)
    cnt = (cnt + _SEG - 1) // _SEG * _SEG
    base = jnp.cumsum(cnt, axis=1) - cnt
    total = jnp.sum(cnt, axis=0)
    region = (total + te - 1) // te * te
    region_end = jnp.cumsum(region)
    region_start = region_end - region
    off = region_start[None, :] + jnp.cumsum(cnt, axis=0) - cnt
    n_used = region_end[-1:] // te
    meta = jnp.concatenate([off.reshape(-1), cnt.reshape(-1), base.reshape(-1), jnp.sum(cnt, axis=1),
                            region_start + total, region - total, n_used]).astype(i32)
    tile_row = jnp.arange(n_tiles, dtype=i32) * te
    tile_group = jnp.minimum(jnp.sum((tile_row[:, None] >= region_end[None, :]).astype(i32), axis=1), N_GROUPS - 1)
    emeta = jnp.concatenate([tile_group, n_used]).astype(i32)

    anyspec = pl.BlockSpec(memory_space=pl.ANY)
    tok = lambda w: pl.BlockSpec((tm, w), lambda i, m: (i, 0))
    xs, ws = pl.pallas_call(
        functools.partial(_moe_sort_kernel, nt=nt, tm=tm, te=te),
        grid_spec=pltpu.PrefetchScalarGridSpec(
            num_scalar_prefetch=1, grid=(nt,),
            in_specs=[tok(d), tok(LANES)],
            out_specs=[anyspec, anyspec],
            scratch_shapes=[pltpu.VMEM((2, slots, d), BF16), pltpu.VMEM((2, slots, LANES), F32),
                            pltpu.VMEM((te, d), BF16), pltpu.VMEM((te, LANES), F32),
                            pltpu.SemaphoreType.DMA((2, 2))]),
        out_shape=[jax.ShapeDtypeStruct((rows, d), BF16), jax.ShapeDtypeStruct((rows, LANES), F32)],
        compiler_params=_cparams(("arbitrary",)),
        name="moe_sort",
    )(meta, u, wf)

    srt = lambda w: pl.BlockSpec((te, w), lambda r, m: (r, 0))
    grp_w = pl.BlockSpec((EXPERTS_PER_GROUP, d, f), lambda r, m: (m[r], 0, 0))
    ys = pl.pallas_call(
        functools.partial(_moe_expert_kernel, n_tiles=n_tiles),
        grid_spec=pltpu.PrefetchScalarGridSpec(
            num_scalar_prefetch=1, grid=(n_tiles,),
            in_specs=[srt(d), srt(LANES), grp_w, grp_w,
                      pl.BlockSpec((EXPERTS_PER_GROUP * f, d), lambda r, m: (m[r], 0))],
            out_specs=srt(d)),
        out_shape=jax.ShapeDtypeStruct((rows, d), BF16),
        compiler_params=_cparams(("parallel",)),
        name="moe_expert",
    )(emeta, xs, ws, wg, wu, wd.reshape(ne * f, d))

    return pl.pallas_call(
        functools.partial(_moe_unsort_kernel, nt=nt, tm=tm),
        grid_spec=pltpu.PrefetchScalarGridSpec(
            num_scalar_prefetch=1, grid=(nt,),
            in_specs=[tok(LANES), tok(d), pl.BlockSpec((1, d), lambda i, m: (0, 0)), anyspec],
            out_specs=tok(d),
            scratch_shapes=[pltpu.VMEM((2, slots, d), BF16), pltpu.SemaphoreType.DMA((2,))]),
        out_shape=jax.ShapeDtypeStruct((n, d), F32),
        compiler_params=_cparams(("arbitrary",)),
        name="moe_unsort",
    )(meta, wf, h, gf.reshape(1, d), ys)


def _pick(n, pref):
    t = min(n, pref)
    while n % t:
        t -= 8
    return t


def _block_diag_pairs(s):
    b, h, n, _ = s.shape
    s = s.reshape(b, h // 2, 2, n, n)
    z = jnp.zeros_like(s[:, :, 0])
    top = jnp.concatenate([s[:, :, 0], z], axis=-1)
    bot = jnp.concatenate([z, s[:, :, 1]], axis=-1)
    return jnp.concatenate([top, bot], axis=-2)


def _pairs_to_heads(sbd):
    b, p, n2, _ = sbd.shape
    n = n2 // 2
    s = jnp.stack([sbd[:, :, :n, :n], sbd[:, :, n:, n:]], axis=2)
    return s.reshape(b, 2 * p, n, n)


def kernel(x_prompt, x_sample, mem_prompt, cache_mem_k, cache_mem_v, state_hgrn, state_rwkv, state_rwkv_shift, norm_mix_g, w_in, hgrn_lower_bounds, hgrn_norm_g, rwkv_mu, rwkv_w0, rwkv_w2, rwkv_a0, rwkv_a2, rwkv_g2, rwkv_k_k, rwkv_k_a, rwkv_r_k, rwkv_ln_g, rwkv_ln_b, w_out, norm_x_g, norm_mem_g, wq_x, wk_x, wv_x, wo_x, norm_ffn_g, router_group_w, router_group_b, router_expert_w, router_expert_b, expert_w_gate, expert_w_up, expert_w_down, norm_final_g):
    assert w_in.shape[0] == 1, "single-layer configuration"
    d = x_prompt.shape[-1]
    n_lora_w, n_lora_a, n_lora_g = rwkv_w2.shape[1], rwkv_a2.shape[1], rwkv_g2.shape[1]
    assert n_lora_w + n_lora_a == LANES and LANES < n_lora_g <= 2 * LANES
    n_lora = n_lora_w + n_lora_a + n_lora_g
    n_rwkv_in = 3 * d + n_lora
    lora_pad = 3 * LANES - n_lora

    pr0 = 4 * d
    gate0 = pr0 + n_rwkv_in
    n_cols = w_in.shape[2] + lora_pad
    w_perm = _permute_columns(w_in, [(0, pr0 + 3 * d), (gate0, 2 * d), (pr0 + 3 * d, n_lora)], n_cols,
                              _pick(d, 128))
    lora_col0 = 9 * d
    gate_block0 = 7
    mu = rwkv_mu[0]
    mul = jnp.pad(mu[3 * d:], (0, lora_pad)).reshape(1, 3 * LANES)
    par = jnp.stack([mu[:d], mu[d:2 * d], mu[2 * d:3 * d], rwkv_w0[0], rwkv_a0[0], rwkv_k_k[0], rwkv_k_a[0],
                     rwkv_r_k[0].reshape(d), rwkv_ln_g[0], rwkv_ln_b[0]])
    par = jnp.pad(par, ((0, _P_ROWS - par.shape[0]), (0, 0)))
    w2p = jnp.pad(rwkv_w2[0], ((0, n_lora_a), (0, 0))).astype(BF16)
    a2p = jnp.pad(rwkv_a2[0], ((n_lora_w, 0), (0, 0))).astype(BF16)
    g2a = rwkv_g2[0][:LANES].astype(BF16)
    g2b = jnp.pad(rwkv_g2[0][LANES:], ((0, 2 * LANES - n_lora_g), (0, 0))).astype(BF16)
    wout_b, wq_b, wo_b = w_out[0].astype(BF16), wq_x[0].astype(BF16), wo_x[0].astype(BF16)
    wr = jnp.concatenate([router_expert_w[0], router_group_w[0],
                          jnp.zeros((d, LANES - N_EXPERTS - N_GROUPS), F32)], axis=1)
    wr_hi, wr_lo = _split(wr)
    rb = jnp.pad(jnp.concatenate([router_expert_b[0].reshape(-1), router_group_b[0]]),
                 (0, LANES - N_EXPERTS - N_GROUPS)).reshape(1, LANES)
    wg_b, wu_b, wd_b = (expert_w_gate[0].astype(BF16), expert_w_up[0].astype(BF16),
                        expert_w_down[0].astype(BF16))

    nb_p = mem_prompt.shape[0]
    mem_k_p, mem_v_p = _mem_kv(mem_prompt, norm_mem_g[0], wk_x[0].astype(BF16), wv_x[0].astype(BF16))

    def group(x3, mem_k, mem_v, s_h, s_r, buf, tile_h, tile_r, chunk_r, bt, tt, te):
        bsz, tlen, _ = x3.shape
        n = bsz * tlen
        p = _norm_matmul(x3.reshape(n, d), norm_mix_g[0], w_perm, _pick(n, 1024), n_cols // 3)
        p3 = p.reshape(bsz, tlen, n_cols)
        o_h, s_h_new = _hgrn(p3, hgrn_lower_bounds, hgrn_norm_g[0], s_h, tile_h, 32)
        buf_rkv = buf[:, :, :3 * d]
        buf_l = jnp.pad(buf[:, :, 3 * d:], ((0, 0), (0, 0), (0, lora_pad)))
        o_r, s_r_bd = _rwkv(p3, buf_rkv, buf_l, par, mul, w2p, a2p, g2a, g2b, _block_diag_pairs(s_r), d, tile_r, chunk_r)
        h2, u, wf = _attn(x3, o_h, o_r, p3, mem_k, mem_v, wout_b, norm_x_g[0], wq_b, wo_b, norm_ffn_g[0],
                          wr_hi, wr_lo, rb, bt, tt, gate_block0)
        y = _moe(u.reshape(n, d), wf.reshape(n, LANES), h2.reshape(n, d), wg_b, wu_b, wd_b, norm_final_g,
                 _pick(n, 512), te)
        shift = jnp.concatenate([p3[:, -1:, pr0:pr0 + 3 * d], p3[:, -1:, lora_col0:lora_col0 + n_lora]], axis=-1)
        return y.reshape(bsz, tlen, d), s_h_new, _pairs_to_heads(s_r_bd), shift

    nb_s, t_s, _ = x_sample.shape
    t_p = x_prompt.shape[1]
    zeros_like_state = lambda s: jnp.zeros((nb_p,) + s.shape[2:], s.dtype)
    y_p, sh_p, sr_p, bf_p = group(
        x_prompt, mem_k_p, mem_v_p, zeros_like_state(state_hgrn), zeros_like_state(state_rwkv),
        zeros_like_state(state_rwkv_shift), _pick(t_p, 512), _pick(t_p, 256), 64, 1, _pick(t_p, 512), 512)
    y_s, sh_s, sr_s, bf_s = group(
        x_sample, cache_mem_k[0], cache_mem_v[0],
        state_hgrn[0], state_rwkv[0], state_rwkv_shift[0], t_s, t_s, t_s, _pick(nb_s, 4), t_s, 256)

    kv_shape = (1,) + mem_k_p.shape[:2] + cache_mem_k.shape[3:]
    return (y_p, y_s, mem_k_p.reshape(kv_shape), mem_v_p.reshape(kv_shape), sh_p[None], sh_s[None],
            sr_p[None], sr_s[None], bf_p[None], bf_s[None])
```

```python
import functools
import math

import jax
import jax.numpy as jnp
from jax import lax
from jax.experimental import pallas as pl
from jax.experimental.pallas import tpu as pltpu

F32 = jnp.float32
BF16 = jnp.bfloat16

NORM_EPS = 1e-6
RWKV_GN_EPS = 64e-5
LANES = 128
HGRN_HEAD = 128
RWKV_HEAD = 64
X_HEADS = 4
N_GROUPS = 4
EXPERTS_PER_GROUP = 8
N_EXPERTS = N_GROUPS * EXPERTS_PER_GROUP
NEG_BIG = -1e30
VMEM_LIMIT = 52 * 1024 * 1024
_GID_LANE = N_EXPERTS
_SEG = 16


def _dot(a, b):
    return jnp.dot(a, b, preferred_element_type=F32)


def _dot_nt(a, b):
    return lax.dot_general(a, b, (((1,), (1,)), ((), ())), preferred_element_type=F32)


def _dot_tn(a, b):
    return lax.dot_general(a, b, (((0,), (0,)), ((), ())), preferred_element_type=F32)


def _split(x):
    hi = x.astype(BF16)
    lo = (x - hi.astype(F32)).astype(BF16)
    return hi, lo


def _sigmoid(x):
    return 0.5 * jnp.tanh(0.5 * x) + 0.5


def _silu(x):
    return x * _sigmoid(x)


def _rmsnorm(x, g):
    return x * lax.rsqrt(jnp.mean(x * x, axis=-1, keepdims=True) + NORM_EPS) * g


def _cparams(sem):
    return pltpu.CompilerParams(dimension_semantics=sem, vmem_limit_bytes=VMEM_LIMIT)


def _norm_matmul_kernel(x_ref, g_ref, w_ref, o_ref, u_ref):
    @pl.when(pl.program_id(1) == 0)
    def _():
        u_ref[...] = _rmsnorm(x_ref[...], g_ref[...]).astype(BF16)

    o_ref[...] = _dot(u_ref[...], w_ref[...])


def _norm_matmul(x, g, w, tm, tn):
    n, d = x.shape
    m = w.shape[1]
    return pl.pallas_call(
        _norm_matmul_kernel,
        grid=(n // tm, m // tn),
        in_specs=[
            pl.BlockSpec((tm, d), lambda i, j: (i, 0)),
            pl.BlockSpec((1, d), lambda i, j: (0, 0)),
            pl.BlockSpec((d, tn), lambda i, j: (0, j)),
        ],
        out_specs=pl.BlockSpec((tm, tn), lambda i, j: (i, j)),
        out_shape=jax.ShapeDtypeStruct((n, m), F32),
        scratch_shapes=[pltpu.VMEM((tm, d), BF16)],
        compiler_params=_cparams(("parallel", "arbitrary")),
        name="norm_matmul",
    )(x, g.reshape(1, d), w)


def _permute_columns_kernel(w_ref, o_ref, *, pieces):
    col = 0
    for src, width in pieces:
        o_ref[:, col:col + width] = w_ref[:, src:src + width].astype(o_ref.dtype)
        col += width
    if col < o_ref.shape[1]:
        o_ref[:, col:] = jnp.zeros((o_ref.shape[0], o_ref.shape[1] - col), o_ref.dtype)


def _permute_columns(w, pieces, n_cols, tr):
    _, rows, cols = w.shape
    return pl.pallas_call(
        functools.partial(_permute_columns_kernel, pieces=pieces),
        grid=(rows // tr,),
        in_specs=[pl.BlockSpec((None, tr, cols), lambda i: (0, i, 0))],
        out_specs=pl.BlockSpec((tr, n_cols), lambda i: (i, 0)),
        out_shape=jax.ShapeDtypeStruct((rows, n_cols), BF16),
        compiler_params=_cparams(("parallel",)),
        name="permute_columns",
    )(w)


def _mem_kv_kernel(m_ref, g_ref, wk_ref, wv_ref, k_ref, v_ref):
    u = _rmsnorm(m_ref[...], g_ref[...]).astype(BF16)
    k_ref[...] = _dot(u, wk_ref[...])
    v_ref[...] = _dot(u, wv_ref[...])


def _mem_kv(mem, g, wk, wv):
    bsz, nmem, d = mem.shape
    full = lambda a: pl.BlockSpec(a.shape, lambda b: (0,) * a.ndim)
    out = pl.BlockSpec((None, nmem, d), lambda b: (b, 0, 0))
    g = g.reshape(1, d)
    return pl.pallas_call(
        _mem_kv_kernel,
        grid=(bsz,),
        in_specs=[pl.BlockSpec((None, nmem, d), lambda b: (b, 0, 0)), full(g), full(wk), full(wv)],
        out_specs=[out, out],
        out_shape=[jax.ShapeDtypeStruct((bsz, nmem, d), F32)] * 2,
        compiler_params=_cparams(("parallel",)),
        name="mem_kv",
    )(mem, g, wk, wv)


def _hgrn_kernel(q_ref, f_ref, i_ref, g_ref, lbp_ref, ng_ref, s0_ref, o_ref, sout_ref, st_ref,
                 *, chunk, n_chunks, heads):
    t = pl.program_id(1)

    @pl.when(t == 0)
    def _():
        for h in range(heads):
            st_ref[h] = s0_ref[h].T

    lbp = lbp_ref[...]
    e = jnp.exp(lbp - jnp.max(lbp, axis=0, keepdims=True))
    lb = e[0:1] / jnp.sum(e, axis=0, keepdims=True)
    ng = ng_ref[...]

    row = lax.broadcasted_iota(jnp.int32, (chunk, chunk), 0)
    col = lax.broadcasted_iota(jnp.int32, (chunk, chunk), 1)
    causal = row >= col
    mid = chunk // 2
    to_mid = jnp.where((col >= mid) & (col <= row), 1.0, 0.0) - jnp.where((col > row) & (col < mid), 1.0, 0.0)
    coef = jnp.concatenate([causal.astype(F32), to_mid, (col > row).astype(F32)], axis=0).astype(BF16)

    chunks = range(n_chunks)
    qd, kd, qi, kl, el, vv, gate = [], [], [], [], [], [], []
    for c in chunks:
        rows = slice(c * chunk, (c + 1) * chunk)
        f = lb + (1.0 - lb) * _sigmoid(f_ref[rows, :])
        g_hi, g_lo = _split(jnp.log(f))
        bb = _dot(coef, g_hi) + _dot(coef, g_lo)
        b, b_to_mid, b_to_last = bb[:chunk], bb[chunk:2 * chunk], bb[2 * chunk:]
        q = _silu(q_ref[rows, :])
        k = 1.0 - f
        qd.append((q * jnp.exp(b_to_mid)).astype(BF16))
        kd.append((k * jnp.exp(-b_to_mid)).astype(BF16))
        qi.append((q * jnp.exp(b)).astype(BF16))
        kl.append((k * jnp.exp(b_to_last)).astype(BF16))
        el.append(jnp.exp(b[chunk - 1:chunk, :]))
        vv.append(i_ref[rows, :].astype(BF16))
        gate.append(_silu(g_ref[rows, :]))

    head = lambda x, h: x[:, h * HGRN_HEAD:(h + 1) * HGRN_HEAD]
    items = [(c, h) for c in chunks for h in range(heads)]
    sc = {ch: jnp.where(causal, _dot_nt(head(qd[ch[0]], ch[1]), head(kd[ch[0]], ch[1])), 0.0).astype(BF16)
          for ch in items}
    intra = {ch: _dot(sc[ch], head(vv[ch[0]], ch[1])) for ch in items}
    upd = {ch: _dot_tn(head(vv[ch[0]], ch[1]), head(kl[ch[0]], ch[1])) for ch in items}
    for h in range(heads):
        st = st_ref[h]
        for c in chunks:
            o = intra[c, h] + _dot_nt(head(qi[c], h), st.astype(BF16))
            st = st * head(el[c], h) + upd[c, h]
            o = o * lax.rsqrt(jnp.mean(o * o, axis=-1, keepdims=True) + NORM_EPS)
            o_ref[c * chunk:(c + 1) * chunk, h * HGRN_HEAD:(h + 1) * HGRN_HEAD] = o * ng * head(gate[c], h)
        st_ref[h] = st

    @pl.when(t == pl.num_programs(1) - 1)
    def _():
        for h in range(heads):
            sout_ref[h] = st_ref[h].T


def _hgrn(p3, lbp, ng, s0, tile, chunk):
    bsz, tlen, _ = p3.shape
    heads, dk, dv = s0.shape[1:]
    d = heads * HGRN_HEAD
    kern = functools.partial(_hgrn_kernel, chunk=chunk, n_chunks=tile // chunk, heads=heads)
    pspec = lambda cb: pl.BlockSpec((None, tile, d), lambda b, t: (b, t, cb))
    return pl.pallas_call(
        kern,
        grid=(bsz, tlen // tile),
        in_specs=[
            pspec(0), pspec(1), pspec(2), pspec(3),
            pl.BlockSpec(lbp.shape, lambda b, t: (0, 0)),
            pl.BlockSpec((1, dv), lambda b, t: (0, 0)),
            pl.BlockSpec((None, heads, dk, dv), lambda b, t: (b, 0, 0, 0)),
        ],
        out_specs=[
            pl.BlockSpec((None, tile, d), lambda b, t: (b, t, 0)),
            pl.BlockSpec((None, heads, dk, dv), lambda b, t: (b, 0, 0, 0)),
        ],
        out_shape=[
            jax.ShapeDtypeStruct((bsz, tlen, d), F32),
            jax.ShapeDtypeStruct((bsz, heads, dk, dv), F32),
        ],
        scratch_shapes=[pltpu.VMEM((heads, dv, dk), F32)],
        compiler_params=_cparams(("parallel", "arbitrary")),
        name="hgrn",
    )(p3, p3, p3, p3, lbp, ng.reshape(1, dv), s0)


_P_MU_R, _P_MU_K, _P_MU_V, _P_W0, _P_A0, _P_KK, _P_KA, _P_RK, _P_LNG, _P_LNB = range(10)
_P_ROWS = 16


def _rwkv_kernel(r_ref, k_ref, v_ref, l_ref, br_ref, bk_ref, bv_ref, bl_ref, par_ref, mul_ref,
                 w2_ref, a2_ref, g2a_ref, g2b_ref, s0_ref,
                 o_ref, sout_ref, st_ref, prev_ref, prevl_ref, *, chunk, n_chunks, pairs):
    t = pl.program_id(1)
    tile = chunk * n_chunks

    @pl.when(t == 0)
    def _():
        zero = jnp.zeros((RWKV_HEAD, RWKV_HEAD), F32)
        for j in range(pairs):
            st_ref[j] = jnp.concatenate([jnp.concatenate([s0_ref[2 * j], zero], axis=1),
                                         jnp.concatenate([zero, s0_ref[2 * j + 1]], axis=1)], axis=0)
        prev_ref[0:1, :] = br_ref[...]
        prev_ref[1:2, :] = bk_ref[...]
        prev_ref[2:3, :] = bv_ref[...]
        prevl_ref[0:1, :] = bl_ref[...]

    par = par_ref[...]
    prow = lambda i: par[i:i + 1, :]

    def shifted(x, prev_row, mu):
        row_id = lax.broadcasted_iota(jnp.int32, x.shape, 0)
        prev = jnp.where(row_id == 0, prev_row, pltpu.roll(x, 1, 0))
        return x + (prev - x) * mu

    pr, pk, pv, plr = r_ref[...], k_ref[...], v_ref[...], l_ref[...]
    r = shifted(pr, prev_ref[0:1, :], prow(_P_MU_R))
    k = shifted(pk, prev_ref[1:2, :], prow(_P_MU_K))
    v = shifted(pv, prev_ref[2:3, :], prow(_P_MU_V))
    xl = shifted(plr, prevl_ref[0:1, :], mul_ref[...])
    for slot, x in enumerate((pr, pk, pv)):
        prev_ref[slot:slot + 1, :] = x[tile - 1:tile, :]
    prevl_ref[0:1, :] = plr[tile - 1:tile, :]
    x0 = xl[:, 0:LANES]
    x1 = xl[:, LANES:2 * LANES]
    x2 = xl[:, 2 * LANES:3 * LANES]

    w_lin = prow(_P_W0) + _dot(jnp.tanh(x0).astype(BF16), w2_ref[...])
    z = -w_lin
    softplus = jnp.maximum(z, 0.0) + jnp.log(1.0 + jnp.exp(-jnp.abs(z)))
    lw = -jnp.exp(-softplus - 0.5)
    a_icl = _sigmoid(prow(_P_A0) + _dot(x0.astype(BF16), a2_ref[...]))
    g_out = _dot(_sigmoid(x1).astype(BF16), g2a_ref[...]) + _dot(_sigmoid(x2).astype(BF16), g2b_ref[...])

    lane = lax.broadcasted_iota(jnp.int32, (1, LANES), 1)
    m0 = (lane < RWKV_HEAD).astype(F32)
    m1 = 1.0 - m0
    head_shift = RWKV_HEAD.bit_length() - 1
    hr = lax.broadcasted_iota(jnp.int32, (LANES, LANES), 0) >> head_shift
    hc = lax.broadcasted_iota(jnp.int32, (LANES, LANES), 1) >> head_shift
    head_ones = (hr == hc).astype(BF16)
    tile_of = lambda x, j: x[:, j * LANES:(j + 1) * LANES]

    def headsum(x):
        xs = jnp.concatenate([tile_of(x, j) for j in range(pairs)], axis=0)
        ys = _dot(xs.astype(BF16), head_ones)
        return jnp.concatenate([ys[j * tile:(j + 1) * tile] for j in range(pairs)], axis=1)

    kk = k * prow(_P_KK)
    kk = kk * lax.rsqrt(jnp.maximum(headsum(kk * kk), 1e-24))
    k_mod = k * (1.0 + (a_icl - 1.0) * prow(_P_KA))
    a = -kk
    b = kk * a_icl

    ti = lax.broadcasted_iota(jnp.int32, (chunk, chunk), 0)
    tj = lax.broadcasted_iota(jnp.int32, (chunk, chunk), 1)
    tri = (ti >= tj).astype(BF16)
    n2 = 2 * chunk
    si = lax.broadcasted_iota(jnp.int32, (n2, n2), 0)
    sj = lax.broadcasted_iota(jnp.int32, (n2, n2), 1)
    chunk_shift = chunk.bit_length() - 1
    same_head = (si >> chunk_shift) == (sj >> chunk_shift)
    strict = same_head & ((sj & (chunk - 1)) < (si & (chunk - 1)))
    incl = same_head & ((sj & (chunk - 1)) <= (si & (chunk - 1)))
    eye = (si == sj).astype(F32)
    m0_b, m1_b = m0.astype(BF16), m1.astype(BF16)
    tile_b = lambda x, j: tile_of(x, j).astype(BF16)
    twice = lambda x: jnp.concatenate([x, x], axis=0)
    every = range(pairs)

    def stack_b(x, j):
        xb = tile_b(x, j)
        return jnp.concatenate([xb * m0_b, xb * m1_b], axis=0)

    def decayed(rows):
        lw_c, a_c, b_c, k_c = lw[rows], a[rows], b[rows], k_mod[rows]
        lw_hi, lw_lo = _split(lw_c)
        cum = _dot(tri, lw_hi) + _dot(tri, lw_lo)
        cum_last = cum[chunk - 1:chunk, :]
        e_neg = jnp.exp(-cum)
        e_end = jnp.exp(cum_last - cum)
        return dict(a_t=a_c * jnp.exp(cum - lw_c), b_t=b_c * e_neg, k_t=k_c * e_neg, r_t=r[rows] * jnp.exp(cum),
                    b_end=b_c * e_end, k_end=k_c * e_end, v=v[rows], g_last=jnp.exp(cum_last))

    ops = [decayed(slice(c * chunk, (c + 1) * chunk)) for c in range(n_chunks)]
    items = [(c, j) for c in range(n_chunks) for j in every]
    idx = range(len(items))
    stacked = lambda name: [stack_b(ops[c][name], j) for c, j in items]
    a_st, r_st, v_st, bend_st, kend_st = (stacked(name) for name in ("a_t", "r_t", "v", "b_end", "k_end"))
    sc = [_dot_nt(jnp.concatenate([tile_b(ops[c]["a_t"], j), tile_b(ops[c]["r_t"], j)], axis=0),
                  jnp.concatenate([stack_b(ops[c]["b_t"], j), stack_b(ops[c]["k_t"], j)], axis=0)) for c, j in items]
    a_ab = [jnp.where(strict, twice(s[:chunk, :n2]), 0.0) for s in sc]
    a_xk = [jnp.concatenate([jnp.where(strict, twice(s[:chunk, n2:]), 0.0),
                             jnp.where(incl, twice(s[chunk:, n2:]), 0.0)], axis=0).astype(BF16) for s in sc]
    a_rb = [jnp.where(incl, twice(s[chunk:, :n2]), 0.0).astype(BF16) for s in sc]

    inv = [eye + x for x in a_ab]
    pw = [_dot(x.astype(BF16), x.astype(BF16)) for x in a_ab]
    for _ in range(int(math.log2(chunk)) - 2):
        if n2 % LANES == 0:
            both = [_dot(p.astype(BF16), jnp.concatenate([p, i], axis=1).astype(BF16)) for p, i in zip(pw, inv)]
            pw_next = [x[:, :n2] for x in both]
            inv = [i + x[:, n2:] for i, x in zip(inv, both)]
        else:
            pw_next = [_dot(p.astype(BF16), p.astype(BF16)) for p in pw]
            inv = [i + _dot(p.astype(BF16), i.astype(BF16)) for i, p in zip(inv, pw)]
        pw = pw_next
    inv = [i + _dot(p.astype(BF16), i.astype(BF16)) for i, p in zip(inv, pw)]

    av = [_dot(a_xk[i], v_st[i]) for i in idx]
    wu = [_dot(inv[i].astype(BF16), jnp.concatenate([a_st[i], av[i][:n2].astype(BF16)], axis=1)).astype(BF16)
          for i in idx]
    qo = [_dot(a_rb[i], wu[i]) + jnp.concatenate([r_st[i].astype(F32), av[i][n2:]], axis=1) for i in idx]
    mn = [_dot_tn(wu[i], bend_st[i]) for i in idx]
    nn = [mn[i][LANES:] + _dot_tn(v_st[i], kend_st[i]) for i in idx]

    outs = []
    for c in range(n_chunks):
        row = []
        for j in every:
            i = c * pairs + j
            s0 = st_ref[j]
            s0_b = s0.astype(BF16)
            q = (qo[i][:chunk, :LANES] + qo[i][chunk:, :LANES]).astype(BF16)
            row.append(_dot_nt(q, s0_b) + qo[i][:chunk, LANES:] + qo[i][chunk:, LANES:])
            st_ref[j] = s0 * tile_of(ops[c]["g_last"], j) + _dot(s0_b, mn[i][:LANES].astype(BF16)) + nn[i]
        outs.append(jnp.concatenate(row, axis=1))
    o = outs[0] if n_chunks == 1 else jnp.concatenate(outs, axis=0)

    inv_n = 1.0 / RWKV_HEAD
    mean = headsum(o) * inv_n
    dlt = o - mean
    var = headsum(dlt * dlt) * inv_n
    o = dlt * lax.rsqrt(var + RWKV_GN_EPS) * prow(_P_LNG) + prow(_P_LNB)
    bonus = headsum(r * k_mod * prow(_P_RK))
    o_ref[...] = (o + bonus * v) * g_out

    @pl.when(t == pl.num_programs(1) - 1)
    def _():
        for j in range(pairs):
            st = st_ref[j]
            sout_ref[2 * j] = st[:RWKV_HEAD, :RWKV_HEAD]
            sout_ref[2 * j + 1] = st[RWKV_HEAD:, RWKV_HEAD:]


def _rwkv(p3, buf_rkv, buf_l, par, mul, w2p, a2p, g2a, g2b, s0, d, tile, chunk):
    bsz, tlen, _ = p3.shape
    pairs = d // LANES
    nl = 3 * LANES
    kern = functools.partial(_rwkv_kernel, chunk=chunk, n_chunks=tile // chunk, pairs=pairs)
    pcol = lambda cb: pl.BlockSpec((None, tile, d), lambda b, t: (b, t, cb))
    bcol = lambda cb: pl.BlockSpec((None, 1, d), lambda b, t: (b, 0, cb))
    full = lambda a: pl.BlockSpec(a.shape, lambda b, t: (0,) * a.ndim)
    state = pl.BlockSpec((None,) + s0.shape[1:], lambda b, t: (b, 0, 0, 0))
    return pl.pallas_call(
        kern,
        grid=(bsz, tlen // tile),
        in_specs=[
            pcol(4), pcol(5), pcol(6),
            pl.BlockSpec((None, tile, nl), lambda b, t: (b, t, 9 * d // nl)),
            bcol(0), bcol(1), bcol(2),
            pl.BlockSpec((None, 1, nl), lambda b, t: (b, 0, 0)),
            full(par), full(mul), full(w2p), full(a2p), full(g2a), full(g2b), state,
        ],
        out_specs=[pl.BlockSpec((None, tile, d), lambda b, t: (b, t, 0)), state],
        out_shape=[
            jax.ShapeDtypeStruct((bsz, tlen, d), F32),
            jax.ShapeDtypeStruct(s0.shape, F32),
        ],
        scratch_shapes=[pltpu.VMEM((pairs, LANES, LANES), F32), pltpu.VMEM((8, d), F32),
                        pltpu.VMEM((8, nl), F32)],
        compiler_params=_cparams(("parallel", "arbitrary")),
        name="rwkv",
    )(p3, p3, p3, p3, buf_rkv, buf_rkv, buf_rkv, buf_l, par, mul, w2p, a2p, g2a, g2b, s0)


def _attn_kernel(x_ref, oh_ref, or_ref, ga_ref, gb_ref, mk_ref, mv_ref, wout_ref, nxg_ref, wq_ref, wo_ref,
                 nfg_ref, wrh_ref, wrl_ref, rb_ref, h_ref, u_ref, wf_ref, att_ref, *, bt, tt):
    n = bt * tt
    d = x_ref.shape[-1]
    flat = lambda ref: ref[...].reshape(n, d)
    merged = _sigmoid(flat(ga_ref)) * flat(oh_ref) + _sigmoid(flat(gb_ref)) * flat(or_ref)
    h1 = flat(x_ref) + _dot(merged.astype(BF16), wout_ref[...])

    q = _dot(_rmsnorm(h1, nxg_ref[...]).astype(BF16), wq_ref[...])
    dh = d // X_HEADS
    scale = dh ** -0.5
    for bi in range(bt):
        rows = slice(bi * tt, (bi + 1) * tt)
        if len(mk_ref.shape) == 3:
            mem_heads = lambda ref: [ref[bi, :, hh * dh:(hh + 1) * dh].astype(BF16) for hh in range(X_HEADS)]
        else:
            def mem_heads(ref):
                by_head = jnp.transpose(ref[bi], (1, 0, 2)).astype(BF16)
                return [by_head[hh] for hh in range(X_HEADS)]
        k_heads, v_heads = mem_heads(mk_ref), mem_heads(mv_ref)
        for hh in range(X_HEADS):
            cols = slice(hh * dh, (hh + 1) * dh)
            s = _dot_nt(q[rows, cols].astype(BF16), k_heads[hh]) * scale
            p = jnp.exp(s - jnp.max(s, axis=-1, keepdims=True))
            p = p / jnp.sum(p, axis=-1, keepdims=True)
            att_ref[rows, cols] = _dot(p.astype(BF16), v_heads[hh]).astype(BF16)
    h2 = h1 + _dot(att_ref[...], wo_ref[...])
    h_ref[...] = h2.reshape(bt, tt, d)

    u = _rmsnorm(h2, nfg_ref[...])
    u_ref[...] = u.astype(BF16).reshape(bt, tt, d)
    u_hi, u_lo = _split(u)
    logits = (_dot(u_hi, wrh_ref[...]) + _dot(u_hi, wrl_ref[...]) + _dot(u_lo, wrh_ref[...])) + rb_ref[...]
    lane = lax.broadcasted_iota(jnp.int32, (n, LANES), 1)
    lane_f = lane.astype(F32)
    is_group = (lane >= N_EXPERTS) & (lane < N_EXPERTS + N_GROUPS)
    lg = jnp.where(is_group, logits, NEG_BIG)
    g_max = jnp.max(lg, axis=-1, keepdims=True)
    g_idx = jnp.min(jnp.where(lg == g_max, lane_f, 1e9), axis=-1, keepdims=True) - N_EXPERTS
    pg = 1.0 / jnp.sum(jnp.exp(lg - g_max), axis=-1, keepdims=True)
    in_group = (lane < N_EXPERTS) & ((lane >> (EXPERTS_PER_GROUP.bit_length() - 1)).astype(F32) == g_idx)
    le = jnp.where(in_group, logits, NEG_BIG)
    m1 = jnp.max(le, axis=-1, keepdims=True)
    i1 = jnp.min(jnp.where(le == m1, lane_f, 1e9), axis=-1, keepdims=True)
    le2 = jnp.where(lane_f == i1, NEG_BIG, le)
    m2 = jnp.max(le2, axis=-1, keepdims=True)
    i2 = jnp.min(jnp.where(le2 == m2, lane_f, 1e9), axis=-1, keepdims=True)
    e2 = jnp.exp(m2 - m1)
    w1 = pg / (1.0 + e2)
    wf = jnp.where(lane_f == i1, w1, 0.0) + jnp.where(lane_f == i2, w1 * e2, 0.0)
    wf = wf + jnp.where(lane == _GID_LANE, g_idx, 0.0)
    wf_ref[...] = wf.reshape(bt, tt, LANES)


def _attn(x3, oh3, or3, p3, mk, mv, wout, nxg, wq, wo, nfg, wrh, wrl, rb, bt, tt, gate_block0):
    bsz, tlen, d = x3.shape
    kern = functools.partial(_attn_kernel, bt=bt, tt=tt)
    tok = lambda cb: pl.BlockSpec((bt, tt, d), lambda b, t: (b, t, cb))
    mem = pl.BlockSpec((bt,) + mk.shape[1:], lambda b, t: (b,) + (0,) * (mk.ndim - 1))
    full = lambda a: pl.BlockSpec(a.shape, lambda b, t: (0,) * a.ndim, pipeline_mode=pl.Buffered(1))
    nxg, nfg = nxg.reshape(1, d), nfg.reshape(1, d)
    return pl.pallas_call(
        kern,
        grid=(bsz // bt, tlen // tt),
        in_specs=[tok(0), tok(0), tok(0), tok(gate_block0), tok(gate_block0 + 1), mem, mem,
                  full(wout), full(nxg), full(wq), full(wo), full(nfg), full(wrh), full(wrl), full(rb)],
        out_specs=[
            pl.BlockSpec((bt, tt, d), lambda b, t: (b, t, 0)),
            pl.BlockSpec((bt, tt, d), lambda b, t: (b, t, 0)),
            pl.BlockSpec((bt, tt, LANES), lambda b, t: (b, t, 0)),
        ],
        out_shape=[
            jax.ShapeDtypeStruct((bsz, tlen, d), F32),
            jax.ShapeDtypeStruct((bsz, tlen, d), BF16),
            jax.ShapeDtypeStruct((bsz, tlen, LANES), F32),
        ],
        scratch_shapes=[pltpu.VMEM((bt * tt, d), BF16)],
        compiler_params=_cparams(("parallel", "parallel")),
        name="attn",
    )(x3, oh3, or3, p3, p3, mk, mv, wout, nxg, wq, wo, nfg, wrh, wrl, rb)


def _segment_dma(src, dst, src_row, dst_row, length, max_len, sem, start):
    size = _SEG
    while size * 2 <= max_len:
        size *= 2
    while size >= _SEG:
        offset = length & (-2 * size)

        @pl.when((length & size) != 0)
        def _(offset=offset, size=size):
            copy = pltpu.make_async_copy(src.at[pl.ds(pl.multiple_of(src_row + offset, _SEG), size)],
                                         dst.at[pl.ds(pl.multiple_of(dst_row + offset, _SEG), size)], sem)
            if start:
                copy.start()
            else:
                copy.wait()

        size //= 2


def _tile_meta(meta_ref, nt, tile):
    field = lambda k: [meta_ref[(k * nt + tile) * N_GROUPS + g] for g in range(N_GROUPS)]
    return field(0), field(1), field(2)


def _moe_sort_kernel(meta_ref, u_ref, wf_ref, xs_ref, ws_ref, su_ref, sw_ref, zu_ref, zw_ref, sem, *, nt, tm, te):
    i = pl.program_id(0)
    buf = lax.rem(i, 2)
    slots = su_ref.shape[1]
    _, _, base = _tile_meta(meta_ref, nt, i)
    wf = wf_ref[...]
    gid = wf.T[_GID_LANE:_GID_LANE + 1, :]
    onehot = lax.broadcasted_iota(jnp.int32, (8, tm), 0).astype(F32) == gid
    earlier = (lax.broadcasted_iota(jnp.int32, (tm, tm), 0)
               < lax.broadcasted_iota(jnp.int32, (tm, tm), 1)).astype(BF16)
    seen = _dot(onehot.astype(BF16), earlier)
    pos = jnp.sum(jnp.where(onehot, seen, 0.0), axis=0, keepdims=True)
    for g in range(N_GROUPS):
        pos = pos + jnp.where(gid == float(g), base[g].astype(F32), 0.0)
    perm = (lax.broadcasted_iota(jnp.int32, (slots, tm), 0).astype(F32) == pos).astype(BF16)
    su_ref[buf] = _dot(perm, u_ref[...]).astype(BF16)
    w_hi, w_lo = _split(wf)
    sw_ref[buf] = _dot(perm, w_hi) + _dot(perm, w_lo)

    def copies(tile, b, start):
        off, cnt, base = _tile_meta(meta_ref, nt, tile)
        for g in range(N_GROUPS):
            _segment_dma(su_ref.at[b], xs_ref, base[g], off[g], cnt[g], tm, sem.at[0, b], start)
            _segment_dma(sw_ref.at[b], ws_ref, base[g], off[g], cnt[g], tm, sem.at[1, b], start)

    copies(i, buf, True)

    @pl.when(i > 0)
    def _():
        copies(i - 1, 1 - buf, False)

    @pl.when(i == nt - 1)
    def _():
        copies(i, buf, False)
        zu_ref[...] = jnp.zeros_like(zu_ref)
        zw_ref[...] = jnp.zeros_like(zw_ref)
        tail = 3 * nt * N_GROUPS + nt
        for start in (True, False):
            for g in range(N_GROUPS):
                off, cnt = meta_ref[tail + g], meta_ref[tail + N_GROUPS + g]
                _segment_dma(zu_ref, xs_ref, 0, off, cnt, te, sem.at[0, 0], start)
                _segment_dma(zw_ref, ws_ref, 0, off, cnt, te, sem.at[1, 0], start)

            def unused(r, carry, start=start):
                row = pl.multiple_of(r * te, te)
                for src, dst, s in ((zu_ref, xs_ref, sem.at[0, 1]), (zw_ref, ws_ref, sem.at[1, 1])):
                    copy = pltpu.make_async_copy(src, dst.at[pl.ds(row, te)], s)
                    if start:
                        copy.start()
                    else:
                        copy.wait()
                return carry

            lax.fori_loop(meta_ref[tail + 2 * N_GROUPS], xs_ref.shape[0] // te, unused, 0)


def _moe_expert_kernel(meta_ref, xs_ref, ws_ref, wg_ref, wu_ref, wd_ref, ys_ref, *, n_tiles):
    r = pl.program_id(0)
    used = r < meta_ref[n_tiles]

    @pl.when(used)
    def _():
        first = meta_ref[r] * EXPERTS_PER_GROUP
        x = xs_ref[...]
        ws = ws_ref[...]
        lane = lax.broadcasted_iota(jnp.int32, ws.shape, 1)
        hid = []
        for e in range(EXPERTS_PER_GROUP):
            h = _silu(_dot(x, wg_ref[e])) * _dot(x, wu_ref[e])
            w_col = jnp.sum(jnp.where(lane == first + e, ws, 0.0), axis=-1, keepdims=True)
            hid.append((h * w_col).astype(BF16))
        ys_ref[...] = _dot(jnp.concatenate(hid, axis=1), wd_ref[...]).astype(ys_ref.dtype)

    @pl.when(jnp.logical_not(used))
    def _():
        ys_ref[...] = jnp.zeros_like(ys_ref)


def _moe_unsort_kernel(meta_ref, wf_ref, h_ref, gf_ref, ys_ref, y_ref, sy_ref, sem, *, nt, tm):
    i = pl.program_id(0)
    buf = lax.rem(i, 2)
    slots = sy_ref.shape[1]

    def copies(tile, b, start):
        off, cnt, base = _tile_meta(meta_ref, nt, tile)
        for g in range(N_GROUPS):
            _segment_dma(ys_ref, sy_ref.at[b], off[g], base[g], cnt[g], tm, sem.at[b], start)

    @pl.when(i == 0)
    def _():
        sy_ref[...] = jnp.zeros_like(sy_ref)
        copies(0, 0, True)

    @pl.when(i + 1 < nt)
    def _():
        copies(i + 1, 1 - buf, True)

    _, _, base = _tile_meta(meta_ref, nt, i)
    gid = wf_ref[...][:, _GID_LANE:_GID_LANE + 1]
    onehot = lax.broadcasted_iota(jnp.int32, (tm, LANES), 1).astype(F32) == gid
    earlier = (lax.broadcasted_iota(jnp.int32, (tm, tm), 1)
               < lax.broadcasted_iota(jnp.int32, (tm, tm), 0)).astype(BF16)
    seen = _dot(earlier, onehot.astype(BF16))
    pos = jnp.sum(jnp.where(onehot, seen, 0.0), axis=1, keepdims=True)
    for g in range(N_GROUPS):
        pos = pos + jnp.where(gid == float(g), base[g].astype(F32), 0.0)
    perm_t = (lax.broadcasted_iota(jnp.int32, (tm, slots), 1).astype(F32) == pos).astype(BF16)
    copies(i, buf, False)
    slot_row = lax.broadcasted_iota(jnp.int32, (slots, 1), 0)
    ys = sy_ref[buf]
    moe = _dot(perm_t, jnp.where(slot_row < meta_ref[3 * nt * N_GROUPS + i], ys, jnp.zeros_like(ys)))
    y_ref[...] = _rmsnorm(h_ref[...] + moe, gf_ref[...])


def _moe(u, wf, h, wg, wu, wd, gf, tm, te):
    n, d = u.shape
    ne, _, f = wg.shape
    nt = n // tm
    slots = tm + LANES
    n_tiles = -(-(n + nt * N_GROUPS * (_SEG - 1) + N_GROUPS * (te - 1)) // te)
    rows = n_tiles * te
    i32 = jnp.int32

    gid = wf[:, _GID_LANE].astype(i32).reshape(nt, tm)
    cnt = jnp.sum((gid[:, :, None] == jnp.arange(N_GROUPS, dtype=i32)).astype(i32), axis=1)
    cnt = (cnt + _SEG - 1) // _SEG * _SEG
    base = jnp.cumsum(cnt, axis=1) - cnt
    total = jnp.sum(cnt, axis=0)
    region = (total + te - 1) // te * te
    region_end = jnp.cumsum(region)
    region_start = region_end - region
    off = region_start[None, :] + jnp.cumsum(cnt, axis=0) - cnt
    n_used = region_end[-1:] // te
    meta = jnp.concatenate([off.reshape(-1), cnt.reshape(-1), base.reshape(-1), jnp.sum(cnt, axis=1),
                            region_start + total, region - total, n_used]).astype(i32)
    tile_row = jnp.arange(n_tiles, dtype=i32) * te
    tile_group = jnp.minimum(jnp.sum((tile_row[:, None] >= region_end[None, :]).astype(i32), axis=1), N_GROUPS - 1)
    emeta = jnp.concatenate([tile_group, n_used]).astype(i32)

    anyspec = pl.BlockSpec(memory_space=pl.ANY)
    tok = lambda w: pl.BlockSpec((tm, w), lambda i, m: (i, 0))
    xs, ws = pl.pallas_call(
        functools.partial(_moe_sort_kernel, nt=nt, tm=tm, te=te),
        grid_spec=pltpu.PrefetchScalarGridSpec(
            num_scalar_prefetch=1, grid=(nt,),
            in_specs=[tok(d), tok(LANES)],
            out_specs=[anyspec, anyspec],
            scratch_shapes=[pltpu.VMEM((2, slots, d), BF16), pltpu.VMEM((2, slots, LANES), F32),
                            pltpu.VMEM((te, d), BF16), pltpu.VMEM((te, LANES), F32),
                            pltpu.SemaphoreType.DMA((2, 2))]),
        out_shape=[jax.ShapeDtypeStruct((rows, d), BF16), jax.ShapeDtypeStruct((rows, LANES), F32)],
        compiler_params=_cparams(("arbitrary",)),
        name="moe_sort",
    )(meta, u, wf)

    srt = lambda w: pl.BlockSpec((te, w), lambda r, m: (r, 0))
    grp_w = pl.BlockSpec((EXPERTS_PER_GROUP, d, f), lambda r, m: (m[r], 0, 0))
    ys = pl.pallas_call(
        functools.partial(_moe_expert_kernel, n_tiles=n_tiles),
        grid_spec=pltpu.PrefetchScalarGridSpec(
            num_scalar_prefetch=1, grid=(n_tiles,),
            in_specs=[srt(d), srt(LANES), grp_w, grp_w,
                      pl.BlockSpec((EXPERTS_PER_GROUP * f, d), lambda r, m: (m[r], 0))],
            out_specs=srt(d)),
        out_shape=jax.ShapeDtypeStruct((rows, d), BF16),
        compiler_params=_cparams(("parallel",)),
        name="moe_expert",
    )(emeta, xs, ws, wg, wu, wd.reshape(ne * f, d))

    return pl.pallas_call(
        functools.partial(_moe_unsort_kernel, nt=nt, tm=tm),
        grid_spec=pltpu.PrefetchScalarGridSpec(
            num_scalar_prefetch=1, grid=(nt,),
            in_specs=[tok(LANES), tok(d), pl.BlockSpec((1, d), lambda i, m: (0, 0)), anyspec],
            out_specs=tok(d),
            scratch_shapes=[pltpu.VMEM((2, slots, d), BF16), pltpu.SemaphoreType.DMA((2,))]),
        out_shape=jax.ShapeDtypeStruct((n, d), F32),
        compiler_params=_cparams(("arbitrary",)),
        name="moe_unsort",
    )(meta, wf, h, gf.reshape(1, d), ys)


def _pick(n, pref):
    t = min(n, pref)
    while n % t:
        t -= 8
    return t


def kernel(x_prompt, x_sample, mem_prompt, cache_mem_k, cache_mem_v, state_hgrn, state_rwkv, state_rwkv_shift, norm_mix_g, w_in, hgrn_lower_bounds, hgrn_norm_g, rwkv_mu, rwkv_w0, rwkv_w2, rwkv_a0, rwkv_a2, rwkv_g2, rwkv_k_k, rwkv_k_a, rwkv_r_k, rwkv_ln_g, rwkv_ln_b, w_out, norm_x_g, norm_mem_g, wq_x, wk_x, wv_x, wo_x, norm_ffn_g, router_group_w, router_group_b, router_expert_w, router_expert_b, expert_w_gate, expert_w_up, expert_w_down, norm_final_g):
    assert w_in.shape[0] == 1, "single-layer configuration"
    d = x_prompt.shape[-1]
    n_lora_w, n_lora_a, n_lora_g = rwkv_w2.shape[1], rwkv_a2.shape[1], rwkv_g2.shape[1]
    assert n_lora_w + n_lora_a == LANES and LANES < n_lora_g <= 2 * LANES
    n_lora = n_lora_w + n_lora_a + n_lora_g
    n_rwkv_in = 3 * d + n_lora
    lora_pad = 3 * LANES - n_lora

    pr0 = 4 * d
    gate0 = pr0 + n_rwkv_in
    n_cols = w_in.shape[2] + lora_pad
    w_perm = _permute_columns(w_in, [(0, pr0 + 3 * d), (gate0, 2 * d), (pr0 + 3 * d, n_lora)], n_cols,
                              _pick(d, 128))
    lora_col0 = 9 * d
    gate_block0 = 7
    mu = rwkv_mu[0]
    mul = jnp.pad(mu[3 * d:], (0, lora_pad)).reshape(1, 3 * LANES)
    par = jnp.stack([mu[:d], mu[d:2 * d], mu[2 * d:3 * d], rwkv_w0[0], rwkv_a0[0], rwkv_k_k[0], rwkv_k_a[0],
                     rwkv_r_k[0].reshape(d), rwkv_ln_g[0], rwkv_ln_b[0]])
    par = jnp.pad(par, ((0, _P_ROWS - par.shape[0]), (0, 0)))
    w2p = jnp.pad(rwkv_w2[0], ((0, n_lora_a), (0, 0))).astype(BF16)
    a2p = jnp.pad(rwkv_a2[0], ((n_lora_w, 0), (0, 0))).astype(BF16)
    g2a = rwkv_g2[0][:LANES].astype(BF16)
    g2b = jnp.pad(rwkv_g2[0][LANES:], ((0, 2 * LANES - n_lora_g), (0, 0))).astype(BF16)
    wout_b, wq_b, wo_b = w_out[0].astype(BF16), wq_x[0].astype(BF16), wo_x[0].astype(BF16)
    wr = jnp.concatenate([router_expert_w[0], router_group_w[0],
                          jnp.zeros((d, LANES - N_EXPERTS - N_GROUPS), F32)], axis=1)
    wr_hi, wr_lo = _split(wr)
    rb = jnp.pad(jnp.concatenate([router_expert_b[0].reshape(-1), router_group_b[0]]),
                 (0, LANES - N_EXPERTS - N_GROUPS)).reshape(1, LANES)
    wg_b, wu_b, wd_b = (expert_w_gate[0].astype(BF16), expert_w_up[0].astype(BF16),
                        expert_w_down[0].astype(BF16))

    nb_p = mem_prompt.shape[0]
    mem_k_p, mem_v_p = _mem_kv(mem_prompt, norm_mem_g[0], wk_x[0].astype(BF16), wv_x[0].astype(BF16))

    def group(x3, mem_k, mem_v, s_h, s_r, buf, tile_h, tile_r, chunk_r, bt, tt, te):
        bsz, tlen, _ = x3.shape
        n = bsz * tlen
        p = _norm_matmul(x3.reshape(n, d), norm_mix_g[0], w_perm, _pick(n, 1024), n_cols // 3)
        p3 = p.reshape(bsz, tlen, n_cols)
        o_h, s_h_new = _hgrn(p3, hgrn_lower_bounds, hgrn_norm_g[0], s_h, tile_h, 32)
        buf_rkv = buf[:, :, :3 * d]
        buf_l = jnp.pad(buf[:, :, 3 * d:], ((0, 0), (0, 0), (0, lora_pad)))
        o_r, s_r_new = _rwkv(p3, buf_rkv, buf_l, par, mul, w2p, a2p, g2a, g2b, s_r, d, tile_r, chunk_r)
        h2, u, wf = _attn(x3, o_h, o_r, p3, mem_k, mem_v, wout_b, norm_x_g[0], wq_b, wo_b, norm_ffn_g[0],
                          wr_hi, wr_lo, rb, bt, tt, gate_block0)
        y = _moe(u.reshape(n, d), wf.reshape(n, LANES), h2.reshape(n, d), wg_b, wu_b, wd_b, norm_final_g,
                 _pick(n, 512), te)
        shift = jnp.concatenate([p3[:, -1:, pr0:pr0 + 3 * d], p3[:, -1:, lora_col0:lora_col0 + n_lora]], axis=-1)
        return y.reshape(bsz, tlen, d), s_h_new, s_r_new, shift

    nb_s, t_s, _ = x_sample.shape
    t_p = x_prompt.shape[1]
    zeros_like_state = lambda s: jnp.zeros((nb_p,) + s.shape[2:], s.dtype)
    y_p, sh_p, sr_p, bf_p = group(
        x_prompt, mem_k_p, mem_v_p, zeros_like_state(state_hgrn), zeros_like_state(state_rwkv),
        zeros_like_state(state_rwkv_shift), _pick(t_p, 512), _pick(t_p, 256), 64, 1, _pick(t_p, 512), 512)
    y_s, sh_s, sr_s, bf_s = group(
        x_sample, cache_mem_k[0], cache_mem_v[0],
        state_hgrn[0], state_rwkv[0], state_rwkv_shift[0], t_s, t_s, t_s, _pick(nb_s, 4), t_s, 256)

    kv_shape = (1,) + mem_k_p.shape[:2] + cache_mem_k.shape[3:]
    return (y_p, y_s, mem_k_p.reshape(kv_shape), mem_v_p.reshape(kv_shape), sh_p[None], sh_s[None],
            sr_p[None], sr_s[None], bf_p[None], bf_s[None])
```

```python
import functools
import math

import jax
import jax.numpy as jnp
from jax import lax
from jax.experimental import pallas as pl
from jax.experimental.pallas import tpu as pltpu

F32 = jnp.float32
BF16 = jnp.bfloat16

NORM_EPS = 1e-6
RWKV_GN_EPS = 64e-5
LANES = 128
HGRN_HEAD = 128
RWKV_HEAD = 64
X_HEADS = 4
N_GROUPS = 4
EXPERTS_PER_GROUP = 8
N_EXPERTS = N_GROUPS * EXPERTS_PER_GROUP
NEG_BIG = -1e30
VMEM_LIMIT = 52 * 1024 * 1024
_GID_LANE = N_EXPERTS
_SEG = 16


def _dot(a, b):
    return jnp.dot(a, b, preferred_element_type=F32)


def _dot_nt(a, b):
    return lax.dot_general(a, b, (((1,), (1,)), ((), ())), preferred_element_type=F32)


def _dot_tn(a, b):
    return lax.dot_general(a, b, (((0,), (0,)), ((), ())), preferred_element_type=F32)


def _split(x):
    hi = x.astype(BF16)
    lo = (x - hi.astype(F32)).astype(BF16)
    return hi, lo


def _sigmoid(x):
    return 0.5 * jnp.tanh(0.5 * x) + 0.5


def _silu(x):
    return x * _sigmoid(x)


def _rmsnorm(x, g):
    return x * lax.rsqrt(jnp.mean(x * x, axis=-1, keepdims=True) + NORM_EPS) * g


def _cparams(sem):
    return pltpu.CompilerParams(dimension_semantics=sem, vmem_limit_bytes=VMEM_LIMIT)


def _norm_matmul_kernel(x_ref, g_ref, w_ref, o_ref, u_ref):
    @pl.when(pl.program_id(1) == 0)
    def _():
        u_ref[...] = _rmsnorm(x_ref[...], g_ref[...]).astype(BF16)

    o_ref[...] = _dot(u_ref[...], w_ref[...])


def _norm_matmul(x, g, w, tm, tn):
    n, d = x.shape
    m = w.shape[1]
    return pl.pallas_call(
        _norm_matmul_kernel,
        grid=(n // tm, m // tn),
        in_specs=[
            pl.BlockSpec((tm, d), lambda i, j: (i, 0)),
            pl.BlockSpec((1, d), lambda i, j: (0, 0)),
            pl.BlockSpec((d, tn), lambda i, j: (0, j)),
        ],
        out_specs=pl.BlockSpec((tm, tn), lambda i, j: (i, j)),
        out_shape=jax.ShapeDtypeStruct((n, m), F32),
        scratch_shapes=[pltpu.VMEM((tm, d), BF16)],
        compiler_params=_cparams(("parallel", "arbitrary")),
        name="norm_matmul",
    )(x, g.reshape(1, d), w)


def _permute_columns_kernel(w_ref, o_ref, *, pieces):
    col = 0
    for src, width in pieces:
        o_ref[:, col:col + width] = w_ref[:, src:src + width].astype(o_ref.dtype)
        col += width
    if col < o_ref.shape[1]:
        o_ref[:, col:] = jnp.zeros((o_ref.shape[0], o_ref.shape[1] - col), o_ref.dtype)


def _permute_columns(w, pieces, n_cols, tr):
    _, rows, cols = w.shape
    return pl.pallas_call(
        functools.partial(_permute_columns_kernel, pieces=pieces),
        grid=(rows // tr,),
        in_specs=[pl.BlockSpec((None, tr, cols), lambda i: (0, i, 0))],
        out_specs=pl.BlockSpec((tr, n_cols), lambda i: (i, 0)),
        out_shape=jax.ShapeDtypeStruct((rows, n_cols), BF16),
        compiler_params=_cparams(("parallel",)),
        name="permute_columns",
    )(w)


def _mem_kv_kernel(m_ref, g_ref, wk_ref, wv_ref, k_ref, v_ref):
    u = _rmsnorm(m_ref[...], g_ref[...]).astype(BF16)
    k_ref[...] = _dot(u, wk_ref[...])
    v_ref[...] = _dot(u, wv_ref[...])


def _mem_kv(mem, g, wk, wv):
    bsz, nmem, d = mem.shape
    full = lambda a: pl.BlockSpec(a.shape, lambda b: (0,) * a.ndim)
    out = pl.BlockSpec((None, nmem, d), lambda b: (b, 0, 0))
    g = g.reshape(1, d)
    return pl.pallas_call(
        _mem_kv_kernel,
        grid=(bsz,),
        in_specs=[pl.BlockSpec((None, nmem, d), lambda b: (b, 0, 0)), full(g), full(wk), full(wv)],
        out_specs=[out, out],
        out_shape=[jax.ShapeDtypeStruct((bsz, nmem, d), F32)] * 2,
        compiler_params=_cparams(("parallel",)),
        name="mem_kv",
    )(mem, g, wk, wv)


def _hgrn_kernel(q_ref, f_ref, i_ref, g_ref, lbp_ref, ng_ref, s0_ref, o_ref, sout_ref, st_ref,
                 *, chunk, n_chunks, heads):
    t = pl.program_id(1)

    @pl.when(t == 0)
    def _():
        for h in range(heads):
            st_ref[h] = s0_ref[h].T

    lbp = lbp_ref[...]
    e = jnp.exp(lbp - jnp.max(lbp, axis=0, keepdims=True))
    lb = e[0:1] / jnp.sum(e, axis=0, keepdims=True)
    ng = ng_ref[...]

    row = lax.broadcasted_iota(jnp.int32, (chunk, chunk), 0)
    col = lax.broadcasted_iota(jnp.int32, (chunk, chunk), 1)
    causal = row >= col
    mid = chunk // 2
    to_mid = jnp.where((col >= mid) & (col <= row), 1.0, 0.0) - jnp.where((col > row) & (col < mid), 1.0, 0.0)
    coef = jnp.concatenate([causal.astype(F32), to_mid, (col > row).astype(F32)], axis=0).astype(BF16)

    chunks = range(n_chunks)
    qd, kd, qi, kl, el, vv, gate = [], [], [], [], [], [], []
    for c in chunks:
        rows = slice(c * chunk, (c + 1) * chunk)
        f = lb + (1.0 - lb) * _sigmoid(f_ref[rows, :])
        g_hi, g_lo = _split(jnp.log(f))
        bb = _dot(coef, g_hi) + _dot(coef, g_lo)
        b, b_to_mid, b_to_last = bb[:chunk], bb[chunk:2 * chunk], bb[2 * chunk:]
        q = _silu(q_ref[rows, :])
        k = 1.0 - f
        qd.append((q * jnp.exp(b_to_mid)).astype(BF16))
        kd.append((k * jnp.exp(-b_to_mid)).astype(BF16))
        qi.append((q * jnp.exp(b)).astype(BF16))
        kl.append((k * jnp.exp(b_to_last)).astype(BF16))
        el.append(jnp.exp(b[chunk - 1:chunk, :]))
        vv.append(i_ref[rows, :].astype(BF16))
        gate.append(_silu(g_ref[rows, :]))

    head = lambda x, h: x[:, h * HGRN_HEAD:(h + 1) * HGRN_HEAD]
    items = [(c, h) for c in chunks for h in range(heads)]
    sc = {ch: jnp.where(causal, _dot_nt(head(qd[ch[0]], ch[1]), head(kd[ch[0]], ch[1])), 0.0).astype(BF16)
          for ch in items}
    intra = {ch: _dot(sc[ch], head(vv[ch[0]], ch[1])) for ch in items}
    upd = {ch: _dot_tn(head(vv[ch[0]], ch[1]), head(kl[ch[0]], ch[1])) for ch in items}
    for h in range(heads):
        st = st_ref[h]
        for c in chunks:
            o = intra[c, h] + _dot_nt(head(qi[c], h), st.astype(BF16))
            st = st * head(el[c], h) + upd[c, h]
            o = o * lax.rsqrt(jnp.mean(o * o, axis=-1, keepdims=True) + NORM_EPS)
            o_ref[c * chunk:(c + 1) * chunk, h * HGRN_HEAD:(h + 1) * HGRN_HEAD] = o * ng * head(gate[c], h)
        st_ref[h] = st

    @pl.when(t == pl.num_programs(1) - 1)
    def _():
        for h in range(heads):
            sout_ref[h] = st_ref[h].T


def _hgrn(p3, lbp, ng, s0, tile, chunk):
    bsz, tlen, _ = p3.shape
    heads, dk, dv = s0.shape[1:]
    d = heads * HGRN_HEAD
    kern = functools.partial(_hgrn_kernel, chunk=chunk, n_chunks=tile // chunk, heads=heads)
    pspec = lambda cb: pl.BlockSpec((None, tile, d), lambda b, t: (b, t, cb))
    return pl.pallas_call(
        kern,
        grid=(bsz, tlen // tile),
        in_specs=[
            pspec(0), pspec(1), pspec(2), pspec(3),
            pl.BlockSpec(lbp.shape, lambda b, t: (0, 0)),
            pl.BlockSpec((1, dv), lambda b, t: (0, 0)),
            pl.BlockSpec((None, heads, dk, dv), lambda b, t: (b, 0, 0, 0)),
        ],
        out_specs=[
            pl.BlockSpec((None, tile, d), lambda b, t: (b, t, 0)),
            pl.BlockSpec((None, heads, dk, dv), lambda b, t: (b, 0, 0, 0)),
        ],
        out_shape=[
            jax.ShapeDtypeStruct((bsz, tlen, d), F32),
            jax.ShapeDtypeStruct((bsz, heads, dk, dv), F32),
        ],
        scratch_shapes=[pltpu.VMEM((heads, dv, dk), F32)],
        compiler_params=_cparams(("parallel", "arbitrary")),
        name="hgrn",
    )(p3, p3, p3, p3, lbp, ng.reshape(1, dv), s0)


_P_MU_R, _P_MU_K, _P_MU_V, _P_W0, _P_A0, _P_KK, _P_KA, _P_RK, _P_LNG, _P_LNB = range(10)
_P_ROWS = 16


def _rwkv_kernel(r_ref, k_ref, v_ref, l_ref, br_ref, bk_ref, bv_ref, bl_ref, par_ref, mul_ref,
                 w2_ref, a2_ref, g2a_ref, g2b_ref, s0_ref,
                 o_ref, sout_ref, st_ref, prev_ref, prevl_ref, *, chunk, n_chunks, pairs):
    t = pl.program_id(1)
    tile = chunk * n_chunks

    @pl.when(t == 0)
    def _():
        zero = jnp.zeros((RWKV_HEAD, RWKV_HEAD), F32)
        for j in range(pairs):
            st_ref[j] = jnp.concatenate([jnp.concatenate([s0_ref[2 * j], zero], axis=1),
                                         jnp.concatenate([zero, s0_ref[2 * j + 1]], axis=1)], axis=0)
        prev_ref[0:1, :] = br_ref[...]
        prev_ref[1:2, :] = bk_ref[...]
        prev_ref[2:3, :] = bv_ref[...]
        prevl_ref[0:1, :] = bl_ref[...]

    par = par_ref[...]
    prow = lambda i: par[i:i + 1, :]

    def shifted(x, prev_row, mu):
        row_id = lax.broadcasted_iota(jnp.int32, x.shape, 0)
        prev = jnp.where(row_id == 0, prev_row, pltpu.roll(x, 1, 0))
        return x + (prev - x) * mu

    pr, pk, pv, plr = r_ref[...], k_ref[...], v_ref[...], l_ref[...]
    r = shifted(pr, prev_ref[0:1, :], prow(_P_MU_R))
    k = shifted(pk, prev_ref[1:2, :], prow(_P_MU_K))
    v = shifted(pv, prev_ref[2:3, :], prow(_P_MU_V))
    xl = shifted(plr, prevl_ref[0:1, :], mul_ref[...])
    for slot, x in enumerate((pr, pk, pv)):
        prev_ref[slot:slot + 1, :] = x[tile - 1:tile, :]
    prevl_ref[0:1, :] = plr[tile - 1:tile, :]
    x0 = xl[:, 0:LANES]
    x1 = xl[:, LANES:2 * LANES]
    x2 = xl[:, 2 * LANES:3 * LANES]

    w_lin = prow(_P_W0) + _dot(jnp.tanh(x0).astype(BF16), w2_ref[...])
    z = -w_lin
    softplus = jnp.maximum(z, 0.0) + jnp.log(1.0 + jnp.exp(-jnp.abs(z)))
    lw = -jnp.exp(-softplus - 0.5)
    a_icl = _sigmoid(prow(_P_A0) + _dot(x0.astype(BF16), a2_ref[...]))
    g_out = _dot(_sigmoid(x1).astype(BF16), g2a_ref[...]) + _dot(_sigmoid(x2).astype(BF16), g2b_ref[...])

    lane = lax.broadcasted_iota(jnp.int32, (1, LANES), 1)
    m0 = (lane < RWKV_HEAD).astype(F32)
    m1 = 1.0 - m0
    head_shift = RWKV_HEAD.bit_length() - 1
    hr = lax.broadcasted_iota(jnp.int32, (LANES, LANES), 0) >> head_shift
    hc = lax.broadcasted_iota(jnp.int32, (LANES, LANES), 1) >> head_shift
    head_ones = (hr == hc).astype(BF16)
    tile_of = lambda x, j: x[:, j * LANES:(j + 1) * LANES]

    def headsum(x):
        xs = jnp.concatenate([tile_of(x, j) for j in range(pairs)], axis=0)
        ys = _dot(xs.astype(BF16), head_ones)
        return jnp.concatenate([ys[j * tile:(j + 1) * tile] for j in range(pairs)], axis=1)

    kk = k * prow(_P_KK)
    kk = kk * lax.rsqrt(jnp.maximum(headsum(kk * kk), 1e-24))
    k_mod = k * (1.0 + (a_icl - 1.0) * prow(_P_KA))
    a = -kk
    b = kk * a_icl

    ti = lax.broadcasted_iota(jnp.int32, (chunk, chunk), 0)
    tj = lax.broadcasted_iota(jnp.int32, (chunk, chunk), 1)
    tri = (ti >= tj).astype(BF16)
    n2 = 2 * chunk
    si = lax.broadcasted_iota(jnp.int32, (n2, n2), 0)
    sj = lax.broadcasted_iota(jnp.int32, (n2, n2), 1)
    chunk_shift = chunk.bit_length() - 1
    same_head = (si >> chunk_shift) == (sj >> chunk_shift)
    strict = same_head & ((sj & (chunk - 1)) < (si & (chunk - 1)))
    incl = same_head & ((sj & (chunk - 1)) <= (si & (chunk - 1)))
    eye = (si == sj).astype(F32)
    m0_b, m1_b = m0.astype(BF16), m1.astype(BF16)
    tile_b = lambda x, j: tile_of(x, j).astype(BF16)
    twice = lambda x: jnp.concatenate([x, x], axis=0)
    every = range(pairs)

    def stack_b(x, j):
        xb = tile_b(x, j)
        return jnp.concatenate([xb * m0_b, xb * m1_b], axis=0)

    def decayed(rows):
        lw_c, a_c, b_c, k_c = lw[rows], a[rows], b[rows], k_mod[rows]
        lw_hi, lw_lo = _split(lw_c)
        cum = _dot(tri, lw_hi) + _dot(tri, lw_lo)
        cum_last = cum[chunk - 1:chunk, :]
        e_neg = jnp.exp(-cum)
        e_end = jnp.exp(cum_last - cum)
        return dict(a_t=a_c * jnp.exp(cum - lw_c), b_t=b_c * e_neg, k_t=k_c * e_neg, r_t=r[rows] * jnp.exp(cum),
                    b_end=b_c * e_end, k_end=k_c * e_end, v=v[rows], g_last=jnp.exp(cum_last))

    ops = [decayed(slice(c * chunk, (c + 1) * chunk)) for c in range(n_chunks)]
    items = [(c, j) for c in range(n_chunks) for j in every]
    idx = range(len(items))
    stacked = lambda name: [stack_b(ops[c][name], j) for c, j in items]
    a_st, r_st, v_st, bend_st, kend_st = (stacked(name) for name in ("a_t", "r_t", "v", "b_end", "k_end"))
    sc = [_dot_nt(jnp.concatenate([tile_b(ops[c]["a_t"], j), tile_b(ops[c]["r_t"], j)], axis=0),
                  jnp.concatenate([stack_b(ops[c]["b_t"], j), stack_b(ops[c]["k_t"], j)], axis=0)) for c, j in items]
    a_ab = [jnp.where(strict, twice(s[:chunk, :n2]), 0.0) for s in sc]
    a_xk = [jnp.concatenate([jnp.where(strict, twice(s[:chunk, n2:]), 0.0),
                             jnp.where(incl, twice(s[chunk:, n2:]), 0.0)], axis=0).astype(BF16) for s in sc]
    a_rb = [jnp.where(incl, twice(s[chunk:, :n2]), 0.0).astype(BF16) for s in sc]

    inv = [eye + x for x in a_ab]
    pw = [_dot(x.astype(BF16), x.astype(BF16)) for x in a_ab]
    for _ in range(int(math.log2(chunk)) - 2):
        if n2 % LANES == 0:
            both = [_dot(p.astype(BF16), jnp.concatenate([p, i], axis=1).astype(BF16)) for p, i in zip(pw, inv)]
            pw_next = [x[:, :n2] for x in both]
            inv = [i + x[:, n2:] for i, x in zip(inv, both)]
        else:
            pw_next = [_dot(p.astype(BF16), p.astype(BF16)) for p in pw]
            inv = [i + _dot(p.astype(BF16), i.astype(BF16)) for i, p in zip(inv, pw)]
        pw = pw_next
    inv = [i + _dot(p.astype(BF16), i.astype(BF16)) for i, p in zip(inv, pw)]

    av = [_dot(a_xk[i], v_st[i]) for i in idx]
    wu = [_dot(inv[i].astype(BF16), jnp.concatenate([a_st[i], av[i][:n2].astype(BF16)], axis=1)).astype(BF16)
          for i in idx]
    qo = [_dot(a_rb[i], wu[i]) + jnp.concatenate([r_st[i].astype(F32), av[i][n2:]], axis=1) for i in idx]
    mn = [_dot_tn(wu[i], bend_st[i]) for i in idx]
    nn = [mn[i][LANES:] + _dot_tn(v_st[i], kend_st[i]) for i in idx]

    outs = []
    for c in range(n_chunks):
        row = []
        for j in every:
            i = c * pairs + j
            s0 = st_ref[j]
            s0_b = s0.astype(BF16)
            q = (qo[i][:chunk, :LANES] + qo[i][chunk:, :LANES]).astype(BF16)
            row.append(_dot_nt(q, s0_b) + qo[i][:chunk, LANES:] + qo[i][chunk:, LANES:])
            st_ref[j] = s0 * tile_of(ops[c]["g_last"], j) + _dot(s0_b, mn[i][:LANES].astype(BF16)) + nn[i]
        outs.append(jnp.concatenate(row, axis=1))
    o = outs[0] if n_chunks == 1 else jnp.concatenate(outs, axis=0)

    inv_n = 1.0 / RWKV_HEAD
    mean = headsum(o) * inv_n
    dlt = o - mean
    var = headsum(dlt * dlt) * inv_n
    o = dlt * lax.rsqrt(var + RWKV_GN_EPS) * prow(_P_LNG) + prow(_P_LNB)
    bonus = headsum(r * k_mod * prow(_P_RK))
    o_ref[...] = (o + bonus * v) * g_out

    @pl.when(t == pl.num_programs(1) - 1)
    def _():
        for j in range(pairs):
            st = st_ref[j]
            sout_ref[2 * j] = st[:RWKV_HEAD, :RWKV_HEAD]
            sout_ref[2 * j + 1] = st[RWKV_HEAD:, RWKV_HEAD:]


def _rwkv(p3, buf_rkv, buf_l, par, mul, w2p, a2p, g2a, g2b, s0, d, tile, chunk):
    bsz, tlen, _ = p3.shape
    pairs = d // LANES
    nl = 3 * LANES
    kern = functools.partial(_rwkv_kernel, chunk=chunk, n_chunks=tile // chunk, pairs=pairs)
    pcol = lambda cb: pl.BlockSpec((None, tile, d), lambda b, t: (b, t, cb))
    bcol = lambda cb: pl.BlockSpec((None, 1, d), lambda b, t: (b, 0, cb))
    full = lambda a: pl.BlockSpec(a.shape, lambda b, t: (0,) * a.ndim)
    state = pl.BlockSpec((None,) + s0.shape[1:], lambda b, t: (b, 0, 0, 0))
    return pl.pallas_call(
        kern,
        grid=(bsz, tlen // tile),
        in_specs=[
            pcol(4), pcol(5), pcol(6),
            pl.BlockSpec((None, tile, nl), lambda b, t: (b, t, 9 * d // nl)),
            bcol(0), bcol(1), bcol(2),
            pl.BlockSpec((None, 1, nl), lambda b, t: (b, 0, 0)),
            full(par), full(mul), full(w2p), full(a2p), full(g2a), full(g2b), state,
        ],
        out_specs=[pl.BlockSpec((None, tile, d), lambda b, t: (b, t, 0)), state],
        out_shape=[
            jax.ShapeDtypeStruct((bsz, tlen, d), F32),
            jax.ShapeDtypeStruct(s0.shape, F32),
        ],
        scratch_shapes=[pltpu.VMEM((pairs, LANES, LANES), F32), pltpu.VMEM((8, d), F32),
                        pltpu.VMEM((8, nl), F32)],
        compiler_params=_cparams(("parallel", "arbitrary")),
        name="rwkv",
    )(p3, p3, p3, p3, buf_rkv, buf_rkv, buf_rkv, buf_l, par, mul, w2p, a2p, g2a, g2b, s0)


def _attn_kernel(x_ref, oh_ref, or_ref, ga_ref, gb_ref, mk_ref, mv_ref, wout_ref, nxg_ref, wq_ref, wo_ref,
                 nfg_ref, wrh_ref, wrl_ref, rb_ref, h_ref, u_ref, wf_ref, att_ref, *, bt, tt):
    n = bt * tt
    d = x_ref.shape[-1]
    flat = lambda ref: ref[...].reshape(n, d)
    merged = _sigmoid(flat(ga_ref)) * flat(oh_ref) + _sigmoid(flat(gb_ref)) * flat(or_ref)
    h1 = flat(x_ref) + _dot(merged.astype(BF16), wout_ref[...])

    q = _dot(_rmsnorm(h1, nxg_ref[...]).astype(BF16), wq_ref[...])
    dh = d // X_HEADS
    scale = dh ** -0.5
    def mem_heads(ref, bi):
        if len(ref.shape) == 3:
            return [ref[bi, :, hh * dh:(hh + 1) * dh].astype(BF16) for hh in range(X_HEADS)]
        by_head = jnp.transpose(ref[bi], (1, 0, 2)).astype(BF16)
        return [by_head[hh] for hh in range(X_HEADS)]

    items = [(bi, hh) for bi in range(bt) for hh in range(X_HEADS)]
    block = lambda bi, hh: (slice(bi * tt, (bi + 1) * tt), slice(hh * dh, (hh + 1) * dh))
    k_heads = [mem_heads(mk_ref, bi) for bi in range(bt)]
    v_heads = [mem_heads(mv_ref, bi) for bi in range(bt)]
    s = [_dot_nt(q[block(bi, hh)].astype(BF16), k_heads[bi][hh]) * scale for bi, hh in items]
    p = [jnp.exp(x - jnp.max(x, axis=-1, keepdims=True)) for x in s]
    p = [(x / jnp.sum(x, axis=-1, keepdims=True)).astype(BF16) for x in p]
    for (bi, hh), x in zip(items, p):
        att_ref[block(bi, hh)] = _dot(x, v_heads[bi][hh]).astype(BF16)
    h2 = h1 + _dot(att_ref[...], wo_ref[...])
    h_ref[...] = h2.reshape(bt, tt, d)

    u = _rmsnorm(h2, nfg_ref[...])
    u_ref[...] = u.astype(BF16).reshape(bt, tt, d)
    u_hi, u_lo = _split(u)
    logits = (_dot(u_hi, wrh_ref[...]) + _dot(u_hi, wrl_ref[...]) + _dot(u_lo, wrh_ref[...])) + rb_ref[...]
    lane = lax.broadcasted_iota(jnp.int32, (n, LANES), 1)
    lane_f = lane.astype(F32)
    is_group = (lane >= N_EXPERTS) & (lane < N_EXPERTS + N_GROUPS)
    lg = jnp.where(is_group, logits, NEG_BIG)
    g_max = jnp.max(lg, axis=-1, keepdims=True)
    g_idx = jnp.min(jnp.where(lg == g_max, lane_f, 1e9), axis=-1, keepdims=True) - N_EXPERTS
    pg = 1.0 / jnp.sum(jnp.exp(lg - g_max), axis=-1, keepdims=True)
    in_group = (lane < N_EXPERTS) & ((lane >> (EXPERTS_PER_GROUP.bit_length() - 1)).astype(F32) == g_idx)
    le = jnp.where(in_group, logits, NEG_BIG)
    m1 = jnp.max(le, axis=-1, keepdims=True)
    i1 = jnp.min(jnp.where(le == m1, lane_f, 1e9), axis=-1, keepdims=True)
    le2 = jnp.where(lane_f == i1, NEG_BIG, le)
    m2 = jnp.max(le2, axis=-1, keepdims=True)
    i2 = jnp.min(jnp.where(le2 == m2, lane_f, 1e9), axis=-1, keepdims=True)
    e2 = jnp.exp(m2 - m1)
    w1 = pg / (1.0 + e2)
    wf = jnp.where(lane_f == i1, w1, 0.0) + jnp.where(lane_f == i2, w1 * e2, 0.0)
    wf = wf + jnp.where(lane == _GID_LANE, g_idx, 0.0)
    wf_ref[...] = wf.reshape(bt, tt, LANES)


def _attn(x3, oh3, or3, p3, mk, mv, wout, nxg, wq, wo, nfg, wrh, wrl, rb, bt, tt, gate_block0):
    bsz, tlen, d = x3.shape
    kern = functools.partial(_attn_kernel, bt=bt, tt=tt)
    tok = lambda cb: pl.BlockSpec((bt, tt, d), lambda b, t: (b, t, cb))
    mem = pl.BlockSpec((bt,) + mk.shape[1:], lambda b, t: (b,) + (0,) * (mk.ndim - 1))
    full = lambda a: pl.BlockSpec(a.shape, lambda b, t: (0,) * a.ndim, pipeline_mode=pl.Buffered(1))
    nxg, nfg = nxg.reshape(1, d), nfg.reshape(1, d)
    return pl.pallas_call(
        kern,
        grid=(bsz // bt, tlen // tt),
        in_specs=[tok(0), tok(0), tok(0), tok(gate_block0), tok(gate_block0 + 1), mem, mem,
                  full(wout), full(nxg), full(wq), full(wo), full(nfg), full(wrh), full(wrl), full(rb)],
        out_specs=[
            pl.BlockSpec((bt, tt, d), lambda b, t: (b, t, 0)),
            pl.BlockSpec((bt, tt, d), lambda b, t: (b, t, 0)),
            pl.BlockSpec((bt, tt, LANES), lambda b, t: (b, t, 0)),
        ],
        out_shape=[
            jax.ShapeDtypeStruct((bsz, tlen, d), F32),
            jax.ShapeDtypeStruct((bsz, tlen, d), BF16),
            jax.ShapeDtypeStruct((bsz, tlen, LANES), F32),
        ],
        scratch_shapes=[pltpu.VMEM((bt * tt, d), BF16)],
        compiler_params=_cparams(("parallel", "parallel")),
        name="attn",
    )(x3, oh3, or3, p3, p3, mk, mv, wout, nxg, wq, wo, nfg, wrh, wrl, rb)


def _segment_dma(src, dst, src_row, dst_row, length, max_len, sem, start):
    size = _SEG
    while size * 2 <= max_len:
        size *= 2
    while size >= _SEG:
        offset = length & (-2 * size)

        @pl.when((length & size) != 0)
        def _(offset=offset, size=size):
            copy = pltpu.make_async_copy(src.at[pl.ds(pl.multiple_of(src_row + offset, _SEG), size)],
                                         dst.at[pl.ds(pl.multiple_of(dst_row + offset, _SEG), size)], sem)
            if start:
                copy.start()
            else:
                copy.wait()

        size //= 2


def _tile_meta(meta_ref, nt, tile):
    field = lambda k: [meta_ref[(k * nt + tile) * N_GROUPS + g] for g in range(N_GROUPS)]
    return field(0), field(1), field(2)


def _moe_sort_kernel(meta_ref, u_ref, wf_ref, xs_ref, ws_ref, su_ref, sw_ref, zu_ref, zw_ref, sem, *, nt, tm, te):
    i = pl.program_id(0)
    buf = lax.rem(i, 2)
    slots = su_ref.shape[1]
    _, _, base = _tile_meta(meta_ref, nt, i)
    wf = wf_ref[...]
    gid = wf.T[_GID_LANE:_GID_LANE + 1, :]
    onehot = lax.broadcasted_iota(jnp.int32, (8, tm), 0).astype(F32) == gid
    earlier = (lax.broadcasted_iota(jnp.int32, (tm, tm), 0)
               < lax.broadcasted_iota(jnp.int32, (tm, tm), 1)).astype(BF16)
    seen = _dot(onehot.astype(BF16), earlier)
    pos = jnp.sum(jnp.where(onehot, seen, 0.0), axis=0, keepdims=True)
    for g in range(N_GROUPS):
        pos = pos + jnp.where(gid == float(g), base[g].astype(F32), 0.0)
    perm = (lax.broadcasted_iota(jnp.int32, (slots, tm), 0).astype(F32) == pos).astype(BF16)
    su_ref[buf] = _dot(perm, u_ref[...]).astype(BF16)
    w_hi, w_lo = _split(wf)
    sw_ref[buf] = _dot(perm, w_hi) + _dot(perm, w_lo)

    def copies(tile, b, start):
        off, cnt, base = _tile_meta(meta_ref, nt, tile)
        for g in range(N_GROUPS):
            _segment_dma(su_ref.at[b], xs_ref, base[g], off[g], cnt[g], tm, sem.at[0, b], start)
            _segment_dma(sw_ref.at[b], ws_ref, base[g], off[g], cnt[g], tm, sem.at[1, b], start)

    copies(i, buf, True)

    @pl.when(i > 0)
    def _():
        copies(i - 1, 1 - buf, False)

    @pl.when(i == nt - 1)
    def _():
        copies(i, buf, False)
        zu_ref[...] = jnp.zeros_like(zu_ref)
        zw_ref[...] = jnp.zeros_like(zw_ref)
        tail = 3 * nt * N_GROUPS + nt
        for start in (True, False):
            for g in range(N_GROUPS):
                off, cnt = meta_ref[tail + g], meta_ref[tail + N_GROUPS + g]
                _segment_dma(zu_ref, xs_ref, 0, off, cnt, te, sem.at[0, 0], start)
                _segment_dma(zw_ref, ws_ref, 0, off, cnt, te, sem.at[1, 0], start)

            def unused(r, carry, start=start):
                row = pl.multiple_of(r * te, te)
                for src, dst, s in ((zu_ref, xs_ref, sem.at[0, 1]), (zw_ref, ws_ref, sem.at[1, 1])):
                    copy = pltpu.make_async_copy(src, dst.at[pl.ds(row, te)], s)
                    if start:
                        copy.start()
                    else:
                        copy.wait()
                return carry

            lax.fori_loop(meta_ref[tail + 2 * N_GROUPS], xs_ref.shape[0] // te, unused, 0)


def _moe_expert_kernel(meta_ref, xs_ref, ws_ref, wg_ref, wu_ref, wd_ref, ys_ref, *, n_tiles):
    r = pl.program_id(0)
    used = r < meta_ref[n_tiles]

    @pl.when(used)
    def _():
        first = meta_ref[r] * EXPERTS_PER_GROUP
        x = xs_ref[...]
        ws = ws_ref[...]
        lane = lax.broadcasted_iota(jnp.int32, ws.shape, 1)
        hid = []
        for e in range(EXPERTS_PER_GROUP):
            h = _silu(_dot(x, wg_ref[e])) * _dot(x, wu_ref[e])
            w_col = jnp.sum(jnp.where(lane == first + e, ws, 0.0), axis=-1, keepdims=True)
            hid.append((h * w_col).astype(BF16))
        ys_ref[...] = _dot(jnp.concatenate(hid, axis=1), wd_ref[...]).astype(ys_ref.dtype)

    @pl.when(jnp.logical_not(used))
    def _():
        ys_ref[...] = jnp.zeros_like(ys_ref)


def _moe_unsort_kernel(meta_ref, wf_ref, h_ref, gf_ref, ys_ref, y_ref, sy_ref, sem, *, nt, tm):
    i = pl.program_id(0)
    buf = lax.rem(i, 2)
    slots = sy_ref.shape[1]

    def copies(tile, b, start):
        off, cnt, base = _tile_meta(meta_ref, nt, tile)
        for g in range(N_GROUPS):
            _segment_dma(ys_ref, sy_ref.at[b], off[g], base[g], cnt[g], tm, sem.at[b], start)

    @pl.when(i == 0)
    def _():
        sy_ref[...] = jnp.zeros_like(sy_ref)
        copies(0, 0, True)

    @pl.when(i + 1 < nt)
    def _():
        copies(i + 1, 1 - buf, True)

    _, _, base = _tile_meta(meta_ref, nt, i)
    gid = wf_ref[...][:, _GID_LANE:_GID_LANE + 1]
    onehot = lax.broadcasted_iota(jnp.int32, (tm, LANES), 1).astype(F32) == gid
    earlier = (lax.broadcasted_iota(jnp.int32, (tm, tm), 1)
               < lax.broadcasted_iota(jnp.int32, (tm, tm), 0)).astype(BF16)
    seen = _dot(earlier, onehot.astype(BF16))
    pos = jnp.sum(jnp.where(onehot, seen, 0.0), axis=1, keepdims=True)
    for g in range(N_GROUPS):
        pos = pos + jnp.where(gid == float(g), base[g].astype(F32), 0.0)
    perm_t = (lax.broadcasted_iota(jnp.int32, (tm, slots), 1).astype(F32) == pos).astype(BF16)
    copies(i, buf, False)
    slot_row = lax.broadcasted_iota(jnp.int32, (slots, 1), 0)
    ys = sy_ref[buf]
    moe = _dot(perm_t, jnp.where(slot_row < meta_ref[3 * nt * N_GROUPS + i], ys, jnp.zeros_like(ys)))
    y_ref[...] = _rmsnorm(h_ref[...] + moe, gf_ref[...])


def _moe(u, wf, h, wg, wu, wd, gf, tm, te):
    n, d = u.shape
    ne, _, f = wg.shape
    nt = n // tm
    slots = tm + LANES
    n_tiles = -(-(n + nt * N_GROUPS * (_SEG - 1) + N_GROUPS * (te - 1)) // te)
    rows = n_tiles * te
    i32 = jnp.int32

    gid = wf[:, _GID_LANE].astype(i32).reshape(nt, tm)
    cnt = jnp.sum((gid[:, :, None] == jnp.arange(N_GROUPS, dtype=i32)).astype(i32), axis=1)
    cnt = (cnt + _SEG - 1) // _SEG * _SEG
    base = jnp.cumsum(cnt, axis=1) - cnt
    total = jnp.sum(cnt, axis=0)
    region = (total + te - 1) // te * te
    region_end = jnp.cumsum(region)
    region_start = region_end - region
    off = region_start[None, :] + jnp.cumsum(cnt, axis=0) - cnt
    n_used = region_end[-1:] // te
    meta = jnp.concatenate([off.reshape(-1), cnt.reshape(-1), base.reshape(-1), jnp.sum(cnt, axis=1),
                            region_start + total, region - total, n_used]).astype(i32)
    tile_row = jnp.arange(n_tiles, dtype=i32) * te
    tile_group = jnp.minimum(jnp.sum((tile_row[:, None] >= region_end[None, :]).astype(i32), axis=1), N_GROUPS - 1)
    emeta = jnp.concatenate([tile_group, n_used]).astype(i32)

    anyspec = pl.BlockSpec(memory_space=pl.ANY)
    tok = lambda w: pl.BlockSpec((tm, w), lambda i, m: (i, 0))
    xs, ws = pl.pallas_call(
        functools.partial(_moe_sort_kernel, nt=nt, tm=tm, te=te),
        grid_spec=pltpu.PrefetchScalarGridSpec(
            num_scalar_prefetch=1, grid=(nt,),
            in_specs=[tok(d), tok(LANES)],
            out_specs=[anyspec, anyspec],
            scratch_shapes=[pltpu.VMEM((2, slots, d), BF16), pltpu.VMEM((2, slots, LANES), F32),
                            pltpu.VMEM((te, d), BF16), pltpu.VMEM((te, LANES), F32),
                            pltpu.SemaphoreType.DMA((2, 2))]),
        out_shape=[jax.ShapeDtypeStruct((rows, d), BF16), jax.ShapeDtypeStruct((rows, LANES), F32)],
        compiler_params=_cparams(("arbitrary",)),
        name="moe_sort",
    )(meta, u, wf)

    srt = lambda w: pl.BlockSpec((te, w), lambda r, m: (r, 0))
    grp_w = pl.BlockSpec((EXPERTS_PER_GROUP, d, f), lambda r, m: (m[r], 0, 0))
    ys = pl.pallas_call(
        functools.partial(_moe_expert_kernel, n_tiles=n_tiles),
        grid_spec=pltpu.PrefetchScalarGridSpec(
            num_scalar_prefetch=1, grid=(n_tiles,),
            in_specs=[srt(d), srt(LANES), grp_w, grp_w,
                      pl.BlockSpec((EXPERTS_PER_GROUP * f, d), lambda r, m: (m[r], 0))],
            out_specs=srt(d)),
        out_shape=jax.ShapeDtypeStruct((rows, d), BF16),
        compiler_params=_cparams(("parallel",)),
        name="moe_expert",
    )(emeta, xs, ws, wg, wu, wd.reshape(ne * f, d))

    return pl.pallas_call(
        functools.partial(_moe_unsort_kernel, nt=nt, tm=tm),
        grid_spec=pltpu.PrefetchScalarGridSpec(
            num_scalar_prefetch=1, grid=(nt,),
            in_specs=[tok(LANES), tok(d), pl.BlockSpec((1, d), lambda i, m: (0, 0)), anyspec],
            out_specs=tok(d),
            scratch_shapes=[pltpu.VMEM((2, slots, d), BF16), pltpu.SemaphoreType.DMA((2,))]),
        out_shape=jax.ShapeDtypeStruct((n, d), F32),
        compiler_params=_cparams(("arbitrary",)),
        name="moe_unsort",
    )(meta, wf, h, gf.reshape(1, d), ys)


def _pick(n, pref):
    t = min(n, pref)
    while n % t:
        t -= 8
    return t


def kernel(x_prompt, x_sample, mem_prompt, cache_mem_k, cache_mem_v, state_hgrn, state_rwkv, state_rwkv_shift, norm_mix_g, w_in, hgrn_lower_bounds, hgrn_norm_g, rwkv_mu, rwkv_w0, rwkv_w2, rwkv_a0, rwkv_a2, rwkv_g2, rwkv_k_k, rwkv_k_a, rwkv_r_k, rwkv_ln_g, rwkv_ln_b, w_out, norm_x_g, norm_mem_g, wq_x, wk_x, wv_x, wo_x, norm_ffn_g, router_group_w, router_group_b, router_expert_w, router_expert_b, expert_w_gate, expert_w_up, expert_w_down, norm_final_g):
    assert w_in.shape[0] == 1, "single-layer configuration"
    d = x_prompt.shape[-1]
    n_lora_w, n_lora_a, n_lora_g = rwkv_w2.shape[1], rwkv_a2.shape[1], rwkv_g2.shape[1]
    assert n_lora_w + n_lora_a == LANES and LANES < n_lora_g <= 2 * LANES
    n_lora = n_lora_w + n_lora_a + n_lora_g
    n_rwkv_in = 3 * d + n_lora
    lora_pad = 3 * LANES - n_lora

    pr0 = 4 * d
    gate0 = pr0 + n_rwkv_in
    n_cols = w_in.shape[2] + lora_pad
    w_perm = _permute_columns(w_in, [(0, pr0 + 3 * d), (gate0, 2 * d), (pr0 + 3 * d, n_lora)], n_cols,
                              _pick(d, 128))
    lora_col0 = 9 * d
    gate_block0 = 7
    mu = rwkv_mu[0]
    mul = jnp.pad(mu[3 * d:], (0, lora_pad)).reshape(1, 3 * LANES)
    par = jnp.stack([mu[:d], mu[d:2 * d], mu[2 * d:3 * d], rwkv_w0[0], rwkv_a0[0], rwkv_k_k[0], rwkv_k_a[0],
                     rwkv_r_k[0].reshape(d), rwkv_ln_g[0], rwkv_ln_b[0]])
    par = jnp.pad(par, ((0, _P_ROWS - par.shape[0]), (0, 0)))
    w2p = jnp.pad(rwkv_w2[0], ((0, n_lora_a), (0, 0))).astype(BF16)
    a2p = jnp.pad(rwkv_a2[0], ((n_lora_w, 0), (0, 0))).astype(BF16)
    g2a = rwkv_g2[0][:LANES].astype(BF16)
    g2b = jnp.pad(rwkv_g2[0][LANES:], ((0, 2 * LANES - n_lora_g), (0, 0))).astype(BF16)
    wout_b, wq_b, wo_b = w_out[0].astype(BF16), wq_x[0].astype(BF16), wo_x[0].astype(BF16)
    wr = jnp.concatenate([router_expert_w[0], router_group_w[0],
                          jnp.zeros((d, LANES - N_EXPERTS - N_GROUPS), F32)], axis=1)
    wr_hi, wr_lo = _split(wr)
    rb = jnp.pad(jnp.concatenate([router_expert_b[0].reshape(-1), router_group_b[0]]),
                 (0, LANES - N_EXPERTS - N_GROUPS)).reshape(1, LANES)
    wg_b, wu_b, wd_b = (expert_w_gate[0].astype(BF16), expert_w_up[0].astype(BF16),
                        expert_w_down[0].astype(BF16))

    nb_p = mem_prompt.shape[0]
    mem_k_p, mem_v_p = _mem_kv(mem_prompt, norm_mem_g[0], wk_x[0].astype(BF16), wv_x[0].astype(BF16))

    def group(x3, mem_k, mem_v, s_h, s_r, buf, tile_h, tile_r, chunk_r, bt, tt, te):
        bsz, tlen, _ = x3.shape
        n = bsz * tlen
        p = _norm_matmul(x3.reshape(n, d), norm_mix_g[0], w_perm, _pick(n, 1024), n_cols // 3)
        p3 = p.reshape(bsz, tlen, n_cols)
        o_h, s_h_new = _hgrn(p3, hgrn_lower_bounds, hgrn_norm_g[0], s_h, tile_h, 32)
        buf_rkv = buf[:, :, :3 * d]
        buf_l = jnp.pad(buf[:, :, 3 * d:], ((0, 0), (0, 0), (0, lora_pad)))
        o_r, s_r_new = _rwkv(p3, buf_rkv, buf_l, par, mul, w2p, a2p, g2a, g2b, s_r, d, tile_r, chunk_r)
        h2, u, wf = _attn(x3, o_h, o_r, p3, mem_k, mem_v, wout_b, norm_x_g[0], wq_b, wo_b, norm_ffn_g[0],
                          wr_hi, wr_lo, rb, bt, tt, gate_block0)
        y = _moe(u.reshape(n, d), wf.reshape(n, LANES), h2.reshape(n, d), wg_b, wu_b, wd_b, norm_final_g,
                 _pick(n, 512), te)
        shift = jnp.concatenate([p3[:, -1:, pr0:pr0 + 3 * d], p3[:, -1:, lora_col0:lora_col0 + n_lora]], axis=-1)
        return y.reshape(bsz, tlen, d), s_h_new, s_r_new, shift

    nb_s, t_s, _ = x_sample.shape
    t_p = x_prompt.shape[1]
    zeros_like_state = lambda s: jnp.zeros((nb_p,) + s.shape[2:], s.dtype)
    y_p, sh_p, sr_p, bf_p = group(
        x_prompt, mem_k_p, mem_v_p, zeros_like_state(state_hgrn), zeros_like_state(state_rwkv),
        zeros_like_state(state_rwkv_shift), _pick(t_p, 512), _pick(t_p, 256), 64, 1, _pick(t_p, 512), 512)
    y_s, sh_s, sr_s, bf_s = group(
        x_sample, cache_mem_k[0], cache_mem_v[0],
        state_hgrn[0], state_rwkv[0], state_rwkv_shift[0], t_s, t_s, t_s, _pick(nb_s, 4), t_s, 256)

    kv_shape = (1,) + mem_k_p.shape[:2] + cache_mem_k.shape[3:]
    return (y_p, y_s, mem_k_p.reshape(kv_shape), mem_v_p.reshape(kv_shape), sh_p[None], sh_s[None],
            sr_p[None], sr_s[None], bf_p[None], bf_s[None])
```

```python
import functools
import math

import jax
import jax.numpy as jnp
from jax import lax
from jax.experimental import pallas as pl
from jax.experimental.pallas import tpu as pltpu

F32 = jnp.float32
BF16 = jnp.bfloat16

NORM_EPS = 1e-6
RWKV_GN_EPS = 64e-5
LANES = 128
HGRN_HEAD = 128
RWKV_HEAD = 64
X_HEADS = 4
N_GROUPS = 4
EXPERTS_PER_GROUP = 8
N_EXPERTS = N_GROUPS * EXPERTS_PER_GROUP
NEG_BIG = -1e30
VMEM_LIMIT = 52 * 1024 * 1024
_GID_LANE = N_EXPERTS
_SEG = 16


def _dot(a, b):
    return jnp.dot(a, b, preferred_element_type=F32)


def _dot_nt(a, b):
    return lax.dot_general(a, b, (((1,), (1,)), ((), ())), preferred_element_type=F32)


def _dot_tn(a, b):
    return lax.dot_general(a, b, (((0,), (0,)), ((), ())), preferred_element_type=F32)


def _split(x):
    hi = x.astype(BF16)
    lo = (x - hi.astype(F32)).astype(BF16)
    return hi, lo


def _sigmoid(x):
    return 0.5 * jnp.tanh(0.5 * x) + 0.5


def _silu(x):
    return x * _sigmoid(x)


def _rmsnorm(x, g):
    return x * lax.rsqrt(jnp.mean(x * x, axis=-1, keepdims=True) + NORM_EPS) * g


def _cparams(sem):
    return pltpu.CompilerParams(dimension_semantics=sem, vmem_limit_bytes=VMEM_LIMIT)


def _norm_matmul_kernel(x_ref, g_ref, w_ref, o_ref, u_ref):
    @pl.when(pl.program_id(1) == 0)
    def _():
        u_ref[...] = _rmsnorm(x_ref[...], g_ref[...]).astype(BF16)

    o_ref[...] = _dot(u_ref[...], w_ref[...])


def _norm_matmul(x, g, w, tm, tn):
    n, d = x.shape
    m = w.shape[1]
    return pl.pallas_call(
        _norm_matmul_kernel,
        grid=(n // tm, m // tn),
        in_specs=[
            pl.BlockSpec((tm, d), lambda i, j: (i, 0)),
            pl.BlockSpec((1, d), lambda i, j: (0, 0)),
            pl.BlockSpec((d, tn), lambda i, j: (0, j)),
        ],
        out_specs=pl.BlockSpec((tm, tn), lambda i, j: (i, j)),
        out_shape=jax.ShapeDtypeStruct((n, m), F32),
        scratch_shapes=[pltpu.VMEM((tm, d), BF16)],
        compiler_params=_cparams(("parallel", "arbitrary")),
        name="norm_matmul",
    )(x, g.reshape(1, d), w)


def _permute_columns_kernel(w_ref, o_ref, *, pieces):
    col = 0
    for src, width in pieces:
        o_ref[:, col:col + width] = w_ref[:, src:src + width].astype(o_ref.dtype)
        col += width
    if col < o_ref.shape[1]:
        o_ref[:, col:] = jnp.zeros((o_ref.shape[0], o_ref.shape[1] - col), o_ref.dtype)


def _permute_columns(w, pieces, n_cols, tr):
    _, rows, cols = w.shape
    return pl.pallas_call(
        functools.partial(_permute_columns_kernel, pieces=pieces),
        grid=(rows // tr,),
        in_specs=[pl.BlockSpec((None, tr, cols), lambda i: (0, i, 0))],
        out_specs=pl.BlockSpec((tr, n_cols), lambda i: (i, 0)),
        out_shape=jax.ShapeDtypeStruct((rows, n_cols), BF16),
        compiler_params=_cparams(("parallel",)),
        name="permute_columns",
    )(w)


def _mem_kv_kernel(m_ref, g_ref, wk_ref, wv_ref, k_ref, v_ref):
    u = _rmsnorm(m_ref[...], g_ref[...]).astype(BF16)
    k_ref[...] = _dot(u, wk_ref[...])
    v_ref[...] = _dot(u, wv_ref[...])


def _mem_kv(mem, g, wk, wv):
    bsz, nmem, d = mem.shape
    full = lambda a: pl.BlockSpec(a.shape, lambda b: (0,) * a.ndim)
    out = pl.BlockSpec((None, nmem, d), lambda b: (b, 0, 0))
    g = g.reshape(1, d)
    return pl.pallas_call(
        _mem_kv_kernel,
        grid=(bsz,),
        in_specs=[pl.BlockSpec((None, nmem, d), lambda b: (b, 0, 0)), full(g), full(wk), full(wv)],
        out_specs=[out, out],
        out_shape=[jax.ShapeDtypeStruct((bsz, nmem, d), F32)] * 2,
        compiler_params=_cparams(("parallel",)),
        name="mem_kv",
    )(mem, g, wk, wv)


def _hgrn_kernel(q_ref, f_ref, i_ref, g_ref, lbp_ref, ng_ref, s0_ref, o_ref, sout_ref, st_ref,
                 *, chunk, n_chunks, heads):
    t = pl.program_id(1)

    @pl.when(t == 0)
    def _():
        for h in range(heads):
            st_ref[h] = s0_ref[h].T

    lbp = lbp_ref[...]
    e = jnp.exp(lbp - jnp.max(lbp, axis=0, keepdims=True))
    lb = e[0:1] / jnp.sum(e, axis=0, keepdims=True)
    ng = ng_ref[...]

    row = lax.broadcasted_iota(jnp.int32, (chunk, chunk), 0)
    col = lax.broadcasted_iota(jnp.int32, (chunk, chunk), 1)
    causal = row >= col
    mid = chunk // 2
    to_mid = jnp.where((col >= mid) & (col <= row), 1.0, 0.0) - jnp.where((col > row) & (col < mid), 1.0, 0.0)
    coef = jnp.concatenate([causal.astype(F32), to_mid, (col > row).astype(F32)], axis=0).astype(BF16)

    chunks = range(n_chunks)
    qd, kd, qi, kl, el, vv, gate = [], [], [], [], [], [], []
    for c in chunks:
        rows = slice(c * chunk, (c + 1) * chunk)
        f = lb + (1.0 - lb) * _sigmoid(f_ref[rows, :])
        g_hi, g_lo = _split(jnp.log(f))
        bb = _dot(coef, g_hi) + _dot(coef, g_lo)
        b, b_to_mid, b_to_last = bb[:chunk], bb[chunk:2 * chunk], bb[2 * chunk:]
        q = _silu(q_ref[rows, :])
        k = 1.0 - f
        qd.append((q * jnp.exp(b_to_mid)).astype(BF16))
        kd.append((k * jnp.exp(-b_to_mid)).astype(BF16))
        qi.append((q * jnp.exp(b)).astype(BF16))
        kl.append((k * jnp.exp(b_to_last)).astype(BF16))
        el.append(jnp.exp(b[chunk - 1:chunk, :]))
        vv.append(i_ref[rows, :].astype(BF16))
        gate.append(_silu(g_ref[rows, :]))

    head = lambda x, h: x[:, h * HGRN_HEAD:(h + 1) * HGRN_HEAD]
    items = [(c, h) for c in chunks for h in range(heads)]
    sc = {ch: jnp.where(causal, _dot_nt(head(qd[ch[0]], ch[1]), head(kd[ch[0]], ch[1])), 0.0).astype(BF16)
          for ch in items}
    intra = {ch: _dot(sc[ch], head(vv[ch[0]], ch[1])) for ch in items}
    upd = {ch: _dot_tn(head(vv[ch[0]], ch[1]), head(kl[ch[0]], ch[1])) for ch in items}
    before = {}
    for h in range(heads):
        st = st_ref[h]
        for c in chunks:
            before[c, h] = st.astype(BF16)
            st = st * head(el[c], h) + upd[c, h]
        st_ref[h] = st
    out = {ch: intra[ch] + _dot_nt(head(qi[ch[0]], ch[1]), before[ch]) for ch in items}
    out = {ch: o * lax.rsqrt(jnp.mean(o * o, axis=-1, keepdims=True) + NORM_EPS) for ch, o in out.items()}
    for (c, h), o in out.items():
        o_ref[c * chunk:(c + 1) * chunk, h * HGRN_HEAD:(h + 1) * HGRN_HEAD] = o * ng * head(gate[c], h)

    @pl.when(t == pl.num_programs(1) - 1)
    def _():
        for h in range(heads):
            sout_ref[h] = st_ref[h].T


def _hgrn(p3, lbp, ng, s0, tile, chunk):
    bsz, tlen, _ = p3.shape
    heads, dk, dv = s0.shape[1:]
    d = heads * HGRN_HEAD
    kern = functools.partial(_hgrn_kernel, chunk=chunk, n_chunks=tile // chunk, heads=heads)
    pspec = lambda cb: pl.BlockSpec((None, tile, d), lambda b, t: (b, t, cb))
    return pl.pallas_call(
        kern,
        grid=(bsz, tlen // tile),
        in_specs=[
            pspec(0), pspec(1), pspec(2), pspec(3),
            pl.BlockSpec(lbp.shape, lambda b, t: (0, 0)),
            pl.BlockSpec((1, dv), lambda b, t: (0, 0)),
            pl.BlockSpec((None, heads, dk, dv), lambda b, t: (b, 0, 0, 0)),
        ],
        out_specs=[
            pl.BlockSpec((None, tile, d), lambda b, t: (b, t, 0)),
            pl.BlockSpec((None, heads, dk, dv), lambda b, t: (b, 0, 0, 0)),
        ],
        out_shape=[
            jax.ShapeDtypeStruct((bsz, tlen, d), F32),
            jax.ShapeDtypeStruct((bsz, heads, dk, dv), F32),
        ],
        scratch_shapes=[pltpu.VMEM((heads, dv, dk), F32)],
        compiler_params=_cparams(("parallel", "arbitrary")),
        name="hgrn",
    )(p3, p3, p3, p3, lbp, ng.reshape(1, dv), s0)


_P_MU_R, _P_MU_K, _P_MU_V, _P_W0, _P_A0, _P_KK, _P_KA, _P_RK, _P_LNG, _P_LNB = range(10)
_P_ROWS = 16


def _rwkv_kernel(r_ref, k_ref, v_ref, l_ref, br_ref, bk_ref, bv_ref, bl_ref, par_ref, mul_ref,
                 w2_ref, a2_ref, g2a_ref, g2b_ref, s0_ref,
                 o_ref, sout_ref, st_ref, prev_ref, prevl_ref, *, chunk, n_chunks, pairs):
    t = pl.program_id(1)
    tile = chunk * n_chunks

    @pl.when(t == 0)
    def _():
        zero = jnp.zeros((RWKV_HEAD, RWKV_HEAD), F32)
        for j in range(pairs):
            st_ref[j] = jnp.concatenate([jnp.concatenate([s0_ref[2 * j], zero], axis=1),
                                         jnp.concatenate([zero, s0_ref[2 * j + 1]], axis=1)], axis=0)
        prev_ref[0:1, :] = br_ref[...]
        prev_ref[1:2, :] = bk_ref[...]
        prev_ref[2:3, :] = bv_ref[...]
        prevl_ref[0:1, :] = bl_ref[...]

    par = par_ref[...]
    prow = lambda i: par[i:i + 1, :]

    def shifted(x, prev_row, mu):
        row_id = lax.broadcasted_iota(jnp.int32, x.shape, 0)
        prev = jnp.where(row_id == 0, prev_row, pltpu.roll(x, 1, 0))
        return x + (prev - x) * mu

    pr, pk, pv, plr = r_ref[...], k_ref[...], v_ref[...], l_ref[...]
    r = shifted(pr, prev_ref[0:1, :], prow(_P_MU_R))
    k = shifted(pk, prev_ref[1:2, :], prow(_P_MU_K))
    v = shifted(pv, prev_ref[2:3, :], prow(_P_MU_V))
    xl = shifted(plr, prevl_ref[0:1, :], mul_ref[...])
    for slot, x in enumerate((pr, pk, pv)):
        prev_ref[slot:slot + 1, :] = x[tile - 1:tile, :]
    prevl_ref[0:1, :] = plr[tile - 1:tile, :]
    x0 = xl[:, 0:LANES]
    x1 = xl[:, LANES:2 * LANES]
    x2 = xl[:, 2 * LANES:3 * LANES]

    w_lin = prow(_P_W0) + _dot(jnp.tanh(x0).astype(BF16), w2_ref[...])
    z = -w_lin
    softplus = jnp.maximum(z, 0.0) + jnp.log(1.0 + jnp.exp(-jnp.abs(z)))
    lw = -jnp.exp(-softplus - 0.5)
    a_icl = _sigmoid(prow(_P_A0) + _dot(x0.astype(BF16), a2_ref[...]))
    g_out = _dot(_sigmoid(x1).astype(BF16), g2a_ref[...]) + _dot(_sigmoid(x2).astype(BF16), g2b_ref[...])

    lane = lax.broadcasted_iota(jnp.int32, (1, LANES), 1)
    m0 = (lane < RWKV_HEAD).astype(F32)
    m1 = 1.0 - m0
    head_shift = RWKV_HEAD.bit_length() - 1
    hr = lax.broadcasted_iota(jnp.int32, (LANES, LANES), 0) >> head_shift
    hc = lax.broadcasted_iota(jnp.int32, (LANES, LANES), 1) >> head_shift
    head_ones = (hr == hc).astype(BF16)
    tile_of = lambda x, j: x[:, j * LANES:(j + 1) * LANES]

    def headsum(x):
        xs = jnp.concatenate([tile_of(x, j) for j in range(pairs)], axis=0)
        ys = _dot(xs.astype(BF16), head_ones)
        return jnp.concatenate([ys[j * tile:(j + 1) * tile] for j in range(pairs)], axis=1)

    kk = k * prow(_P_KK)
    kk = kk * lax.rsqrt(jnp.maximum(headsum(kk * kk), 1e-24))
    k_mod = k * (1.0 + (a_icl - 1.0) * prow(_P_KA))
    a = -kk
    b = kk * a_icl

    ti = lax.broadcasted_iota(jnp.int32, (chunk, chunk), 0)
    tj = lax.broadcasted_iota(jnp.int32, (chunk, chunk), 1)
    tri = (ti >= tj).astype(BF16)
    n2 = 2 * chunk
    si = lax.broadcasted_iota(jnp.int32, (n2, n2), 0)
    sj = lax.broadcasted_iota(jnp.int32, (n2, n2), 1)
    chunk_shift = chunk.bit_length() - 1
    same_head = (si >> chunk_shift) == (sj >> chunk_shift)
    strict = same_head & ((sj & (chunk - 1)) < (si & (chunk - 1)))
    incl = same_head & ((sj & (chunk - 1)) <= (si & (chunk - 1)))
    eye = (si == sj).astype(F32)
    m0_b, m1_b = m0.astype(BF16), m1.astype(BF16)
    tile_b = lambda x, j: tile_of(x, j).astype(BF16)
    twice = lambda x: jnp.concatenate([x, x], axis=0)
    every = range(pairs)

    def stack_b(x, j):
        xb = tile_b(x, j)
        return jnp.concatenate([xb * m0_b, xb * m1_b], axis=0)

    def decayed(rows):
        lw_c, a_c, b_c, k_c = lw[rows], a[rows], b[rows], k_mod[rows]
        lw_hi, lw_lo = _split(lw_c)
        cum = _dot(tri, lw_hi) + _dot(tri, lw_lo)
        cum_last = cum[chunk - 1:chunk, :]
        e_neg = jnp.exp(-cum)
        e_end = jnp.exp(cum_last - cum)
        return dict(a_t=a_c * jnp.exp(cum - lw_c), b_t=b_c * e_neg, k_t=k_c * e_neg, r_t=r[rows] * jnp.exp(cum),
                    b_end=b_c * e_end, k_end=k_c * e_end, v=v[rows], g_last=jnp.exp(cum_last))

    ops = [decayed(slice(c * chunk, (c + 1) * chunk)) for c in range(n_chunks)]
    items = [(c, j) for c in range(n_chunks) for j in every]
    idx = range(len(items))
    stacked = lambda name: [stack_b(ops[c][name], j) for c, j in items]
    a_st, r_st, v_st, bend_st, kend_st = (stacked(name) for name in ("a_t", "r_t", "v", "b_end", "k_end"))
    sc = [_dot_nt(jnp.concatenate([tile_b(ops[c]["a_t"], j), tile_b(ops[c]["r_t"], j)], axis=0),
                  jnp.concatenate([stack_b(ops[c]["b_t"], j), stack_b(ops[c]["k_t"], j)], axis=0)) for c, j in items]
    a_ab = [jnp.where(strict, twice(s[:chunk, :n2]), 0.0) for s in sc]
    a_xk = [jnp.concatenate([jnp.where(strict, twice(s[:chunk, n2:]), 0.0),
                             jnp.where(incl, twice(s[chunk:, n2:]), 0.0)], axis=0).astype(BF16) for s in sc]
    a_rb = [jnp.where(incl, twice(s[chunk:, :n2]), 0.0).astype(BF16) for s in sc]

    inv = [eye + x for x in a_ab]
    pw = [_dot(x.astype(BF16), x.astype(BF16)) for x in a_ab]
    for _ in range(int(math.log2(chunk)) - 2):
        if n2 % LANES == 0:
            both = [_dot(p.astype(BF16), jnp.concatenate([p, i], axis=1).astype(BF16)) for p, i in zip(pw, inv)]
            pw_next = [x[:, :n2] for x in both]
            inv = [i + x[:, n2:] for i, x in zip(inv, both)]
        else:
            pw_next = [_dot(p.astype(BF16), p.astype(BF16)) for p in pw]
            inv = [i + _dot(p.astype(BF16), i.astype(BF16)) for i, p in zip(inv, pw)]
        pw = pw_next
    inv = [i + _dot(p.astype(BF16), i.astype(BF16)) for i, p in zip(inv, pw)]

    av = [_dot(a_xk[i], v_st[i]) for i in idx]
    wu = [_dot(inv[i].astype(BF16), jnp.concatenate([a_st[i], av[i][:n2].astype(BF16)], axis=1)).astype(BF16)
          for i in idx]
    qo = [_dot(a_rb[i], wu[i]) + jnp.concatenate([r_st[i].astype(F32), av[i][n2:]], axis=1) for i in idx]
    mn = [_dot_tn(wu[i], bend_st[i]) for i in idx]
    nn = [mn[i][LANES:] + _dot_tn(v_st[i], kend_st[i]) for i in idx]

    outs = []
    for c in range(n_chunks):
        row = []
        for j in every:
            i = c * pairs + j
            s0 = st_ref[j]
            s0_b = s0.astype(BF16)
            q = (qo[i][:chunk, :LANES] + qo[i][chunk:, :LANES]).astype(BF16)
            row.append(_dot_nt(q, s0_b) + qo[i][:chunk, LANES:] + qo[i][chunk:, LANES:])
            st_ref[j] = s0 * tile_of(ops[c]["g_last"], j) + _dot(s0_b, mn[i][:LANES].astype(BF16)) + nn[i]
        outs.append(jnp.concatenate(row, axis=1))
    o = outs[0] if n_chunks == 1 else jnp.concatenate(outs, axis=0)

    inv_n = 1.0 / RWKV_HEAD
    mean = headsum(o) * inv_n
    dlt = o - mean
    var = headsum(dlt * dlt) * inv_n
    o = dlt * lax.rsqrt(var + RWKV_GN_EPS) * prow(_P_LNG) + prow(_P_LNB)
    bonus = headsum(r * k_mod * prow(_P_RK))
    o_ref[...] = (o + bonus * v) * g_out

    @pl.when(t == pl.num_programs(1) - 1)
    def _():
        for j in range(pairs):
            st = st_ref[j]
            sout_ref[2 * j] = st[:RWKV_HEAD, :RWKV_HEAD]
            sout_ref[2 * j + 1] = st[RWKV_HEAD:, RWKV_HEAD:]


def _rwkv(p3, buf_rkv, buf_l, par, mul, w2p, a2p, g2a, g2b, s0, d, tile, chunk):
    bsz, tlen, _ = p3.shape
    pairs = d // LANES
    nl = 3 * LANES
    kern = functools.partial(_rwkv_kernel, chunk=chunk, n_chunks=tile // chunk, pairs=pairs)
    pcol = lambda cb: pl.BlockSpec((None, tile, d), lambda b, t: (b, t, cb))
    bcol = lambda cb: pl.BlockSpec((None, 1, d), lambda b, t: (b, 0, cb))
    full = lambda a: pl.BlockSpec(a.shape, lambda b, t: (0,) * a.ndim)
    state = pl.BlockSpec((None,) + s0.shape[1:], lambda b, t: (b, 0, 0, 0))
    return pl.pallas_call(
        kern,
        grid=(bsz, tlen // tile),
        in_specs=[
            pcol(4), pcol(5), pcol(6),
            pl.BlockSpec((None, tile, nl), lambda b, t: (b, t, 9 * d // nl)),
            bcol(0), bcol(1), bcol(2),
            pl.BlockSpec((None, 1, nl), lambda b, t: (b, 0, 0)),
            full(par), full(mul), full(w2p), full(a2p), full(g2a), full(g2b), state,
        ],
        out_specs=[pl.BlockSpec((None, tile, d), lambda b, t: (b, t, 0)), state],
        out_shape=[
            jax.ShapeDtypeStruct((bsz, tlen, d), F32),
            jax.ShapeDtypeStruct(s0.shape, F32),
        ],
        scratch_shapes=[pltpu.VMEM((pairs, LANES, LANES), F32), pltpu.VMEM((8, d), F32),
                        pltpu.VMEM((8, nl), F32)],
        compiler_params=_cparams(("parallel", "arbitrary")),
        name="rwkv",
    )(p3, p3, p3, p3, buf_rkv, buf_rkv, buf_rkv, buf_l, par, mul, w2p, a2p, g2a, g2b, s0)


def _attn_kernel(x_ref, oh_ref, or_ref, ga_ref, gb_ref, mk_ref, mv_ref, wout_ref, nxg_ref, wq_ref, wo_ref,
                 nfg_ref, wrh_ref, wrl_ref, rb_ref, h_ref, u_ref, wf_ref, att_ref, *, bt, tt):
    n = bt * tt
    d = x_ref.shape[-1]
    flat = lambda ref: ref[...].reshape(n, d)
    merged = _sigmoid(flat(ga_ref)) * flat(oh_ref) + _sigmoid(flat(gb_ref)) * flat(or_ref)
    h1 = flat(x_ref) + _dot(merged.astype(BF16), wout_ref[...])

    q = _dot(_rmsnorm(h1, nxg_ref[...]).astype(BF16), wq_ref[...])
    dh = d // X_HEADS
    scale = dh ** -0.5
    def mem_heads(ref, bi):
        if len(ref.shape) == 3:
            return [ref[bi, :, hh * dh:(hh + 1) * dh].astype(BF16) for hh in range(X_HEADS)]
        by_head = jnp.transpose(ref[bi], (1, 0, 2)).astype(BF16)
        return [by_head[hh] for hh in range(X_HEADS)]

    items = [(bi, hh) for bi in range(bt) for hh in range(X_HEADS)]
    block = lambda bi, hh: (slice(bi * tt, (bi + 1) * tt), slice(hh * dh, (hh + 1) * dh))
    k_heads = [mem_heads(mk_ref, bi) for bi in range(bt)]
    v_heads = [mem_heads(mv_ref, bi) for bi in range(bt)]
    s = [_dot_nt(q[block(bi, hh)].astype(BF16), k_heads[bi][hh]) * scale for bi, hh in items]
    p = [jnp.exp(x - jnp.max(x, axis=-1, keepdims=True)) for x in s]
    p = [(x / jnp.sum(x, axis=-1, keepdims=True)).astype(BF16) for x in p]
    for (bi, hh), x in zip(items, p):
        att_ref[block(bi, hh)] = _dot(x, v_heads[bi][hh]).astype(BF16)
    h2 = h1 + _dot(att_ref[...], wo_ref[...])
    h_ref[...] = h2.reshape(bt, tt, d)

    u = _rmsnorm(h2, nfg_ref[...])
    u_ref[...] = u.astype(BF16).reshape(bt, tt, d)
    u_hi, u_lo = _split(u)
    logits = (_dot(u_hi, wrh_ref[...]) + _dot(u_hi, wrl_ref[...]) + _dot(u_lo, wrh_ref[...])) + rb_ref[...]
    lane = lax.broadcasted_iota(jnp.int32, (n, LANES), 1)
    lane_f = lane.astype(F32)
    is_group = (lane >= N_EXPERTS) & (lane < N_EXPERTS + N_GROUPS)
    lg = jnp.where(is_group, logits, NEG_BIG)
    g_max = jnp.max(lg, axis=-1, keepdims=True)
    g_idx = jnp.min(jnp.where(lg == g_max, lane_f, 1e9), axis=-1, keepdims=True) - N_EXPERTS
    pg = 1.0 / jnp.sum(jnp.exp(lg - g_max), axis=-1, keepdims=True)
    in_group = (lane < N_EXPERTS) & ((lane >> (EXPERTS_PER_GROUP.bit_length() - 1)).astype(F32) == g_idx)
    le = jnp.where(in_group, logits, NEG_BIG)
    m1 = jnp.max(le, axis=-1, keepdims=True)
    i1 = jnp.min(jnp.where(le == m1, lane_f, 1e9), axis=-1, keepdims=True)
    le2 = jnp.where(lane_f == i1, NEG_BIG, le)
    m2 = jnp.max(le2, axis=-1, keepdims=True)
    i2 = jnp.min(jnp.where(le2 == m2, lane_f, 1e9), axis=-1, keepdims=True)
    e2 = jnp.exp(m2 - m1)
    w1 = pg / (1.0 + e2)
    wf = jnp.where(lane_f == i1, w1, 0.0) + jnp.where(lane_f == i2, w1 * e2, 0.0)
    wf = wf + jnp.where(lane == _GID_LANE, g_idx, 0.0)
    wf_ref[...] = wf.reshape(bt, tt, LANES)


def _attn(x3, oh3, or3, p3, mk, mv, wout, nxg, wq, wo, nfg, wrh, wrl, rb, bt, tt, gate_block0):
    bsz, tlen, d = x3.shape
    kern = functools.partial(_attn_kernel, bt=bt, tt=tt)
    tok = lambda cb: pl.BlockSpec((bt, tt, d), lambda b, t: (b, t, cb))
    mem = pl.BlockSpec((bt,) + mk.shape[1:], lambda b, t: (b,) + (0,) * (mk.ndim - 1))
    full = lambda a: pl.BlockSpec(a.shape, lambda b, t: (0,) * a.ndim, pipeline_mode=pl.Buffered(1))
    nxg, nfg = nxg.reshape(1, d), nfg.reshape(1, d)
    return pl.pallas_call(
        kern,
        grid=(bsz // bt, tlen // tt),
        in_specs=[tok(0), tok(0), tok(0), tok(gate_block0), tok(gate_block0 + 1), mem, mem,
                  full(wout), full(nxg), full(wq), full(wo), full(nfg), full(wrh), full(wrl), full(rb)],
        out_specs=[
            pl.BlockSpec((bt, tt, d), lambda b, t: (b, t, 0)),
            pl.BlockSpec((bt, tt, d), lambda b, t: (b, t, 0)),
            pl.BlockSpec((bt, tt, LANES), lambda b, t: (b, t, 0)),
        ],
        out_shape=[
            jax.ShapeDtypeStruct((bsz, tlen, d), F32),
            jax.ShapeDtypeStruct((bsz, tlen, d), BF16),
            jax.ShapeDtypeStruct((bsz, tlen, LANES), F32),
        ],
        scratch_shapes=[pltpu.VMEM((bt * tt, d), BF16)],
        compiler_params=_cparams(("parallel", "parallel")),
        name="attn",
    )(x3, oh3, or3, p3, p3, mk, mv, wout, nxg, wq, wo, nfg, wrh, wrl, rb)


def _segment_dma(src, dst, src_row, dst_row, length, max_len, sem, start):
    size = _SEG
    while size * 2 <= max_len:
        size *= 2
    while size >= _SEG:
        offset = length & (-2 * size)

        @pl.when((length & size) != 0)
        def _(offset=offset, size=size):
            copy = pltpu.make_async_copy(src.at[pl.ds(pl.multiple_of(src_row + offset, _SEG), size)],
                                         dst.at[pl.ds(pl.multiple_of(dst_row + offset, _SEG), size)], sem)
            if start:
                copy.start()
            else:
                copy.wait()

        size //= 2


def _tile_meta(meta_ref, nt, tile):
    field = lambda k: [meta_ref[(k * nt + tile) * N_GROUPS + g] for g in range(N_GROUPS)]
    return field(0), field(1), field(2)


def _moe_sort_kernel(meta_ref, u_ref, wf_ref, xs_ref, ws_ref, su_ref, sw_ref, zu_ref, zw_ref, sem, *, nt, tm, te):
    i = pl.program_id(0)
    buf = lax.rem(i, 2)
    slots = su_ref.shape[1]
    _, _, base = _tile_meta(meta_ref, nt, i)
    wf = wf_ref[...]
    gid = wf.T[_GID_LANE:_GID_LANE + 1, :]
    onehot = lax.broadcasted_iota(jnp.int32, (8, tm), 0).astype(F32) == gid
    earlier = (lax.broadcasted_iota(jnp.int32, (tm, tm), 0)
               < lax.broadcasted_iota(jnp.int32, (tm, tm), 1)).astype(BF16)
    seen = _dot(onehot.astype(BF16), earlier)
    pos = jnp.sum(jnp.where(onehot, seen, 0.0), axis=0, keepdims=True)
    for g in range(N_GROUPS):
        pos = pos + jnp.where(gid == float(g), base[g].astype(F32), 0.0)
    perm = (lax.broadcasted_iota(jnp.int32, (slots, tm), 0).astype(F32) == pos).astype(BF16)
    su_ref[buf] = _dot(perm, u_ref[...]).astype(BF16)
    w_hi, w_lo = _split(wf)
    sw_ref[buf] = _dot(perm, w_hi) + _dot(perm, w_lo)

    def copies(tile, b, start):
        off, cnt, base = _tile_meta(meta_ref, nt, tile)
        for g in range(N_GROUPS):
            _segment_dma(su_ref.at[b], xs_ref, base[g], off[g], cnt[g], tm, sem.at[0, b], start)
            _segment_dma(sw_ref.at[b], ws_ref, base[g], off[g], cnt[g], tm, sem.at[1, b], start)

    copies(i, buf, True)

    @pl.when(i > 0)
    def _():
        copies(i - 1, 1 - buf, False)

    @pl.when(i == nt - 1)
    def _():
        copies(i, buf, False)
        zu_ref[...] = jnp.zeros_like(zu_ref)
        zw_ref[...] = jnp.zeros_like(zw_ref)
        tail = 3 * nt * N_GROUPS + nt
        for start in (True, False):
            for g in range(N_GROUPS):
                off, cnt = meta_ref[tail + g], meta_ref[tail + N_GROUPS + g]
                _segment_dma(zu_ref, xs_ref, 0, off, cnt, te, sem.at[0, 0], start)
                _segment_dma(zw_ref, ws_ref, 0, off, cnt, te, sem.at[1, 0], start)

            def unused(r, carry, start=start):
                row = pl.multiple_of(r * te, te)
                for src, dst, s in ((zu_ref, xs_ref, sem.at[0, 1]), (zw_ref, ws_ref, sem.at[1, 1])):
                    copy = pltpu.make_async_copy(src, dst.at[pl.ds(row, te)], s)
                    if start:
                        copy.start()
                    else:
                        copy.wait()
                return carry

            lax.fori_loop(meta_ref[tail + 2 * N_GROUPS], xs_ref.shape[0] // te, unused, 0)


def _moe_expert_kernel(meta_ref, xs_ref, ws_ref, wg_ref, wu_ref, wd_ref, ys_ref, *, n_tiles):
    r = pl.program_id(0)
    used = r < meta_ref[n_tiles]

    @pl.when(used)
    def _():
        first = meta_ref[r] * EXPERTS_PER_GROUP
        x = xs_ref[...]
        ws = ws_ref[...]
        lane = lax.broadcasted_iota(jnp.int32, ws.shape, 1)
        experts = range(EXPERTS_PER_GROUP)
        gate = [_dot(x, wg_ref[e]) for e in experts]
        up = [_dot(x, wu_ref[e]) for e in experts]
        w_col = [jnp.sum(jnp.where(lane == first + e, ws, 0.0), axis=-1, keepdims=True) for e in experts]
        hid = [(_silu(g) * u * w).astype(BF16) for g, u, w in zip(gate, up, w_col)]
        ys_ref[...] = _dot(jnp.concatenate(hid, axis=1), wd_ref[...]).astype(ys_ref.dtype)

    @pl.when(jnp.logical_not(used))
    def _():
        ys_ref[...] = jnp.zeros_like(ys_ref)


def _moe_unsort_kernel(meta_ref, wf_ref, h_ref, gf_ref, ys_ref, y_ref, sy_ref, sem, *, nt, tm):
    i = pl.program_id(0)
    buf = lax.rem(i, 2)
    slots = sy_ref.shape[1]

    def copies(tile, b, start):
        off, cnt, base = _tile_meta(meta_ref, nt, tile)
        for g in range(N_GROUPS):
            _segment_dma(ys_ref, sy_ref.at[b], off[g], base[g], cnt[g], tm, sem.at[b], start)

    @pl.when(i == 0)
    def _():
        sy_ref[...] = jnp.zeros_like(sy_ref)
        copies(0, 0, True)

    @pl.when(i + 1 < nt)
    def _():
        copies(i + 1, 1 - buf, True)

    _, _, base = _tile_meta(meta_ref, nt, i)
    gid = wf_ref[...][:, _GID_LANE:_GID_LANE + 1]
    onehot = lax.broadcasted_iota(jnp.int32, (tm, LANES), 1).astype(F32) == gid
    earlier = (lax.broadcasted_iota(jnp.int32, (tm, tm), 1)
               < lax.broadcasted_iota(jnp.int32, (tm, tm), 0)).astype(BF16)
    seen = _dot(earlier, onehot.astype(BF16))
    pos = jnp.sum(jnp.where(onehot, seen, 0.0), axis=1, keepdims=True)
    for g in range(N_GROUPS):
        pos = pos + jnp.where(gid == float(g), base[g].astype(F32), 0.0)
    perm_t = (lax.broadcasted_iota(jnp.int32, (tm, slots), 1).astype(F32) == pos).astype(BF16)
    copies(i, buf, False)
    slot_row = lax.broadcasted_iota(jnp.int32, (slots, 1), 0)
    ys = sy_ref[buf]
    moe = _dot(perm_t, jnp.where(slot_row < meta_ref[3 * nt * N_GROUPS + i], ys, jnp.zeros_like(ys)))
    y_ref[...] = _rmsnorm(h_ref[...] + moe, gf_ref[...])


def _moe(u, wf, h, wg, wu, wd, gf, tm, te):
    n, d = u.shape
    ne, _, f = wg.shape
    nt = n // tm
    slots = tm + LANES
    n_tiles = -(-(n + nt * N_GROUPS * (_SEG - 1) + N_GROUPS * (te - 1)) // te)
    rows = n_tiles * te
    i32 = jnp.int32

    gid = wf[:, _GID_LANE].astype(i32).reshape(nt, tm)
    cnt = jnp.sum((gid[:, :, None] == jnp.arange(N_GROUPS, dtype=i32)).astype(i32), axis=1)
    cnt = (cnt + _SEG - 1) // _SEG * _SEG
    base = jnp.cumsum(cnt, axis=1) - cnt
    total = jnp.sum(cnt, axis=0)
    region = (total + te - 1) // te * te
    region_end = jnp.cumsum(region)
    region_start = region_end - region
    off = region_start[None, :] + jnp.cumsum(cnt, axis=0) - cnt
    n_used = region_end[-1:] // te
    meta = jnp.concatenate([off.reshape(-1), cnt.reshape(-1), base.reshape(-1), jnp.sum(cnt, axis=1),
                            region_start + total, region - total, n_used]).astype(i32)
    tile_row = jnp.arange(n_tiles, dtype=i32) * te
    tile_group = jnp.minimum(jnp.sum((tile_row[:, None] >= region_end[None, :]).astype(i32), axis=1), N_GROUPS - 1)
    emeta = jnp.concatenate([tile_group, n_used]).astype(i32)

    anyspec = pl.BlockSpec(memory_space=pl.ANY)
    tok = lambda w: pl.BlockSpec((tm, w), lambda i, m: (i, 0))
    xs, ws = pl.pallas_call(
        functools.partial(_moe_sort_kernel, nt=nt, tm=tm, te=te),
        grid_spec=pltpu.PrefetchScalarGridSpec(
            num_scalar_prefetch=1, grid=(nt,),
            in_specs=[tok(d), tok(LANES)],
            out_specs=[anyspec, anyspec],
            scratch_shapes=[pltpu.VMEM((2, slots, d), BF16), pltpu.VMEM((2, slots, LANES), F32),
                            pltpu.VMEM((te, d), BF16), pltpu.VMEM((te, LANES), F32),
                            pltpu.SemaphoreType.DMA((2, 2))]),
        out_shape=[jax.ShapeDtypeStruct((rows, d), BF16), jax.ShapeDtypeStruct((rows, LANES), F32)],
        compiler_params=_cparams(("arbitrary",)),
        name="moe_sort",
    )(meta, u, wf)

    srt = lambda w: pl.BlockSpec((te, w), lambda r, m: (r, 0))
    grp_w = pl.BlockSpec((EXPERTS_PER_GROUP, d, f), lambda r, m: (m[r], 0, 0))
    ys = pl.pallas_call(
        functools.partial(_moe_expert_kernel, n_tiles=n_tiles),
        grid_spec=pltpu.PrefetchScalarGridSpec(
            num_scalar_prefetch=1, grid=(n_tiles,),
            in_specs=[srt(d), srt(LANES), grp_w, grp_w,
                      pl.BlockSpec((EXPERTS_PER_GROUP * f, d), lambda r, m: (m[r], 0))],
            out_specs=srt(d)),
        out_shape=jax.ShapeDtypeStruct((rows, d), BF16),
        compiler_params=_cparams(("parallel",)),
        name="moe_expert",
    )(emeta, xs, ws, wg, wu, wd.reshape(ne * f, d))

    return pl.pallas_call(
        functools.partial(_moe_unsort_kernel, nt=nt, tm=tm),
        grid_spec=pltpu.PrefetchScalarGridSpec(
            num_scalar_prefetch=1, grid=(nt,),
            in_specs=[tok(LANES), tok(d), pl.BlockSpec((1, d), lambda i, m: (0, 0)), anyspec],
            out_specs=tok(d),
            scratch_shapes=[pltpu.VMEM((2, slots, d), BF16), pltpu.SemaphoreType.DMA((2,))]),
        out_shape=jax.ShapeDtypeStruct((n, d), F32),
        compiler_params=_cparams(("arbitrary",)),
        name="moe_unsort",
    )(meta, wf, h, gf.reshape(1, d), ys)


def _pick(n, pref):
    t = min(n, pref)
    while n % t:
        t -= 8
    return t


def kernel(x_prompt, x_sample, mem_prompt, cache_mem_k, cache_mem_v, state_hgrn, state_rwkv, state_rwkv_shift, norm_mix_g, w_in, hgrn_lower_bounds, hgrn_norm_g, rwkv_mu, rwkv_w0, rwkv_w2, rwkv_a0, rwkv_a2, rwkv_g2, rwkv_k_k, rwkv_k_a, rwkv_r_k, rwkv_ln_g, rwkv_ln_b, w_out, norm_x_g, norm_mem_g, wq_x, wk_x, wv_x, wo_x, norm_ffn_g, router_group_w, router_group_b, router_expert_w, router_expert_b, expert_w_gate, expert_w_up, expert_w_down, norm_final_g):
    assert w_in.shape[0] == 1, "single-layer configuration"
    d = x_prompt.shape[-1]
    n_lora_w, n_lora_a, n_lora_g = rwkv_w2.shape[1], rwkv_a2.shape[1], rwkv_g2.shape[1]
    assert n_lora_w + n_lora_a == LANES and LANES < n_lora_g <= 2 * LANES
    n_lora = n_lora_w + n_lora_a + n_lora_g
    n_rwkv_in = 3 * d + n_lora
    lora_pad = 3 * LANES - n_lora

    pr0 = 4 * d
    gate0 = pr0 + n_rwkv_in
    n_cols = w_in.shape[2] + lora_pad
    w_perm = _permute_columns(w_in, [(0, pr0 + 3 * d), (gate0, 2 * d), (pr0 + 3 * d, n_lora)], n_cols,
                              _pick(d, 128))
    lora_col0 = 9 * d
    gate_block0 = 7
    mu = rwkv_mu[0]
    mul = jnp.pad(mu[3 * d:], (0, lora_pad)).reshape(1, 3 * LANES)
    par = jnp.stack([mu[:d], mu[d:2 * d], mu[2 * d:3 * d], rwkv_w0[0], rwkv_a0[0], rwkv_k_k[0], rwkv_k_a[0],
                     rwkv_r_k[0].reshape(d), rwkv_ln_g[0], rwkv_ln_b[0]])
    par = jnp.pad(par, ((0, _P_ROWS - par.shape[0]), (0, 0)))
    w2p = jnp.pad(rwkv_w2[0], ((0, n_lora_a), (0, 0))).astype(BF16)
    a2p = jnp.pad(rwkv_a2[0], ((n_lora_w, 0), (0, 0))).astype(BF16)
    g2a = rwkv_g2[0][:LANES].astype(BF16)
    g2b = jnp.pad(rwkv_g2[0][LANES:], ((0, 2 * LANES - n_lora_g), (0, 0))).astype(BF16)
    wout_b, wq_b, wo_b = w_out[0].astype(BF16), wq_x[0].astype(BF16), wo_x[0].astype(BF16)
    wr = jnp.concatenate([router_expert_w[0], router_group_w[0],
                          jnp.zeros((d, LANES - N_EXPERTS - N_GROUPS), F32)], axis=1)
    wr_hi, wr_lo = _split(wr)
    rb = jnp.pad(jnp.concatenate([router_expert_b[0].reshape(-1), router_group_b[0]]),
                 (0, LANES - N_EXPERTS - N_GROUPS)).reshape(1, LANES)
    wg_b, wu_b, wd_b = (expert_w_gate[0].astype(BF16), expert_w_up[0].astype(BF16),
                        expert_w_down[0].astype(BF16))

    nb_p = mem_prompt.shape[0]
    mem_k_p, mem_v_p = _mem_kv(mem_prompt, norm_mem_g[0], wk_x[0].astype(BF16), wv_x[0].astype(BF16))

    def group(x3, mem_k, mem_v, s_h, s_r, buf, tile_h, tile_r, chunk_r, bt, tt, te):
        bsz, tlen, _ = x3.shape
        n = bsz * tlen
        p = _norm_matmul(x3.reshape(n, d), norm_mix_g[0], w_perm, _pick(n, 1024), n_cols // 3)
        p3 = p.reshape(bsz, tlen, n_cols)
        o_h, s_h_new = _hgrn(p3, hgrn_lower_bounds, hgrn_norm_g[0], s_h, tile_h, 32)
        buf_rkv = buf[:, :, :3 * d]
        buf_l = jnp.pad(buf[:, :, 3 * d:], ((0, 0), (0, 0), (0, lora_pad)))
        o_r, s_r_new = _rwkv(p3, buf_rkv, buf_l, par, mul, w2p, a2p, g2a, g2b, s_r, d, tile_r, chunk_r)
        h2, u, wf = _attn(x3, o_h, o_r, p3, mem_k, mem_v, wout_b, norm_x_g[0], wq_b, wo_b, norm_ffn_g[0],
                          wr_hi, wr_lo, rb, bt, tt, gate_block0)
        y = _moe(u.reshape(n, d), wf.reshape(n, LANES), h2.reshape(n, d), wg_b, wu_b, wd_b, norm_final_g,
                 _pick(n, 512), te)
        shift = jnp.concatenate([p3[:, -1:, pr0:pr0 + 3 * d], p3[:, -1:, lora_col0:lora_col0 + n_lora]], axis=-1)
        return y.reshape(bsz, tlen, d), s_h_new, s_r_new, shift

    nb_s, t_s, _ = x_sample.shape
    t_p = x_prompt.shape[1]
    zeros_like_state = lambda s: jnp.zeros((nb_p,) + s.shape[2:], s.dtype)
    y_p, sh_p, sr_p, bf_p = group(
        x_prompt, mem_k_p, mem_v_p, zeros_like_state(state_hgrn), zeros_like_state(state_rwkv),
        zeros_like_state(state_rwkv_shift), _pick(t_p, 512), _pick(t_p, 256), 64, 1, _pick(t_p, 512), 512)
    y_s, sh_s, sr_s, bf_s = group(
        x_sample, cache_mem_k[0], cache_mem_v[0],
        state_hgrn[0], state_rwkv[0], state_rwkv_shift[0], t_s, t_s, t_s, _pick(nb_s, 4), t_s, 256)

    kv_shape = (1,) + mem_k_p.shape[:2] + cache_mem_k.shape[3:]
    return (y_p, y_s, mem_k_p.reshape(kv_shape), mem_v_p.reshape(kv_shape), sh_p[None], sh_s[None],
            sr_p[None], sr_s[None], bf_p[None], bf_s[None])
```
